```python
import jax, jax.numpy as jnp
from jax import lax
import numpy as np

D_MODEL = 1024
BATCH = 8
SEQ = 2048
DEPTH = 1
DEC_BATCH = 128
DEC_SEQ = 4
PAST_LEN = 8192
PAGE_SIZE = 128

M_HEADS = 4
M_DQK = 128
M_DV = 256
M_CHUNK = 64
A_HEADS = 16
A_KV_HEADS = 4
A_DH = 64
WINDOW = 128
ROPE_THETA = 10000.0
ATTN_SCALE = A_DH ** -0.5
D_FF = 2816
LN_EPS = 1e-5
HEAD_NORM_EPS = 1e-6
ADA_CHUNKS = 9
DEEPNORM_ALPHA = (2.0 * DEPTH) ** 0.25
DEEPNORM_BETA = (8.0 * DEPTH) ** -0.25

SPLITS = (M_HEADS * M_DQK, M_HEADS * M_DQK, M_HEADS * M_DV, M_HEADS * M_DV, M_HEADS, M_HEADS,
          A_HEADS * A_DH, A_KV_HEADS * A_DH, A_KV_HEADS * A_DH, D_MODEL, D_MODEL)
D_IN = sum(SPLITS)
SPLIT_IDX = tuple(np.cumsum(SPLITS)[:-1].tolist())

kernel_name = "hybrid_mlstm_swa_macaron_deepnorm_step"


def layer_norm(x, g, b):
    xf = x.astype(jnp.float32)
    mu = xf.mean(-1, keepdims=True)
    var = jnp.mean(jnp.square(xf - mu), -1, keepdims=True)
    return ((xf - mu) * lax.rsqrt(var + LN_EPS)).astype(x.dtype) * g + b


def head_norm(h, g):
    hf = h.astype(jnp.float32)
    mu = hf.mean(-1, keepdims=True)
    var = jnp.mean(jnp.square(hf - mu), -1, keepdims=True)
    y = ((hf - mu) * lax.rsqrt(var + HEAD_NORM_EPS)).astype(h.dtype)
    B, S, H, DV = h.shape
    return y.reshape(B, S, H * DV) * g


def modulate(x, shift, scale):
    return x * (1 + scale) + shift


def swiglu(h, w_up, w_down):
    a, u = jnp.split(h @ w_up, 2, axis=-1)
    return (jax.nn.silu(a) * u) @ w_down


def rope(x, pos):
    half = x.shape[-1] // 2
    inv = ROPE_THETA ** (-jnp.arange(half, dtype=jnp.float32) / half)
    ang = pos.astype(jnp.float32)[:, None] * inv[None, :]
    cos = jnp.cos(ang)[:, None, :].astype(x.dtype)
    sin = jnp.sin(ang)[:, None, :].astype(x.dtype)
    x1, x2 = x[..., :half], x[..., half:]
    return jnp.concatenate([x1 * cos - x2 * sin, x2 * cos + x1 * sin], axis=-1)


def sink_softmax(scores, sinks):
    s = sinks.astype(jnp.float32)[..., None, None]
    mx = jnp.maximum(scores.max(-1, keepdims=True), s)
    p = jnp.exp(scores - mx)
    return p / (p.sum(-1, keepdims=True) + jnp.exp(s - mx))


def mlstm_chunkwise(q, k, v, i_pre, logf, C0, n0, m0, chunk):
    B, S, H, _ = q.shape
    nc = S // chunk

    def to_chunks(t):
        return jnp.moveaxis(t.reshape((B, nc, chunk) + t.shape[2:]), 1, 0)

    xs = tuple(to_chunks(t) for t in (q, k, v, i_pre, logf))
    causal = jnp.tril(jnp.ones((chunk, chunk), dtype=bool))

    def step(carry, inp):
        C, n, m = carry
        qc, kc, vc, ic, fc = inp
        b = jnp.cumsum(fc, axis=1)
        logD = b[:, :, None, :] - b[:, None, :, :] + ic[:, None, :, :]
        logD = jnp.where(causal[None, :, :, None], logD, -jnp.inf)
        inter = b + m[:, None, :]
        m_t = jnp.maximum(inter, logD.max(axis=2))
        dmat = jnp.exp(logD - m_t[:, :, None, :])
        e_inter = jnp.exp(inter - m_t)
        s = jnp.einsum('bthd,bshd->btsh', qc, kc) * dmat
        num = jnp.einsum('btsh,bshv->bthv', s, vc) + e_inter[..., None] * jnp.einsum('bhvd,bthd->bthv', C, qc)
        den = s.sum(axis=2) + e_inter * jnp.einsum('bhd,bthd->bth', n, qc)
        h = num / jnp.maximum(jnp.abs(den), jnp.exp(-m_t))[..., None]
        m_new = m_t[:, -1]
        w_s = jnp.exp(b[:, -1:, :] - b + ic - m_new[:, None, :])
        e_c = jnp.exp(b[:, -1] + m - m_new)
        C_new = e_c[..., None, None] * C + jnp.einsum('bsh,bshv,bshd->bhvd', w_s, vc, kc)
        n_new = e_c[..., None] * n + jnp.einsum('bsh,bshd->bhd', w_s, kc)
        return (C_new.astype(C.dtype), n_new.astype(n.dtype), m_new.astype(m.dtype)), h

    (C, n, m), hs = lax.scan(step, (C0, n0, m0), xs)
    h = jnp.moveaxis(hs, 0, 1).reshape(B, S, H, -1)
    return h, C, n, m


def swa_prompt(q, k, v, sinks):
    B, S, HQ, D = q.shape
    R = HQ // A_KV_HEADS
    nb = S // WINDOW
    qb = q.reshape(B, nb, WINDOW, A_KV_HEADS, R, D)
    kb = k.reshape(B, nb, WINDOW, A_KV_HEADS, D)
    vb = v.reshape(B, nb, WINDOW, A_KV_HEADS, D)
    padw = ((0, 0), (1, 0), (0, 0), (0, 0), (0, 0))
    kk = jnp.concatenate([jnp.pad(kb[:, :-1], padw), kb], axis=2)
    vv = jnp.concatenate([jnp.pad(vb[:, :-1], padw), vb], axis=2)
    scores = jnp.einsum('bnqgrd,bnkgd->bngrqk', qb, kk).astype(jnp.float32) * ATTN_SCALE
    qi = WINDOW + jnp.arange(WINDOW)[:, None]
    ki = jnp.arange(2 * WINDOW)[None, :]
    delta = qi - ki
    band = (delta >= 0) & (delta < WINDOW)
    has_prev = (jnp.arange(nb) > 0)[:, None, None] | (ki >= WINDOW)[None]
    mask = band[None] & has_prev
    scores = jnp.where(mask[None, :, None, None], scores, -jnp.inf)
    pr = sink_softmax(scores, sinks.reshape(A_KV_HEADS, R))
    o = jnp.einsum('bngrqk,bnkgd->bnqgrd', pr.astype(vv.dtype), vv).reshape(B, S, HQ * D)
    nbuf = min(WINDOW, S)
    return o, k[:, S - nbuf:], v[:, S - nbuf:]


def swa_sample(q, k, v, kbuf, vbuf, sinks):
    B, T, HQ, D = q.shape
    R = HQ // A_KV_HEADS
    Wb = kbuf.shape[1]
    kk = jnp.concatenate([kbuf, k.astype(kbuf.dtype)], axis=1)
    vv = jnp.concatenate([vbuf, v.astype(vbuf.dtype)], axis=1)
    qg = q.reshape(B, T, A_KV_HEADS, R, D)
    scores = jnp.einsum('btgrd,bkgd->bgrtk', qg, kk).astype(jnp.float32) * ATTN_SCALE
    delta = (Wb + jnp.arange(T))[:, None] - jnp.arange(Wb + T)[None, :]
    mask = (delta >= 0) & (delta < WINDOW)
    scores = jnp.where(mask, scores, -jnp.inf)
    pr = sink_softmax(scores, sinks.reshape(A_KV_HEADS, R))
    o = jnp.einsum('bgrtk,bkgd->btgrd', pr.astype(vv.dtype), vv).reshape(B, T, HQ * D)
    return o, kk[:, -Wb:], vv[:, -Wb:]


def decoder_layer(x, c, pos, C0, n0, m0, kbuf, vbuf, chunk, p):
    B, S, _ = x.shape
    mod = jax.nn.silu(c) @ p['w_ada'] + p['b_ada']
    sh1, sc1, g1, sh2, sc2, g2, sh3, sc3, g3 = jnp.split(mod[:, None, :], ADA_CHUNKS, axis=-1)
    f = swiglu(modulate(x, sh1, sc1), p['w_ffn1_up'], p['w_ffn1_down'])
    x = layer_norm(DEEPNORM_ALPHA * x + 0.5 * (1 + g1) * f, p['ln1_g'], p['ln1_b'])
    h = modulate(x, sh2, sc2)
    mq, mk, mv, mo, mi, mf, aq, ak, av, gm, ga = jnp.split(h @ p['w_in'], SPLIT_IDX, axis=-1)
    qm = mq.reshape(B, S, M_HEADS, M_DQK)
    km = mk.reshape(B, S, M_HEADS, M_DQK) * (M_DQK ** -0.5)
    vm = mv.reshape(B, S, M_HEADS, M_DV)
    i_pre = (mi + p['b_igate']).astype(jnp.float32)
    logf = jax.nn.log_sigmoid((mf + p['b_fgate']).astype(jnp.float32))
    hm, C, n, m = mlstm_chunkwise(qm, km, vm, i_pre, logf, C0, n0, m0, chunk)
    ym = (head_norm(hm, p['m_norm_g']) * jax.nn.sigmoid(mo)) @ p['w_branch_m']
    qa = rope(aq.reshape(B, S, A_HEADS, A_DH), pos)
    ka = rope(ak.reshape(B, S, A_KV_HEADS, A_DH), pos)
    va = av.reshape(B, S, A_KV_HEADS, A_DH)
    if kbuf is None:
        oa, kb_new, vb_new = swa_prompt(qa, ka, va, p['sinks'])
    else:
        oa, kb_new, vb_new = swa_sample(qa, ka, va, kbuf, vbuf, p['sinks'])
    ya = oa @ p['w_branch_a']
    mix = jax.nn.sigmoid(gm) * ym + jax.nn.sigmoid(ga) * ya
    x = layer_norm(DEEPNORM_ALPHA * x + (1 + g2) * (mix @ p['w_out']), p['ln2_g'], p['ln2_b'])
    f = swiglu(modulate(x, sh3, sc3), p['w_ffn2_up'], p['w_ffn2_down'])
    x = layer_norm(DEEPNORM_ALPHA * x + 0.5 * (1 + g3) * f, p['ln3_g'], p['ln3_b'])
    return x, (C, n, m, kb_new, vb_new)


def setup_inputs(seed: int = 0) -> dict:
    key = jax.random.key(seed)
    keys = jax.random.split(key, 32)

    def nrm(i, shape, scale):
        return jax.random.normal(keys[i], shape, jnp.float32) * scale

    L, D = DEPTH, D_MODEL
    wbuf = min(WINDOW, PAST_LEN)
    fin = D ** -0.5
    return {
        'x_prompt': nrm(0, (BATCH, SEQ, D), 1.0),
        'x_sample': nrm(1, (DEC_BATCH, DEC_SEQ, D), 1.0),
        'state_mlstm_C': nrm(2, (L, DEC_BATCH, M_HEADS, M_DV, M_DQK), 0.5),
        'state_mlstm_n': nrm(3, (L, DEC_BATCH, M_HEADS, M_DQK), 0.5),
        'state_mlstm_m': nrm(4, (L, DEC_BATCH, M_HEADS), 1.0),
        'cache_swa_k': nrm(5, (L, DEC_BATCH, wbuf, A_KV_HEADS, A_DH), 1.0),
        'cache_swa_v': nrm(6, (L, DEC_BATCH, wbuf, A_KV_HEADS, A_DH), 1.0),
        'c_prompt': nrm(7, (BATCH, D), 1.0),
        'c_sample': nrm(8, (DEC_BATCH, D), 1.0),
        'w_ada': nrm(9, (L, D, ADA_CHUNKS * D), 0.1 * fin),
        'b_ada': nrm(10, (L, ADA_CHUNKS * D), 0.01),
        'w_ffn1_up': nrm(11, (L, D, 2 * D_FF), fin),
        'w_ffn1_down': nrm(12, (L, D_FF, D), DEEPNORM_BETA * D_FF ** -0.5),
        'ln1_g': 1.0 + nrm(13, (L, D), 0.02),
        'ln1_b': nrm(14, (L, D), 0.02),
        'w_in': nrm(15, (L, D, D_IN), fin),
        'b_igate': nrm(16, (L, M_HEADS), 0.1),
        'b_fgate': 3.0 + 3.0 * jax.random.uniform(keys[17], (L, M_HEADS), jnp.float32),
        'm_norm_g': 1.0 + nrm(18, (L, M_HEADS * M_DV), 0.02),
        'sinks': nrm(19, (L, A_HEADS), 0.5),
        'w_branch_m': nrm(20, (L, M_HEADS * M_DV, D), DEEPNORM_BETA * (M_HEADS * M_DV) ** -0.5),
        'w_branch_a': nrm(21, (L, A_HEADS * A_DH, D), DEEPNORM_BETA * (A_HEADS * A_DH) ** -0.5),
        'w_out': nrm(22, (L, D, D), DEEPNORM_BETA * fin),
        'ln2_g': 1.0 + nrm(23, (L, D), 0.02),
        'ln2_b': nrm(24, (L, D), 0.02),
        'w_ffn2_up': nrm(25, (L, D, 2 * D_FF), fin),
        'w_ffn2_down': nrm(26, (L, D_FF, D), DEEPNORM_BETA * D_FF ** -0.5),
        'ln3_g': 1.0 + nrm(27, (L, D), 0.02),
        'ln3_b': nrm(28, (L, D), 0.02),
    }


def reference(x_prompt, x_sample, state_mlstm_C, state_mlstm_n, state_mlstm_m, cache_swa_k, cache_swa_v,
              c_prompt, c_sample, w_ada, b_ada, w_ffn1_up, w_ffn1_down, ln1_g, ln1_b, w_in, b_igate,
              b_fgate, m_norm_g, sinks, w_branch_m, w_branch_a, w_out, ln2_g, ln2_b, w_ffn2_up,
              w_ffn2_down, ln3_g, ln3_b):
    Bp, Sp, _ = x_prompt.shape
    Ts = x_sample.shape[1]
    pos_p = jnp.arange(Sp)
    pos_s = PAST_LEN + jnp.arange(Ts)
    yp, ys = x_prompt, x_sample
    outs_p, outs_s = [], []
    for l in range(DEPTH):
        p = dict(w_ada=w_ada[l], b_ada=b_ada[l], w_ffn1_up=w_ffn1_up[l], w_ffn1_down=w_ffn1_down[l],
                 ln1_g=ln1_g[l], ln1_b=ln1_b[l], w_in=w_in[l], b_igate=b_igate[l], b_fgate=b_fgate[l],
                 m_norm_g=m_norm_g[l], sinks=sinks[l], w_branch_m=w_branch_m[l],
                 w_branch_a=w_branch_a[l], w_out=w_out[l], ln2_g=ln2_g[l], ln2_b=ln2_b[l],
                 w_ffn2_up=w_ffn2_up[l], w_ffn2_down=w_ffn2_down[l], ln3_g=ln3_g[l], ln3_b=ln3_b[l])
        C0 = jnp.zeros((Bp, M_HEADS, M_DV, M_DQK), x_prompt.dtype)
        n0 = jnp.zeros((Bp, M_HEADS, M_DQK), x_prompt.dtype)
        m0 = jnp.zeros((Bp, M_HEADS), x_prompt.dtype)
        yp, st_p = decoder_layer(yp, c_prompt, pos_p, C0, n0, m0, None, None, min(M_CHUNK, Sp), p)
        ys, st_s = decoder_layer(ys, c_sample, pos_s, state_mlstm_C[l], state_mlstm_n[l], state_mlstm_m[l],
                                 cache_swa_k[l], cache_swa_v[l], Ts, p)
        outs_p.append(st_p)
        outs_s.append(st_s)
    C_p, n_p, m_p, k_p, v_p = [jnp.stack(t) for t in zip(*outs_p)]
    C_s, n_s, m_s, k_s, v_s = [jnp.stack(t) for t in zip(*outs_s)]
    return (yp, ys, C_p, n_p, m_p, k_p, v_p, C_s, n_s, m_s, k_s, v_s)
```

```python
import functools

import jax
import jax.numpy as jnp
from jax import lax
from jax.experimental import pallas as pl
from jax.experimental.pallas import tpu as pltpu

F32 = jnp.float32
BF16 = jnp.bfloat16

D_MODEL = 1024
D_FF = 2816
DEPTH = 1
M_HEADS = 4
M_DQK = 128
M_DV = 256
A_HEADS = 16
A_KV_HEADS = 4
A_DH = 64
WINDOW = 128
PAST_LEN = 8192
ROPE_THETA = 10000.0
ATTN_SCALE = A_DH ** -0.5
LN_EPS = 1e-5
HEAD_NORM_EPS = 1e-6
ADA_CHUNKS = 9
DEEPNORM_ALPHA = (2.0 * DEPTH) ** 0.25
K_SCALE = M_DQK ** -0.5

LANES = 128
VMEM_LIMIT_BYTES = 56 * 1024 * 1024

Z_Q, Z_K, Z_V, Z_O = 0, 512, 1024, 2048
Z_AQ, Z_AK, Z_AV = 3072, 4096, 4352
Z_GM, Z_GA, Z_IF = 4608, 5632, 6656
Z_TOTAL = 6784

ROW_TILE = 512
FF_CHUNK = 256
MLSTM_CHUNK = 128

_NT = (((1,), (1,)), ((), ()))


def _params():
    return pltpu.CompilerParams(vmem_limit_bytes=VMEM_LIMIT_BYTES)


def _resident(shape):
    return pl.BlockSpec(shape, lambda *_: (0,) * len(shape), pipeline_mode=pl.Buffered(1))


def _rows(ref):
    v = ref[...]
    return v.reshape(v.shape[-2], v.shape[-1])


def _layer_norm(y, g, b, eps):
    mu = jnp.mean(y, axis=-1, keepdims=True)
    d = y - mu
    var = jnp.mean(d * d, axis=-1, keepdims=True)
    return d * lax.rsqrt(var + eps) * g + b


def _sigmoid(x):
    return 1.0 / (1.0 + jnp.exp(-x))


def _mod_specs(per_row, tm, tiles_per_batch, chunks):
    if per_row:
        return [pl.BlockSpec((tm, D_MODEL), lambda i, c=c: (i, c)) for c in chunks]
    return [pl.BlockSpec((1, 1, D_MODEL), lambda i, c=c: (i // tiles_per_batch, 0, c)) for c in chunks]


def _ada_kernel(c_ref, w_ref, b_ref, o_ref):
    c = c_ref[...]
    s = (c * _sigmoid(c)).astype(BF16)
    o_ref[...] = jnp.dot(s, w_ref[...].astype(BF16), preferred_element_type=F32) + b_ref[...]


def _ada(c_all, w_ada, b_ada):
    rows = c_all.shape[0]
    n_out = w_ada.shape[1]
    tn = D_MODEL
    return pl.pallas_call(
        _ada_kernel,
        grid=(n_out // tn,),
        in_specs=[pl.BlockSpec((rows, D_MODEL), lambda j: (0, 0)),
                  pl.BlockSpec((D_MODEL, tn), lambda j: (0, j)),
                  pl.BlockSpec((1, tn), lambda j: (0, j))],
        out_specs=pl.BlockSpec((rows, tn), lambda j: (0, j)),
        out_shape=jax.ShapeDtypeStruct((rows, n_out), F32),
        compiler_params=_params(),
        name="ada",
    )(c_all, w_ada, b_ada.reshape(1, n_out))


def _ffn_kernel(x_ref, sh_ref, sc_ref, g_ref, wup_ref, wdn_ref, lg_ref, lb_ref, o_ref, act_ref):
    x = x_ref[...]
    h = (x * (1.0 + _rows(sc_ref)) + _rows(sh_ref)).astype(BF16)
    for c in range(0, D_FF, FF_CHUNK):
        a = jnp.dot(h, wup_ref[:, c:c + FF_CHUNK], preferred_element_type=F32)
        u = jnp.dot(h, wup_ref[:, D_FF + c:D_FF + c + FF_CHUNK], preferred_element_type=F32)
        act_ref[:, c:c + FF_CHUNK] = (a * _sigmoid(a) * u).astype(BF16)
    f = jnp.dot(act_ref[...], wdn_ref[...], preferred_element_type=F32)
    y = DEEPNORM_ALPHA * x + (0.5 * (1.0 + _rows(g_ref))) * f
    o_ref[...] = _layer_norm(y, lg_ref[...], lb_ref[...], LN_EPS)


def _ffn(x, mod, per_row, rows_per_batch, chunks, w_up, w_down, ln_g, ln_b):
    m = x.shape[0]
    tm = ROW_TILE
    return pl.pallas_call(
        _ffn_kernel,
        grid=(m // tm,),
        in_specs=[pl.BlockSpec((tm, D_MODEL), lambda i: (i, 0))]
        + _mod_specs(per_row, tm, rows_per_batch // tm, chunks)
        + [_resident((D_MODEL, 2 * D_FF)), _resident((D_FF, D_MODEL)),
           _resident((1, D_MODEL)), _resident((1, D_MODEL))],
        out_specs=pl.BlockSpec((tm, D_MODEL), lambda i: (i, 0)),
        out_shape=jax.ShapeDtypeStruct((m, D_MODEL), F32),
        scratch_shapes=[pltpu.VMEM((tm, D_FF), BF16)],
        compiler_params=_params(),
        name="ffn",
    )(x, mod, mod, mod, w_up, w_down, ln_g.reshape(1, D_MODEL), ln_b.reshape(1, D_MODEL))


def _proj_kernel(x_ref, sh_ref, sc_ref, w_ref, bif_ref, cos_ref, sin_ref,
                 q_ref, k_ref, v_ref, o_ref, aq_ref, ak_ref, av_ref, gm_ref, ga_ref, gt_ref):
    x = x_ref[...]
    tm = x.shape[0]
    h = (x * (1.0 + _rows(sc_ref)) + _rows(sh_ref)).astype(BF16)

    def seg(lo, width=256):
        return jnp.dot(h, w_ref[:, lo:lo + width], preferred_element_type=F32)

    for c in range(0, 512, 256):
        q_ref[:, c:c + 256] = seg(Z_Q + c).astype(BF16)
        k_ref[:, c:c + 256] = (seg(Z_K + c) * K_SCALE).astype(BF16)
    for c in range(0, 1024, 256):
        v_ref[:, c:c + 256] = seg(Z_V + c).astype(BF16)
        o_ref[:, c:c + 256] = seg(Z_O + c)
        gm_ref[:, c:c + 256] = seg(Z_GM + c)
        ga_ref[:, c:c + 256] = seg(Z_GA + c)

    cos = cos_ref[...]
    sin = sin_ref[...]
    lane = lax.broadcasted_iota(jnp.int32, (tm, LANES), 1)
    low_half = (lane & 32) == 0

    def rope(z):
        partner = jnp.where(low_half, pltpu.roll(z, LANES - 32, 1), pltpu.roll(z, 32, 1))
        return z * cos + partner * sin

    for c in range(0, 1024, 256):
        z = seg(Z_AQ + c)
        aq_ref[:, c:c + LANES] = rope(z[:, :LANES]).astype(BF16)
        aq_ref[:, c + LANES:c + 256] = rope(z[:, LANES:]).astype(BF16)
    z = seg(Z_AK)
    ak_ref[:, :LANES] = rope(z[:, :LANES])
    ak_ref[:, LANES:] = rope(z[:, LANES:])
    av_ref[...] = seg(Z_AV)

    zg = seg(Z_IF, LANES) + bif_ref[...]
    logsig = jnp.minimum(zg, 0.0) - jnp.log(1.0 + jnp.exp(-jnp.abs(zg)))
    gt_ref[...] = jnp.where(lane < M_HEADS, zg, logsig)


def _proj(x, mod, per_row, rows_per_batch, chunks, wz, bif, cos_t, sin_t):
    m = x.shape[0]
    tm = ROW_TILE
    n_pos_tiles = cos_t.shape[0] // tm

    def tok(width):
        return pl.BlockSpec((tm, width), lambda i: (i, 0))

    widths = (512, 512, 1024, 1024, 1024, 256, 256, 1024, 1024, LANES)
    dtypes = (BF16, BF16, BF16, F32, BF16, F32, F32, F32, F32, F32)
    return pl.pallas_call(
        _proj_kernel,
        grid=(m // tm,),
        in_specs=[tok(D_MODEL)] + _mod_specs(per_row, tm, rows_per_batch // tm, chunks)
        + [_resident((D_MODEL, Z_TOTAL)), _resident((1, LANES)),
           pl.BlockSpec((tm, LANES), lambda i: (i % n_pos_tiles, 0)),
           pl.BlockSpec((tm, LANES), lambda i: (i % n_pos_tiles, 0))],
        out_specs=[tok(w) for w in widths],
        out_shape=[jax.ShapeDtypeStruct((m, w), d) for w, d in zip(widths, dtypes)],
        compiler_params=_params(),
        name="proj",
    )(x, mod, mod, wz, bif, cos_t, sin_t)


def _mlstm_kernel(*refs, nb, tpb, hps, zero_init):
    if zero_init:
        (q_ref, k_ref, v_ref, g_ref, mo_ref, ng_ref,
         out_ref, c_ref, n_ref, m_ref, w_s, e_s, gt_s, ct_s) = refs
    else:
        (q_ref, k_ref, v_ref, g_ref, mo_ref, ng_ref, c0_ref, n0_ref, m0_ref,
         out_ref, c_ref, n_ref, m_ref, w_s, e_s, gt_s, ct_s) = refs
    L = nb * tpb
    shift = tpb.bit_length() - 1
    first_chunk = pl.program_id(2) == 0

    @pl.when(first_chunk)
    def _():
        if zero_init:
            c_ref[...] = jnp.zeros_like(c_ref)
            n_ref[...] = jnp.zeros_like(n_ref)
        else:
            c_ref[...] = c0_ref[...]
            n_ref[...] = n0_ref[...]

    @pl.when(first_chunk & (pl.program_id(1) == 0))
    def _():
        m_ref[...] = jnp.zeros_like(m_ref) if zero_init else m0_ref[...]

    row = lax.broadcasted_iota(jnp.int32, (L, L), 0)
    col = lax.broadcasted_iota(jnp.int32, (L, L), 1)
    same = (row >> shift) == (col >> shift)
    causal = same & (col <= row)
    lane = lax.broadcasted_iota(jnp.int32, (L, LANES), 1)
    row_seq = lax.broadcasted_iota(jnp.int32, (L, 1), 0) >> shift

    def lane_col(x, idx):
        return jnp.sum(jnp.where(lane == idx, x, 0.0), axis=1, keepdims=True)

    gates = g_ref[...]
    cum = jnp.dot(causal.astype(F32), gates, precision=lax.Precision.HIGHEST,
                  preferred_element_type=F32)
    gt_s[...] = gates.T
    ct_s[...] = cum.T
    m_rows = m_ref[...]
    m_next = m_rows

    for hl in range(hps):
        head = hl if hps == M_HEADS else pl.program_id(1) * hps + hl
        qs = slice(hl * M_DQK, (hl + 1) * M_DQK)
        vs = slice(hl * M_DV, (hl + 1) * M_DV)
        q = q_ref[:, qs]
        k = k_ref[:, qs]
        v = v_ref[:, vs]
        b_c = lane_col(cum, M_HEADS + head)
        m_p = lane_col(m_rows, head)
        i_r = gt_s[pl.ds(head, 1), :]
        b_r = ct_s[pl.ds(M_HEADS + head, 1), :]

        log_d = jnp.where(causal, b_c - b_r + i_r, -jnp.inf)
        m_t = jnp.maximum(b_c + m_p, jnp.max(log_d, axis=1, keepdims=True))
        dmat = jnp.exp(log_d - m_t)
        e_int = jnp.exp(b_c + m_p - m_t)

        b_last = jnp.min(jnp.where(same, b_r, jnp.inf), axis=1, keepdims=True)
        log_w = jnp.where(same, b_last - b_r + i_r, -jnp.inf)
        m_new = jnp.maximum(b_last + m_p, jnp.max(log_w, axis=1, keepdims=True))
        w_mat = jnp.exp(log_w - m_new)
        e_c = jnp.exp(b_last + m_p - m_new)
        w_s[...] = w_mat
        e_s[...] = jnp.broadcast_to(e_c, (L, LANES))

        s = lax.dot_general(q, k, _NT, preferred_element_type=F32) * dmat
        v_t = v.astype(F32).T

        def seq_step(j, acc, hl=hl, q=q, k=k, v_t=v_t):
            c_j = c_ref[j, hl]
            q_j = jnp.where(row_seq == j, q, jnp.zeros_like(q))
            acc = acc + lax.dot_general(q_j, c_j.astype(BF16), _NT, preferred_element_type=F32)
            w_row = w_s[pl.ds(j * tpb, 1), :]
            lhs = (v_t * w_row).astype(BF16)
            c_ref[j, hl] = (e_s[pl.ds(j * tpb, 1), :] * c_j
                            + jnp.dot(lhs, k, preferred_element_type=F32))
            return acc

        zero = jnp.zeros((L, M_DV), F32)
        inter = seq_step(0, zero) if nb == 1 else lax.fori_loop(0, nb, seq_step, zero)

        n_rows = n_ref[:, qs]
        qn = jnp.sum(q.astype(F32) * n_rows, axis=1, keepdims=True)
        num = jnp.dot(s.astype(BF16), v, preferred_element_type=F32) + e_int * inter
        den = jnp.sum(s, axis=1, keepdims=True) + e_int * qn
        hh = num / jnp.maximum(jnp.abs(den), jnp.exp(-m_t))

        mu = jnp.mean(hh, axis=1, keepdims=True)
        dlt = hh - mu
        var = jnp.mean(dlt * dlt, axis=1, keepdims=True)
        y = dlt * lax.rsqrt(var + HEAD_NORM_EPS)
        out_ref[:, vs] = (y * ng_ref[:, vs] * _sigmoid(mo_ref[:, vs])).astype(BF16)

        n_ref[:, qs] = e_c * n_rows + jnp.dot(w_mat.astype(BF16), k, preferred_element_type=F32)
        m_next = jnp.where(lane == head, m_new, m_next)

    m_ref[...] = m_next


def _mlstm(q, k, v, gates, mo, norm_g, state, nb, tpb, n_chunks, hps):
    m = q.shape[0]
    L = nb * tpb
    n_blocks = m // (L * n_chunks)
    hd = M_HEADS * M_DQK

    def tok(width):
        return pl.BlockSpec((L, width), lambda b, g, c: (b * n_chunks + c, g))

    c_spec = pl.BlockSpec((nb, hps, M_DV, M_DQK), lambda b, g, c: (b, g, 0, 0))
    n_spec = pl.BlockSpec((L, hps * M_DQK), lambda b, g, c: (b, g))
    m_spec = pl.BlockSpec((L, LANES), lambda b, g, c: (b, 0))
    in_specs = [tok(hps * M_DQK), tok(hps * M_DQK), tok(hps * M_DV),
                pl.BlockSpec((L, LANES), lambda b, g, c: (b * n_chunks + c, 0)),
                tok(hps * M_DV), pl.BlockSpec((1, hps * M_DV), lambda b, g, c: (0, g))]
    args = [q, k, v, gates, mo, norm_g.reshape(1, M_HEADS * M_DV)]
    if state is not None:
        in_specs += [c_spec, n_spec, m_spec]
        args += list(state)
    return pl.pallas_call(
        functools.partial(_mlstm_kernel, nb=nb, tpb=tpb, hps=hps, zero_init=state is None),
        grid=(n_blocks, M_HEADS // hps, n_chunks),
        in_specs=in_specs,
        out_specs=[tok(hps * M_DV), c_spec, n_spec, m_spec],
        out_shape=[jax.ShapeDtypeStruct((m, M_HEADS * M_DV), BF16),
                   jax.ShapeDtypeStruct((n_blocks * nb, M_HEADS, M_DV, M_DQK), F32),
                   jax.ShapeDtypeStruct((n_blocks * L, hd), F32),
                   jax.ShapeDtypeStruct((n_blocks * L, LANES), F32)],
        scratch_shapes=[pltpu.VMEM((L, L), F32), pltpu.VMEM((L, LANES), F32),
                        pltpu.VMEM((LANES, L), F32), pltpu.VMEM((LANES, L), F32)],
        compiler_params=_params(),
        name="mlstm",
    )(*args)


def _swa_kernel(sink_ref, q_ref, kp_ref, kc_ref, vp_ref, vc_ref, o_ref, *, nbb, tq, first_has_no_prev):
    rows = A_HEADS // A_KV_HEADS * tq
    lane = lax.broadcasted_iota(jnp.int32, (WINDOW, LANES), 1)
    low_k = lane < A_DH
    low_q = lax.broadcasted_iota(jnp.int32, (tq, LANES), 1) < A_DH
    qi = lax.broadcasted_iota(jnp.int32, (rows, WINDOW), 0) & (tq - 1)
    kj = lax.broadcasted_iota(jnp.int32, (rows, WINDOW), 1)
    valid_prev = kj > qi
    if first_has_no_prev:
        valid_prev = valid_prev & (pl.program_id(1) > 0)
    valid_cur = kj <= qi
    head_of_row = lax.broadcasted_iota(jnp.int32, (rows, 1), 0) >> (tq.bit_length() - 1)

    def dup(x, odd):
        rolled = pltpu.roll(x, A_DH, 1)
        return (jnp.where(low_k, rolled, x) if odd else jnp.where(low_k, x, rolled)).astype(BF16)

    def pad_keys(x):
        if x.shape[0] == WINDOW:
            return x
        return jnp.concatenate([x, jnp.zeros((WINDOW - x.shape[0], x.shape[1]), x.dtype)], axis=0)

    for jb in range(nbb):
        q = q_ref[jb]
        kp, vp = kp_ref[jb], vp_ref[jb]
        kc, vc = pad_keys(kc_ref[jb]), pad_keys(vc_ref[jb])
        for g in range(A_KV_HEADS):
            ks = slice((g // 2) * LANES, (g // 2 + 1) * LANES)
            kdp, kdc = dup(kp[:, ks], g % 2), dup(kc[:, ks], g % 2)
            vdp, vdc = dup(vp[:, ks], g % 2), dup(vc[:, ks], g % 2)
            parts = []
            sink = jnp.zeros((rows, 1), F32)
            for r in range(A_HEADS // A_KV_HEADS):
                head = g * (A_HEADS // A_KV_HEADS) + r
                qb = q[:, (head // 2) * LANES:(head // 2 + 1) * LANES]
                keep = low_q if head % 2 == 0 else ~low_q
                parts.append(jnp.where(keep, qb, jnp.zeros_like(qb)))
                sink = jnp.where(head_of_row == r, sink_ref[head], sink)
            q4 = jnp.concatenate(parts, axis=0)
            sp = lax.dot_general(q4, kdp, _NT, preferred_element_type=F32) * ATTN_SCALE
            sc = lax.dot_general(q4, kdc, _NT, preferred_element_type=F32) * ATTN_SCALE
            sp = jnp.where(valid_prev, sp, -jnp.inf)
            sc = jnp.where(valid_cur, sc, -jnp.inf)
            mx = jnp.maximum(jnp.maximum(jnp.max(sp, axis=1, keepdims=True),
                                         jnp.max(sc, axis=1, keepdims=True)), sink)
            pp = jnp.exp(sp - mx)
            pc = jnp.exp(sc - mx)
            den = (jnp.sum(pp, axis=1, keepdims=True) + jnp.sum(pc, axis=1, keepdims=True)
                   + jnp.exp(sink - mx))
            o4 = (jnp.dot(pp.astype(BF16), vdp, preferred_element_type=F32)
                  + jnp.dot(pc.astype(BF16), vdc, preferred_element_type=F32)) / den
            for pair in range(2):
                even = o4[(2 * pair) * tq:(2 * pair + 1) * tq]
                odd = o4[(2 * pair + 1) * tq:(2 * pair + 2) * tq]
                blk = 2 * g + pair
                o_ref[jb, :, blk * LANES:(blk + 1) * LANES] = jnp.where(low_q, even, odd).astype(BF16)


def _swa(sinks, q, k_prev_src, k_cur, v_prev_src, v_cur, nbb, prev_index, grid, first_has_no_prev):
    n, tq, _ = q.shape
    tk = k_cur.shape[1]
    kvw = A_KV_HEADS * A_DH

    def cur_index(*ids):
        return (ids[0] if len(ids) == 1 else ids[0] * grid[1] + ids[1], 0, 0)

    prev_spec = pl.BlockSpec((nbb, WINDOW, kvw), prev_index)
    cur_spec = pl.BlockSpec((nbb, tk, kvw), cur_index)
    return pl.pallas_call(
        functools.partial(_swa_kernel, nbb=nbb, tq=tq, first_has_no_prev=first_has_no_prev),
        grid=grid,
        in_specs=[pl.BlockSpec(memory_space=pltpu.SMEM),
                  pl.BlockSpec((nbb, tq, A_HEADS * A_DH), cur_index),
                  prev_spec, cur_spec, prev_spec, cur_spec],
        out_specs=pl.BlockSpec((nbb, tq, A_HEADS * A_DH), cur_index),
        out_shape=jax.ShapeDtypeStruct((n, tq, A_HEADS * A_DH), BF16),
        compiler_params=_params(),
        name="swa",
    )(sinks, q, k_prev_src, k_cur, v_prev_src, v_cur)


def _merge_kernel(x_ref, g_ref, hm_ref, oa_ref, gm_ref, ga_ref, wm_ref, wa_ref, wo_ref,
                  lg_ref, lb_ref, o_ref):
    ym = jnp.dot(hm_ref[...], wm_ref[...], preferred_element_type=F32)
    ya = jnp.dot(oa_ref[...], wa_ref[...], preferred_element_type=F32)
    mix = _sigmoid(gm_ref[...]) * ym + _sigmoid(ga_ref[...]) * ya
    t = jnp.dot(mix.astype(BF16), wo_ref[...], preferred_element_type=F32)
    y = DEEPNORM_ALPHA * x_ref[...] + (1.0 + _rows(g_ref)) * t
    o_ref[...] = _layer_norm(y, lg_ref[...], lb_ref[...], LN_EPS)


def _merge(x, mod, per_row, rows_per_batch, chunk, hm, oa, gm, ga, wm, wa, wo, ln_g, ln_b):
    m = x.shape[0]
    tm = ROW_TILE

    def tok():
        return pl.BlockSpec((tm, D_MODEL), lambda i: (i, 0))

    return pl.pallas_call(
        _merge_kernel,
        grid=(m // tm,),
        in_specs=[tok()] + _mod_specs(per_row, tm, rows_per_batch // tm, (chunk,))
        + [tok(), tok(), tok(), tok()]
        + [_resident((D_MODEL, D_MODEL))] * 3 + [_resident((1, D_MODEL))] * 2,
        out_specs=tok(),
        out_shape=jax.ShapeDtypeStruct((m, D_MODEL), F32),
        compiler_params=_params(),
        name="merge",
    )(x, mod, hm, oa, gm, ga, wm, wa, wo, ln_g.reshape(1, D_MODEL), ln_b.reshape(1, D_MODEL))


def _rope_tables(pos):
    half = A_DH // 2
    inv = ROPE_THETA ** (-jnp.arange(half, dtype=F32) / half)
    ang = pos.astype(F32)[:, None] * inv[None, :]
    cos, sin = jnp.cos(ang), jnp.sin(ang)
    return jnp.tile(cos, (1, 4)), jnp.concatenate([-sin, sin, -sin, sin], axis=1)


def _token_stage_1(x, mod, per_row, rows_per_batch, w, pos):
    x1 = _ffn(x, mod, per_row, rows_per_batch, (0, 1, 2), w["up1"], w["down1"], w["ln1_g"], w["ln1_b"])
    cos_t, sin_t = _rope_tables(pos)
    return x1, _proj(x1, mod, per_row, rows_per_batch, (3, 4), w["wz"], w["bif"], cos_t, sin_t)


def _token_stage_2(x1, mod, per_row, rows_per_batch, w, hm, oa, gm, ga):
    x2 = _merge(x1, mod, per_row, rows_per_batch, 5, hm, oa, gm, ga,
                w["wm"], w["wa"], w["wo"], w["ln2_g"], w["ln2_b"])
    return _ffn(x2, mod, per_row, rows_per_batch, (6, 7, 8), w["up2"], w["down2"], w["ln3_g"], w["ln3_b"])


def kernel(x_prompt, x_sample, state_mlstm_C, state_mlstm_n, state_mlstm_m, cache_swa_k, cache_swa_v, c_prompt, c_sample, w_ada, b_ada, w_ffn1_up, w_ffn1_down, ln1_g, ln1_b, w_in, b_igate, b_fgate, m_norm_g, sinks, w_branch_m, w_branch_a, w_out, ln2_g, ln2_b, w_ffn2_up, w_ffn2_down, ln3_g, ln3_b):
    assert w_ada.shape[0] == DEPTH == 1
    bp, sp, _ = x_prompt.shape
    bs, ts, _ = x_sample.shape
    kvw = A_KV_HEADS * A_DH

    win = w_in[0]
    wz = jnp.concatenate([win[:, :3072], win[:, 3080:], win[:, 3072:3080],
                          jnp.zeros((D_MODEL, Z_TOTAL - 6664), F32)], axis=1).astype(BF16)
    w = dict(
        up1=w_ffn1_up[0].astype(BF16), down1=w_ffn1_down[0].astype(BF16),
        up2=w_ffn2_up[0].astype(BF16), down2=w_ffn2_down[0].astype(BF16),
        wz=wz, wm=w_branch_m[0].astype(BF16), wa=w_branch_a[0].astype(BF16), wo=w_out[0].astype(BF16),
        bif=jnp.concatenate([b_igate[0], b_fgate[0], jnp.zeros((LANES - 2 * M_HEADS,), F32)]).reshape(1, LANES),
        ln1_g=ln1_g[0], ln1_b=ln1_b[0], ln2_g=ln2_g[0], ln2_b=ln2_b[0], ln3_g=ln3_g[0], ln3_b=ln3_b[0],
    )

    c_all = jnp.concatenate([c_prompt, jnp.repeat(c_sample, ts, axis=0)], axis=0)
    mod = _ada(c_all, w_ada[0], b_ada[0])
    mod_p = mod[:bp].reshape(bp, 1, ADA_CHUNKS * D_MODEL)
    mod_s = mod[bp:]

    mp = bp * sp
    x1p, (qm, km, vm, mo, aq, ak, av, gm, ga, gt) = _token_stage_1(
        x_prompt.reshape(mp, D_MODEL), mod_p, False, sp, w, jnp.arange(sp))
    n_chunks = sp // MLSTM_CHUNK
    hm, c_p, n_rows, m_rows = _mlstm(qm, km, vm, gt, mo, m_norm_g[0], None, 1, MLSTM_CHUNK, n_chunks, M_HEADS)
    n_p = n_rows.reshape(bp, MLSTM_CHUNK, M_HEADS, M_DQK)[:, 0]
    m_p = m_rows.reshape(bp, MLSTM_CHUNK, LANES)[:, 0, :M_HEADS]
    nblk = sp // WINDOW
    ak3 = ak.reshape(bp * nblk, WINDOW, kvw)
    av3 = av.reshape(bp * nblk, WINDOW, kvw)
    oa = _swa(sinks[0], aq.reshape(bp * nblk, WINDOW, A_HEADS * A_DH), ak3, ak3, av3, av3, 1,
              lambda b, n: (b * nblk + jnp.maximum(n - 1, 0), 0, 0), (bp, nblk), True)
    y_p = _token_stage_2(x1p, mod_p, False, sp, w, hm, oa.reshape(mp, D_MODEL), gm, ga)
    kb_p = ak.reshape(bp, sp, A_KV_HEADS, A_DH)[:, sp - WINDOW:]
    vb_p = av.reshape(bp, sp, A_KV_HEADS, A_DH)[:, sp - WINDOW:]

    ms = bs * ts
    x1s, (qm, km, vm, mo, aq, ak, av, gm, ga, gt) = _token_stage_1(
        x_sample.reshape(ms, D_MODEL), mod_s, True, ms, w, PAST_LEN + jnp.arange(ms) % ts)
    seqs = MLSTM_CHUNK // ts
    n0_rows = jnp.repeat(state_mlstm_n[0].reshape(bs, M_HEADS * M_DQK), ts, axis=0)
    m0_rows = jnp.repeat(jnp.pad(state_mlstm_m[0], ((0, 0), (0, LANES - M_HEADS))), ts, axis=0)
    hm, c_s, n_rows, m_rows = _mlstm(qm, km, vm, gt, mo, m_norm_g[0],
                                     (state_mlstm_C[0], n0_rows, m0_rows), seqs, ts, 1, 1)
    n_s = n_rows.reshape(bs, ts, M_HEADS, M_DQK)[:, 0]
    m_s = m_rows.reshape(bs, ts, LANES)[:, 0, :M_HEADS]
    tq = 16
    pad_t = ((0, 0), (0, tq - ts), (0, 0))
    k_new = ak.reshape(bs, ts, kvw)
    v_new = av.reshape(bs, ts, kvw)
    cache_k = cache_swa_k[0].reshape(bs, WINDOW, kvw)
    cache_v = cache_swa_v[0].reshape(bs, WINDOW, kvw)
    nbb = 4
    oa = _swa(sinks[0], jnp.pad(aq.reshape(bs, ts, A_HEADS * A_DH), pad_t),
              cache_k, jnp.pad(k_new, pad_t), cache_v, jnp.pad(v_new, pad_t), nbb,
              lambda i: (i, 0, 0), (bs // nbb,), False)
    y_s = _token_stage_2(x1s, mod_s, True, ms, w, hm, oa[:, :ts].reshape(ms, D_MODEL), gm, ga)
    kb_s = jnp.concatenate([cache_k[:, ts:], k_new], axis=1).reshape(bs, WINDOW, A_KV_HEADS, A_DH)
    vb_s = jnp.concatenate([cache_v[:, ts:], v_new], axis=1).reshape(bs, WINDOW, A_KV_HEADS, A_DH)

    return (y_p.reshape(bp, sp, D_MODEL), y_s.reshape(bs, ts, D_MODEL),
            c_p[None], n_p[None], m_p[None], kb_p[None], vb_p[None],
            c_s[None], n_s[None], m_s[None], kb_s[None], vb_s[None])
```

```python
import functools

import jax
import jax.numpy as jnp
from jax import lax
from jax.experimental import pallas as pl
from jax.experimental.pallas import tpu as pltpu

F32 = jnp.float32
BF16 = jnp.bfloat16

D_MODEL = 1024
D_FF = 2816
DEPTH = 1
M_HEADS = 4
M_DQK = 128
M_DV = 256
A_HEADS = 16
A_KV_HEADS = 4
A_DH = 64
WINDOW = 128
PAST_LEN = 8192
ROPE_THETA = 10000.0
ATTN_SCALE = A_DH ** -0.5
LN_EPS = 1e-5
HEAD_NORM_EPS = 1e-6
ADA_CHUNKS = 9
DEEPNORM_ALPHA = (2.0 * DEPTH) ** 0.25
K_SCALE = M_DQK ** -0.5

LANES = 128
BF16_SUBLANES = 16
VMEM_LIMIT_BYTES = 56 * 1024 * 1024

W_MQ = M_HEADS * M_DQK
W_MV = M_HEADS * M_DV
W_AQ = A_HEADS * A_DH
W_AKV = A_KV_HEADS * A_DH
IN_IF = 2 * W_MQ + 2 * W_MV
IN_AQ = IN_IF + 2 * M_HEADS
IN_AK = IN_AQ + W_AQ
IN_END = IN_AK + 2 * W_AKV + 2 * D_MODEL
A_Q, A_K, A_V, A_O = 0, W_MQ, 2 * W_MQ, 2 * W_MQ + W_MV
R_AK, R_AV, R_GM, R_GA = 0, W_AKV, 2 * W_AKV, 2 * W_AKV + D_MODEL

ROW_TILE = 512
FF_CHUNK = 256
MLSTM_CHUNK = 128
SAMPLE_TQ = BF16_SUBLANES
SWA_PROMPT_BLOCKS = 2
SWA_SAMPLE_SEQS = 4

_NT = (((1,), (1,)), ((), ()))


def _params():
    return pltpu.CompilerParams(vmem_limit_bytes=VMEM_LIMIT_BYTES)


def _resident(shape):
    return pl.BlockSpec(shape, lambda *_: (0,) * len(shape), pipeline_mode=pl.Buffered(1))


def _rows(ref):
    v = ref[...]
    return v.reshape(v.shape[-2], v.shape[-1])


def _layer_norm(y, g, b, eps):
    mu = jnp.mean(y, axis=-1, keepdims=True)
    d = y - mu
    var = jnp.mean(d * d, axis=-1, keepdims=True)
    return d * lax.rsqrt(var + eps) * g + b


def _sigmoid(x):
    return 1.0 / (1.0 + jnp.exp(-x))


def _mod_specs(per_row, tm, tiles_per_batch, chunks):
    if per_row:
        return [pl.BlockSpec((tm, D_MODEL), lambda i, c=c: (i, c)) for c in chunks]
    return [pl.BlockSpec((1, 1, D_MODEL), lambda i, c=c: (i // tiles_per_batch, 0, c)) for c in chunks]


def _ada_kernel(c_ref, w_ref, b_ref, o_ref):
    c = c_ref[...]
    s = (c * _sigmoid(c)).astype(BF16)
    o_ref[...] = jnp.dot(s, w_ref[...].astype(BF16), preferred_element_type=F32) + b_ref[...]


def _ada(c_all, w_ada, b_ada):
    rows = c_all.shape[0]
    n_out = w_ada.shape[1]
    tn = D_MODEL
    return pl.pallas_call(
        _ada_kernel,
        grid=(n_out // tn,),
        in_specs=[pl.BlockSpec((rows, D_MODEL), lambda j: (0, 0)),
                  pl.BlockSpec((D_MODEL, tn), lambda j: (0, j)),
                  pl.BlockSpec((1, tn), lambda j: (0, j))],
        out_specs=pl.BlockSpec((rows, tn), lambda j: (0, j)),
        out_shape=jax.ShapeDtypeStruct((rows, n_out), F32),
        compiler_params=_params(),
        name="ada",
    )(c_all, w_ada, b_ada.reshape(1, n_out))


def _ffn_kernel(x_ref, sh_ref, sc_ref, g_ref, wup_ref, wdn_ref, lg_ref, lb_ref, o_ref, act_ref):
    x = x_ref[...]
    h = (x * (1.0 + _rows(sc_ref)) + _rows(sh_ref)).astype(BF16)
    for c in range(0, D_FF, FF_CHUNK):
        a = jnp.dot(h, wup_ref[:, c:c + FF_CHUNK], preferred_element_type=F32)
        u = jnp.dot(h, wup_ref[:, D_FF + c:D_FF + c + FF_CHUNK], preferred_element_type=F32)
        act_ref[:, c:c + FF_CHUNK] = (a * _sigmoid(a) * u).astype(BF16)
    f = jnp.dot(act_ref[...], wdn_ref[...], preferred_element_type=F32)
    y = DEEPNORM_ALPHA * x + (0.5 * (1.0 + _rows(g_ref))) * f
    o_ref[...] = _layer_norm(y, lg_ref[...], lb_ref[...], LN_EPS)


def _ffn(x, mod, per_row, rows_per_batch, chunks, w_up, w_down, ln_g, ln_b):
    m = x.shape[0]
    tm = ROW_TILE
    return pl.pallas_call(
        _ffn_kernel,
        grid=(m // tm,),
        in_specs=[pl.BlockSpec((tm, D_MODEL), lambda i: (i, 0))]
        + _mod_specs(per_row, tm, rows_per_batch // tm, chunks)
        + [_resident((D_MODEL, 2 * D_FF)), _resident((D_FF, D_MODEL)),
           _resident((1, D_MODEL)), _resident((1, D_MODEL))],
        out_specs=pl.BlockSpec((tm, D_MODEL), lambda i: (i, 0)),
        out_shape=jax.ShapeDtypeStruct((m, D_MODEL), F32),
        scratch_shapes=[pltpu.VMEM((tm, D_FF), BF16)],
        compiler_params=_params(),
        name="ffn",
    )(x, mod, mod, mod, w_up, w_down, ln_g.reshape(1, D_MODEL), ln_b.reshape(1, D_MODEL))


def _proj_kernel(x_ref, sh_ref, sc_ref, wa_ref, wq_ref, wr_ref, wif_ref, bif_ref, cos_ref, sin_ref,
                 q_ref, k_ref, v_ref, o_ref, aq_ref, ak_ref, av_ref, gm_ref, ga_ref, gt_ref):
    x = x_ref[...]
    tm = x.shape[0]
    h = (x * (1.0 + _rows(sc_ref)) + _rows(sh_ref)).astype(BF16)

    def seg(w_ref, lo, width=256):
        return jnp.dot(h, w_ref[:, lo:lo + width], preferred_element_type=F32)

    for c in range(0, W_MQ, 256):
        q_ref[:, c:c + 256] = seg(wa_ref, A_Q + c).astype(BF16)
        k_ref[:, c:c + 256] = (seg(wa_ref, A_K + c) * K_SCALE).astype(BF16)
    for c in range(0, D_MODEL, 256):
        v_ref[:, c:c + 256] = seg(wa_ref, A_V + c).astype(BF16)
        o_ref[:, c:c + 256] = seg(wa_ref, A_O + c)
        gm_ref[:, c:c + 256] = seg(wr_ref, R_GM + c)
        ga_ref[:, c:c + 256] = seg(wr_ref, R_GA + c)

    cos = cos_ref[...]
    sin = sin_ref[...]
    lane = lax.broadcasted_iota(jnp.int32, (tm, LANES), 1)
    low_half = (lane & (A_DH // 2)) == 0

    def rope(z):
        partner = jnp.where(low_half, pltpu.roll(z, LANES - A_DH // 2, 1), pltpu.roll(z, A_DH // 2, 1))
        return z * cos + partner * sin

    for c in range(0, W_AQ, 256):
        z = seg(wq_ref, c)
        aq_ref[:, c:c + LANES] = rope(z[:, :LANES]).astype(BF16)
        aq_ref[:, c + LANES:c + 256] = rope(z[:, LANES:]).astype(BF16)
    z = seg(wr_ref, R_AK)
    ak_ref[:, :LANES] = rope(z[:, :LANES])
    ak_ref[:, LANES:] = rope(z[:, LANES:])
    av_ref[...] = seg(wr_ref, R_AV)

    zg = jnp.dot(h, wif_ref[...], preferred_element_type=F32) + bif_ref[...]
    logsig = jnp.minimum(zg, 0.0) - jnp.log(1.0 + jnp.exp(-jnp.abs(zg)))
    gt_ref[...] = jnp.where(lane < M_HEADS, zg, logsig)


def _proj(x, mod, per_row, rows_per_batch, chunks, w, cos_t, sin_t):
    m = x.shape[0]
    tm = ROW_TILE
    n_pos_tiles = cos_t.shape[0] // tm

    def tok(width):
        return pl.BlockSpec((tm, width), lambda i: (i, 0))

    widths = (W_MQ, W_MQ, W_MV, W_MV, W_AQ, W_AKV, W_AKV, D_MODEL, D_MODEL, LANES)
    dtypes = (BF16, BF16, BF16, F32, BF16, F32, F32, F32, F32, F32)
    weights = (w["w_a"], w["w_aq"], w["w_r"], w["w_if"], w["bif"])
    return pl.pallas_call(
        _proj_kernel,
        grid=(m // tm,),
        in_specs=[tok(D_MODEL)] + _mod_specs(per_row, tm, rows_per_batch // tm, chunks)
        + [_resident(a.shape) for a in weights]
        + [pl.BlockSpec((tm, LANES), lambda i: (i % n_pos_tiles, 0)),
           pl.BlockSpec((tm, LANES), lambda i: (i % n_pos_tiles, 0))],
        out_specs=[tok(wd) for wd in widths],
        out_shape=[jax.ShapeDtypeStruct((m, wd), d) for wd, d in zip(widths, dtypes)],
        compiler_params=_params(),
        name="proj",
    )(x, mod, mod, *weights, cos_t, sin_t)


def _mlstm_kernel(*refs, nb, tpb, hps, zero_init):
    if zero_init:
        (q_ref, k_ref, v_ref, g_ref, mo_ref, ng_ref,
         out_ref, c_ref, n_ref, m_ref, w_s, e_s, gt_s, ct_s) = refs
    else:
        (q_ref, k_ref, v_ref, g_ref, mo_ref, ng_ref, c0_ref, n0_ref, m0_ref,
         out_ref, c_ref, n_ref, m_ref, w_s, e_s, gt_s, ct_s) = refs
    L = nb * tpb
    shift = tpb.bit_length() - 1
    first_chunk = pl.program_id(2) == 0

    @pl.when(first_chunk)
    def _():
        if zero_init:
            c_ref[...] = jnp.zeros_like(c_ref)
            n_ref[...] = jnp.zeros_like(n_ref)
        else:
            c_ref[...] = c0_ref[...]
            n_ref[...] = n0_ref[...]

    @pl.when(first_chunk & (pl.program_id(1) == 0))
    def _():
        m_ref[...] = jnp.zeros_like(m_ref) if zero_init else m0_ref[...]

    row = lax.broadcasted_iota(jnp.int32, (L, L), 0)
    col = lax.broadcasted_iota(jnp.int32, (L, L), 1)
    same = (row >> shift) == (col >> shift)
    causal = same & (col <= row)
    lane = lax.broadcasted_iota(jnp.int32, (L, LANES), 1)
    row_seq = lax.broadcasted_iota(jnp.int32, (L, 1), 0) >> shift

    def lane_col(x, idx):
        return jnp.sum(jnp.where(lane == idx, x, 0.0), axis=1, keepdims=True)

    gates = g_ref[...]
    cum = jnp.dot(causal.astype(F32), gates, precision=lax.Precision.HIGHEST,
                  preferred_element_type=F32)
    gt_s[...] = gates.T
    ct_s[...] = cum.T
    m_rows = m_ref[...]
    m_next = m_rows

    for hl in range(hps):
        head = hl if hps == M_HEADS else pl.program_id(1) * hps + hl
        qs = slice(hl * M_DQK, (hl + 1) * M_DQK)
        vs = slice(hl * M_DV, (hl + 1) * M_DV)
        q = q_ref[:, qs]
        k = k_ref[:, qs]
        v = v_ref[:, vs]
        b_c = lane_col(cum, M_HEADS + head)
        m_p = lane_col(m_rows, head)
        i_r = gt_s[pl.ds(head, 1), :]
        b_r = ct_s[pl.ds(M_HEADS + head, 1), :]

        log_d = jnp.where(causal, b_c - b_r + i_r, -jnp.inf)
        m_t = jnp.maximum(b_c + m_p, jnp.max(log_d, axis=1, keepdims=True))
        dmat = jnp.exp(log_d - m_t)
        e_int = jnp.exp(b_c + m_p - m_t)

        b_last = jnp.min(jnp.where(same, b_r, jnp.inf), axis=1, keepdims=True)
        log_w = jnp.where(same, b_last - b_r + i_r, -jnp.inf)
        m_new = jnp.maximum(b_last + m_p, jnp.max(log_w, axis=1, keepdims=True))
        w_mat = jnp.exp(log_w - m_new)
        e_c = jnp.exp(b_last + m_p - m_new)
        w_s[...] = w_mat
        e_s[...] = jnp.broadcast_to(e_c, (L, LANES))

        s = lax.dot_general(q, k, _NT, preferred_element_type=F32) * dmat
        v_t = v.astype(F32).T

        def seq_step(j, acc, hl=hl, q=q, k=k, v_t=v_t):
            c_j = c_ref[j, hl]
            q_j = jnp.where(row_seq == j, q, jnp.zeros_like(q))
            acc = acc + lax.dot_general(q_j, c_j.astype(BF16), _NT, preferred_element_type=F32)
            w_row = w_s[pl.ds(j * tpb, 1), :]
            lhs = (v_t * w_row).astype(BF16)
            c_ref[j, hl] = (e_s[pl.ds(j * tpb, 1), :] * c_j
                            + jnp.dot(lhs, k, preferred_element_type=F32))
            return acc

        zero = jnp.zeros((L, M_DV), F32)
        inter = seq_step(0, zero) if nb == 1 else lax.fori_loop(0, nb, seq_step, zero)

        n_rows = n_ref[:, qs]
        qn = jnp.sum(q.astype(F32) * n_rows, axis=1, keepdims=True)
        num = jnp.dot(s.astype(BF16), v, preferred_element_type=F32) + e_int * inter
        den = jnp.sum(s, axis=1, keepdims=True) + e_int * qn
        hh = num / jnp.maximum(jnp.abs(den), jnp.exp(-m_t))

        mu = jnp.mean(hh, axis=1, keepdims=True)
        dlt = hh - mu
        var = jnp.mean(dlt * dlt, axis=1, keepdims=True)
        y = dlt * lax.rsqrt(var + HEAD_NORM_EPS)
        out_ref[:, vs] = (y * ng_ref[:, vs] * _sigmoid(mo_ref[:, vs])).astype(BF16)

        n_ref[:, qs] = e_c * n_rows + jnp.dot(w_mat.astype(BF16), k, preferred_element_type=F32)
        m_next = jnp.where(lane == head, m_new, m_next)

    m_ref[...] = m_next


def _mlstm(q, k, v, gates, mo, norm_g, state, nb, tpb, n_chunks, hps):
    m = q.shape[0]
    L = nb * tpb
    n_blocks = m // (L * n_chunks)
    hd = M_HEADS * M_DQK

    def tok(width):
        return pl.BlockSpec((L, width), lambda b, g, c: (b * n_chunks + c, g))

    c_spec = pl.BlockSpec((nb, hps, M_DV, M_DQK), lambda b, g, c: (b, g, 0, 0))
    n_spec = pl.BlockSpec((L, hps * M_DQK), lambda b, g, c: (b, g))
    m_spec = pl.BlockSpec((L, LANES), lambda b, g, c: (b, 0))
    in_specs = [tok(hps * M_DQK), tok(hps * M_DQK), tok(hps * M_DV),
                pl.BlockSpec((L, LANES), lambda b, g, c: (b * n_chunks + c, 0)),
                tok(hps * M_DV), pl.BlockSpec((1, hps * M_DV), lambda b, g, c: (0, g))]
    args = [q, k, v, gates, mo, norm_g.reshape(1, M_HEADS * M_DV)]
    if state is not None:
        in_specs += [c_spec, n_spec, m_spec]
        args += list(state)
    return pl.pallas_call(
        functools.partial(_mlstm_kernel, nb=nb, tpb=tpb, hps=hps, zero_init=state is None),
        grid=(n_blocks, M_HEADS // hps, n_chunks),
        in_specs=in_specs,
        out_specs=[tok(hps * M_DV), c_spec, n_spec, m_spec],
        out_shape=[jax.ShapeDtypeStruct((m, M_HEADS * M_DV), BF16),
                   jax.ShapeDtypeStruct((n_blocks * nb, M_HEADS, M_DV, M_DQK), F32),
                   jax.ShapeDtypeStruct((n_blocks * L, hd), F32),
                   jax.ShapeDtypeStruct((n_blocks * L, LANES), F32)],
        scratch_shapes=[pltpu.VMEM((L, L), F32), pltpu.VMEM((L, LANES), F32),
                        pltpu.VMEM((LANES, L), F32), pltpu.VMEM((LANES, L), F32)],
        compiler_params=_params(),
        name="mlstm",
    )(*args)


def _swa_kernel(*refs, nbb, tq, chained, n_new):
    if n_new:
        sink_ref, q_ref, kp_ref, kc_ref, vp_ref, vc_ref, o_ref, kn_ref, vn_ref = refs
    else:
        sink_ref, q_ref, kp_ref, kc_ref, vp_ref, vc_ref, o_ref = refs
    reps = A_HEADS // A_KV_HEADS
    low_q = lax.broadcasted_iota(jnp.int32, (tq, LANES), 1) < A_DH
    qi = lax.broadcasted_iota(jnp.int32, (tq, WINDOW), 0)
    kj = lax.broadcasted_iota(jnp.int32, (tq, WINDOW), 1)
    sink_lane = kj == 0
    valid_cur = kj <= qi
    key_row = lax.broadcasted_iota(jnp.int32, (WINDOW, LANES), 0)
    scale = jnp.asarray(ATTN_SCALE, BF16)
    neg_inf = -jnp.inf

    def pad_keys(x):
        if x.shape[0] == WINDOW:
            return x
        return jnp.concatenate([x, jnp.zeros((WINDOW - x.shape[0], x.shape[1]), x.dtype)], axis=0)

    for jb in range(nbb):
        q = q_ref[jb]
        kc, vc = pad_keys(kc_ref[jb]), pad_keys(vc_ref[jb])
        valid_prev = kj > qi
        if chained and jb > 0:
            kp, vp = kc_ref[jb - 1], vc_ref[jb - 1]
        else:
            kp, vp = kp_ref[jb], vp_ref[jb]
            if chained:
                valid_prev = valid_prev & (pl.program_id(1) > 0)

        if n_new:
            keep_old = lax.broadcasted_iota(jnp.int32, (WINDOW, W_AKV), 0) < WINDOW - n_new
            kn_ref[jb] = jnp.where(keep_old, pltpu.roll(kp, WINDOW - n_new, 0),
                                   pltpu.roll(kc, WINDOW - n_new, 0))
            vn_ref[jb] = jnp.where(keep_old, pltpu.roll(vp, WINDOW - n_new, 0),
                                   pltpu.roll(vc, WINDOW - n_new, 0))

        for p in range(A_KV_HEADS // 2):
            ks = slice(p * LANES, (p + 1) * LANES)
            k_prev, k_cur = kp[:, ks].astype(BF16), kc[:, ks].astype(BF16)
            v_prev = jnp.where(key_row == 0, 0.0, vp[:, ks]).astype(BF16)
            v_cur = vc[:, ks].astype(BF16)
            slabs, heads = [], []
            for r in range(reps):
                qb = q[:, (reps * p + r) * LANES:(reps * p + r + 1) * LANES] * scale
                slabs += [jnp.where(low_q, qb, jnp.zeros_like(qb)), jnp.where(low_q, jnp.zeros_like(qb), qb)]
                heads += [2 * reps * p + r, 2 * reps * p + reps + r]
            q8 = jnp.concatenate(slabs, axis=0)
            sp = lax.dot_general(q8, k_prev, _NT, preferred_element_type=F32)
            sc = lax.dot_general(q8, k_cur, _NT, preferred_element_type=F32)
            pps, pcs, invs = [], [], []
            for i, head in enumerate(heads):
                rows = slice(i * tq, (i + 1) * tq)
                sp_i = jnp.where(sink_lane, sink_ref[head], jnp.where(valid_prev, sp[rows], neg_inf))
                sc_i = jnp.where(valid_cur, sc[rows], neg_inf)
                mx = jnp.max(jnp.maximum(sp_i, sc_i), axis=1, keepdims=True)
                pp = jnp.exp(sp_i - mx)
                pc = jnp.exp(sc_i - mx)
                invs.append(1.0 / jnp.sum(pp + pc, axis=1, keepdims=True))
                pps.append(pp.astype(BF16))
                pcs.append(pc.astype(BF16))
            o8 = (jnp.dot(jnp.concatenate(pps, axis=0), v_prev, preferred_element_type=F32)
                  + jnp.dot(jnp.concatenate(pcs, axis=0), v_cur, preferred_element_type=F32))
            for r in range(reps):
                lo = o8[(2 * r) * tq:(2 * r + 1) * tq] * invs[2 * r]
                hi = o8[(2 * r + 1) * tq:(2 * r + 2) * tq] * invs[2 * r + 1]
                blk = reps * p + r
                o_ref[jb, :, blk * LANES:(blk + 1) * LANES] = jnp.where(low_q, lo, hi).astype(BF16)


def _swa(sinks, q, k_prev_src, k_cur, v_prev_src, v_cur, *, nbb, chained, n_new, blocks_per_seq):
    n, tq, _ = q.shape
    tk = k_cur.shape[1]
    if chained:
        steps = blocks_per_seq // nbb
        grid = (n // blocks_per_seq, steps)
        cur_index = lambda b, i: (b * steps + i, 0, 0)
        prev_spec = pl.BlockSpec((1, WINDOW, W_AKV),
                                 lambda b, i: (b * blocks_per_seq + jnp.maximum(i * nbb - 1, 0), 0, 0))
    else:
        grid = (n // nbb,)
        cur_index = lambda i: (i, 0, 0)
        prev_spec = pl.BlockSpec((nbb, WINDOW, W_AKV), cur_index)
    cur_spec = pl.BlockSpec((nbb, tk, W_AKV), cur_index)
    q_spec = pl.BlockSpec((nbb, tq, W_AQ), cur_index)
    out_specs = [q_spec]
    out_shape = [jax.ShapeDtypeStruct((n, tq, W_AQ), BF16)]
    if n_new:
        out_specs += [prev_spec, prev_spec]
        out_shape += [jax.ShapeDtypeStruct((n, WINDOW, W_AKV), F32)] * 2
    return pl.pallas_call(
        functools.partial(_swa_kernel, nbb=nbb, tq=tq, chained=chained, n_new=n_new),
        grid=grid,
        in_specs=[pl.BlockSpec(memory_space=pltpu.SMEM), q_spec, prev_spec, cur_spec, prev_spec, cur_spec],
        out_specs=out_specs,
        out_shape=out_shape,
        compiler_params=_params(),
        name="swa",
    )(sinks, q, k_prev_src, k_cur, v_prev_src, v_cur)


def _merge_kernel(x_ref, g_ref, hm_ref, oa_ref, gm_ref, ga_ref, wm_ref, wa_ref, wo_ref,
                  lg_ref, lb_ref, o_ref):
    ym = jnp.dot(hm_ref[...], wm_ref[...], preferred_element_type=F32)
    ya = jnp.dot(oa_ref[...], wa_ref[...], preferred_element_type=F32)
    mix = _sigmoid(gm_ref[...]) * ym + _sigmoid(ga_ref[...]) * ya
    t = jnp.dot(mix.astype(BF16), wo_ref[...], preferred_element_type=F32)
    y = DEEPNORM_ALPHA * x_ref[...] + (1.0 + _rows(g_ref)) * t
    o_ref[...] = _layer_norm(y, lg_ref[...], lb_ref[...], LN_EPS)


def _merge(x, mod, per_row, rows_per_batch, chunk, hm, oa, gm, ga, wm, wa, wo, ln_g, ln_b):
    m = x.shape[0]
    tm = ROW_TILE

    def tok():
        return pl.BlockSpec((tm, D_MODEL), lambda i: (i, 0))

    return pl.pallas_call(
        _merge_kernel,
        grid=(m // tm,),
        in_specs=[tok()] + _mod_specs(per_row, tm, rows_per_batch // tm, (chunk,))
        + [tok(), tok(), tok(), tok()]
        + [_resident((D_MODEL, D_MODEL))] * 3 + [_resident((1, D_MODEL))] * 2,
        out_specs=tok(),
        out_shape=jax.ShapeDtypeStruct((m, D_MODEL), F32),
        compiler_params=_params(),
        name="merge",
    )(x, mod, hm, oa, gm, ga, wm, wa, wo, ln_g.reshape(1, D_MODEL), ln_b.reshape(1, D_MODEL))


def _rope_tables(pos):
    half = A_DH // 2
    inv = ROPE_THETA ** (-jnp.arange(half, dtype=F32) / half)
    ang = pos.astype(F32)[:, None] * inv[None, :]
    cos, sin = jnp.cos(ang), jnp.sin(ang)
    return jnp.tile(cos, (1, 4)), jnp.concatenate([-sin, sin, -sin, sin], axis=1)


def _pair_heads(x, axis):
    shape = x.shape
    reps = A_HEADS // A_KV_HEADS
    x = x.reshape(shape[:axis] + (A_KV_HEADS // 2, 2, reps, A_DH) + shape[axis + 1:])
    return jnp.swapaxes(x, axis + 1, axis + 2).reshape(shape)


def _token_stage_1(x, mod, per_row, rows_per_batch, w, pos):
    x1 = _ffn(x, mod, per_row, rows_per_batch, (0, 1, 2), w["up1"], w["down1"], w["ln1_g"], w["ln1_b"])
    cos_t, sin_t = _rope_tables(pos)
    return x1, _proj(x1, mod, per_row, rows_per_batch, (3, 4), w, cos_t, sin_t)


def _token_stage_2(x1, mod, per_row, rows_per_batch, w, hm, oa, gm, ga):
    x2 = _merge(x1, mod, per_row, rows_per_batch, 5, hm, oa, gm, ga,
                w["wm"], w["wa"], w["wo"], w["ln2_g"], w["ln2_b"])
    return _ffn(x2, mod, per_row, rows_per_batch, (6, 7, 8), w["up2"], w["down2"], w["ln3_g"], w["ln3_b"])


def kernel(x_prompt, x_sample, state_mlstm_C, state_mlstm_n, state_mlstm_m, cache_swa_k, cache_swa_v, c_prompt, c_sample, w_ada, b_ada, w_ffn1_up, w_ffn1_down, ln1_g, ln1_b, w_in, b_igate, b_fgate, m_norm_g, sinks, w_branch_m, w_branch_a, w_out, ln2_g, ln2_b, w_ffn2_up, w_ffn2_down, ln3_g, ln3_b):
    assert w_ada.shape[0] == DEPTH == 1
    bp, sp, _ = x_prompt.shape
    bs, ts, _ = x_sample.shape

    win = w_in[0]
    w = dict(
        up1=w_ffn1_up[0].astype(BF16), down1=w_ffn1_down[0].astype(BF16),
        up2=w_ffn2_up[0].astype(BF16), down2=w_ffn2_down[0].astype(BF16),
        w_a=win[:, :IN_IF].astype(BF16),
        w_aq=_pair_heads(win[:, IN_AQ:IN_AK], 1).astype(BF16),
        w_r=win[:, IN_AK:IN_END].astype(BF16),
        w_if=jnp.pad(win[:, IN_IF:IN_AQ], ((0, 0), (0, LANES - 2 * M_HEADS))).astype(BF16),
        bif=jnp.concatenate([b_igate[0], b_fgate[0], jnp.zeros((LANES - 2 * M_HEADS,), F32)]).reshape(1, LANES),
        wm=w_branch_m[0].astype(BF16), wa=_pair_heads(w_branch_a[0], 0).astype(BF16),
        wo=w_out[0].astype(BF16),
        ln1_g=ln1_g[0], ln1_b=ln1_b[0], ln2_g=ln2_g[0], ln2_b=ln2_b[0], ln3_g=ln3_g[0], ln3_b=ln3_b[0],
    )

    ms = bs * ts
    c_all = jnp.concatenate([jnp.repeat(c_sample, ts, axis=0), c_prompt], axis=0)
    mod = _ada(c_all, w_ada[0], b_ada[0])
    mod_p = mod[ms:].reshape(bp, 1, ADA_CHUNKS * D_MODEL)

    mp = bp * sp
    x1p, (qm, km, vm, mo, aq, ak, av, gm, ga, gt) = _token_stage_1(
        x_prompt.reshape(mp, D_MODEL), mod_p, False, sp, w, jnp.arange(sp))
    n_chunks = sp // MLSTM_CHUNK
    hm, c_p, n_rows, m_rows = _mlstm(qm, km, vm, gt, mo, m_norm_g[0], None, 1, MLSTM_CHUNK, n_chunks, M_HEADS)
    n_p = n_rows.reshape(bp, MLSTM_CHUNK, M_HEADS, M_DQK)[:, 0]
    m_p = m_rows.reshape(bp, MLSTM_CHUNK, LANES)[:, 0, :M_HEADS]
    nblk = sp // WINDOW
    ak3 = ak.reshape(bp * nblk, WINDOW, W_AKV)
    av3 = av.reshape(bp * nblk, WINDOW, W_AKV)
    (oa,) = _swa(sinks[0], aq.reshape(bp * nblk, WINDOW, W_AQ), ak3, ak3, av3, av3,
                 nbb=SWA_PROMPT_BLOCKS, chained=True, n_new=0, blocks_per_seq=nblk)
    y_p = _token_stage_2(x1p, mod_p, False, sp, w, hm, oa.reshape(mp, D_MODEL), gm, ga)
    kb_p = ak.reshape(bp, sp, A_KV_HEADS, A_DH)[:, sp - WINDOW:]
    vb_p = av.reshape(bp, sp, A_KV_HEADS, A_DH)[:, sp - WINDOW:]

    x1s, (qm, km, vm, mo, aq, ak, av, gm, ga, gt) = _token_stage_1(
        x_sample.reshape(ms, D_MODEL), mod, True, ms, w, PAST_LEN + jnp.arange(ms) % ts)
    seqs = MLSTM_CHUNK // ts
    n0_rows = jnp.repeat(state_mlstm_n[0].reshape(bs, M_HEADS * M_DQK), ts, axis=0)
    m0_rows = jnp.repeat(jnp.pad(state_mlstm_m[0], ((0, 0), (0, LANES - M_HEADS))), ts, axis=0)
    hm, c_s, n_rows, m_rows = _mlstm(qm, km, vm, gt, mo, m_norm_g[0],
                                     (state_mlstm_C[0], n0_rows, m0_rows), seqs, ts, 1, 1)
    n_s = n_rows.reshape(bs, ts, M_HEADS, M_DQK)[:, 0]
    m_s = m_rows.reshape(bs, ts, LANES)[:, 0, :M_HEADS]
    pad_t = ((0, 0), (0, SAMPLE_TQ - ts), (0, 0))
    oa, kb_s, vb_s = _swa(
        sinks[0], jnp.pad(aq.reshape(bs, ts, W_AQ), pad_t),
        cache_swa_k[0].reshape(bs, WINDOW, W_AKV), jnp.pad(ak.reshape(bs, ts, W_AKV), pad_t),
        cache_swa_v[0].reshape(bs, WINDOW, W_AKV), jnp.pad(av.reshape(bs, ts, W_AKV), pad_t),
        nbb=SWA_SAMPLE_SEQS, chained=False, n_new=ts, blocks_per_seq=1)
    y_s = _token_stage_2(x1s, mod, True, ms, w, hm, oa[:, :ts].reshape(ms, D_MODEL), gm, ga)
    kb_s = kb_s.reshape(bs, WINDOW, A_KV_HEADS, A_DH)
    vb_s = vb_s.reshape(bs, WINDOW, A_KV_HEADS, A_DH)

    return (y_p.reshape(bp, sp, D_MODEL), y_s.reshape(bs, ts, D_MODEL),
            c_p[None], n_p[None], m_p[None], kb_p[None], vb_p[None],
            c_s[None], n_s[None], m_s[None], kb_s[None], vb_s[None])
```

```python
import functools

import jax
import jax.numpy as jnp
from jax import lax
from jax.experimental import pallas as pl
from jax.experimental.pallas import tpu as pltpu

F32 = jnp.float32
BF16 = jnp.bfloat16

D_MODEL = 1024
D_FF = 2816
DEPTH = 1
M_HEADS = 4
M_DQK = 128
M_DV = 256
A_HEADS = 16
A_KV_HEADS = 4
A_DH = 64
WINDOW = 128
PAST_LEN = 8192
ROPE_THETA = 10000.0
ATTN_SCALE = A_DH ** -0.5
LN_EPS = 1e-5
HEAD_NORM_EPS = 1e-6
ADA_CHUNKS = 9
DEEPNORM_ALPHA = (2.0 * DEPTH) ** 0.25
K_SCALE = M_DQK ** -0.5

LANES = 128
BF16_SUBLANES = 16
VMEM_LIMIT_BYTES = 56 * 1024 * 1024

W_MQ = M_HEADS * M_DQK
W_MV = M_HEADS * M_DV
W_AQ = A_HEADS * A_DH
W_AKV = A_KV_HEADS * A_DH
IN_IF = 2 * W_MQ + 2 * W_MV
IN_AQ = IN_IF + 2 * M_HEADS
IN_AK = IN_AQ + W_AQ
IN_END = IN_AK + 2 * W_AKV + 2 * D_MODEL
A_Q, A_K, A_V, A_O = 0, W_MQ, 2 * W_MQ, 2 * W_MQ + W_MV
R_AK, R_AV, R_GM, R_GA = 0, W_AKV, 2 * W_AKV, 2 * W_AKV + D_MODEL

ROW_TILE = 512
FF_CHUNK = 256
MLSTM_CHUNK = 128
SAMPLE_TQ = BF16_SUBLANES
SWA_PROMPT_BLOCKS = 2
SWA_SAMPLE_SEQS = 4

_NT = (((1,), (1,)), ((), ()))


def _params():
    return pltpu.CompilerParams(vmem_limit_bytes=VMEM_LIMIT_BYTES)


def _resident(shape):
    return pl.BlockSpec(shape, lambda *_: (0,) * len(shape), pipeline_mode=pl.Buffered(1))


def _rows(ref):
    v = ref[...]
    return v.reshape(v.shape[-2], v.shape[-1])


def _layer_norm(y, g, b, eps):
    mu = jnp.mean(y, axis=-1, keepdims=True)
    d = y - mu
    var = jnp.mean(d * d, axis=-1, keepdims=True)
    return d * lax.rsqrt(var + eps) * g + b


def _sigmoid(x):
    return 1.0 / (1.0 + jnp.exp(-x))


def _mod_specs(per_row, tm, tiles_per_batch, chunks):
    if per_row:
        return [pl.BlockSpec((tm, D_MODEL), lambda i, c=c: (i, c)) for c in chunks]
    return [pl.BlockSpec((1, 1, D_MODEL), lambda i, c=c: (i // tiles_per_batch, 0, c)) for c in chunks]


def _ada_kernel(c_ref, w_ref, b_ref, o_ref):
    c = c_ref[...]
    s = (c * _sigmoid(c)).astype(BF16)
    o_ref[...] = jnp.dot(s, w_ref[...].astype(BF16), preferred_element_type=F32) + b_ref[...]


def _ada(c_all, w_ada, b_ada):
    rows = c_all.shape[0]
    n_out = w_ada.shape[1]
    tn = D_MODEL
    return pl.pallas_call(
        _ada_kernel,
        grid=(n_out // tn,),
        in_specs=[pl.BlockSpec((rows, D_MODEL), lambda j: (0, 0)),
                  pl.BlockSpec((D_MODEL, tn), lambda j: (0, j)),
                  pl.BlockSpec((1, tn), lambda j: (0, j))],
        out_specs=pl.BlockSpec((rows, tn), lambda j: (0, j)),
        out_shape=jax.ShapeDtypeStruct((rows, n_out), F32),
        compiler_params=_params(),
        name="ada",
    )(c_all, w_ada, b_ada.reshape(1, n_out))


def _ffn_kernel(x_ref, sh_ref, sc_ref, g_ref, wup_ref, wdn_ref, lg_ref, lb_ref, o_ref, act_ref):
    x = x_ref[...]
    h = (x * (1.0 + _rows(sc_ref)) + _rows(sh_ref)).astype(BF16)
    for c in range(0, D_FF, FF_CHUNK):
        a = jnp.dot(h, wup_ref[:, c:c + FF_CHUNK], preferred_element_type=F32)
        u = jnp.dot(h, wup_ref[:, D_FF + c:D_FF + c + FF_CHUNK], preferred_element_type=F32)
        act_ref[:, c:c + FF_CHUNK] = (a * _sigmoid(a) * u).astype(BF16)
    f = jnp.dot(act_ref[...], wdn_ref[...], preferred_element_type=F32)
    y = DEEPNORM_ALPHA * x + (0.5 * (1.0 + _rows(g_ref))) * f
    o_ref[...] = _layer_norm(y, lg_ref[...], lb_ref[...], LN_EPS)


def _ffn(x, mod, per_row, rows_per_batch, chunks, w_up, w_down, ln_g, ln_b):
    m = x.shape[0]
    tm = ROW_TILE
    return pl.pallas_call(
        _ffn_kernel,
        grid=(m // tm,),
        in_specs=[pl.BlockSpec((tm, D_MODEL), lambda i: (i, 0))]
        + _mod_specs(per_row, tm, rows_per_batch // tm, chunks)
        + [_resident((D_MODEL, 2 * D_FF)), _resident((D_FF, D_MODEL)),
           _resident((1, D_MODEL)), _resident((1, D_MODEL))],
        out_specs=pl.BlockSpec((tm, D_MODEL), lambda i: (i, 0)),
        out_shape=jax.ShapeDtypeStruct((m, D_MODEL), F32),
        scratch_shapes=[pltpu.VMEM((tm, D_FF), BF16)],
        compiler_params=_params(),
        name="ffn",
    )(x, mod, mod, mod, w_up, w_down, ln_g.reshape(1, D_MODEL), ln_b.reshape(1, D_MODEL))


def _proj_kernel(x_ref, sh_ref, sc_ref, wa_ref, wq_ref, wr_ref, wif_ref, bif_ref, cos_ref, sin_ref,
                 q_ref, k_ref, v_ref, o_ref, aq_ref, ak_ref, av_ref, gm_ref, ga_ref, gt_ref):
    x = x_ref[...]
    tm = x.shape[0]
    h = (x * (1.0 + _rows(sc_ref)) + _rows(sh_ref)).astype(BF16)

    def seg(w_ref, lo, width=256):
        return jnp.dot(h, w_ref[:, lo:lo + width], preferred_element_type=F32)

    for c in range(0, W_MQ, 256):
        q_ref[:, c:c + 256] = seg(wa_ref, A_Q + c).astype(BF16)
        k_ref[:, c:c + 256] = (seg(wa_ref, A_K + c) * K_SCALE).astype(BF16)
    for c in range(0, D_MODEL, 256):
        v_ref[:, c:c + 256] = seg(wa_ref, A_V + c).astype(BF16)
        o_ref[:, c:c + 256] = seg(wa_ref, A_O + c)
        gm_ref[:, c:c + 256] = seg(wr_ref, R_GM + c)
        ga_ref[:, c:c + 256] = seg(wr_ref, R_GA + c)

    cos = cos_ref[...]
    sin = sin_ref[...]
    lane = lax.broadcasted_iota(jnp.int32, (tm, LANES), 1)
    low_half = (lane & (A_DH // 2)) == 0

    def rope(z):
        partner = jnp.where(low_half, pltpu.roll(z, LANES - A_DH // 2, 1), pltpu.roll(z, A_DH // 2, 1))
        return z * cos + partner * sin

    for c in range(0, W_AQ, 256):
        z = seg(wq_ref, c)
        aq_ref[:, c:c + LANES] = rope(z[:, :LANES]).astype(BF16)
        aq_ref[:, c + LANES:c + 256] = rope(z[:, LANES:]).astype(BF16)
    z = seg(wr_ref, R_AK)
    ak_ref[:, :LANES] = rope(z[:, :LANES])
    ak_ref[:, LANES:] = rope(z[:, LANES:])
    av_ref[...] = seg(wr_ref, R_AV)

    zg = jnp.dot(h, wif_ref[...], preferred_element_type=F32) + bif_ref[...]
    logsig = jnp.minimum(zg, 0.0) - jnp.log(1.0 + jnp.exp(-jnp.abs(zg)))
    gt_ref[...] = jnp.where(lane < M_HEADS, zg, logsig)


def _proj(x, mod, per_row, rows_per_batch, chunks, w, cos_t, sin_t):
    m = x.shape[0]
    tm = ROW_TILE
    n_pos_tiles = cos_t.shape[0] // tm

    def tok(width):
        return pl.BlockSpec((tm, width), lambda i: (i, 0))

    widths = (W_MQ, W_MQ, W_MV, W_MV, W_AQ, W_AKV, W_AKV, D_MODEL, D_MODEL, LANES)
    dtypes = (BF16, BF16, BF16, F32, BF16, F32, F32, F32, F32, F32)
    weights = (w["w_a"], w["w_aq"], w["w_r"], w["w_if"], w["bif"])
    return pl.pallas_call(
        _proj_kernel,
        grid=(m // tm,),
        in_specs=[tok(D_MODEL)] + _mod_specs(per_row, tm, rows_per_batch // tm, chunks)
        + [_resident(a.shape) for a in weights]
        + [pl.BlockSpec((tm, LANES), lambda i: (i % n_pos_tiles, 0)),
           pl.BlockSpec((tm, LANES), lambda i: (i % n_pos_tiles, 0))],
        out_specs=[tok(wd) for wd in widths],
        out_shape=[jax.ShapeDtypeStruct((m, wd), d) for wd, d in zip(widths, dtypes)],
        compiler_params=_params(),
        name="proj",
    )(x, mod, mod, *weights, cos_t, sin_t)


def _mlstm_kernel(*refs, nb, tpb, hps, zero_init):
    if zero_init:
        (q_ref, k_ref, v_ref, g_ref, mo_ref, ng_ref,
         out_ref, c_ref, n_ref, m_ref, gt_s, ct_s) = refs
    else:
        (q_ref, k_ref, v_ref, g_ref, mo_ref, ng_ref, c0_ref, n0_ref, m0_ref,
         out_ref, c_ref, n_ref, m_ref, gt_s, ct_s) = refs
    L = nb * tpb
    shift = tpb.bit_length() - 1
    first_chunk = pl.program_id(2) == 0

    @pl.when(first_chunk)
    def _():
        if zero_init:
            c_ref[...] = jnp.zeros_like(c_ref)
            n_ref[...] = jnp.zeros_like(n_ref)
        else:
            c_ref[...] = c0_ref[...]
            n_ref[...] = n0_ref[...]

    @pl.when(first_chunk & (pl.program_id(1) == 0))
    def _():
        m_ref[...] = jnp.zeros_like(m_ref) if zero_init else m0_ref[...]

    row = lax.broadcasted_iota(jnp.int32, (L, L), 0)
    col = lax.broadcasted_iota(jnp.int32, (L, L), 1)
    same = (row >> shift) == (col >> shift)
    causal = same & (col <= row)
    lane = lax.broadcasted_iota(jnp.int32, (L, LANES), 1)
    row_seq = lax.broadcasted_iota(jnp.int32, (L, 1), 0) >> shift

    def lane_col(x, idx):
        return jnp.sum(jnp.where(lane == idx, x, 0.0), axis=1, keepdims=True)

    gates = g_ref[...]
    tri = jnp.where(causal, 1.0, 0.0).astype(BF16)
    g_hi = gates.astype(BF16)
    rem = gates - g_hi.astype(F32)
    g_mid = rem.astype(BF16)
    g_lo = (rem - g_mid.astype(F32)).astype(BF16)
    cum = (jnp.dot(tri, g_hi, preferred_element_type=F32)
           + jnp.dot(tri, g_mid, preferred_element_type=F32)
           + jnp.dot(tri, g_lo, preferred_element_type=F32))
    gt_s[...] = gates.T
    ct_s[...] = cum.T
    m_rows = m_ref[...]
    m_next = m_rows

    for hl in range(hps):
        head = hl if hps == M_HEADS else pl.program_id(1) * hps + hl
        qs = slice(hl * M_DQK, (hl + 1) * M_DQK)
        vs = slice(hl * M_DV, (hl + 1) * M_DV)
        q = q_ref[:, qs]
        k = k_ref[:, qs]
        v = v_ref[:, vs]
        b_c = lane_col(cum, M_HEADS + head)
        m_p = lane_col(m_rows, head)
        i_r = gt_s[pl.ds(head, 1), :]
        b_r = ct_s[pl.ds(M_HEADS + head, 1), :]

        log_d = jnp.where(causal, b_c - b_r + i_r, -jnp.inf)
        m_t = jnp.maximum(b_c + m_p, jnp.max(log_d, axis=1, keepdims=True))
        dmat = jnp.exp(log_d - m_t)
        e_int = jnp.exp(b_c + m_p - m_t)

        if nb == 1:
            last = slice(L - 1, L)
            m_new = jnp.broadcast_to(m_t[last], (L, 1))
            e_c = jnp.broadcast_to(e_int[last], (L, 1))
            w_mat = jnp.broadcast_to(dmat[last], (L, L))
        else:
            b_last = jnp.min(jnp.where(same, b_r, jnp.inf), axis=1, keepdims=True)
            log_w = jnp.where(same, b_last - b_r + i_r, -jnp.inf)
            m_new = jnp.maximum(b_last + m_p, jnp.max(log_w, axis=1, keepdims=True))
            w_mat = jnp.exp(log_w - m_new)
            e_c = jnp.exp(b_last + m_p - m_new)
        e_cb = jnp.broadcast_to(e_c, (L, LANES))

        s = lax.dot_general(q, k, _NT, preferred_element_type=F32) * dmat
        v_t = v.astype(F32).T

        inter = jnp.zeros((L, M_DV), F32)
        for j in range(nb):
            first = slice(j * tpb, j * tpb + 1)
            c_j = c_ref[j, hl]
            q_j = q if nb == 1 else jnp.where(row_seq == j, q, jnp.zeros_like(q))
            inter = inter + lax.dot_general(q_j, c_j.astype(BF16), _NT, preferred_element_type=F32)
            lhs = (v_t * w_mat[first]).astype(BF16)
            c_ref[j, hl] = e_cb[first] * c_j + jnp.dot(lhs, k, preferred_element_type=F32)

        n_rows = n_ref[:, qs]
        qn = jnp.sum(q.astype(F32) * n_rows, axis=1, keepdims=True)
        num = jnp.dot(s.astype(BF16), v, preferred_element_type=F32) + e_int * inter
        den = jnp.sum(s, axis=1, keepdims=True) + e_int * qn
        hh = num / jnp.maximum(jnp.abs(den), jnp.exp(-m_t))

        mu = jnp.mean(hh, axis=1, keepdims=True)
        dlt = hh - mu
        var = jnp.mean(dlt * dlt, axis=1, keepdims=True)
        y = dlt * lax.rsqrt(var + HEAD_NORM_EPS)
        out_ref[:, vs] = (y * ng_ref[:, vs] * _sigmoid(mo_ref[:, vs])).astype(BF16)

        n_ref[:, qs] = e_c * n_rows + jnp.dot(w_mat.astype(BF16), k, preferred_element_type=F32)
        m_next = jnp.where(lane == head, m_new, m_next)

    m_ref[...] = m_next


def _mlstm(q, k, v, gates, mo, norm_g, state, nb, tpb, n_chunks, hps):
    m = q.shape[0]
    L = nb * tpb
    n_blocks = m // (L * n_chunks)
    hd = M_HEADS * M_DQK

    def tok(width):
        return pl.BlockSpec((L, width), lambda b, g, c: (b * n_chunks + c, g))

    c_spec = pl.BlockSpec((nb, hps, M_DV, M_DQK), lambda b, g, c: (b, g, 0, 0))
    n_spec = pl.BlockSpec((L, hps * M_DQK), lambda b, g, c: (b, g))
    m_spec = pl.BlockSpec((L, LANES), lambda b, g, c: (b, 0))
    in_specs = [tok(hps * M_DQK), tok(hps * M_DQK), tok(hps * M_DV),
                pl.BlockSpec((L, LANES), lambda b, g, c: (b * n_chunks + c, 0)),
                tok(hps * M_DV), pl.BlockSpec((1, hps * M_DV), lambda b, g, c: (0, g))]
    args = [q, k, v, gates, mo, norm_g.reshape(1, M_HEADS * M_DV)]
    if state is not None:
        in_specs += [c_spec, n_spec, m_spec]
        args += list(state)
    return pl.pallas_call(
        functools.partial(_mlstm_kernel, nb=nb, tpb=tpb, hps=hps, zero_init=state is None),
        grid=(n_blocks, M_HEADS // hps, n_chunks),
        in_specs=in_specs,
        out_specs=[tok(hps * M_DV), c_spec, n_spec, m_spec],
        out_shape=[jax.ShapeDtypeStruct((m, M_HEADS * M_DV), BF16),
                   jax.ShapeDtypeStruct((n_blocks * nb, M_HEADS, M_DV, M_DQK), F32),
                   jax.ShapeDtypeStruct((n_blocks * L, hd), F32),
                   jax.ShapeDtypeStruct((n_blocks * L, LANES), F32)],
        scratch_shapes=[pltpu.VMEM((LANES, L), F32), pltpu.VMEM((LANES, L), F32)],
        compiler_params=_params(),
        name="mlstm",
    )(*args)


def _swa_kernel(*refs, nbb, tq, chained, n_new):
    if n_new:
        sink_ref, q_ref, kp_ref, kc_ref, vp_ref, vc_ref, o_ref, kn_ref, vn_ref = refs
    else:
        sink_ref, q_ref, kp_ref, kc_ref, vp_ref, vc_ref, o_ref = refs
    reps = A_HEADS // A_KV_HEADS
    low_q = lax.broadcasted_iota(jnp.int32, (tq, LANES), 1) < A_DH
    qi = lax.broadcasted_iota(jnp.int32, (tq, WINDOW), 0)
    kj = lax.broadcasted_iota(jnp.int32, (tq, WINDOW), 1)
    sink_lane = kj == 0
    valid_cur = kj <= qi
    key_row = lax.broadcasted_iota(jnp.int32, (WINDOW, LANES), 0)
    scale = jnp.asarray(ATTN_SCALE, BF16)
    neg_inf = -jnp.inf

    def pad_keys(x):
        if x.shape[0] == WINDOW:
            return x
        return jnp.concatenate([x, jnp.zeros((WINDOW - x.shape[0], x.shape[1]), x.dtype)], axis=0)

    for jb in range(nbb):
        q = q_ref[jb]
        kc, vc = pad_keys(kc_ref[jb]), pad_keys(vc_ref[jb])
        valid_prev = kj > qi
        if chained and jb > 0:
            kp, vp = kc_ref[jb - 1], vc_ref[jb - 1]
        else:
            kp, vp = kp_ref[jb], vp_ref[jb]
            if chained:
                valid_prev = valid_prev & (pl.program_id(1) > 0)

        if n_new:
            keep_old = lax.broadcasted_iota(jnp.int32, (WINDOW, W_AKV), 0) < WINDOW - n_new
            kn_ref[jb] = jnp.where(keep_old, pltpu.roll(kp, WINDOW - n_new, 0),
                                   pltpu.roll(kc, WINDOW - n_new, 0))
            vn_ref[jb] = jnp.where(keep_old, pltpu.roll(vp, WINDOW - n_new, 0),
                                   pltpu.roll(vc, WINDOW - n_new, 0))

        for p in range(A_KV_HEADS // 2):
            ks = slice(p * LANES, (p + 1) * LANES)
            k_prev, k_cur = kp[:, ks].astype(BF16), kc[:, ks].astype(BF16)
            v_prev = jnp.where(key_row == 0, 0.0, vp[:, ks]).astype(BF16)
            v_cur = vc[:, ks].astype(BF16)
            slabs, heads = [], []
            for r in range(reps):
                qb = q[:, (reps * p + r) * LANES:(reps * p + r + 1) * LANES] * scale
                slabs += [jnp.where(low_q, qb, jnp.zeros_like(qb)), jnp.where(low_q, jnp.zeros_like(qb), qb)]
                heads += [2 * reps * p + r, 2 * reps * p + reps + r]
            q8 = jnp.concatenate(slabs, axis=0)
            sp = lax.dot_general(q8, k_prev, _NT, preferred_element_type=F32)
            sc = lax.dot_general(q8, k_cur, _NT, preferred_element_type=F32)
            pps, pcs, invs = [], [], []
            for i, head in enumerate(heads):
                rows = slice(i * tq, (i + 1) * tq)
                sp_i = jnp.where(sink_lane, sink_ref[head], jnp.where(valid_prev, sp[rows], neg_inf))
                sc_i = jnp.where(valid_cur, sc[rows], neg_inf)
                mx = jnp.max(jnp.maximum(sp_i, sc_i), axis=1, keepdims=True)
                pp = jnp.exp(sp_i - mx)
                pc = jnp.exp(sc_i - mx)
                invs.append(1.0 / jnp.sum(pp + pc, axis=1, keepdims=True))
                pps.append(pp.astype(BF16))
                pcs.append(pc.astype(BF16))
            o8 = (jnp.dot(jnp.concatenate(pps, axis=0), v_prev, preferred_element_type=F32)
                  + jnp.dot(jnp.concatenate(pcs, axis=0), v_cur, preferred_element_type=F32))
            for r in range(reps):
                lo = o8[(2 * r) * tq:(2 * r + 1) * tq] * invs[2 * r]
                hi = o8[(2 * r + 1) * tq:(2 * r + 2) * tq] * invs[2 * r + 1]
                blk = reps * p + r
                o_ref[jb, :, blk * LANES:(blk + 1) * LANES] = jnp.where(low_q, lo, hi).astype(BF16)


def _swa(sinks, q, k_prev_src, k_cur, v_prev_src, v_cur, *, nbb, chained, n_new, blocks_per_seq):
    n, tq, _ = q.shape
    tk = k_cur.shape[1]
    if chained:
        steps = blocks_per_seq // nbb
        grid = (n // blocks_per_seq, steps)
        cur_index = lambda b, i: (b * steps + i, 0, 0)
        prev_spec = pl.BlockSpec((1, WINDOW, W_AKV),
                                 lambda b, i: (b * blocks_per_seq + jnp.maximum(i * nbb - 1, 0), 0, 0))
    else:
        grid = (n // nbb,)
        cur_index = lambda i: (i, 0, 0)
        prev_spec = pl.BlockSpec((nbb, WINDOW, W_AKV), cur_index)
    cur_spec = pl.BlockSpec((nbb, tk, W_AKV), cur_index)
    q_spec = pl.BlockSpec((nbb, tq, W_AQ), cur_index)
    out_specs = [q_spec]
    out_shape = [jax.ShapeDtypeStruct((n, tq, W_AQ), BF16)]
    if n_new:
        out_specs += [prev_spec, prev_spec]
        out_shape += [jax.ShapeDtypeStruct((n, WINDOW, W_AKV), F32)] * 2
    return pl.pallas_call(
        functools.partial(_swa_kernel, nbb=nbb, tq=tq, chained=chained, n_new=n_new),
        grid=grid,
        in_specs=[pl.BlockSpec(memory_space=pltpu.SMEM), q_spec, prev_spec, cur_spec, prev_spec, cur_spec],
        out_specs=out_specs,
        out_shape=out_shape,
        compiler_params=_params(),
        name="swa",
    )(sinks, q, k_prev_src, k_cur, v_prev_src, v_cur)


def _merge_kernel(x_ref, g_ref, hm_ref, oa_ref, gm_ref, ga_ref, wm_ref, wa_ref, wo_ref,
                  lg_ref, lb_ref, o_ref):
    ym = jnp.dot(hm_ref[...], wm_ref[...], preferred_element_type=F32)
    ya = jnp.dot(oa_ref[...], wa_ref[...], preferred_element_type=F32)
    mix = _sigmoid(gm_ref[...]) * ym + _sigmoid(ga_ref[...]) * ya
    t = jnp.dot(mix.astype(BF16), wo_ref[...], preferred_element_type=F32)
    y = DEEPNORM_ALPHA * x_ref[...] + (1.0 + _rows(g_ref)) * t
    o_ref[...] = _layer_norm(y, lg_ref[...], lb_ref[...], LN_EPS)


def _merge(x, mod, per_row, rows_per_batch, chunk, hm, oa, gm, ga, wm, wa, wo, ln_g, ln_b):
    m = x.shape[0]
    tm = ROW_TILE

    def tok():
        return pl.BlockSpec((tm, D_MODEL), lambda i: (i, 0))

    return pl.pallas_call(
        _merge_kernel,
        grid=(m // tm,),
        in_specs=[tok()] + _mod_specs(per_row, tm, rows_per_batch // tm, (chunk,))
        + [tok(), tok(), tok(), tok()]
        + [_resident((D_MODEL, D_MODEL))] * 3 + [_resident((1, D_MODEL))] * 2,
        out_specs=tok(),
        out_shape=jax.ShapeDtypeStruct((m, D_MODEL), F32),
        compiler_params=_params(),
        name="merge",
    )(x, mod, hm, oa, gm, ga, wm, wa, wo, ln_g.reshape(1, D_MODEL), ln_b.reshape(1, D_MODEL))


def _rope_tables(pos):
    half = A_DH // 2
    inv = ROPE_THETA ** (-jnp.arange(half, dtype=F32) / half)
    ang = pos.astype(F32)[:, None] * inv[None, :]
    cos, sin = jnp.cos(ang), jnp.sin(ang)
    return jnp.tile(cos, (1, 4)), jnp.concatenate([-sin, sin, -sin, sin], axis=1)


def _pair_heads(x, axis):
    shape = x.shape
    reps = A_HEADS // A_KV_HEADS
    x = x.reshape(shape[:axis] + (A_KV_HEADS // 2, 2, reps, A_DH) + shape[axis + 1:])
    return jnp.swapaxes(x, axis + 1, axis + 2).reshape(shape)


def _token_stage_1(x, mod, per_row, rows_per_batch, w, pos):
    x1 = _ffn(x, mod, per_row, rows_per_batch, (0, 1, 2), w["up1"], w["down1"], w["ln1_g"], w["ln1_b"])
    cos_t, sin_t = _rope_tables(pos)
    return x1, _proj(x1, mod, per_row, rows_per_batch, (3, 4), w, cos_t, sin_t)


def _token_stage_2(x1, mod, per_row, rows_per_batch, w, hm, oa, gm, ga):
    x2 = _merge(x1, mod, per_row, rows_per_batch, 5, hm, oa, gm, ga,
                w["wm"], w["wa"], w["wo"], w["ln2_g"], w["ln2_b"])
    return _ffn(x2, mod, per_row, rows_per_batch, (6, 7, 8), w["up2"], w["down2"], w["ln3_g"], w["ln3_b"])


def kernel(x_prompt, x_sample, state_mlstm_C, state_mlstm_n, state_mlstm_m, cache_swa_k, cache_swa_v, c_prompt, c_sample, w_ada, b_ada, w_ffn1_up, w_ffn1_down, ln1_g, ln1_b, w_in, b_igate, b_fgate, m_norm_g, sinks, w_branch_m, w_branch_a, w_out, ln2_g, ln2_b, w_ffn2_up, w_ffn2_down, ln3_g, ln3_b):
    assert w_ada.shape[0] == DEPTH == 1
    bp, sp, _ = x_prompt.shape
    bs, ts, _ = x_sample.shape

    win = w_in[0]
    w = dict(
        up1=w_ffn1_up[0].astype(BF16), down1=w_ffn1_down[0].astype(BF16),
        up2=w_ffn2_up[0].astype(BF16), down2=w_ffn2_down[0].astype(BF16),
        w_a=win[:, :IN_IF].astype(BF16),
        w_aq=_pair_heads(win[:, IN_AQ:IN_AK], 1).astype(BF16),
        w_r=win[:, IN_AK:IN_END].astype(BF16),
        w_if=jnp.pad(win[:, IN_IF:IN_AQ], ((0, 0), (0, LANES - 2 * M_HEADS))).astype(BF16),
        bif=jnp.concatenate([b_igate[0], b_fgate[0], jnp.zeros((LANES - 2 * M_HEADS,), F32)]).reshape(1, LANES),
        wm=w_branch_m[0].astype(BF16), wa=_pair_heads(w_branch_a[0], 0).astype(BF16),
        wo=w_out[0].astype(BF16),
        ln1_g=ln1_g[0], ln1_b=ln1_b[0], ln2_g=ln2_g[0], ln2_b=ln2_b[0], ln3_g=ln3_g[0], ln3_b=ln3_b[0],
    )

    ms = bs * ts
    c_all = jnp.concatenate([jnp.repeat(c_sample, ts, axis=0), c_prompt], axis=0)
    mod = _ada(c_all, w_ada[0], b_ada[0])
    mod_p = mod[ms:].reshape(bp, 1, ADA_CHUNKS * D_MODEL)

    mp = bp * sp
    x1p, (qm, km, vm, mo, aq, ak, av, gm, ga, gt) = _token_stage_1(
        x_prompt.reshape(mp, D_MODEL), mod_p, False, sp, w, jnp.arange(sp))
    n_chunks = sp // MLSTM_CHUNK
    hm, c_p, n_rows, m_rows = _mlstm(qm, km, vm, gt, mo, m_norm_g[0], None, 1, MLSTM_CHUNK, n_chunks, M_HEADS)
    n_p = n_rows.reshape(bp, MLSTM_CHUNK, M_HEADS, M_DQK)[:, 0]
    m_p = m_rows.reshape(bp, MLSTM_CHUNK, LANES)[:, 0, :M_HEADS]
    nblk = sp // WINDOW
    ak3 = ak.reshape(bp * nblk, WINDOW, W_AKV)
    av3 = av.reshape(bp * nblk, WINDOW, W_AKV)
    (oa,) = _swa(sinks[0], aq.reshape(bp * nblk, WINDOW, W_AQ), ak3, ak3, av3, av3,
                 nbb=SWA_PROMPT_BLOCKS, chained=True, n_new=0, blocks_per_seq=nblk)
    y_p = _token_stage_2(x1p, mod_p, False, sp, w, hm, oa.reshape(mp, D_MODEL), gm, ga)
    kb_p = ak.reshape(bp, sp, A_KV_HEADS, A_DH)[:, sp - WINDOW:]
    vb_p = av.reshape(bp, sp, A_KV_HEADS, A_DH)[:, sp - WINDOW:]

    x1s, (qm, km, vm, mo, aq, ak, av, gm, ga, gt) = _token_stage_1(
        x_sample.reshape(ms, D_MODEL), mod, True, ms, w, PAST_LEN + jnp.arange(ms) % ts)
    seqs = MLSTM_CHUNK // ts
    n0_rows = jnp.repeat(state_mlstm_n[0].reshape(bs, M_HEADS * M_DQK), ts, axis=0)
    m0_rows = jnp.repeat(jnp.pad(state_mlstm_m[0], ((0, 0), (0, LANES - M_HEADS))), ts, axis=0)
    hm, c_s, n_rows, m_rows = _mlstm(qm, km, vm, gt, mo, m_norm_g[0],
                                     (state_mlstm_C[0], n0_rows, m0_rows), seqs, ts, 1, 1)
    n_s = n_rows.reshape(bs, ts, M_HEADS, M_DQK)[:, 0]
    m_s = m_rows.reshape(bs, ts, LANES)[:, 0, :M_HEADS]
    pad_t = ((0, 0), (0, SAMPLE_TQ - ts), (0, 0))
    oa, kb_s, vb_s = _swa(
        sinks[0], jnp.pad(aq.reshape(bs, ts, W_AQ), pad_t),
        cache_swa_k[0].reshape(bs, WINDOW, W_AKV), jnp.pad(ak.reshape(bs, ts, W_AKV), pad_t),
        cache_swa_v[0].reshape(bs, WINDOW, W_AKV), jnp.pad(av.reshape(bs, ts, W_AKV), pad_t),
        nbb=SWA_SAMPLE_SEQS, chained=False, n_new=ts, blocks_per_seq=1)
    y_s = _token_stage_2(x1s, mod, True, ms, w, hm, oa[:, :ts].reshape(ms, D_MODEL), gm, ga)
    kb_s = kb_s.reshape(bs, WINDOW, A_KV_HEADS, A_DH)
    vb_s = vb_s.reshape(bs, WINDOW, A_KV_HEADS, A_DH)

    return (y_p.reshape(bp, sp, D_MODEL), y_s.reshape(bs, ts, D_MODEL),
            c_p[None], n_p[None], m_p[None], kb_p[None], vb_p[None],
            c_s[None], n_s[None], m_s[None], kb_s[None], vb_s[None])
```

```python
import functools

import jax
import jax.numpy as jnp
from jax import lax
from jax.experimental import pallas as pl
from jax.experimental.pallas import tpu as pltpu

F32 = jnp.float32
BF16 = jnp.bfloat16

D_MODEL = 1024
D_FF = 2816
DEPTH = 1
M_HEADS = 4
M_DQK = 128
M_DV = 256
A_HEADS = 16
A_KV_HEADS = 4
A_DH = 64
WINDOW = 128
PAST_LEN = 8192
ROPE_THETA = 10000.0
ATTN_SCALE = A_DH ** -0.5
LN_EPS = 1e-5
HEAD_NORM_EPS = 1e-6
ADA_CHUNKS = 9
DEEPNORM_ALPHA = (2.0 * DEPTH) ** 0.25
K_SCALE = M_DQK ** -0.5

LANES = 128
BF16_SUBLANES = 16
VMEM_LIMIT_BYTES = 56 * 1024 * 1024

W_MQ = M_HEADS * M_DQK
W_MV = M_HEADS * M_DV
W_AQ = A_HEADS * A_DH
W_AKV = A_KV_HEADS * A_DH
IN_IF = 2 * W_MQ + 2 * W_MV
IN_AQ = IN_IF + 2 * M_HEADS
IN_AK = IN_AQ + W_AQ
IN_END = IN_AK + 2 * W_AKV + 2 * D_MODEL
A_Q, A_K, A_V, A_O = 0, W_MQ, 2 * W_MQ, 2 * W_MQ + W_MV
R_AK, R_AV, R_GM, R_GA = 0, W_AKV, 2 * W_AKV, 2 * W_AKV + D_MODEL

ROW_TILE = 512
FF_CHUNK = 256
MLSTM_CHUNK = 128
SAMPLE_TQ = BF16_SUBLANES
SWA_PROMPT_BLOCKS = 2
SWA_SAMPLE_SEQS = 4

_NT = (((1,), (1,)), ((), ()))


def _params():
    return pltpu.CompilerParams(vmem_limit_bytes=VMEM_LIMIT_BYTES)


def _resident(shape):
    return pl.BlockSpec(shape, lambda *_: (0,) * len(shape), pipeline_mode=pl.Buffered(1))


def _rows(ref):
    v = ref[...]
    return v.reshape(v.shape[-2], v.shape[-1])


def _layer_norm(y, g, b, eps):
    mu = jnp.mean(y, axis=-1, keepdims=True)
    d = y - mu
    var = jnp.mean(d * d, axis=-1, keepdims=True)
    return d * lax.rsqrt(var + eps) * g + b


def _sigmoid(x):
    return 1.0 / (1.0 + jnp.exp(-x))


def _mod_specs(per_row, tm, tiles_per_batch, chunks):
    if per_row:
        return [pl.BlockSpec((tm, D_MODEL), lambda i, c=c: (i, c)) for c in chunks]
    return [pl.BlockSpec((1, 1, D_MODEL), lambda i, c=c: (i // tiles_per_batch, 0, c)) for c in chunks]


def _ada_kernel(c_ref, w_ref, b_ref, o_ref):
    c = c_ref[...]
    s = (c * _sigmoid(c)).astype(BF16)
    o_ref[...] = jnp.dot(s, w_ref[...].astype(BF16), preferred_element_type=F32) + b_ref[...]


def _ada(c_all, w_ada, b_ada):
    rows = c_all.shape[0]
    n_out = w_ada.shape[1]
    tn = D_MODEL
    return pl.pallas_call(
        _ada_kernel,
        grid=(n_out // tn,),
        in_specs=[pl.BlockSpec((rows, D_MODEL), lambda j: (0, 0)),
                  pl.BlockSpec((D_MODEL, tn), lambda j: (0, j)),
                  pl.BlockSpec((1, tn), lambda j: (0, j))],
        out_specs=pl.BlockSpec((rows, tn), lambda j: (0, j)),
        out_shape=jax.ShapeDtypeStruct((rows, n_out), F32),
        compiler_params=_params(),
        name="ada",
    )(c_all, w_ada, b_ada.reshape(1, n_out))


def _ffn_kernel(x_ref, sh_ref, sc_ref, g_ref, wup_ref, wdn_ref, lg_ref, lb_ref, o_ref, act_ref):
    x = x_ref[...]
    h = (x * (1.0 + _rows(sc_ref)) + _rows(sh_ref)).astype(BF16)
    for c in range(0, D_FF, FF_CHUNK):
        a = jnp.dot(h, wup_ref[:, c:c + FF_CHUNK], preferred_element_type=F32)
        u = jnp.dot(h, wup_ref[:, D_FF + c:D_FF + c + FF_CHUNK], preferred_element_type=F32)
        act_ref[:, c:c + FF_CHUNK] = (a * _sigmoid(a) * u).astype(BF16)
    f = jnp.dot(act_ref[...], wdn_ref[...], preferred_element_type=F32)
    y = DEEPNORM_ALPHA * x + (0.5 * (1.0 + _rows(g_ref))) * f
    o_ref[...] = _layer_norm(y, lg_ref[...], lb_ref[...], LN_EPS)


def _ffn(x, mod, per_row, rows_per_batch, chunks, w_up, w_down, ln_g, ln_b):
    m = x.shape[0]
    tm = ROW_TILE
    return pl.pallas_call(
        _ffn_kernel,
        grid=(m // tm,),
        in_specs=[pl.BlockSpec((tm, D_MODEL), lambda i: (i, 0))]
        + _mod_specs(per_row, tm, rows_per_batch // tm, chunks)
        + [_resident((D_MODEL, 2 * D_FF)), _resident((D_FF, D_MODEL)),
           _resident((1, D_MODEL)), _resident((1, D_MODEL))],
        out_specs=pl.BlockSpec((tm, D_MODEL), lambda i: (i, 0)),
        out_shape=jax.ShapeDtypeStruct((m, D_MODEL), F32),
        scratch_shapes=[pltpu.VMEM((tm, D_FF), BF16)],
        compiler_params=_params(),
        name="ffn",
    )(x, mod, mod, mod, w_up, w_down, ln_g.reshape(1, D_MODEL), ln_b.reshape(1, D_MODEL))


def _proj_kernel(x_ref, sh_ref, sc_ref, wa_ref, wq_ref, wr_ref, wif_ref, bif_ref, cos_ref, sin_ref,
                 q_ref, k_ref, v_ref, o_ref, aq_ref, ak_ref, av_ref, gm_ref, ga_ref, gt_ref):
    x = x_ref[...]
    tm = x.shape[0]
    h = (x * (1.0 + _rows(sc_ref)) + _rows(sh_ref)).astype(BF16)

    def seg(w_ref, lo, width=256):
        return jnp.dot(h, w_ref[:, lo:lo + width], preferred_element_type=F32)

    for c in range(0, W_MQ, 256):
        q_ref[:, c:c + 256] = seg(wa_ref, A_Q + c).astype(BF16)
        k_ref[:, c:c + 256] = (seg(wa_ref, A_K + c) * K_SCALE).astype(BF16)
    for c in range(0, D_MODEL, 256):
        v_ref[:, c:c + 256] = seg(wa_ref, A_V + c).astype(BF16)
        o_ref[:, c:c + 256] = _sigmoid(seg(wa_ref, A_O + c)).astype(BF16)
        gm_ref[:, c:c + 256] = _sigmoid(seg(wr_ref, R_GM + c)).astype(BF16)
        ga_ref[:, c:c + 256] = _sigmoid(seg(wr_ref, R_GA + c)).astype(BF16)

    cos = cos_ref[...]
    sin = sin_ref[...]
    lane = lax.broadcasted_iota(jnp.int32, (tm, LANES), 1)
    low_half = (lane & (A_DH // 2)) == 0

    def rope(z):
        partner = jnp.where(low_half, pltpu.roll(z, LANES - A_DH // 2, 1), pltpu.roll(z, A_DH // 2, 1))
        return z * cos + partner * sin

    for c in range(0, W_AQ, 256):
        z = seg(wq_ref, c)
        aq_ref[:, c:c + LANES] = rope(z[:, :LANES]).astype(BF16)
        aq_ref[:, c + LANES:c + 256] = rope(z[:, LANES:]).astype(BF16)
    z = seg(wr_ref, R_AK)
    ak_ref[:, :LANES] = rope(z[:, :LANES])
    ak_ref[:, LANES:] = rope(z[:, LANES:])
    av_ref[...] = seg(wr_ref, R_AV)

    zg = jnp.dot(h, wif_ref[...], preferred_element_type=F32) + bif_ref[...]
    logsig = jnp.minimum(zg, 0.0) - jnp.log(1.0 + jnp.exp(-jnp.abs(zg)))
    gt_ref[...] = jnp.where(lane < M_HEADS, zg, logsig)


def _proj(x, mod, per_row, rows_per_batch, chunks, w, cos_t, sin_t):
    m = x.shape[0]
    tm = ROW_TILE
    n_pos_tiles = cos_t.shape[0] // tm

    def tok(width):
        return pl.BlockSpec((tm, width), lambda i: (i, 0))

    widths = (W_MQ, W_MQ, W_MV, W_MV, W_AQ, W_AKV, W_AKV, D_MODEL, D_MODEL, LANES)
    dtypes = (BF16, BF16, BF16, BF16, BF16, F32, F32, BF16, BF16, F32)
    weights = (w["w_a"], w["w_aq"], w["w_r"], w["w_if"], w["bif"])
    return pl.pallas_call(
        _proj_kernel,
        grid=(m // tm,),
        in_specs=[tok(D_MODEL)] + _mod_specs(per_row, tm, rows_per_batch // tm, chunks)
        + [_resident(a.shape) for a in weights]
        + [pl.BlockSpec((tm, LANES), lambda i: (i % n_pos_tiles, 0)),
           pl.BlockSpec((tm, LANES), lambda i: (i % n_pos_tiles, 0))],
        out_specs=[tok(wd) for wd in widths],
        out_shape=[jax.ShapeDtypeStruct((m, wd), d) for wd, d in zip(widths, dtypes)],
        compiler_params=_params(),
        name="proj",
    )(x, mod, mod, *weights, cos_t, sin_t)


def _mlstm_kernel(*refs, nb, tpb, hps, zero_init):
    if zero_init:
        (q_ref, k_ref, v_ref, g_ref, mo_ref, ng_ref,
         out_ref, c_ref, n_ref, m_ref, gt_s, ct_s) = refs
    else:
        (q_ref, k_ref, v_ref, g_ref, mo_ref, ng_ref, c0_ref, n0_ref, m0_ref,
         out_ref, c_ref, n_ref, m_ref, gt_s, ct_s) = refs
    first_chunk = pl.program_id(2) == 0

    @pl.when(first_chunk)
    def _():
        if zero_init:
            c_ref[...] = jnp.zeros_like(c_ref)
            n_ref[...] = jnp.zeros_like(n_ref)
        else:
            c_ref[...] = c0_ref[...]
            n_ref[...] = n0_ref[...]

    @pl.when(first_chunk & (pl.program_id(1) == 0))
    def _():
        m_ref[...] = jnp.zeros_like(m_ref) if zero_init else m0_ref[...]

    head0 = 0 if hps == M_HEADS else pl.program_id(1) * hps
    _mlstm_chunk(q_ref, k_ref, v_ref, g_ref, mo_ref, ng_ref, out_ref, c_ref, n_ref, m_ref, gt_s, ct_s,
                 nb=nb, tpb=tpb, hps=hps, head0=head0)


def _mlstm_chunk(q_ref, k_ref, v_ref, g_ref, mo_ref, ng_ref, out_ref, c_ref, n_ref, m_ref, gt_s, ct_s,
                 *, nb, tpb, hps, head0):
    L = nb * tpb
    shift = tpb.bit_length() - 1
    row = lax.broadcasted_iota(jnp.int32, (L, L), 0)
    col = lax.broadcasted_iota(jnp.int32, (L, L), 1)
    same = (row >> shift) == (col >> shift)
    causal = same & (col <= row)
    lane = lax.broadcasted_iota(jnp.int32, (L, LANES), 1)
    row_seq = lax.broadcasted_iota(jnp.int32, (L, 1), 0) >> shift

    def lane_col(x, idx):
        return jnp.sum(jnp.where(lane == idx, x, 0.0), axis=1, keepdims=True)

    gates = g_ref[...]
    tri = jnp.where(causal, 1.0, 0.0).astype(BF16)
    g_hi = gates.astype(BF16)
    rem = gates - g_hi.astype(F32)
    g_mid = rem.astype(BF16)
    g_lo = (rem - g_mid.astype(F32)).astype(BF16)
    cum = (jnp.dot(tri, g_hi, preferred_element_type=F32)
           + jnp.dot(tri, g_mid, preferred_element_type=F32)
           + jnp.dot(tri, g_lo, preferred_element_type=F32))
    gt_s[...] = gates.T
    ct_s[...] = cum.T
    m_rows = m_ref[...]
    m_next = m_rows

    for hl in range(hps):
        head = head0 + hl
        qs = slice(hl * M_DQK, (hl + 1) * M_DQK)
        vs = slice(hl * M_DV, (hl + 1) * M_DV)
        q = q_ref[:, qs]
        k = k_ref[:, qs]
        v = v_ref[:, vs]
        b_c = lane_col(cum, M_HEADS + head)
        m_p = lane_col(m_rows, head)
        i_r = gt_s[pl.ds(head, 1), :]
        b_r = ct_s[pl.ds(M_HEADS + head, 1), :]

        log_d = jnp.where(causal, b_c - b_r + i_r, -jnp.inf)
        m_t = jnp.maximum(b_c + m_p, jnp.max(log_d, axis=1, keepdims=True))
        dmat = jnp.exp(log_d - m_t)
        e_int = jnp.exp(b_c + m_p - m_t)

        if nb == 1:
            last = slice(L - 1, L)
            m_new = jnp.broadcast_to(m_t[last], (L, 1))
            e_c = jnp.broadcast_to(e_int[last], (L, 1))
            w_mat = jnp.broadcast_to(dmat[last], (L, L))
        else:
            b_last = jnp.min(jnp.where(same, b_r, jnp.inf), axis=1, keepdims=True)
            log_w = jnp.where(same, b_last - b_r + i_r, -jnp.inf)
            m_new = jnp.maximum(b_last + m_p, jnp.max(log_w, axis=1, keepdims=True))
            w_mat = jnp.exp(log_w - m_new)
            e_c = jnp.exp(b_last + m_p - m_new)
        e_cb = jnp.broadcast_to(e_c, (L, LANES))

        s = lax.dot_general(q, k, _NT, preferred_element_type=F32) * dmat
        v_t = v.astype(F32).T

        inter = jnp.zeros((L, M_DV), F32)
        for j in range(nb):
            first = slice(j * tpb, j * tpb + 1)
            c_j = c_ref[j, hl]
            q_j = q if nb == 1 else jnp.where(row_seq == j, q, jnp.zeros_like(q))
            inter = inter + lax.dot_general(q_j, c_j.astype(BF16), _NT, preferred_element_type=F32)
            lhs = (v_t * w_mat[first]).astype(BF16)
            c_ref[j, hl] = e_cb[first] * c_j + jnp.dot(lhs, k, preferred_element_type=F32)

        n_rows = n_ref[:, qs]
        qn = jnp.sum(q.astype(F32) * n_rows, axis=1, keepdims=True)
        num = jnp.dot(s.astype(BF16), v, preferred_element_type=F32) + e_int * inter
        den = jnp.sum(s, axis=1, keepdims=True) + e_int * qn
        hh = num / jnp.maximum(jnp.abs(den), jnp.exp(-m_t))

        mu = jnp.mean(hh, axis=1, keepdims=True)
        dlt = hh - mu
        var = jnp.mean(dlt * dlt, axis=1, keepdims=True)
        y = dlt * lax.rsqrt(var + HEAD_NORM_EPS)
        out_ref[:, vs] = (y * ng_ref[:, vs] * mo_ref[:, vs].astype(F32)).astype(BF16)

        n_ref[:, qs] = e_c * n_rows + jnp.dot(w_mat.astype(BF16), k, preferred_element_type=F32)
        m_next = jnp.where(lane == head, m_new, m_next)

    m_ref[...] = m_next


def _mlstm(q, k, v, gates, mo, norm_g, state, nb, tpb, n_chunks, hps):
    m = q.shape[0]
    L = nb * tpb
    n_blocks = m // (L * n_chunks)
    hd = M_HEADS * M_DQK

    def tok(width):
        return pl.BlockSpec((L, width), lambda b, g, c: (b * n_chunks + c, g))

    c_spec = pl.BlockSpec((nb, hps, M_DV, M_DQK), lambda b, g, c: (b, g, 0, 0))
    n_spec = pl.BlockSpec((L, hps * M_DQK), lambda b, g, c: (b, g))
    m_spec = pl.BlockSpec((L, LANES), lambda b, g, c: (b, 0))
    in_specs = [tok(hps * M_DQK), tok(hps * M_DQK), tok(hps * M_DV),
                pl.BlockSpec((L, LANES), lambda b, g, c: (b * n_chunks + c, 0)),
                tok(hps * M_DV), pl.BlockSpec((1, hps * M_DV), lambda b, g, c: (0, g))]
    args = [q, k, v, gates, mo, norm_g.reshape(1, M_HEADS * M_DV)]
    if state is not None:
        in_specs += [c_spec, n_spec, m_spec]
        args += list(state)
    return pl.pallas_call(
        functools.partial(_mlstm_kernel, nb=nb, tpb=tpb, hps=hps, zero_init=state is None),
        grid=(n_blocks, M_HEADS // hps, n_chunks),
        in_specs=in_specs,
        out_specs=[tok(hps * M_DV), c_spec, n_spec, m_spec],
        out_shape=[jax.ShapeDtypeStruct((m, M_HEADS * M_DV), BF16),
                   jax.ShapeDtypeStruct((n_blocks * nb, M_HEADS, M_DV, M_DQK), F32),
                   jax.ShapeDtypeStruct((n_blocks * L, hd), F32),
                   jax.ShapeDtypeStruct((n_blocks * L, LANES), F32)],
        scratch_shapes=[pltpu.VMEM((LANES, L), F32), pltpu.VMEM((LANES, L), F32)],
        compiler_params=_params(),
        name="mlstm",
    )(*args)


def _swa_kernel(*refs, nbb, tq, chained, n_new):
    if n_new:
        sink_ref, q_ref, kp_ref, kc_ref, vp_ref, vc_ref, o_ref, kn_ref, vn_ref = refs
    else:
        sink_ref, q_ref, kp_ref, kc_ref, vp_ref, vc_ref, o_ref = refs
    qi = lax.broadcasted_iota(jnp.int32, (tq, WINDOW), 0)
    kj = lax.broadcasted_iota(jnp.int32, (tq, WINDOW), 1)

    def pad_keys(x):
        if x.shape[0] == WINDOW:
            return x
        return jnp.concatenate([x, jnp.zeros((WINDOW - x.shape[0], x.shape[1]), x.dtype)], axis=0)

    for jb in range(nbb):
        kc, vc = pad_keys(kc_ref[jb]), pad_keys(vc_ref[jb])
        valid_prev = kj > qi
        if chained and jb > 0:
            kp, vp = kc_ref[jb - 1], vc_ref[jb - 1]
        else:
            kp, vp = kp_ref[jb], vp_ref[jb]
            if chained:
                valid_prev = valid_prev & (pl.program_id(1) > 0)

        if n_new:
            keep_old = lax.broadcasted_iota(jnp.int32, (WINDOW, W_AKV), 0) < WINDOW - n_new
            kn_ref[jb] = jnp.where(keep_old, pltpu.roll(kp, WINDOW - n_new, 0),
                                   pltpu.roll(kc, WINDOW - n_new, 0))
            vn_ref[jb] = jnp.where(keep_old, pltpu.roll(vp, WINDOW - n_new, 0),
                                   pltpu.roll(vc, WINDOW - n_new, 0))

        _swa_block(sink_ref, q_ref[jb], kp, vp, kc, vc, valid_prev, o_ref.at[jb])


def _swa_block(sink_ref, q, kp, vp, kc, vc, valid_prev, o_ref):
    tq = q.shape[0]
    reps = A_HEADS // A_KV_HEADS
    low_q = lax.broadcasted_iota(jnp.int32, (tq, LANES), 1) < A_DH
    qi = lax.broadcasted_iota(jnp.int32, (tq, WINDOW), 0)
    kj = lax.broadcasted_iota(jnp.int32, (tq, WINDOW), 1)
    sink_lane = kj == 0
    valid_cur = kj <= qi
    key_row = lax.broadcasted_iota(jnp.int32, (WINDOW, LANES), 0)
    scale = jnp.asarray(ATTN_SCALE, BF16)
    neg_inf = -jnp.inf

    for p in range(A_KV_HEADS // 2):
        ks = slice(p * LANES, (p + 1) * LANES)
        k_prev, k_cur = kp[:, ks].astype(BF16), kc[:, ks].astype(BF16)
        v_prev = jnp.where(key_row == 0, 0.0, vp[:, ks]).astype(BF16)
        v_cur = vc[:, ks].astype(BF16)
        slabs, heads = [], []
        for r in range(reps):
            qb = q[:, (reps * p + r) * LANES:(reps * p + r + 1) * LANES] * scale
            slabs += [jnp.where(low_q, qb, jnp.zeros_like(qb)), jnp.where(low_q, jnp.zeros_like(qb), qb)]
            heads += [2 * reps * p + r, 2 * reps * p + reps + r]
        q8 = jnp.concatenate(slabs, axis=0)
        sp = lax.dot_general(q8, k_prev, _NT, preferred_element_type=F32)
        sc = lax.dot_general(q8, k_cur, _NT, preferred_element_type=F32)
        pps, pcs, invs = [], [], []
        for i, head in enumerate(heads):
            rows = slice(i * tq, (i + 1) * tq)
            sp_i = jnp.where(sink_lane, sink_ref[head], jnp.where(valid_prev, sp[rows], neg_inf))
            sc_i = jnp.where(valid_cur, sc[rows], neg_inf)
            mx = jnp.max(jnp.maximum(sp_i, sc_i), axis=1, keepdims=True)
            pp = jnp.exp(sp_i - mx)
            pc = jnp.exp(sc_i - mx)
            invs.append(1.0 / jnp.sum(pp + pc, axis=1, keepdims=True))
            pps.append(pp.astype(BF16))
            pcs.append(pc.astype(BF16))
        o8 = (jnp.dot(jnp.concatenate(pps, axis=0), v_prev, preferred_element_type=F32)
              + jnp.dot(jnp.concatenate(pcs, axis=0), v_cur, preferred_element_type=F32))
        for r in range(reps):
            lo = o8[(2 * r) * tq:(2 * r + 1) * tq] * invs[2 * r]
            hi = o8[(2 * r + 1) * tq:(2 * r + 2) * tq] * invs[2 * r + 1]
            blk = reps * p + r
            o_ref[:, blk * LANES:(blk + 1) * LANES] = jnp.where(low_q, lo, hi).astype(BF16)


def _swa(sinks, q, k_prev_src, k_cur, v_prev_src, v_cur, *, nbb, chained, n_new, blocks_per_seq):
    n, tq, _ = q.shape
    tk = k_cur.shape[1]
    if chained:
        steps = blocks_per_seq // nbb
        grid = (n // blocks_per_seq, steps)
        cur_index = lambda b, i: (b * steps + i, 0, 0)
        prev_spec = pl.BlockSpec((1, WINDOW, W_AKV),
                                 lambda b, i: (b * blocks_per_seq + jnp.maximum(i * nbb - 1, 0), 0, 0))
    else:
        grid = (n // nbb,)
        cur_index = lambda i: (i, 0, 0)
        prev_spec = pl.BlockSpec((nbb, WINDOW, W_AKV), cur_index)
    cur_spec = pl.BlockSpec((nbb, tk, W_AKV), cur_index)
    q_spec = pl.BlockSpec((nbb, tq, W_AQ), cur_index)
    out_specs = [q_spec]
    out_shape = [jax.ShapeDtypeStruct((n, tq, W_AQ), BF16)]
    if n_new:
        out_specs += [prev_spec, prev_spec]
        out_shape += [jax.ShapeDtypeStruct((n, WINDOW, W_AKV), F32)] * 2
    return pl.pallas_call(
        functools.partial(_swa_kernel, nbb=nbb, tq=tq, chained=chained, n_new=n_new),
        grid=grid,
        in_specs=[pl.BlockSpec(memory_space=pltpu.SMEM), q_spec, prev_spec, cur_spec, prev_spec, cur_spec],
        out_specs=out_specs,
        out_shape=out_shape,
        compiler_params=_params(),
        name="swa",
    )(sinks, q, k_prev_src, k_cur, v_prev_src, v_cur)


def _merge_kernel(x_ref, g_ref, hm_ref, oa_ref, gm_ref, ga_ref, wm_ref, wa_ref, wo_ref,
                  lg_ref, lb_ref, o_ref):
    ym = jnp.dot(hm_ref[...], wm_ref[...], preferred_element_type=F32)
    ya = jnp.dot(oa_ref[...], wa_ref[...], preferred_element_type=F32)
    mix = gm_ref[...].astype(F32) * ym + ga_ref[...].astype(F32) * ya
    t = jnp.dot(mix.astype(BF16), wo_ref[...], preferred_element_type=F32)
    y = DEEPNORM_ALPHA * x_ref[...] + (1.0 + _rows(g_ref)) * t
    o_ref[...] = _layer_norm(y, lg_ref[...], lb_ref[...], LN_EPS)


def _merge(x, mod, per_row, rows_per_batch, chunk, hm, oa, gm, ga, wm, wa, wo, ln_g, ln_b):
    m = x.shape[0]
    tm = ROW_TILE

    def tok():
        return pl.BlockSpec((tm, D_MODEL), lambda i: (i, 0))

    return pl.pallas_call(
        _merge_kernel,
        grid=(m // tm,),
        in_specs=[tok()] + _mod_specs(per_row, tm, rows_per_batch // tm, (chunk,))
        + [tok(), tok(), tok(), tok()]
        + [_resident((D_MODEL, D_MODEL))] * 3 + [_resident((1, D_MODEL))] * 2,
        out_specs=tok(),
        out_shape=jax.ShapeDtypeStruct((m, D_MODEL), F32),
        compiler_params=_params(),
        name="merge",
    )(x, mod, hm, oa, gm, ga, wm, wa, wo, ln_g.reshape(1, D_MODEL), ln_b.reshape(1, D_MODEL))


def _mixer_kernel(sink_ref, x_ref, sh_ref, sc_ref, g_ref, wa_ref, wq_ref, wr_ref, wif_ref, bif_ref,
                  cos_ref, sin_ref, ng_ref, wm_ref, wba_ref, wo_ref, lg_ref, lb_ref,
                  y_ref, c_ref, n_ref, m_ref, kb_ref, vb_ref,
                  q_s, k_s, v_s, mo_s, aq_s, ak_s, av_s, gm_s, ga_s, gt_s, hm_s, oa_s, gtt_s, ctt_s):
    tm = x_ref.shape[0]
    tile = pl.program_id(1)

    @pl.when(tile == 0)
    def _():
        c_ref[...] = jnp.zeros_like(c_ref)
        n_ref[...] = jnp.zeros_like(n_ref)
        m_ref[...] = jnp.zeros_like(m_ref)
        ak_s[0:WINDOW, :] = jnp.zeros((WINDOW, W_AKV), F32)
        av_s[0:WINDOW, :] = jnp.zeros((WINDOW, W_AKV), F32)

    cur = pl.ds(WINDOW, tm)
    _proj_kernel(x_ref, sh_ref, sc_ref, wa_ref, wq_ref, wr_ref, wif_ref, bif_ref, cos_ref, sin_ref,
                 q_s, k_s, v_s, mo_s, aq_s, ak_s.at[cur], av_s.at[cur], gm_s, ga_s, gt_s)

    qi = lax.broadcasted_iota(jnp.int32, (WINDOW, WINDOW), 0)
    kj = lax.broadcasted_iota(jnp.int32, (WINDOW, WINDOW), 1)
    for j in range(tm // MLSTM_CHUNK):
        rows = pl.ds(j * MLSTM_CHUNK, MLSTM_CHUNK)
        _mlstm_chunk(q_s.at[rows], k_s.at[rows], v_s.at[rows], gt_s.at[rows], mo_s.at[rows], ng_ref,
                     hm_s.at[rows], c_ref, n_ref, m_ref, gtt_s, ctt_s,
                     nb=1, tpb=MLSTM_CHUNK, hps=M_HEADS, head0=0)
        valid_prev = kj > qi
        if j == 0:
            valid_prev = valid_prev & (tile > 0)
        prev = pl.ds(j * WINDOW, WINDOW)
        own = pl.ds((j + 1) * WINDOW, WINDOW)
        _swa_block(sink_ref, aq_s[rows], ak_s[prev], av_s[prev], ak_s[own], av_s[own], valid_prev,
                   oa_s.at[rows])

    for half in range(2):
        rows = pl.ds(half * (tm // 2), tm // 2)
        _merge_kernel(x_ref.at[rows], g_ref, hm_s.at[rows], oa_s.at[rows], gm_s.at[rows], ga_s.at[rows],
                      wm_ref, wba_ref, wo_ref, lg_ref, lb_ref, y_ref.at[rows])

    last = pl.ds(tm, WINDOW)
    k_last, v_last = ak_s[last], av_s[last]
    kb_ref[0] = k_last
    vb_ref[0] = v_last
    ak_s[0:WINDOW, :] = k_last
    av_s[0:WINDOW, :] = v_last


def _mixer(x1, mod, w, sinks, norm_g, cos_t, sin_t, bp, sp):
    tm = ROW_TILE
    tiles = sp // tm
    hd = M_HEADS * M_DQK
    weights = (w["w_a"], w["w_aq"], w["w_r"], w["w_if"], w["bif"])
    tok = pl.BlockSpec((tm, D_MODEL), lambda b, t: (b * tiles + t, 0))
    pos = pl.BlockSpec((tm, LANES), lambda b, t: (t, 0))
    mods = [pl.BlockSpec((1, 1, D_MODEL), lambda b, t, c=c: (b, 0, c)) for c in (3, 4, 5)]

    def per_seq(*shape):
        return pl.BlockSpec((1,) + shape, lambda b, t: (b,) + (0,) * len(shape))

    return pl.pallas_call(
        _mixer_kernel,
        grid=(bp, tiles),
        in_specs=[pl.BlockSpec(memory_space=pltpu.SMEM), tok] + mods
        + [_resident(a.shape) for a in weights] + [pos, pos, _resident((1, M_HEADS * M_DV))]
        + [_resident((D_MODEL, D_MODEL))] * 3 + [_resident((1, D_MODEL))] * 2,
        out_specs=[tok, per_seq(M_HEADS, M_DV, M_DQK),
                   pl.BlockSpec((MLSTM_CHUNK, hd), lambda b, t: (b, 0)),
                   pl.BlockSpec((MLSTM_CHUNK, LANES), lambda b, t: (b, 0)),
                   per_seq(WINDOW, W_AKV), per_seq(WINDOW, W_AKV)],
        out_shape=[jax.ShapeDtypeStruct((bp * sp, D_MODEL), F32),
                   jax.ShapeDtypeStruct((bp, M_HEADS, M_DV, M_DQK), F32),
                   jax.ShapeDtypeStruct((bp * MLSTM_CHUNK, hd), F32),
                   jax.ShapeDtypeStruct((bp * MLSTM_CHUNK, LANES), F32),
                   jax.ShapeDtypeStruct((bp, WINDOW, W_AKV), F32),
                   jax.ShapeDtypeStruct((bp, WINDOW, W_AKV), F32)],
        scratch_shapes=[pltpu.VMEM((tm, W_MQ), BF16), pltpu.VMEM((tm, W_MQ), BF16),
                        pltpu.VMEM((tm, W_MV), BF16), pltpu.VMEM((tm, W_MV), BF16),
                        pltpu.VMEM((tm, W_AQ), BF16),
                        pltpu.VMEM((tm + WINDOW, W_AKV), F32), pltpu.VMEM((tm + WINDOW, W_AKV), F32),
                        pltpu.VMEM((tm, D_MODEL), BF16), pltpu.VMEM((tm, D_MODEL), BF16),
                        pltpu.VMEM((tm, LANES), F32),
                        pltpu.VMEM((tm, W_MV), BF16), pltpu.VMEM((tm, W_AQ), BF16),
                        pltpu.VMEM((LANES, MLSTM_CHUNK), F32), pltpu.VMEM((LANES, MLSTM_CHUNK), F32)],
        compiler_params=_params(),
        name="mixer",
    )(sinks, x1, mod, mod, mod, *weights, cos_t, sin_t, norm_g.reshape(1, M_HEADS * M_DV),
      w["wm"], w["wa"], w["wo"], w["ln2_g"].reshape(1, D_MODEL), w["ln2_b"].reshape(1, D_MODEL))


def _rope_tables(pos):
    half = A_DH // 2
    inv = ROPE_THETA ** (-jnp.arange(half, dtype=F32) / half)
    ang = pos.astype(F32)[:, None] * inv[None, :]
    cos, sin = jnp.cos(ang), jnp.sin(ang)
    return jnp.tile(cos, (1, 4)), jnp.concatenate([-sin, sin, -sin, sin], axis=1)


def _pair_heads(x, axis):
    shape = x.shape
    reps = A_HEADS // A_KV_HEADS
    x = x.reshape(shape[:axis] + (A_KV_HEADS // 2, 2, reps, A_DH) + shape[axis + 1:])
    return jnp.swapaxes(x, axis + 1, axis + 2).reshape(shape)


def _token_stage_1(x, mod, per_row, rows_per_batch, w, pos):
    x1 = _ffn(x, mod, per_row, rows_per_batch, (0, 1, 2), w["up1"], w["down1"], w["ln1_g"], w["ln1_b"])
    cos_t, sin_t = _rope_tables(pos)
    return x1, _proj(x1, mod, per_row, rows_per_batch, (3, 4), w, cos_t, sin_t)


def _token_stage_2(x1, mod, per_row, rows_per_batch, w, hm, oa, gm, ga):
    x2 = _merge(x1, mod, per_row, rows_per_batch, 5, hm, oa, gm, ga,
                w["wm"], w["wa"], w["wo"], w["ln2_g"], w["ln2_b"])
    return _ffn(x2, mod, per_row, rows_per_batch, (6, 7, 8), w["up2"], w["down2"], w["ln3_g"], w["ln3_b"])


def kernel(x_prompt, x_sample, state_mlstm_C, state_mlstm_n, state_mlstm_m, cache_swa_k, cache_swa_v, c_prompt, c_sample, w_ada, b_ada, w_ffn1_up, w_ffn1_down, ln1_g, ln1_b, w_in, b_igate, b_fgate, m_norm_g, sinks, w_branch_m, w_branch_a, w_out, ln2_g, ln2_b, w_ffn2_up, w_ffn2_down, ln3_g, ln3_b):
    assert w_ada.shape[0] == DEPTH == 1
    bp, sp, _ = x_prompt.shape
    bs, ts, _ = x_sample.shape

    win = w_in[0]
    w = dict(
        up1=w_ffn1_up[0].astype(BF16), down1=w_ffn1_down[0].astype(BF16),
        up2=w_ffn2_up[0].astype(BF16), down2=w_ffn2_down[0].astype(BF16),
        w_a=win[:, :IN_IF].astype(BF16),
        w_aq=_pair_heads(win[:, IN_AQ:IN_AK], 1).astype(BF16),
        w_r=win[:, IN_AK:IN_END].astype(BF16),
        w_if=jnp.pad(win[:, IN_IF:IN_AQ], ((0, 0), (0, LANES - 2 * M_HEADS))).astype(BF16),
        bif=jnp.concatenate([b_igate[0], b_fgate[0], jnp.zeros((LANES - 2 * M_HEADS,), F32)]).reshape(1, LANES),
        wm=w_branch_m[0].astype(BF16), wa=_pair_heads(w_branch_a[0], 0).astype(BF16),
        wo=w_out[0].astype(BF16),
        ln1_g=ln1_g[0], ln1_b=ln1_b[0], ln2_g=ln2_g[0], ln2_b=ln2_b[0], ln3_g=ln3_g[0], ln3_b=ln3_b[0],
    )

    ms = bs * ts
    c_all = jnp.concatenate([jnp.repeat(c_sample, ts, axis=0), c_prompt], axis=0)
    mod = _ada(c_all, w_ada[0], b_ada[0])
    mod_p = mod[ms:].reshape(bp, 1, ADA_CHUNKS * D_MODEL)

    mp = bp * sp
    x1p = _ffn(x_prompt.reshape(mp, D_MODEL), mod_p, False, sp, (0, 1, 2),
               w["up1"], w["down1"], w["ln1_g"], w["ln1_b"])
    cos_t, sin_t = _rope_tables(jnp.arange(sp))
    x2p, c_p, n_rows, m_rows, kb_p, vb_p = _mixer(x1p, mod_p, w, sinks[0], m_norm_g[0], cos_t, sin_t, bp, sp)
    y_p = _ffn(x2p, mod_p, False, sp, (6, 7, 8), w["up2"], w["down2"], w["ln3_g"], w["ln3_b"])
    n_p = n_rows.reshape(bp, MLSTM_CHUNK, M_HEADS, M_DQK)[:, 0]
    m_p = m_rows.reshape(bp, MLSTM_CHUNK, LANES)[:, 0, :M_HEADS]
    kb_p = kb_p.reshape(bp, WINDOW, A_KV_HEADS, A_DH)
    vb_p = vb_p.reshape(bp, WINDOW, A_KV_HEADS, A_DH)

    x1s, (qm, km, vm, mo, aq, ak, av, gm, ga, gt) = _token_stage_1(
        x_sample.reshape(ms, D_MODEL), mod, True, ms, w, PAST_LEN + jnp.arange(ms) % ts)
    seqs = MLSTM_CHUNK // ts
    n0_rows = jnp.repeat(state_mlstm_n[0].reshape(bs, M_HEADS * M_DQK), ts, axis=0)
    m0_rows = jnp.repeat(jnp.pad(state_mlstm_m[0], ((0, 0), (0, LANES - M_HEADS))), ts, axis=0)
    hm, c_s, n_rows, m_rows = _mlstm(qm, km, vm, gt, mo, m_norm_g[0],
                                     (state_mlstm_C[0], n0_rows, m0_rows), seqs, ts, 1, 1)
    n_s = n_rows.reshape(bs, ts, M_HEADS, M_DQK)[:, 0]
    m_s = m_rows.reshape(bs, ts, LANES)[:, 0, :M_HEADS]
    pad_t = ((0, 0), (0, SAMPLE_TQ - ts), (0, 0))
    oa, kb_s, vb_s = _swa(
        sinks[0], jnp.pad(aq.reshape(bs, ts, W_AQ), pad_t),
        cache_swa_k[0].reshape(bs, WINDOW, W_AKV), jnp.pad(ak.reshape(bs, ts, W_AKV), pad_t),
        cache_swa_v[0].reshape(bs, WINDOW, W_AKV), jnp.pad(av.reshape(bs, ts, W_AKV), pad_t),
        nbb=SWA_SAMPLE_SEQS, chained=False, n_new=ts, blocks_per_seq=1)
    y_s = _token_stage_2(x1s, mod, True, ms, w, hm, oa[:, :ts].reshape(ms, D_MODEL), gm, ga)
    kb_s = kb_s.reshape(bs, WINDOW, A_KV_HEADS, A_DH)
    vb_s = vb_s.reshape(bs, WINDOW, A_KV_HEADS, A_DH)

    return (y_p.reshape(bp, sp, D_MODEL), y_s.reshape(bs, ts, D_MODEL),
            c_p[None], n_p[None], m_p[None], kb_p[None], vb_p[None],
            c_s[None], n_s[None], m_s[None], kb_s[None], vb_s[None])
```

```python
import functools

import jax
import jax.numpy as jnp
from jax import lax
from jax.experimental import pallas as pl
from jax.experimental.pallas import tpu as pltpu

F32 = jnp.float32
BF16 = jnp.bfloat16

D_MODEL = 1024
D_FF = 2816
DEPTH = 1
M_HEADS = 4
M_DQK = 128
M_DV = 256
A_HEADS = 16
A_KV_HEADS = 4
A_DH = 64
WINDOW = 128
PAST_LEN = 8192
ROPE_THETA = 10000.0
ATTN_SCALE = A_DH ** -0.5
LN_EPS = 1e-5
HEAD_NORM_EPS = 1e-6
ADA_CHUNKS = 9
DEEPNORM_ALPHA = (2.0 * DEPTH) ** 0.25
K_SCALE = M_DQK ** -0.5

LANES = 128
BF16_SUBLANES = 16
VMEM_LIMIT_BYTES = 56 * 1024 * 1024

W_MQ = M_HEADS * M_DQK
W_MV = M_HEADS * M_DV
W_AQ = A_HEADS * A_DH
W_AKV = A_KV_HEADS * A_DH
IN_IF = 2 * W_MQ + 2 * W_MV
IN_AQ = IN_IF + 2 * M_HEADS
IN_AK = IN_AQ + W_AQ
IN_END = IN_AK + 2 * W_AKV + 2 * D_MODEL
A_Q, A_K, A_V, A_O = 0, W_MQ, 2 * W_MQ, 2 * W_MQ + W_MV
R_AK, R_AV, R_GM, R_GA = 0, W_AKV, 2 * W_AKV, 2 * W_AKV + D_MODEL

ROW_TILE = 512
MIXER_TILE = 256
MIXER_PROJ_PER_HEAD = 1
MIXER_PROJ_PER_ATTN = 3
FF_CHUNK = 256
MLSTM_CHUNK = 128
SAMPLE_TQ = BF16_SUBLANES
SWA_PROMPT_BLOCKS = 2
SWA_SAMPLE_SEQS = 4

_NT = (((1,), (1,)), ((), ()))


def _params():
    return pltpu.CompilerParams(vmem_limit_bytes=VMEM_LIMIT_BYTES)


def _resident(shape):
    return pl.BlockSpec(shape, lambda *_: (0,) * len(shape), pipeline_mode=pl.Buffered(1))


def _rows(ref):
    v = ref[...]
    return v.reshape(v.shape[-2], v.shape[-1])


def _layer_norm(y, g, b, eps):
    mu = jnp.mean(y, axis=-1, keepdims=True)
    d = y - mu
    var = jnp.mean(d * d, axis=-1, keepdims=True)
    return d * lax.rsqrt(var + eps) * g + b


def _sigmoid(x):
    return 1.0 / (1.0 + jnp.exp(-x))


def _mod_specs(per_row, tm, tiles_per_batch, chunks):
    if per_row:
        return [pl.BlockSpec((tm, D_MODEL), lambda i, c=c: (i, c)) for c in chunks]
    return [pl.BlockSpec((1, 1, D_MODEL), lambda i, c=c: (i // tiles_per_batch, 0, c)) for c in chunks]


def _ada_kernel(c_ref, w_ref, b_ref, o_ref):
    c = c_ref[...]
    s = (c * _sigmoid(c)).astype(BF16)
    o_ref[...] = jnp.dot(s, w_ref[...].astype(BF16), preferred_element_type=F32) + b_ref[...]


def _ada(c_all, w_ada, b_ada):
    rows = c_all.shape[0]
    n_out = w_ada.shape[1]
    tn = D_MODEL
    return pl.pallas_call(
        _ada_kernel,
        grid=(n_out // tn,),
        in_specs=[pl.BlockSpec((rows, D_MODEL), lambda j: (0, 0)),
                  pl.BlockSpec((D_MODEL, tn), lambda j: (0, j)),
                  pl.BlockSpec((1, tn), lambda j: (0, j))],
        out_specs=pl.BlockSpec((rows, tn), lambda j: (0, j)),
        out_shape=jax.ShapeDtypeStruct((rows, n_out), F32),
        compiler_params=_params(),
        name="ada",
    )(c_all, w_ada, b_ada.reshape(1, n_out))


def _ffn_kernel(x_ref, sh_ref, sc_ref, g_ref, wup_ref, wdn_ref, lg_ref, lb_ref, o_ref, act_ref):
    x = x_ref[...]
    h = (x * (1.0 + _rows(sc_ref)) + _rows(sh_ref)).astype(BF16)
    for c in range(0, D_FF, FF_CHUNK):
        a = jnp.dot(h, wup_ref[:, c:c + FF_CHUNK], preferred_element_type=F32)
        u = jnp.dot(h, wup_ref[:, D_FF + c:D_FF + c + FF_CHUNK], preferred_element_type=F32)
        act_ref[:, c:c + FF_CHUNK] = (a * _sigmoid(a) * u).astype(BF16)
    f = jnp.dot(act_ref[...], wdn_ref[...], preferred_element_type=F32)
    y = DEEPNORM_ALPHA * x + (0.5 * (1.0 + _rows(g_ref))) * f
    o_ref[...] = _layer_norm(y, lg_ref[...], lb_ref[...], LN_EPS)


def _ffn(x, mod, per_row, rows_per_batch, chunks, w_up, w_down, ln_g, ln_b):
    m = x.shape[0]
    tm = ROW_TILE
    return pl.pallas_call(
        _ffn_kernel,
        grid=(m // tm,),
        in_specs=[pl.BlockSpec((tm, D_MODEL), lambda i: (i, 0))]
        + _mod_specs(per_row, tm, rows_per_batch // tm, chunks)
        + [_resident((D_MODEL, 2 * D_FF)), _resident((D_FF, D_MODEL)),
           _resident((1, D_MODEL)), _resident((1, D_MODEL))],
        out_specs=pl.BlockSpec((tm, D_MODEL), lambda i: (i, 0)),
        out_shape=jax.ShapeDtypeStruct((m, D_MODEL), F32),
        scratch_shapes=[pltpu.VMEM((tm, D_FF), BF16)],
        compiler_params=_params(),
        name="ffn",
    )(x, mod, mod, mod, w_up, w_down, ln_g.reshape(1, D_MODEL), ln_b.reshape(1, D_MODEL))


def _proj_parts(x_ref, sh_ref, sc_ref, wa_ref, wq_ref, wr_ref, wif_ref, bif_ref, cos_ref, sin_ref,
                q_ref, k_ref, v_ref, o_ref, aq_ref, ak_ref, av_ref, gm_ref, ga_ref, gt_ref):
    x = x_ref[...]
    tm = x.shape[0]
    h = (x * (1.0 + _rows(sc_ref)) + _rows(sh_ref)).astype(BF16)
    lane = lax.broadcasted_iota(jnp.int32, (tm, LANES), 1)

    def seg(w_ref, lo, width=256):
        return jnp.dot(h, w_ref[:, lo:lo + width], preferred_element_type=F32)

    def plain(dst_ref, w_ref, lo, c, scale=None):
        def run():
            z = seg(w_ref, lo + c)
            dst_ref[:, c:c + 256] = (z if scale is None else z * scale).astype(dst_ref.dtype)
        return run

    def gate(dst_ref, w_ref, lo, c):
        def run():
            dst_ref[:, c:c + 256] = _sigmoid(seg(w_ref, lo + c)).astype(BF16)
        return run

    def forget_input_gates():
        zg = jnp.dot(h, wif_ref[...], preferred_element_type=F32) + bif_ref[...]
        logsig = jnp.minimum(zg, 0.0) - jnp.log(1.0 + jnp.exp(-jnp.abs(zg)))
        gt_ref[...] = jnp.where(lane < M_HEADS, zg, logsig)

    def rotary(dst_ref, w_ref, lo, c):
        def run():
            cos = cos_ref[...]
            sin = sin_ref[...]
            low_half = (lane & (A_DH // 2)) == 0
            z = seg(w_ref, lo + c)
            for half in range(2):
                zh = z[:, half * LANES:(half + 1) * LANES]
                partner = jnp.where(low_half, pltpu.roll(zh, LANES - A_DH // 2, 1),
                                    pltpu.roll(zh, A_DH // 2, 1))
                dst_ref[:, c + half * LANES:c + (half + 1) * LANES] = (
                    zh * cos + partner * sin).astype(dst_ref.dtype)
        return run

    parts = [plain(q_ref, wa_ref, A_Q, c) for c in range(0, W_MQ, 256)]
    parts += [plain(k_ref, wa_ref, A_K, c, K_SCALE) for c in range(0, W_MQ, 256)]
    parts += [plain(v_ref, wa_ref, A_V, c) for c in range(0, W_MV, 256)]
    parts += [forget_input_gates]
    parts += [rotary(aq_ref, wq_ref, 0, c) for c in range(0, W_AQ, 256)]
    parts += [rotary(ak_ref, wr_ref, R_AK, 0), plain(av_ref, wr_ref, R_AV, 0)]
    parts += [gate(o_ref, wa_ref, A_O, c) for c in range(0, W_MV, 256)]
    parts += [gate(gm_ref, wr_ref, R_GM, c) for c in range(0, D_MODEL, 256)]
    parts += [gate(ga_ref, wr_ref, R_GA, c) for c in range(0, D_MODEL, 256)]
    return parts


def _proj_kernel(*refs):
    for part in _proj_parts(*refs):
        part()


def _proj(x, mod, per_row, rows_per_batch, chunks, w, cos_t, sin_t):
    m = x.shape[0]
    tm = ROW_TILE
    n_pos_tiles = cos_t.shape[0] // tm

    def tok(width):
        return pl.BlockSpec((tm, width), lambda i: (i, 0))

    widths = (W_MQ, W_MQ, W_MV, W_MV, W_AQ, W_AKV, W_AKV, D_MODEL, D_MODEL, LANES)
    dtypes = (BF16, BF16, BF16, BF16, BF16, F32, F32, BF16, BF16, F32)
    weights = (w["w_a"], w["w_aq"], w["w_r"], w["w_if"], w["bif"])
    return pl.pallas_call(
        _proj_kernel,
        grid=(m // tm,),
        in_specs=[tok(D_MODEL)] + _mod_specs(per_row, tm, rows_per_batch // tm, chunks)
        + [_resident(a.shape) for a in weights]
        + [pl.BlockSpec((tm, LANES), lambda i: (i % n_pos_tiles, 0)),
           pl.BlockSpec((tm, LANES), lambda i: (i % n_pos_tiles, 0))],
        out_specs=[tok(wd) for wd in widths],
        out_shape=[jax.ShapeDtypeStruct((m, wd), d) for wd, d in zip(widths, dtypes)],
        compiler_params=_params(),
        name="proj",
    )(x, mod, mod, *weights, cos_t, sin_t)


def _mlstm_kernel(*refs, nb, tpb, hps, zero_init):
    if zero_init:
        (q_ref, k_ref, v_ref, g_ref, mo_ref, ng_ref,
         out_ref, c_ref, n_ref, m_ref, gt_s, ct_s) = refs
    else:
        (q_ref, k_ref, v_ref, g_ref, mo_ref, ng_ref, c0_ref, n0_ref, m0_ref,
         out_ref, c_ref, n_ref, m_ref, gt_s, ct_s) = refs
    first_chunk = pl.program_id(2) == 0

    @pl.when(first_chunk)
    def _():
        if zero_init:
            c_ref[...] = jnp.zeros_like(c_ref)
            n_ref[...] = jnp.zeros_like(n_ref)
        else:
            c_ref[...] = c0_ref[...]
            n_ref[...] = n0_ref[...]

    @pl.when(first_chunk & (pl.program_id(1) == 0))
    def _():
        m_ref[...] = jnp.zeros_like(m_ref) if zero_init else m0_ref[...]

    head0 = 0 if hps == M_HEADS else pl.program_id(1) * hps
    _mlstm_chunk(q_ref, k_ref, v_ref, g_ref, mo_ref, ng_ref, out_ref, c_ref, n_ref, m_ref, gt_s, ct_s,
                 nb=nb, tpb=tpb, hps=hps, head0=head0)


def _mlstm_chunk(q_ref, k_ref, v_ref, g_ref, mo_ref, ng_ref, out_ref, c_ref, n_ref, m_ref, gt_s, ct_s,
                 *, nb, tpb, hps, head0, between=None):
    L = nb * tpb
    shift = tpb.bit_length() - 1
    row = lax.broadcasted_iota(jnp.int32, (L, L), 0)
    col = lax.broadcasted_iota(jnp.int32, (L, L), 1)
    same = (row >> shift) == (col >> shift)
    causal = same & (col <= row)
    lane = lax.broadcasted_iota(jnp.int32, (L, LANES), 1)
    row_seq = lax.broadcasted_iota(jnp.int32, (L, 1), 0) >> shift

    def lane_col(x, idx):
        return jnp.sum(jnp.where(lane == idx, x, 0.0), axis=1, keepdims=True)

    gates = g_ref[...]
    tri = jnp.where(causal, 1.0, 0.0).astype(BF16)
    g_hi = gates.astype(BF16)
    rem = gates - g_hi.astype(F32)
    g_mid = rem.astype(BF16)
    g_lo = (rem - g_mid.astype(F32)).astype(BF16)
    cum = (jnp.dot(tri, g_hi, preferred_element_type=F32)
           + jnp.dot(tri, g_mid, preferred_element_type=F32)
           + jnp.dot(tri, g_lo, preferred_element_type=F32))
    gt_s[...] = gates.T
    ct_s[...] = cum.T
    m_rows = m_ref[...]
    m_next = m_rows

    for hl in range(hps):
        if between is not None:
            between()
        head = head0 + hl
        qs = slice(hl * M_DQK, (hl + 1) * M_DQK)
        vs = slice(hl * M_DV, (hl + 1) * M_DV)
        q = q_ref[:, qs]
        k = k_ref[:, qs]
        v = v_ref[:, vs]
        b_c = lane_col(cum, M_HEADS + head)
        m_p = lane_col(m_rows, head)
        i_r = gt_s[pl.ds(head, 1), :]
        b_r = ct_s[pl.ds(M_HEADS + head, 1), :]

        log_d = jnp.where(causal, b_c - b_r + i_r, -jnp.inf)
        m_t = jnp.maximum(b_c + m_p, jnp.max(log_d, axis=1, keepdims=True))
        dmat = jnp.exp(log_d - m_t)
        e_int = jnp.exp(b_c + m_p - m_t)

        if nb == 1:
            last = slice(L - 1, L)
            m_new = jnp.broadcast_to(m_t[last], (L, 1))
            e_c = jnp.broadcast_to(e_int[last], (L, 1))
            w_mat = jnp.broadcast_to(dmat[last], (L, L))
        else:
            b_last = jnp.min(jnp.where(same, b_r, jnp.inf), axis=1, keepdims=True)
            log_w = jnp.where(same, b_last - b_r + i_r, -jnp.inf)
            m_new = jnp.maximum(b_last + m_p, jnp.max(log_w, axis=1, keepdims=True))
            w_mat = jnp.exp(log_w - m_new)
            e_c = jnp.exp(b_last + m_p - m_new)
        e_cb = jnp.broadcast_to(e_c, (L, LANES))

        s = lax.dot_general(q, k, _NT, preferred_element_type=F32) * dmat
        v_t = v.astype(F32).T

        inter = jnp.zeros((L, M_DV), F32)
        for j in range(nb):
            first = slice(j * tpb, j * tpb + 1)
            c_j = c_ref[j, hl]
            q_j = q if nb == 1 else jnp.where(row_seq == j, q, jnp.zeros_like(q))
            inter = inter + lax.dot_general(q_j, c_j.astype(BF16), _NT, preferred_element_type=F32)
            lhs = (v_t * w_mat[first]).astype(BF16)
            c_ref[j, hl] = e_cb[first] * c_j + jnp.dot(lhs, k, preferred_element_type=F32)

        n_rows = n_ref[:, qs]
        qn = jnp.sum(q.astype(F32) * n_rows, axis=1, keepdims=True)
        num = jnp.dot(s.astype(BF16), v, preferred_element_type=F32) + e_int * inter
        den = jnp.sum(s, axis=1, keepdims=True) + e_int * qn
        hh = num / jnp.maximum(jnp.abs(den), jnp.exp(-m_t))

        mu = jnp.mean(hh, axis=1, keepdims=True)
        dlt = hh - mu
        var = jnp.mean(dlt * dlt, axis=1, keepdims=True)
        y = dlt * lax.rsqrt(var + HEAD_NORM_EPS)
        out_ref[:, vs] = (y * ng_ref[:, vs] * mo_ref[:, vs].astype(F32)).astype(BF16)

        n_ref[:, qs] = e_c * n_rows + jnp.dot(w_mat.astype(BF16), k, preferred_element_type=F32)
        m_next = jnp.where(lane == head, m_new, m_next)

    m_ref[...] = m_next


def _mlstm(q, k, v, gates, mo, norm_g, state, nb, tpb, n_chunks, hps):
    m = q.shape[0]
    L = nb * tpb
    n_blocks = m // (L * n_chunks)
    hd = M_HEADS * M_DQK

    def tok(width):
        return pl.BlockSpec((L, width), lambda b, g, c: (b * n_chunks + c, g))

    c_spec = pl.BlockSpec((nb, hps, M_DV, M_DQK), lambda b, g, c: (b, g, 0, 0))
    n_spec = pl.BlockSpec((L, hps * M_DQK), lambda b, g, c: (b, g))
    m_spec = pl.BlockSpec((L, LANES), lambda b, g, c: (b, 0))
    in_specs = [tok(hps * M_DQK), tok(hps * M_DQK), tok(hps * M_DV),
                pl.BlockSpec((L, LANES), lambda b, g, c: (b * n_chunks + c, 0)),
                tok(hps * M_DV), pl.BlockSpec((1, hps * M_DV), lambda b, g, c: (0, g))]
    args = [q, k, v, gates, mo, norm_g.reshape(1, M_HEADS * M_DV)]
    if state is not None:
        in_specs += [c_spec, n_spec, m_spec]
        args += list(state)
    return pl.pallas_call(
        functools.partial(_mlstm_kernel, nb=nb, tpb=tpb, hps=hps, zero_init=state is None),
        grid=(n_blocks, M_HEADS // hps, n_chunks),
        in_specs=in_specs,
        out_specs=[tok(hps * M_DV), c_spec, n_spec, m_spec],
        out_shape=[jax.ShapeDtypeStruct((m, M_HEADS * M_DV), BF16),
                   jax.ShapeDtypeStruct((n_blocks * nb, M_HEADS, M_DV, M_DQK), F32),
                   jax.ShapeDtypeStruct((n_blocks * L, hd), F32),
                   jax.ShapeDtypeStruct((n_blocks * L, LANES), F32)],
        scratch_shapes=[pltpu.VMEM((LANES, L), F32), pltpu.VMEM((LANES, L), F32)],
        compiler_params=_params(),
        name="mlstm",
    )(*args)


def _swa_kernel(*refs, nbb, tq, chained, n_new):
    if n_new:
        sink_ref, q_ref, kp_ref, kc_ref, vp_ref, vc_ref, o_ref, kn_ref, vn_ref = refs
    else:
        sink_ref, q_ref, kp_ref, kc_ref, vp_ref, vc_ref, o_ref = refs
    qi = lax.broadcasted_iota(jnp.int32, (tq, WINDOW), 0)
    kj = lax.broadcasted_iota(jnp.int32, (tq, WINDOW), 1)

    def pad_keys(x):
        if x.shape[0] == WINDOW:
            return x
        return jnp.concatenate([x, jnp.zeros((WINDOW - x.shape[0], x.shape[1]), x.dtype)], axis=0)

    for jb in range(nbb):
        kc, vc = pad_keys(kc_ref[jb]), pad_keys(vc_ref[jb])
        valid_prev = kj > qi
        if chained and jb > 0:
            kp, vp = kc_ref[jb - 1], vc_ref[jb - 1]
        else:
            kp, vp = kp_ref[jb], vp_ref[jb]
            if chained:
                valid_prev = valid_prev & (pl.program_id(1) > 0)

        if n_new:
            keep_old = lax.broadcasted_iota(jnp.int32, (WINDOW, W_AKV), 0) < WINDOW - n_new
            kn_ref[jb] = jnp.where(keep_old, pltpu.roll(kp, WINDOW - n_new, 0),
                                   pltpu.roll(kc, WINDOW - n_new, 0))
            vn_ref[jb] = jnp.where(keep_old, pltpu.roll(vp, WINDOW - n_new, 0),
                                   pltpu.roll(vc, WINDOW - n_new, 0))

        _swa_block(sink_ref, q_ref[jb], kp, vp, kc, vc, valid_prev, o_ref.at[jb])


def _swa_block(sink_ref, q, kp, vp, kc, vc, valid_prev, o_ref, between=None):
    tq = q.shape[0]
    reps = A_HEADS // A_KV_HEADS
    low_q = lax.broadcasted_iota(jnp.int32, (tq, LANES), 1) < A_DH
    low_k = lax.broadcasted_iota(jnp.int32, (WINDOW, LANES), 1) < A_DH
    key_row = lax.broadcasted_iota(jnp.int32, (WINDOW, LANES), 0)
    qi = lax.broadcasted_iota(jnp.int32, (tq, WINDOW), 0)
    kj = lax.broadcasted_iota(jnp.int32, (tq, WINDOW), 1)
    sink_lane = kj == 0
    valid_cur = kj <= qi
    scale = jnp.asarray(ATTN_SCALE, BF16)
    neg_inf = -jnp.inf

    def block_diag(x):
        return jnp.concatenate([jnp.where(low_k, x, 0.0), jnp.where(low_k, 0.0, x)], axis=0).astype(BF16)

    for p in range(A_KV_HEADS // 2):
        if between is not None:
            between()
        ks = slice(p * LANES, (p + 1) * LANES)
        k_prev, k_cur = block_diag(kp[:, ks]), block_diag(kc[:, ks])
        v_prev = block_diag(jnp.where(key_row == 0, 0.0, vp[:, ks]))
        v_cur = block_diag(vc[:, ks])
        q4 = jnp.concatenate([q[:, (reps * p + r) * LANES:(reps * p + r + 1) * LANES] * scale
                              for r in range(reps)], axis=0)
        sp = lax.dot_general(q4, k_prev, _NT, preferred_element_type=F32)
        sc = lax.dot_general(q4, k_cur, _NT, preferred_element_type=F32)
        pps, pcs, invs = [], [], []
        for r in range(reps):
            rows = slice(r * tq, (r + 1) * tq)
            pp_r, pc_r, inv_r = [], [], []
            for half in range(2):
                cols = slice(half * WINDOW, (half + 1) * WINDOW)
                head = 2 * reps * p + reps * half + r
                sp_i = jnp.where(sink_lane, sink_ref[head], jnp.where(valid_prev, sp[rows, cols], neg_inf))
                sc_i = jnp.where(valid_cur, sc[rows, cols], neg_inf)
                mx = jnp.max(jnp.maximum(sp_i, sc_i), axis=1, keepdims=True)
                pp = jnp.exp(sp_i - mx)
                pc = jnp.exp(sc_i - mx)
                inv_r.append(1.0 / jnp.sum(pp + pc, axis=1, keepdims=True))
                pp_r.append(pp.astype(BF16))
                pc_r.append(pc.astype(BF16))
            pps.append(jnp.concatenate(pp_r, axis=1))
            pcs.append(jnp.concatenate(pc_r, axis=1))
            invs.append(jnp.where(low_q, inv_r[0], inv_r[1]))
        if between is not None:
            between()
        o4 = (jnp.dot(jnp.concatenate(pps, axis=0), v_prev, preferred_element_type=F32)
              + jnp.dot(jnp.concatenate(pcs, axis=0), v_cur, preferred_element_type=F32))
        for r in range(reps):
            blk = reps * p + r
            o_ref[:, blk * LANES:(blk + 1) * LANES] = (o4[r * tq:(r + 1) * tq] * invs[r]).astype(BF16)


def _swa(sinks, q, k_prev_src, k_cur, v_prev_src, v_cur, *, nbb, chained, n_new, blocks_per_seq):
    n, tq, _ = q.shape
    tk = k_cur.shape[1]
    if chained:
        steps = blocks_per_seq // nbb
        grid = (n // blocks_per_seq, steps)
        cur_index = lambda b, i: (b * steps + i, 0, 0)
        prev_spec = pl.BlockSpec((1, WINDOW, W_AKV),
                                 lambda b, i: (b * blocks_per_seq + jnp.maximum(i * nbb - 1, 0), 0, 0))
    else:
        grid = (n // nbb,)
        cur_index = lambda i: (i, 0, 0)
        prev_spec = pl.BlockSpec((nbb, WINDOW, W_AKV), cur_index)
    cur_spec = pl.BlockSpec((nbb, tk, W_AKV), cur_index)
    q_spec = pl.BlockSpec((nbb, tq, W_AQ), cur_index)
    out_specs = [q_spec]
    out_shape = [jax.ShapeDtypeStruct((n, tq, W_AQ), BF16)]
    if n_new:
        out_specs += [prev_spec, prev_spec]
        out_shape += [jax.ShapeDtypeStruct((n, WINDOW, W_AKV), F32)] * 2
    return pl.pallas_call(
        functools.partial(_swa_kernel, nbb=nbb, tq=tq, chained=chained, n_new=n_new),
        grid=grid,
        in_specs=[pl.BlockSpec(memory_space=pltpu.SMEM), q_spec, prev_spec, cur_spec, prev_spec, cur_spec],
        out_specs=out_specs,
        out_shape=out_shape,
        compiler_params=_params(),
        name="swa",
    )(sinks, q, k_prev_src, k_cur, v_prev_src, v_cur)


def _merge_kernel(x_ref, g_ref, hm_ref, oa_ref, gm_ref, ga_ref, wm_ref, wa_ref, wo_ref,
                  lg_ref, lb_ref, o_ref):
    ym = jnp.dot(hm_ref[...], wm_ref[...], preferred_element_type=F32)
    ya = jnp.dot(oa_ref[...], wa_ref[...], preferred_element_type=F32)
    mix = gm_ref[...].astype(F32) * ym + ga_ref[...].astype(F32) * ya
    t = jnp.dot(mix.astype(BF16), wo_ref[...], preferred_element_type=F32)
    y = DEEPNORM_ALPHA * x_ref[...] + (1.0 + _rows(g_ref)) * t
    o_ref[...] = _layer_norm(y, lg_ref[...], lb_ref[...], LN_EPS)


def _merge(x, mod, per_row, rows_per_batch, chunk, hm, oa, gm, ga, wm, wa, wo, ln_g, ln_b):
    m = x.shape[0]
    tm = ROW_TILE

    def tok():
        return pl.BlockSpec((tm, D_MODEL), lambda i: (i, 0))

    return pl.pallas_call(
        _merge_kernel,
        grid=(m // tm,),
        in_specs=[tok()] + _mod_specs(per_row, tm, rows_per_batch // tm, (chunk,))
        + [tok(), tok(), tok(), tok()]
        + [_resident((D_MODEL, D_MODEL))] * 3 + [_resident((1, D_MODEL))] * 2,
        out_specs=tok(),
        out_shape=jax.ShapeDtypeStruct((m, D_MODEL), F32),
        compiler_params=_params(),
        name="merge",
    )(x, mod, hm, oa, gm, ga, wm, wa, wo, ln_g.reshape(1, D_MODEL), ln_b.reshape(1, D_MODEL))


def _mixer_kernel(sink_ref, xa_ref, xb_ref, sh_ref, sc_ref, g_ref, wa_ref, wq_ref, wr_ref, wif_ref, bif_ref,
                  cos_ref, sin_ref, ng_ref, wm_ref, wba_ref, wo_ref, lg_ref, lb_ref,
                  y_ref, c_ref, n_ref, m_ref, kb_ref, vb_ref, *scratch, tiles_per_seq):
    n_staged = (len(scratch) - 6) // 2
    even, odd = scratch[:n_staged], scratch[n_staged:2 * n_staged]
    kprev_s, vprev_s, hm_s, oa_s, gtt_s, ctt_s = scratch[2 * n_staged:]
    tm = xa_ref.shape[0]
    s = pl.program_id(0)
    tile_b = jnp.maximum(s - 1, 0)
    seq_start = tile_b % tiles_per_seq == 0

    @pl.when(s == 0)
    def _():
        for ref in odd:
            ref[...] = jnp.zeros_like(ref)

    @pl.when(seq_start)
    def _():
        c_ref[...] = jnp.zeros_like(c_ref)
        n_ref[...] = jnp.zeros_like(n_ref)
        m_ref[...] = jnp.zeros_like(m_ref)
        kprev_s[...] = jnp.zeros_like(kprev_s)
        vprev_s[...] = jnp.zeros_like(vprev_s)

    def step(produce, consume):
        proj_parts = _proj_parts(xa_ref, sh_ref, sc_ref, wa_ref, wq_ref, wr_ref, wif_ref, bif_ref,
                                 cos_ref, sin_ref, *produce)
        pending = iter(proj_parts)

        def emit(count):
            def between():
                for _ in range(count):
                    part = next(pending, None)
                    if part is not None:
                        part()
            return between

        q_s, k_s, v_s, mo_s, aq_s, ak_s, av_s, gm_s, ga_s, gt_s = consume
        qi = lax.broadcasted_iota(jnp.int32, (WINDOW, WINDOW), 0)
        kj = lax.broadcasted_iota(jnp.int32, (WINDOW, WINDOW), 1)
        for j in range(tm // MLSTM_CHUNK):
            rows = pl.ds(j * MLSTM_CHUNK, MLSTM_CHUNK)
            _mlstm_chunk(q_s.at[rows], k_s.at[rows], v_s.at[rows], gt_s.at[rows], mo_s.at[rows], ng_ref,
                         hm_s.at[rows], c_ref, n_ref, m_ref, gtt_s, ctt_s,
                         nb=1, tpb=MLSTM_CHUNK, hps=M_HEADS, head0=0, between=emit(MIXER_PROJ_PER_HEAD))
            valid_prev = kj > qi
            if j == 0:
                valid_prev = valid_prev & jnp.logical_not(seq_start)
                k_prev, v_prev = kprev_s[...], vprev_s[...]
            else:
                before = pl.ds((j - 1) * WINDOW, WINDOW)
                k_prev, v_prev = ak_s[before], av_s[before]
            _swa_block(sink_ref, aq_s[rows], k_prev, v_prev, ak_s[rows], av_s[rows], valid_prev,
                       oa_s.at[rows], between=emit(MIXER_PROJ_PER_ATTN))

        for part in pending:
            part()
        _merge_kernel(xb_ref, g_ref, hm_s, oa_s, gm_s, ga_s, wm_ref, wba_ref, wo_ref, lg_ref, lb_ref, y_ref)

        last = pl.ds(tm - WINDOW, WINDOW)
        k_last, v_last = ak_s[last], av_s[last]
        kb_ref[0] = k_last
        vb_ref[0] = v_last
        kprev_s[...] = k_last
        vprev_s[...] = v_last

    @pl.when(s % 2 == 0)
    def _():
        step(even, odd)

    @pl.when(s % 2 == 1)
    def _():
        step(odd, even)


def _mixer(x1, mod, w, sinks, norm_g, cos_t, sin_t, bp, sp):
    tm = MIXER_TILE
    tps = sp // tm
    n_tiles = bp * tps
    hd = M_HEADS * M_DQK
    weights = (w["w_a"], w["w_aq"], w["w_r"], w["w_if"], w["bif"])

    def tile_a(s):
        return jnp.minimum(s, n_tiles - 1)

    def tile_b(s):
        return jnp.maximum(s - 1, 0)

    def mod_spec(tile, chunk):
        return pl.BlockSpec((1, 1, D_MODEL), lambda s: (tile(s) // tps, 0, chunk))

    def per_seq(*shape):
        return pl.BlockSpec((1,) + shape, lambda s: (tile_b(s) // tps,) + (0,) * len(shape))

    pos = pl.BlockSpec((tm, LANES), lambda s: (tile_a(s) % tps, 0))
    staged = [(W_MQ, BF16), (W_MQ, BF16), (W_MV, BF16), (W_MV, BF16), (W_AQ, BF16),
              (W_AKV, F32), (W_AKV, F32), (D_MODEL, BF16), (D_MODEL, BF16), (LANES, F32)]
    return pl.pallas_call(
        functools.partial(_mixer_kernel, tiles_per_seq=tps),
        grid=(n_tiles + 1,),
        in_specs=[pl.BlockSpec(memory_space=pltpu.SMEM),
                  pl.BlockSpec((tm, D_MODEL), lambda s: (tile_a(s), 0)),
                  pl.BlockSpec((tm, D_MODEL), lambda s: (tile_b(s), 0)),
                  mod_spec(tile_a, 3), mod_spec(tile_a, 4), mod_spec(tile_b, 5)]
        + [_resident(a.shape) for a in weights] + [pos, pos, _resident((1, M_HEADS * M_DV))]
        + [_resident((D_MODEL, D_MODEL))] * 3 + [_resident((1, D_MODEL))] * 2,
        out_specs=[pl.BlockSpec((tm, D_MODEL), lambda s: (tile_b(s), 0)),
                   per_seq(M_HEADS, M_DV, M_DQK),
                   pl.BlockSpec((MLSTM_CHUNK, hd), lambda s: (tile_b(s) // tps, 0)),
                   pl.BlockSpec((MLSTM_CHUNK, LANES), lambda s: (tile_b(s) // tps, 0)),
                   per_seq(WINDOW, W_AKV), per_seq(WINDOW, W_AKV)],
        out_shape=[jax.ShapeDtypeStruct((bp * sp, D_MODEL), F32),
                   jax.ShapeDtypeStruct((bp, M_HEADS, M_DV, M_DQK), F32),
                   jax.ShapeDtypeStruct((bp * MLSTM_CHUNK, hd), F32),
                   jax.ShapeDtypeStruct((bp * MLSTM_CHUNK, LANES), F32),
                   jax.ShapeDtypeStruct((bp, WINDOW, W_AKV), F32),
                   jax.ShapeDtypeStruct((bp, WINDOW, W_AKV), F32)],
        scratch_shapes=[pltpu.VMEM((tm, width), dtype) for width, dtype in staged + staged]
        + [pltpu.VMEM((WINDOW, W_AKV), F32), pltpu.VMEM((WINDOW, W_AKV), F32),
           pltpu.VMEM((tm, W_MV), BF16), pltpu.VMEM((tm, W_AQ), BF16),
           pltpu.VMEM((LANES, MLSTM_CHUNK), F32), pltpu.VMEM((LANES, MLSTM_CHUNK), F32)],
        compiler_params=_params(),
        name="mixer",
    )(sinks, x1, x1, mod, mod, mod, *weights, cos_t, sin_t, norm_g.reshape(1, M_HEADS * M_DV),
      w["wm"], w["wa"], w["wo"], w["ln2_g"].reshape(1, D_MODEL), w["ln2_b"].reshape(1, D_MODEL))


def _rope_tables(pos):
    half = A_DH // 2
    inv = ROPE_THETA ** (-jnp.arange(half, dtype=F32) / half)
    ang = pos.astype(F32)[:, None] * inv[None, :]
    cos, sin = jnp.cos(ang), jnp.sin(ang)
    return jnp.tile(cos, (1, 4)), jnp.concatenate([-sin, sin, -sin, sin], axis=1)


def _pair_heads(x, axis):
    shape = x.shape
    reps = A_HEADS // A_KV_HEADS
    x = x.reshape(shape[:axis] + (A_KV_HEADS // 2, 2, reps, A_DH) + shape[axis + 1:])
    return jnp.swapaxes(x, axis + 1, axis + 2).reshape(shape)


def _token_stage_1(x, mod, per_row, rows_per_batch, w, pos):
    x1 = _ffn(x, mod, per_row, rows_per_batch, (0, 1, 2), w["up1"], w["down1"], w["ln1_g"], w["ln1_b"])
    cos_t, sin_t = _rope_tables(pos)
    return x1, _proj(x1, mod, per_row, rows_per_batch, (3, 4), w, cos_t, sin_t)


def _token_stage_2(x1, mod, per_row, rows_per_batch, w, hm, oa, gm, ga):
    x2 = _merge(x1, mod, per_row, rows_per_batch, 5, hm, oa, gm, ga,
                w["wm"], w["wa"], w["wo"], w["ln2_g"], w["ln2_b"])
    return _ffn(x2, mod, per_row, rows_per_batch, (6, 7, 8), w["up2"], w["down2"], w["ln3_g"], w["ln3_b"])


def kernel(x_prompt, x_sample, state_mlstm_C, state_mlstm_n, state_mlstm_m, cache_swa_k, cache_swa_v, c_prompt, c_sample, w_ada, b_ada, w_ffn1_up, w_ffn1_down, ln1_g, ln1_b, w_in, b_igate, b_fgate, m_norm_g, sinks, w_branch_m, w_branch_a, w_out, ln2_g, ln2_b, w_ffn2_up, w_ffn2_down, ln3_g, ln3_b):
    assert w_ada.shape[0] == DEPTH == 1
    bp, sp, _ = x_prompt.shape
    bs, ts, _ = x_sample.shape

    win = w_in[0]
    w = dict(
        up1=w_ffn1_up[0].astype(BF16), down1=w_ffn1_down[0].astype(BF16),
        up2=w_ffn2_up[0].astype(BF16), down2=w_ffn2_down[0].astype(BF16),
        w_a=win[:, :IN_IF].astype(BF16),
        w_aq=_pair_heads(win[:, IN_AQ:IN_AK], 1).astype(BF16),
        w_r=win[:, IN_AK:IN_END].astype(BF16),
        w_if=jnp.pad(win[:, IN_IF:IN_AQ], ((0, 0), (0, LANES - 2 * M_HEADS))).astype(BF16),
        bif=jnp.concatenate([b_igate[0], b_fgate[0], jnp.zeros((LANES - 2 * M_HEADS,), F32)]).reshape(1, LANES),
        wm=w_branch_m[0].astype(BF16), wa=_pair_heads(w_branch_a[0], 0).astype(BF16),
        wo=w_out[0].astype(BF16),
        ln1_g=ln1_g[0], ln1_b=ln1_b[0], ln2_g=ln2_g[0], ln2_b=ln2_b[0], ln3_g=ln3_g[0], ln3_b=ln3_b[0],
    )

    ms = bs * ts
    c_all = jnp.concatenate([jnp.repeat(c_sample, ts, axis=0), c_prompt], axis=0)
    mod = _ada(c_all, w_ada[0], b_ada[0])
    mod_p = mod[ms:].reshape(bp, 1, ADA_CHUNKS * D_MODEL)

    mp = bp * sp
    x1p = _ffn(x_prompt.reshape(mp, D_MODEL), mod_p, False, sp, (0, 1, 2),
               w["up1"], w["down1"], w["ln1_g"], w["ln1_b"])
    cos_t, sin_t = _rope_tables(jnp.arange(sp))
    x2p, c_p, n_rows, m_rows, kb_p, vb_p = _mixer(x1p, mod_p, w, sinks[0], m_norm_g[0], cos_t, sin_t, bp, sp)
    y_p = _ffn(x2p, mod_p, False, sp, (6, 7, 8), w["up2"], w["down2"], w["ln3_g"], w["ln3_b"])
    n_p = n_rows.reshape(bp, MLSTM_CHUNK, M_HEADS, M_DQK)[:, 0]
    m_p = m_rows.reshape(bp, MLSTM_CHUNK, LANES)[:, 0, :M_HEADS]
    kb_p = kb_p.reshape(bp, WINDOW, A_KV_HEADS, A_DH)
    vb_p = vb_p.reshape(bp, WINDOW, A_KV_HEADS, A_DH)

    x1s, (qm, km, vm, mo, aq, ak, av, gm, ga, gt) = _token_stage_1(
        x_sample.reshape(ms, D_MODEL), mod, True, ms, w, PAST_LEN + jnp.arange(ms) % ts)
    seqs = MLSTM_CHUNK // ts
    n0_rows = jnp.repeat(state_mlstm_n[0].reshape(bs, M_HEADS * M_DQK), ts, axis=0)
    m0_rows = jnp.repeat(jnp.pad(state_mlstm_m[0], ((0, 0), (0, LANES - M_HEADS))), ts, axis=0)
    hm, c_s, n_rows, m_rows = _mlstm(qm, km, vm, gt, mo, m_norm_g[0],
                                     (state_mlstm_C[0], n0_rows, m0_rows), seqs, ts, 1, 1)
    n_s = n_rows.reshape(bs, ts, M_HEADS, M_DQK)[:, 0]
    m_s = m_rows.reshape(bs, ts, LANES)[:, 0, :M_HEADS]
    pad_t = ((0, 0), (0, SAMPLE_TQ - ts), (0, 0))
    oa, kb_s, vb_s = _swa(
        sinks[0], jnp.pad(aq.reshape(bs, ts, W_AQ), pad_t),
        cache_swa_k[0].reshape(bs, WINDOW, W_AKV), jnp.pad(ak.reshape(bs, ts, W_AKV), pad_t),
        cache_swa_v[0].reshape(bs, WINDOW, W_AKV), jnp.pad(av.reshape(bs, ts, W_AKV), pad_t),
        nbb=SWA_SAMPLE_SEQS, chained=False, n_new=ts, blocks_per_seq=1)
    y_s = _token_stage_2(x1s, mod, True, ms, w, hm, oa[:, :ts].reshape(ms, D_MODEL), gm, ga)
    kb_s = kb_s.reshape(bs, WINDOW, A_KV_HEADS, A_DH)
    vb_s = vb_s.reshape(bs, WINDOW, A_KV_HEADS, A_DH)

    return (y_p.reshape(bp, sp, D_MODEL), y_s.reshape(bs, ts, D_MODEL),
            c_p[None], n_p[None], m_p[None], kb_p[None], vb_p[None],
            c_s[None], n_s[None], m_s[None], kb_s[None], vb_s[None])
```

```python
import functools

import jax
import jax.numpy as jnp
from jax import lax
from jax.experimental import pallas as pl
from jax.experimental.pallas import tpu as pltpu

F32 = jnp.float32
BF16 = jnp.bfloat16

D_MODEL = 1024
D_FF = 2816
DEPTH = 1
M_HEADS = 4
M_DQK = 128
M_DV = 256
A_HEADS = 16
A_KV_HEADS = 4
A_DH = 64
WINDOW = 128
PAST_LEN = 8192
ROPE_THETA = 10000.0
ATTN_SCALE = A_DH ** -0.5
LN_EPS = 1e-5
HEAD_NORM_EPS = 1e-6
ADA_CHUNKS = 9
DEEPNORM_ALPHA = (2.0 * DEPTH) ** 0.25
K_SCALE = M_DQK ** -0.5

LANES = 128
BF16_SUBLANES = 16
VMEM_LIMIT_BYTES = 56 * 1024 * 1024

W_MQ = M_HEADS * M_DQK
W_MV = M_HEADS * M_DV
W_AQ = A_HEADS * A_DH
W_AKV = A_KV_HEADS * A_DH
IN_IF = 2 * W_MQ + 2 * W_MV
IN_AQ = IN_IF + 2 * M_HEADS
IN_AK = IN_AQ + W_AQ
IN_END = IN_AK + 2 * W_AKV + 2 * D_MODEL
A_Q, A_K, A_V, A_O = 0, W_MQ, 2 * W_MQ, 2 * W_MQ + W_MV
R_AK, R_AV, R_GM, R_GA = 0, W_AKV, 2 * W_AKV, 2 * W_AKV + D_MODEL

ROW_TILE = 512
MIXER_TILE = 256
MIXER_PROJ_PER_HEAD = 1
MIXER_PROJ_PER_ATTN = 3
FF_CHUNK = 256
MLSTM_CHUNK = 128
SAMPLE_TQ = BF16_SUBLANES
SWA_PROMPT_BLOCKS = 2
SWA_SAMPLE_SEQS = 16

_NT = (((1,), (1,)), ((), ()))


def _params():
    return pltpu.CompilerParams(vmem_limit_bytes=VMEM_LIMIT_BYTES)


def _resident(shape):
    return pl.BlockSpec(shape, lambda *_: (0,) * len(shape), pipeline_mode=pl.Buffered(1))


def _rows(ref):
    v = ref[...]
    return v.reshape(v.shape[-2], v.shape[-1])


def _layer_norm(y, g, b, eps):
    mu = jnp.mean(y, axis=-1, keepdims=True)
    d = y - mu
    var = jnp.mean(d * d, axis=-1, keepdims=True)
    return d * lax.rsqrt(var + eps) * g + b


def _sigmoid(x):
    return 1.0 / (1.0 + jnp.exp(-x))


def _mod_specs(per_row, tm, tiles_per_batch, chunks):
    if per_row:
        return [pl.BlockSpec((tm, D_MODEL), lambda i, c=c: (i, c)) for c in chunks]
    return [pl.BlockSpec((1, 1, D_MODEL), lambda i, c=c: (i // tiles_per_batch, 0, c)) for c in chunks]


def _ada_kernel(c_ref, w_ref, b_ref, o_ref):
    c = c_ref[...]
    s = (c * _sigmoid(c)).astype(BF16)
    o_ref[...] = jnp.dot(s, w_ref[...].astype(BF16), preferred_element_type=F32) + b_ref[...]


def _ada(c_all, w_ada, b_ada):
    rows = c_all.shape[0]
    n_out = w_ada.shape[1]
    tn = D_MODEL
    return pl.pallas_call(
        _ada_kernel,
        grid=(n_out // tn,),
        in_specs=[pl.BlockSpec((rows, D_MODEL), lambda j: (0, 0)),
                  pl.BlockSpec((D_MODEL, tn), lambda j: (0, j)),
                  pl.BlockSpec((1, tn), lambda j: (0, j))],
        out_specs=pl.BlockSpec((rows, tn), lambda j: (0, j)),
        out_shape=jax.ShapeDtypeStruct((rows, n_out), F32),
        compiler_params=_params(),
        name="ada",
    )(c_all, w_ada, b_ada.reshape(1, n_out))


def _ffn_kernel(x_ref, sh_ref, sc_ref, g_ref, wup_ref, wdn_ref, lg_ref, lb_ref, o_ref, act_ref):
    x = x_ref[...]
    h = (x * (1.0 + _rows(sc_ref)) + _rows(sh_ref)).astype(BF16)
    for c in range(0, D_FF, FF_CHUNK):
        a = jnp.dot(h, wup_ref[:, c:c + FF_CHUNK], preferred_element_type=F32)
        u = jnp.dot(h, wup_ref[:, D_FF + c:D_FF + c + FF_CHUNK], preferred_element_type=F32)
        act_ref[:, c:c + FF_CHUNK] = (a * _sigmoid(a) * u).astype(BF16)
    f = jnp.dot(act_ref[...], wdn_ref[...], preferred_element_type=F32)
    y = DEEPNORM_ALPHA * x + (0.5 * (1.0 + _rows(g_ref))) * f
    o_ref[...] = _layer_norm(y, lg_ref[...], lb_ref[...], LN_EPS)


def _ffn(x, mod, per_row, rows_per_batch, chunks, w_up, w_down, ln_g, ln_b):
    m = x.shape[0]
    tm = ROW_TILE
    return pl.pallas_call(
        _ffn_kernel,
        grid=(m // tm,),
        in_specs=[pl.BlockSpec((tm, D_MODEL), lambda i: (i, 0))]
        + _mod_specs(per_row, tm, rows_per_batch // tm, chunks)
        + [_resident((D_MODEL, 2 * D_FF)), _resident((D_FF, D_MODEL)),
           _resident((1, D_MODEL)), _resident((1, D_MODEL))],
        out_specs=pl.BlockSpec((tm, D_MODEL), lambda i: (i, 0)),
        out_shape=jax.ShapeDtypeStruct((m, D_MODEL), F32),
        scratch_shapes=[pltpu.VMEM((tm, D_FF), BF16)],
        compiler_params=_params(),
        name="ffn",
    )(x, mod, mod, mod, w_up, w_down, ln_g.reshape(1, D_MODEL), ln_b.reshape(1, D_MODEL))


def _proj_parts(x_ref, sh_ref, sc_ref, wa_ref, wq_ref, wr_ref, wif_ref, bif_ref, cos_ref, sin_ref,
                q_ref, k_ref, v_ref, o_ref, aq_ref, ak_ref, av_ref, gm_ref, ga_ref, gt_ref):
    x = x_ref[...]
    tm = x.shape[0]
    h = (x * (1.0 + _rows(sc_ref)) + _rows(sh_ref)).astype(BF16)
    lane = lax.broadcasted_iota(jnp.int32, (tm, LANES), 1)

    def seg(w_ref, lo, width=256):
        return jnp.dot(h, w_ref[:, lo:lo + width], preferred_element_type=F32)

    def plain(dst_ref, w_ref, lo, c, scale=None):
        def run():
            z = seg(w_ref, lo + c)
            dst_ref[:, c:c + 256] = (z if scale is None else z * scale).astype(dst_ref.dtype)
        return run

    def gate(dst_ref, w_ref, lo, c):
        def run():
            dst_ref[:, c:c + 256] = _sigmoid(seg(w_ref, lo + c)).astype(BF16)
        return run

    def forget_input_gates():
        zg = jnp.dot(h, wif_ref[...], preferred_element_type=F32) + bif_ref[...]
        logsig = jnp.minimum(zg, 0.0) - jnp.log(1.0 + jnp.exp(-jnp.abs(zg)))
        gt_ref[...] = jnp.where(lane < M_HEADS, zg, logsig)

    def rotary(dst_ref, w_ref, lo, c):
        def run():
            cos = cos_ref[...]
            sin = sin_ref[...]
            low_half = (lane & (A_DH // 2)) == 0
            z = seg(w_ref, lo + c)
            for half in range(2):
                zh = z[:, half * LANES:(half + 1) * LANES]
                partner = jnp.where(low_half, pltpu.roll(zh, LANES - A_DH // 2, 1),
                                    pltpu.roll(zh, A_DH // 2, 1))
                dst_ref[:, c + half * LANES:c + (half + 1) * LANES] = (
                    zh * cos + partner * sin).astype(dst_ref.dtype)
        return run

    parts = [plain(q_ref, wa_ref, A_Q, c) for c in range(0, W_MQ, 256)]
    parts += [plain(k_ref, wa_ref, A_K, c, K_SCALE) for c in range(0, W_MQ, 256)]
    parts += [plain(v_ref, wa_ref, A_V, c) for c in range(0, W_MV, 256)]
    parts += [forget_input_gates]
    parts += [rotary(aq_ref, wq_ref, 0, c) for c in range(0, W_AQ, 256)]
    parts += [rotary(ak_ref, wr_ref, R_AK, 0), plain(av_ref, wr_ref, R_AV, 0)]
    parts += [gate(o_ref, wa_ref, A_O, c) for c in range(0, W_MV, 256)]
    parts += [gate(gm_ref, wr_ref, R_GM, c) for c in range(0, D_MODEL, 256)]
    parts += [gate(ga_ref, wr_ref, R_GA, c) for c in range(0, D_MODEL, 256)]
    return parts


def _proj_kernel(*refs):
    for part in _proj_parts(*refs):
        part()


def _proj(x, mod, per_row, rows_per_batch, chunks, w, cos_t, sin_t):
    m = x.shape[0]
    tm = ROW_TILE
    n_pos_tiles = cos_t.shape[0] // tm

    def tok(width):
        return pl.BlockSpec((tm, width), lambda i: (i, 0))

    widths = (W_MQ, W_MQ, W_MV, W_MV, W_AQ, W_AKV, W_AKV, D_MODEL, D_MODEL, LANES)
    dtypes = (BF16, BF16, BF16, BF16, BF16, F32, F32, BF16, BF16, F32)
    weights = (w["w_a"], w["w_aq"], w["w_r"], w["w_if"], w["bif"])
    return pl.pallas_call(
        _proj_kernel,
        grid=(m // tm,),
        in_specs=[tok(D_MODEL)] + _mod_specs(per_row, tm, rows_per_batch // tm, chunks)
        + [_resident(a.shape) for a in weights]
        + [pl.BlockSpec((tm, LANES), lambda i: (i % n_pos_tiles, 0)),
           pl.BlockSpec((tm, LANES), lambda i: (i % n_pos_tiles, 0))],
        out_specs=[tok(wd) for wd in widths],
        out_shape=[jax.ShapeDtypeStruct((m, wd), d) for wd, d in zip(widths, dtypes)],
        compiler_params=_params(),
        name="proj",
    )(x, mod, mod, *weights, cos_t, sin_t)


def _mlstm_kernel(*refs, nb, tpb, hps, zero_init):
    if zero_init:
        (q_ref, k_ref, v_ref, g_ref, mo_ref, ng_ref,
         out_ref, c_ref, n_ref, m_ref, gt_s, ct_s) = refs
    else:
        (q_ref, k_ref, v_ref, g_ref, mo_ref, ng_ref, c0_ref, n0_ref, m0_ref,
         out_ref, c_ref, n_ref, m_ref, gt_s, ct_s) = refs
    first_chunk = pl.program_id(2) == 0

    @pl.when(first_chunk)
    def _():
        if zero_init:
            c_ref[...] = jnp.zeros_like(c_ref)
            n_ref[...] = jnp.zeros_like(n_ref)
        else:
            c_ref[...] = c0_ref[...]
            n_ref[...] = n0_ref[...]

    @pl.when(first_chunk & (pl.program_id(1) == 0))
    def _():
        m_ref[...] = jnp.zeros_like(m_ref) if zero_init else m0_ref[...]

    head0 = 0 if hps == M_HEADS else pl.program_id(1) * hps
    _mlstm_chunk(q_ref, k_ref, v_ref, g_ref, mo_ref, ng_ref, out_ref, c_ref, n_ref, m_ref, gt_s, ct_s,
                 nb=nb, tpb=tpb, hps=hps, head0=head0)


def _mlstm_chunk(q_ref, k_ref, v_ref, g_ref, mo_ref, ng_ref, out_ref, c_ref, n_ref, m_ref, gt_s, ct_s,
                 *, nb, tpb, hps, head0, between=None):
    L = nb * tpb
    shift = tpb.bit_length() - 1
    row = lax.broadcasted_iota(jnp.int32, (L, L), 0)
    col = lax.broadcasted_iota(jnp.int32, (L, L), 1)
    same = (row >> shift) == (col >> shift)
    causal = same & (col <= row)
    lane = lax.broadcasted_iota(jnp.int32, (L, LANES), 1)
    row_seq = lax.broadcasted_iota(jnp.int32, (L, 1), 0) >> shift

    def lane_col(x, idx):
        return jnp.sum(jnp.where(lane == idx, x, 0.0), axis=1, keepdims=True)

    gates = g_ref[...]
    tri = jnp.where(causal, 1.0, 0.0).astype(BF16)
    g_hi = gates.astype(BF16)
    rem = gates - g_hi.astype(F32)
    g_mid = rem.astype(BF16)
    g_lo = (rem - g_mid.astype(F32)).astype(BF16)
    cum = (jnp.dot(tri, g_hi, preferred_element_type=F32)
           + jnp.dot(tri, g_mid, preferred_element_type=F32)
           + jnp.dot(tri, g_lo, preferred_element_type=F32))
    gt_s[...] = gates.T
    ct_s[...] = cum.T
    m_rows = m_ref[...]
    m_next = m_rows

    def head_stages(hl):
        head = head0 + hl
        qs = slice(hl * M_DQK, (hl + 1) * M_DQK)
        vs = slice(hl * M_DV, (hl + 1) * M_DV)
        st = {}

        def gate_stage():
            b_c = lane_col(cum, M_HEADS + head)
            m_p = lane_col(m_rows, head)
            i_r = gt_s[pl.ds(head, 1), :]
            b_r = ct_s[pl.ds(M_HEADS + head, 1), :]
            log_d = jnp.where(causal, b_c - b_r + i_r, -jnp.inf)
            m_t = jnp.maximum(b_c + m_p, jnp.max(log_d, axis=1, keepdims=True))
            dmat = jnp.exp(log_d - m_t)
            e_int = jnp.exp(b_c + m_p - m_t)
            if nb == 1:
                last = slice(L - 1, L)
                m_new = jnp.broadcast_to(m_t[last], (L, 1))
                e_c = jnp.broadcast_to(e_int[last], (L, 1))
                w_mat = jnp.broadcast_to(dmat[last], (L, L))
            else:
                b_last = jnp.min(jnp.where(same, b_r, jnp.inf), axis=1, keepdims=True)
                log_w = jnp.where(same, b_last - b_r + i_r, -jnp.inf)
                m_new = jnp.maximum(b_last + m_p, jnp.max(log_w, axis=1, keepdims=True))
                w_mat = jnp.exp(log_w - m_new)
                e_c = jnp.exp(b_last + m_p - m_new)
            st.update(m_t=m_t, dmat=dmat, e_int=e_int, m_new=m_new, e_c=e_c, w_mat=w_mat)

        def score_stage():
            q, k = q_ref[:, qs], k_ref[:, qs]
            st["s"] = lax.dot_general(q, k, _NT, preferred_element_type=F32) * st.pop("dmat")

        def state_stage():
            q, k = q_ref[:, qs], k_ref[:, qs]
            v_t = v_ref[:, vs].astype(F32).T
            w_mat = st["w_mat"]
            e_cb = jnp.broadcast_to(st["e_c"], (L, LANES))
            inter = jnp.zeros((L, M_DV), F32)
            for j in range(nb):
                first = slice(j * tpb, j * tpb + 1)
                c_j = c_ref[j, hl]
                q_j = q if nb == 1 else jnp.where(row_seq == j, q, jnp.zeros_like(q))
                inter = inter + lax.dot_general(q_j, c_j.astype(BF16), _NT, preferred_element_type=F32)
                lhs = (v_t * w_mat[first]).astype(BF16)
                c_ref[j, hl] = e_cb[first] * c_j + jnp.dot(lhs, k, preferred_element_type=F32)
            st["inter"] = inter

        def output_stage():
            q, k, v = q_ref[:, qs], k_ref[:, qs], v_ref[:, vs]
            s, e_int, m_t = st.pop("s"), st.pop("e_int"), st.pop("m_t")
            n_rows = n_ref[:, qs]
            qn = jnp.sum(q.astype(F32) * n_rows, axis=1, keepdims=True)
            num = jnp.dot(s.astype(BF16), v, preferred_element_type=F32) + e_int * st.pop("inter")
            den = jnp.sum(s, axis=1, keepdims=True) + e_int * qn
            hh = num / jnp.maximum(jnp.abs(den), jnp.exp(-m_t))
            mu = jnp.mean(hh, axis=1, keepdims=True)
            dlt = hh - mu
            var = jnp.mean(dlt * dlt, axis=1, keepdims=True)
            y = dlt * lax.rsqrt(var + HEAD_NORM_EPS)
            out_ref[:, vs] = (y * ng_ref[:, vs] * mo_ref[:, vs].astype(F32)).astype(BF16)
            n_ref[:, qs] = (st.pop("e_c") * n_rows
                            + jnp.dot(st.pop("w_mat").astype(BF16), k, preferred_element_type=F32))

        return (gate_stage, score_stage, state_stage, output_stage), st

    heads = [head_stages(hl) for hl in range(hps)]
    for stage in range(4):
        if between is not None:
            between()
        for stages, _ in heads:
            stages[stage]()
    for hl, (_, st) in enumerate(heads):
        m_next = jnp.where(lane == head0 + hl, st["m_new"], m_next)

    m_ref[...] = m_next


def _mlstm(q, k, v, gates, mo, norm_g, state, nb, tpb, n_chunks, hps):
    m = q.shape[0]
    L = nb * tpb
    n_blocks = m // (L * n_chunks)
    hd = M_HEADS * M_DQK

    def tok(width):
        return pl.BlockSpec((L, width), lambda b, g, c: (b * n_chunks + c, g))

    c_spec = pl.BlockSpec((nb, hps, M_DV, M_DQK), lambda b, g, c: (b, g, 0, 0))
    n_spec = pl.BlockSpec((L, hps * M_DQK), lambda b, g, c: (b, g))
    m_spec = pl.BlockSpec((L, LANES), lambda b, g, c: (b, 0))
    in_specs = [tok(hps * M_DQK), tok(hps * M_DQK), tok(hps * M_DV),
                pl.BlockSpec((L, LANES), lambda b, g, c: (b * n_chunks + c, 0)),
                tok(hps * M_DV), pl.BlockSpec((1, hps * M_DV), lambda b, g, c: (0, g))]
    args = [q, k, v, gates, mo, norm_g.reshape(1, M_HEADS * M_DV)]
    if state is not None:
        in_specs += [c_spec, n_spec, m_spec]
        args += list(state)
    return pl.pallas_call(
        functools.partial(_mlstm_kernel, nb=nb, tpb=tpb, hps=hps, zero_init=state is None),
        grid=(n_blocks, M_HEADS // hps, n_chunks),
        in_specs=in_specs,
        out_specs=[tok(hps * M_DV), c_spec, n_spec, m_spec],
        out_shape=[jax.ShapeDtypeStruct((m, M_HEADS * M_DV), BF16),
                   jax.ShapeDtypeStruct((n_blocks * nb, M_HEADS, M_DV, M_DQK), F32),
                   jax.ShapeDtypeStruct((n_blocks * L, hd), F32),
                   jax.ShapeDtypeStruct((n_blocks * L, LANES), F32)],
        scratch_shapes=[pltpu.VMEM((LANES, L), F32), pltpu.VMEM((LANES, L), F32)],
        compiler_params=_params(),
        name="mlstm",
    )(*args)


def _swa_kernel(*refs, nbb, tq, chained, n_new):
    if n_new:
        sink_ref, q_ref, kp_ref, kc_ref, vp_ref, vc_ref, o_ref, kn_ref, vn_ref = refs
    else:
        sink_ref, q_ref, kp_ref, kc_ref, vp_ref, vc_ref, o_ref = refs
    qi = lax.broadcasted_iota(jnp.int32, (tq, WINDOW), 0)
    kj = lax.broadcasted_iota(jnp.int32, (tq, WINDOW), 1)

    def pad_keys(x):
        if x.shape[0] == WINDOW:
            return x
        return jnp.concatenate([x, jnp.zeros((WINDOW - x.shape[0], x.shape[1]), x.dtype)], axis=0)

    blocks = []
    for jb in range(nbb):
        kc, vc = pad_keys(kc_ref[jb]), pad_keys(vc_ref[jb])
        valid_prev = kj > qi
        if chained and jb > 0:
            kp, vp = kc_ref[jb - 1], vc_ref[jb - 1]
        else:
            kp, vp = kp_ref[jb], vp_ref[jb]
            if chained:
                valid_prev = valid_prev & (pl.program_id(1) > 0)

        if n_new:
            keep_old = lax.broadcasted_iota(jnp.int32, (WINDOW, W_AKV), 0) < WINDOW - n_new
            kn_ref[jb] = jnp.where(keep_old, pltpu.roll(kp, WINDOW - n_new, 0),
                                   pltpu.roll(kc, WINDOW - n_new, 0))
            vn_ref[jb] = jnp.where(keep_old, pltpu.roll(vp, WINDOW - n_new, 0),
                                   pltpu.roll(vc, WINDOW - n_new, 0))

        blocks.append(_swa_stages(sink_ref, q_ref[jb], kp, vp, kc, vc, valid_prev, o_ref.at[jb]))

    for stage in range(3):
        for pairs in blocks:
            for pair in pairs:
                pair[stage]()


def _swa_stages(sink_ref, q, kp, vp, kc, vc, valid_prev, o_ref):
    tq = q.shape[0]
    reps = A_HEADS // A_KV_HEADS
    low_q = lax.broadcasted_iota(jnp.int32, (tq, LANES), 1) < A_DH
    low_k = lax.broadcasted_iota(jnp.int32, (WINDOW, LANES), 1) < A_DH
    key_row = lax.broadcasted_iota(jnp.int32, (WINDOW, LANES), 0)
    qi = lax.broadcasted_iota(jnp.int32, (tq, WINDOW), 0)
    kj = lax.broadcasted_iota(jnp.int32, (tq, WINDOW), 1)
    sink_lane = kj == 0
    valid_cur = kj <= qi
    scale = jnp.asarray(ATTN_SCALE, BF16)
    neg_inf = -jnp.inf

    def block_diag(x):
        return jnp.concatenate([jnp.where(low_k, x, 0.0), jnp.where(low_k, 0.0, x)], axis=0).astype(BF16)

    def pair_stages(p):
        ks = slice(p * LANES, (p + 1) * LANES)
        state = {}

        def scores():
            q4 = jnp.concatenate([q[:, (reps * p + r) * LANES:(reps * p + r + 1) * LANES] * scale
                                  for r in range(reps)], axis=0)
            state["sp"] = lax.dot_general(q4, block_diag(kp[:, ks]), _NT, preferred_element_type=F32)
            state["sc"] = lax.dot_general(q4, block_diag(kc[:, ks]), _NT, preferred_element_type=F32)

        def softmax():
            sp, sc = state.pop("sp"), state.pop("sc")
            pps, pcs, invs = [], [], []
            for r in range(reps):
                rows = slice(r * tq, (r + 1) * tq)
                pp_r, pc_r, inv_r = [], [], []
                for half in range(2):
                    cols = slice(half * WINDOW, (half + 1) * WINDOW)
                    head = 2 * reps * p + reps * half + r
                    sp_i = jnp.where(sink_lane, sink_ref[head], jnp.where(valid_prev, sp[rows, cols], neg_inf))
                    sc_i = jnp.where(valid_cur, sc[rows, cols], neg_inf)
                    mx = jnp.max(jnp.maximum(sp_i, sc_i), axis=1, keepdims=True)
                    pp = jnp.exp(sp_i - mx)
                    pc = jnp.exp(sc_i - mx)
                    inv_r.append(1.0 / jnp.sum(pp + pc, axis=1, keepdims=True))
                    pp_r.append(pp.astype(BF16))
                    pc_r.append(pc.astype(BF16))
                pps.append(jnp.concatenate(pp_r, axis=1))
                pcs.append(jnp.concatenate(pc_r, axis=1))
                invs.append(jnp.where(low_q, inv_r[0], inv_r[1]))
            state.update(pp=jnp.concatenate(pps, axis=0), pc=jnp.concatenate(pcs, axis=0), inv=invs)

        def values():
            v_prev = block_diag(jnp.where(key_row == 0, 0.0, vp[:, ks]))
            o4 = (jnp.dot(state.pop("pp"), v_prev, preferred_element_type=F32)
                  + jnp.dot(state.pop("pc"), block_diag(vc[:, ks]), preferred_element_type=F32))
            invs = state.pop("inv")
            for r in range(reps):
                blk = reps * p + r
                o_ref[:, blk * LANES:(blk + 1) * LANES] = (o4[r * tq:(r + 1) * tq] * invs[r]).astype(BF16)

        return scores, softmax, values

    return [pair_stages(p) for p in range(A_KV_HEADS // 2)]


def _swa_block(sink_ref, q, kp, vp, kc, vc, valid_prev, o_ref, between=None):
    for scores, softmax, values in _swa_stages(sink_ref, q, kp, vp, kc, vc, valid_prev, o_ref):
        if between is not None:
            between()
        scores()
        softmax()
        if between is not None:
            between()
        values()


def _swa(sinks, q, k_prev_src, k_cur, v_prev_src, v_cur, *, nbb, chained, n_new, blocks_per_seq):
    n, tq, _ = q.shape
    tk = k_cur.shape[1]
    if chained:
        steps = blocks_per_seq // nbb
        grid = (n // blocks_per_seq, steps)
        cur_index = lambda b, i: (b * steps + i, 0, 0)
        prev_spec = pl.BlockSpec((1, WINDOW, W_AKV),
                                 lambda b, i: (b * blocks_per_seq + jnp.maximum(i * nbb - 1, 0), 0, 0))
    else:
        grid = (n // nbb,)
        cur_index = lambda i: (i, 0, 0)
        prev_spec = pl.BlockSpec((nbb, WINDOW, W_AKV), cur_index)
    cur_spec = pl.BlockSpec((nbb, tk, W_AKV), cur_index)
    q_spec = pl.BlockSpec((nbb, tq, W_AQ), cur_index)
    out_specs = [q_spec]
    out_shape = [jax.ShapeDtypeStruct((n, tq, W_AQ), BF16)]
    if n_new:
        out_specs += [prev_spec, prev_spec]
        out_shape += [jax.ShapeDtypeStruct((n, WINDOW, W_AKV), F32)] * 2
    return pl.pallas_call(
        functools.partial(_swa_kernel, nbb=nbb, tq=tq, chained=chained, n_new=n_new),
        grid=grid,
        in_specs=[pl.BlockSpec(memory_space=pltpu.SMEM), q_spec, prev_spec, cur_spec, prev_spec, cur_spec],
        out_specs=out_specs,
        out_shape=out_shape,
        compiler_params=_params(),
        name="swa",
    )(sinks, q, k_prev_src, k_cur, v_prev_src, v_cur)


def _merge_kernel(x_ref, g_ref, hm_ref, oa_ref, gm_ref, ga_ref, wm_ref, wa_ref, wo_ref,
                  lg_ref, lb_ref, o_ref):
    ym = jnp.dot(hm_ref[...], wm_ref[...], preferred_element_type=F32)
    ya = jnp.dot(oa_ref[...], wa_ref[...], preferred_element_type=F32)
    mix = gm_ref[...].astype(F32) * ym + ga_ref[...].astype(F32) * ya
    t = jnp.dot(mix.astype(BF16), wo_ref[...], preferred_element_type=F32)
    y = DEEPNORM_ALPHA * x_ref[...] + (1.0 + _rows(g_ref)) * t
    o_ref[...] = _layer_norm(y, lg_ref[...], lb_ref[...], LN_EPS)


def _merge(x, mod, per_row, rows_per_batch, chunk, hm, oa, gm, ga, wm, wa, wo, ln_g, ln_b):
    m = x.shape[0]
    tm = ROW_TILE

    def tok():
        return pl.BlockSpec((tm, D_MODEL), lambda i: (i, 0))

    return pl.pallas_call(
        _merge_kernel,
        grid=(m // tm,),
        in_specs=[tok()] + _mod_specs(per_row, tm, rows_per_batch // tm, (chunk,))
        + [tok(), tok(), tok(), tok()]
        + [_resident((D_MODEL, D_MODEL))] * 3 + [_resident((1, D_MODEL))] * 2,
        out_specs=tok(),
        out_shape=jax.ShapeDtypeStruct((m, D_MODEL), F32),
        compiler_params=_params(),
        name="merge",
    )(x, mod, hm, oa, gm, ga, wm, wa, wo, ln_g.reshape(1, D_MODEL), ln_b.reshape(1, D_MODEL))


def _mixer_kernel(sink_ref, xa_ref, xb_ref, sh_ref, sc_ref, g_ref, wa_ref, wq_ref, wr_ref, wif_ref, bif_ref,
                  cos_ref, sin_ref, ng_ref, wm_ref, wba_ref, wo_ref, lg_ref, lb_ref,
                  y_ref, c_ref, n_ref, m_ref, kb_ref, vb_ref, *scratch, tiles_per_seq):
    n_staged = (len(scratch) - 6) // 2
    even, odd = scratch[:n_staged], scratch[n_staged:2 * n_staged]
    kprev_s, vprev_s, hm_s, oa_s, gtt_s, ctt_s = scratch[2 * n_staged:]
    tm = xa_ref.shape[0]
    s = pl.program_id(0)
    tile_b = jnp.maximum(s - 1, 0)
    seq_start = tile_b % tiles_per_seq == 0

    @pl.when(s == 0)
    def _():
        for ref in odd:
            ref[...] = jnp.zeros_like(ref)

    @pl.when(seq_start)
    def _():
        c_ref[...] = jnp.zeros_like(c_ref)
        n_ref[...] = jnp.zeros_like(n_ref)
        m_ref[...] = jnp.zeros_like(m_ref)
        kprev_s[...] = jnp.zeros_like(kprev_s)
        vprev_s[...] = jnp.zeros_like(vprev_s)

    def step(produce, consume):
        proj_parts = _proj_parts(xa_ref, sh_ref, sc_ref, wa_ref, wq_ref, wr_ref, wif_ref, bif_ref,
                                 cos_ref, sin_ref, *produce)
        pending = iter(proj_parts)

        def emit(count):
            def between():
                for _ in range(count):
                    part = next(pending, None)
                    if part is not None:
                        part()
            return between

        q_s, k_s, v_s, mo_s, aq_s, ak_s, av_s, gm_s, ga_s, gt_s = consume
        qi = lax.broadcasted_iota(jnp.int32, (WINDOW, WINDOW), 0)
        kj = lax.broadcasted_iota(jnp.int32, (WINDOW, WINDOW), 1)
        for j in range(tm // MLSTM_CHUNK):
            rows = pl.ds(j * MLSTM_CHUNK, MLSTM_CHUNK)
            _mlstm_chunk(q_s.at[rows], k_s.at[rows], v_s.at[rows], gt_s.at[rows], mo_s.at[rows], ng_ref,
                         hm_s.at[rows], c_ref, n_ref, m_ref, gtt_s, ctt_s,
                         nb=1, tpb=MLSTM_CHUNK, hps=M_HEADS, head0=0, between=emit(MIXER_PROJ_PER_HEAD))
            valid_prev = kj > qi
            if j == 0:
                valid_prev = valid_prev & jnp.logical_not(seq_start)
                k_prev, v_prev = kprev_s[...], vprev_s[...]
            else:
                before = pl.ds((j - 1) * WINDOW, WINDOW)
                k_prev, v_prev = ak_s[before], av_s[before]
            _swa_block(sink_ref, aq_s[rows], k_prev, v_prev, ak_s[rows], av_s[rows], valid_prev,
                       oa_s.at[rows], between=emit(MIXER_PROJ_PER_ATTN))

        for part in pending:
            part()
        _merge_kernel(xb_ref, g_ref, hm_s, oa_s, gm_s, ga_s, wm_ref, wba_ref, wo_ref, lg_ref, lb_ref, y_ref)

        last = pl.ds(tm - WINDOW, WINDOW)
        k_last, v_last = ak_s[last], av_s[last]
        kb_ref[0] = k_last
        vb_ref[0] = v_last
        kprev_s[...] = k_last
        vprev_s[...] = v_last

    @pl.when(s % 2 == 0)
    def _():
        step(even, odd)

    @pl.when(s % 2 == 1)
    def _():
        step(odd, even)


def _mixer(x1, mod, w, sinks, norm_g, cos_t, sin_t, bp, sp):
    tm = MIXER_TILE
    tps = sp // tm
    n_tiles = bp * tps
    hd = M_HEADS * M_DQK
    weights = (w["w_a"], w["w_aq"], w["w_r"], w["w_if"], w["bif"])

    def tile_a(s):
        return jnp.minimum(s, n_tiles - 1)

    def tile_b(s):
        return jnp.maximum(s - 1, 0)

    def mod_spec(tile, chunk):
        return pl.BlockSpec((1, 1, D_MODEL), lambda s: (tile(s) // tps, 0, chunk))

    def per_seq(*shape):
        return pl.BlockSpec((1,) + shape, lambda s: (tile_b(s) // tps,) + (0,) * len(shape))

    pos = pl.BlockSpec((tm, LANES), lambda s: (tile_a(s) % tps, 0))
    staged = [(W_MQ, BF16), (W_MQ, BF16), (W_MV, BF16), (W_MV, BF16), (W_AQ, BF16),
              (W_AKV, F32), (W_AKV, F32), (D_MODEL, BF16), (D_MODEL, BF16), (LANES, F32)]
    return pl.pallas_call(
        functools.partial(_mixer_kernel, tiles_per_seq=tps),
        grid=(n_tiles + 1,),
        in_specs=[pl.BlockSpec(memory_space=pltpu.SMEM),
                  pl.BlockSpec((tm, D_MODEL), lambda s: (tile_a(s), 0)),
                  pl.BlockSpec((tm, D_MODEL), lambda s: (tile_b(s), 0)),
                  mod_spec(tile_a, 3), mod_spec(tile_a, 4), mod_spec(tile_b, 5)]
        + [_resident(a.shape) for a in weights] + [pos, pos, _resident((1, M_HEADS * M_DV))]
        + [_resident((D_MODEL, D_MODEL))] * 3 + [_resident((1, D_MODEL))] * 2,
        out_specs=[pl.BlockSpec((tm, D_MODEL), lambda s: (tile_b(s), 0)),
                   per_seq(M_HEADS, M_DV, M_DQK),
                   pl.BlockSpec((MLSTM_CHUNK, hd), lambda s: (tile_b(s) // tps, 0)),
                   pl.BlockSpec((MLSTM_CHUNK, LANES), lambda s: (tile_b(s) // tps, 0)),
                   per_seq(WINDOW, W_AKV), per_seq(WINDOW, W_AKV)],
        out_shape=[jax.ShapeDtypeStruct((bp * sp, D_MODEL), F32),
                   jax.ShapeDtypeStruct((bp, M_HEADS, M_DV, M_DQK), F32),
                   jax.ShapeDtypeStruct((bp * MLSTM_CHUNK, hd), F32),
                   jax.ShapeDtypeStruct((bp * MLSTM_CHUNK, LANES), F32),
                   jax.ShapeDtypeStruct((bp, WINDOW, W_AKV), F32),
                   jax.ShapeDtypeStruct((bp, WINDOW, W_AKV), F32)],
        scratch_shapes=[pltpu.VMEM((tm, width), dtype) for width, dtype in staged + staged]
        + [pltpu.VMEM((WINDOW, W_AKV), F32), pltpu.VMEM((WINDOW, W_AKV), F32),
           pltpu.VMEM((tm, W_MV), BF16), pltpu.VMEM((tm, W_AQ), BF16),
           pltpu.VMEM((LANES, MLSTM_CHUNK), F32), pltpu.VMEM((LANES, MLSTM_CHUNK), F32)],
        compiler_params=_params(),
        name="mixer",
    )(sinks, x1, x1, mod, mod, mod, *weights, cos_t, sin_t, norm_g.reshape(1, M_HEADS * M_DV),
      w["wm"], w["wa"], w["wo"], w["ln2_g"].reshape(1, D_MODEL), w["ln2_b"].reshape(1, D_MODEL))


def _rope_tables(pos):
    half = A_DH // 2
    inv = ROPE_THETA ** (-jnp.arange(half, dtype=F32) / half)
    ang = pos.astype(F32)[:, None] * inv[None, :]
    cos, sin = jnp.cos(ang), jnp.sin(ang)
    return jnp.tile(cos, (1, 4)), jnp.concatenate([-sin, sin, -sin, sin], axis=1)


def _pair_heads(x, axis):
    shape = x.shape
    reps = A_HEADS // A_KV_HEADS
    x = x.reshape(shape[:axis] + (A_KV_HEADS // 2, 2, reps, A_DH) + shape[axis + 1:])
    return jnp.swapaxes(x, axis + 1, axis + 2).reshape(shape)


def _token_stage_1(x, mod, per_row, rows_per_batch, w, pos):
    x1 = _ffn(x, mod, per_row, rows_per_batch, (0, 1, 2), w["up1"], w["down1"], w["ln1_g"], w["ln1_b"])
    cos_t, sin_t = _rope_tables(pos)
    return x1, _proj(x1, mod, per_row, rows_per_batch, (3, 4), w, cos_t, sin_t)


def _token_stage_2(x1, mod, per_row, rows_per_batch, w, hm, oa, gm, ga):
    x2 = _merge(x1, mod, per_row, rows_per_batch, 5, hm, oa, gm, ga,
                w["wm"], w["wa"], w["wo"], w["ln2_g"], w["ln2_b"])
    return _ffn(x2, mod, per_row, rows_per_batch, (6, 7, 8), w["up2"], w["down2"], w["ln3_g"], w["ln3_b"])


def kernel(x_prompt, x_sample, state_mlstm_C, state_mlstm_n, state_mlstm_m, cache_swa_k, cache_swa_v, c_prompt, c_sample, w_ada, b_ada, w_ffn1_up, w_ffn1_down, ln1_g, ln1_b, w_in, b_igate, b_fgate, m_norm_g, sinks, w_branch_m, w_branch_a, w_out, ln2_g, ln2_b, w_ffn2_up, w_ffn2_down, ln3_g, ln3_b):
    assert w_ada.shape[0] == DEPTH == 1
    bp, sp, _ = x_prompt.shape
    bs, ts, _ = x_sample.shape

    win = w_in[0]
    w = dict(
        up1=w_ffn1_up[0].astype(BF16), down1=w_ffn1_down[0].astype(BF16),
        up2=w_ffn2_up[0].astype(BF16), down2=w_ffn2_down[0].astype(BF16),
        w_a=win[:, :IN_IF].astype(BF16),
        w_aq=_pair_heads(win[:, IN_AQ:IN_AK], 1).astype(BF16),
        w_r=win[:, IN_AK:IN_END].astype(BF16),
        w_if=jnp.pad(win[:, IN_IF:IN_AQ], ((0, 0), (0, LANES - 2 * M_HEADS))).astype(BF16),
        bif=jnp.concatenate([b_igate[0], b_fgate[0], jnp.zeros((LANES - 2 * M_HEADS,), F32)]).reshape(1, LANES),
        wm=w_branch_m[0].astype(BF16), wa=_pair_heads(w_branch_a[0], 0).astype(BF16),
        wo=w_out[0].astype(BF16),
        ln1_g=ln1_g[0], ln1_b=ln1_b[0], ln2_g=ln2_g[0], ln2_b=ln2_b[0], ln3_g=ln3_g[0], ln3_b=ln3_b[0],
    )

    ms = bs * ts
    c_all = jnp.concatenate([jnp.repeat(c_sample, ts, axis=0), c_prompt], axis=0)
    mod = _ada(c_all, w_ada[0], b_ada[0])
    mod_p = mod[ms:].reshape(bp, 1, ADA_CHUNKS * D_MODEL)

    mp = bp * sp
    x1p = _ffn(x_prompt.reshape(mp, D_MODEL), mod_p, False, sp, (0, 1, 2),
               w["up1"], w["down1"], w["ln1_g"], w["ln1_b"])
    cos_t, sin_t = _rope_tables(jnp.arange(sp))
    x2p, c_p, n_rows, m_rows, kb_p, vb_p = _mixer(x1p, mod_p, w, sinks[0], m_norm_g[0], cos_t, sin_t, bp, sp)
    y_p = _ffn(x2p, mod_p, False, sp, (6, 7, 8), w["up2"], w["down2"], w["ln3_g"], w["ln3_b"])
    n_p = n_rows.reshape(bp, MLSTM_CHUNK, M_HEADS, M_DQK)[:, 0]
    m_p = m_rows.reshape(bp, MLSTM_CHUNK, LANES)[:, 0, :M_HEADS]
    kb_p = kb_p.reshape(bp, WINDOW, A_KV_HEADS, A_DH)
    vb_p = vb_p.reshape(bp, WINDOW, A_KV_HEADS, A_DH)

    x1s, (qm, km, vm, mo, aq, ak, av, gm, ga, gt) = _token_stage_1(
        x_sample.reshape(ms, D_MODEL), mod, True, ms, w, PAST_LEN + jnp.arange(ms) % ts)
    seqs = MLSTM_CHUNK // ts
    n0_rows = jnp.repeat(state_mlstm_n[0].reshape(bs, M_HEADS * M_DQK), ts, axis=0)
    m0_rows = jnp.repeat(jnp.pad(state_mlstm_m[0], ((0, 0), (0, LANES - M_HEADS))), ts, axis=0)
    hm, c_s, n_rows, m_rows = _mlstm(qm, km, vm, gt, mo, m_norm_g[0],
                                     (state_mlstm_C[0], n0_rows, m0_rows), seqs, ts, 1, 1)
    n_s = n_rows.reshape(bs, ts, M_HEADS, M_DQK)[:, 0]
    m_s = m_rows.reshape(bs, ts, LANES)[:, 0, :M_HEADS]
    pad_t = ((0, 0), (0, SAMPLE_TQ - ts), (0, 0))
    oa, kb_s, vb_s = _swa(
        sinks[0], jnp.pad(aq.reshape(bs, ts, W_AQ), pad_t),
        cache_swa_k[0].reshape(bs, WINDOW, W_AKV), jnp.pad(ak.reshape(bs, ts, W_AKV), pad_t),
        cache_swa_v[0].reshape(bs, WINDOW, W_AKV), jnp.pad(av.reshape(bs, ts, W_AKV), pad_t),
        nbb=SWA_SAMPLE_SEQS, chained=False, n_new=ts, blocks_per_seq=1)
    y_s = _token_stage_2(x1s, mod, True, ms, w, hm, oa[:, :ts].reshape(ms, D_MODEL), gm, ga)
    kb_s = kb_s.reshape(bs, WINDOW, A_KV_HEADS, A_DH)
    vb_s = vb_s.reshape(bs, WINDOW, A_KV_HEADS, A_DH)

    return (y_p.reshape(bp, sp, D_MODEL), y_s.reshape(bs, ts, D_MODEL),
            c_p[None], n_p[None], m_p[None], kb_p[None], vb_p[None],
            c_s[None], n_s[None], m_s[None], kb_s[None], vb_s[None])
```

```python
import functools

import jax
import jax.numpy as jnp
from jax import lax
from jax.experimental import pallas as pl
from jax.experimental.pallas import tpu as pltpu

F32 = jnp.float32
BF16 = jnp.bfloat16

D_MODEL = 1024
D_FF = 2816
DEPTH = 1
M_HEADS = 4
M_DQK = 128
M_DV = 256
A_HEADS = 16
A_KV_HEADS = 4
A_DH = 64
WINDOW = 128
PAST_LEN = 8192
ROPE_THETA = 10000.0
ATTN_SCALE = A_DH ** -0.5
LN_EPS = 1e-5
HEAD_NORM_EPS = 1e-6
ADA_CHUNKS = 9
DEEPNORM_ALPHA = (2.0 * DEPTH) ** 0.25
K_SCALE = M_DQK ** -0.5

LANES = 128
BF16_SUBLANES = 16
VMEM_LIMIT_BYTES = 56 * 1024 * 1024

W_MQ = M_HEADS * M_DQK
W_MV = M_HEADS * M_DV
W_AQ = A_HEADS * A_DH
W_AKV = A_KV_HEADS * A_DH
IN_IF = 2 * W_MQ + 2 * W_MV
IN_AQ = IN_IF + 2 * M_HEADS
IN_AK = IN_AQ + W_AQ
IN_END = IN_AK + 2 * W_AKV + 2 * D_MODEL
A_Q, A_K, A_V, A_O = 0, W_MQ, 2 * W_MQ, 2 * W_MQ + W_MV
R_AK, R_AV, R_GM, R_GA = 0, W_AKV, 2 * W_AKV, 2 * W_AKV + D_MODEL

ROW_TILE = 512
MIXER_TILE = 256
MIXER_PROJ_PER_HEAD = 1
MIXER_PROJ_PER_ATTN = 3
ADA_TILE = 1536
FF_CHUNK = 256
FFN_EPILOGUE_PIECES = 4
MLSTM_CHUNK = 128
SAMPLE_TQ = BF16_SUBLANES
SWA_PROMPT_BLOCKS = 2
SWA_SAMPLE_SEQS = 16

_NT = (((1,), (1,)), ((), ()))


def _params():
    return pltpu.CompilerParams(vmem_limit_bytes=VMEM_LIMIT_BYTES)


def _resident(shape):
    return pl.BlockSpec(shape, lambda *_: (0,) * len(shape), pipeline_mode=pl.Buffered(1))


def _rows(ref):
    v = ref[...]
    return v.reshape(v.shape[-2], v.shape[-1])


def _layer_norm(y, g, b, eps):
    mu = jnp.mean(y, axis=-1, keepdims=True)
    d = y - mu
    var = jnp.mean(d * d, axis=-1, keepdims=True)
    return d * lax.rsqrt(var + eps) * g + b


def _sigmoid(x):
    return 1.0 / (1.0 + jnp.exp(-x))


def _mod_specs(per_row, tm, tiles_per_batch, chunks):
    if per_row:
        return [pl.BlockSpec((tm, D_MODEL), lambda i, c=c: (i, c)) for c in chunks]
    return [pl.BlockSpec((1, 1, D_MODEL), lambda i, c=c: (i // tiles_per_batch, 0, c)) for c in chunks]


def _ada_kernel(c_ref, w_ref, b_ref, o_ref):
    c = c_ref[...]
    s = (c * _sigmoid(c)).astype(BF16)
    o_ref[...] = jnp.dot(s, w_ref[...].astype(BF16), preferred_element_type=F32) + b_ref[...]


def _ada(c_all, w_ada, b_ada):
    rows = c_all.shape[0]
    n_out = w_ada.shape[1]
    tn = ADA_TILE
    return pl.pallas_call(
        _ada_kernel,
        grid=(n_out // tn,),
        in_specs=[pl.BlockSpec((rows, D_MODEL), lambda j: (0, 0)),
                  pl.BlockSpec((D_MODEL, tn), lambda j: (0, j)),
                  pl.BlockSpec((1, tn), lambda j: (0, j))],
        out_specs=pl.BlockSpec((rows, tn), lambda j: (0, j)),
        out_shape=jax.ShapeDtypeStruct((rows, n_out), F32),
        compiler_params=_params(),
        name="ada",
    )(c_all, w_ada, b_ada.reshape(1, n_out))


def _ffn_kernel(x_ref, xp_ref, sh_ref, sc_ref, g_ref, wup_ref, wdn_ref, lg_ref, lb_ref, o_ref,
                act_ref, f_ref, *, n_tiles):
    tm = x_ref.shape[0]
    s = pl.program_id(0)
    piece_rows = tm // FFN_EPILOGUE_PIECES

    @pl.when(s == 0)
    def _():
        f_ref[...] = jnp.zeros_like(f_ref)

    def epilogue(piece):
        rows = pl.ds(piece * piece_rows, piece_rows)
        g = _rows(g_ref)
        if g.shape[0] != 1:
            g = g[piece * piece_rows:(piece + 1) * piece_rows]
        y = DEEPNORM_ALPHA * xp_ref[rows, :] + (0.5 * (1.0 + g)) * f_ref[rows, :]
        o_ref[rows, :] = _layer_norm(y, lg_ref[...], lb_ref[...], LN_EPS)

    @pl.when(s < n_tiles)
    def _():
        h = (x_ref[...] * (1.0 + _rows(sc_ref)) + _rows(sh_ref)).astype(BF16)
        chunks = list(range(0, D_FF, FF_CHUNK))
        every = len(chunks) // FFN_EPILOGUE_PIECES
        piece = 0
        for i, c in enumerate(chunks):
            a = jnp.dot(h, wup_ref[:, c:c + FF_CHUNK], preferred_element_type=F32)
            u = jnp.dot(h, wup_ref[:, D_FF + c:D_FF + c + FF_CHUNK], preferred_element_type=F32)
            act_ref[:, c:c + FF_CHUNK] = (a * _sigmoid(a) * u).astype(BF16)
            if i % every == every - 1 and piece < FFN_EPILOGUE_PIECES:
                epilogue(piece)
                piece += 1
        f_ref[...] = jnp.dot(act_ref[...], wdn_ref[...], preferred_element_type=F32)

    @pl.when(s == n_tiles)
    def _():
        for piece in range(FFN_EPILOGUE_PIECES):
            epilogue(piece)


def _ffn(x, mod, per_row, rows_per_batch, chunks, w_up, w_down, ln_g, ln_b):
    m = x.shape[0]
    tm = ROW_TILE
    n_tiles = m // tm
    tpb = rows_per_batch // tm

    def cur(s):
        return jnp.minimum(s, n_tiles - 1)

    def prev(s):
        return jnp.maximum(s - 1, 0)

    def mod_spec(tile, chunk):
        if per_row:
            return pl.BlockSpec((tm, D_MODEL), lambda s: (tile(s), chunk))
        return pl.BlockSpec((1, 1, D_MODEL), lambda s: (tile(s) // tpb, 0, chunk))

    return pl.pallas_call(
        functools.partial(_ffn_kernel, n_tiles=n_tiles),
        grid=(n_tiles + 1,),
        in_specs=[pl.BlockSpec((tm, D_MODEL), lambda s: (cur(s), 0)),
                  pl.BlockSpec((tm, D_MODEL), lambda s: (prev(s), 0)),
                  mod_spec(cur, chunks[0]), mod_spec(cur, chunks[1]), mod_spec(prev, chunks[2]),
                  _resident((D_MODEL, 2 * D_FF)), _resident((D_FF, D_MODEL)),
                  _resident((1, D_MODEL)), _resident((1, D_MODEL))],
        out_specs=pl.BlockSpec((tm, D_MODEL), lambda s: (prev(s), 0)),
        out_shape=jax.ShapeDtypeStruct((m, D_MODEL), F32),
        scratch_shapes=[pltpu.VMEM((tm, D_FF), BF16), pltpu.VMEM((tm, D_MODEL), F32)],
        compiler_params=_params(),
        name="ffn",
    )(x, x, mod, mod, mod, w_up, w_down, ln_g.reshape(1, D_MODEL), ln_b.reshape(1, D_MODEL))


def _proj_parts(x_ref, sh_ref, sc_ref, wa_ref, wq_ref, wr_ref, wif_ref, bif_ref, cos_ref, sin_ref,
                q_ref, k_ref, v_ref, o_ref, aq_ref, ak_ref, av_ref, gm_ref, ga_ref, gt_ref):
    x = x_ref[...]
    tm = x.shape[0]
    h = (x * (1.0 + _rows(sc_ref)) + _rows(sh_ref)).astype(BF16)
    lane = lax.broadcasted_iota(jnp.int32, (tm, LANES), 1)

    def seg(w_ref, lo, width=256):
        return jnp.dot(h, w_ref[:, lo:lo + width], preferred_element_type=F32)

    def plain(dst_ref, w_ref, lo, c, scale=None):
        def run():
            z = seg(w_ref, lo + c)
            dst_ref[:, c:c + 256] = (z if scale is None else z * scale).astype(dst_ref.dtype)
        return run

    def gate(dst_ref, w_ref, lo, c):
        def run():
            dst_ref[:, c:c + 256] = _sigmoid(seg(w_ref, lo + c)).astype(BF16)
        return run

    def forget_input_gates():
        zg = jnp.dot(h, wif_ref[...], preferred_element_type=F32) + bif_ref[...]
        logsig = jnp.minimum(zg, 0.0) - jnp.log(1.0 + jnp.exp(-jnp.abs(zg)))
        gt_ref[...] = jnp.where(lane < M_HEADS, zg, logsig)

    def rotary(dst_ref, w_ref, lo, c):
        def run():
            cos = cos_ref[...]
            sin = sin_ref[...]
            low_half = (lane & (A_DH // 2)) == 0
            z = seg(w_ref, lo + c)
            for half in range(2):
                zh = z[:, half * LANES:(half + 1) * LANES]
                partner = jnp.where(low_half, pltpu.roll(zh, LANES - A_DH // 2, 1),
                                    pltpu.roll(zh, A_DH // 2, 1))
                dst_ref[:, c + half * LANES:c + (half + 1) * LANES] = (
                    zh * cos + partner * sin).astype(dst_ref.dtype)
        return run

    parts = [plain(q_ref, wa_ref, A_Q, c) for c in range(0, W_MQ, 256)]
    parts += [plain(k_ref, wa_ref, A_K, c, K_SCALE) for c in range(0, W_MQ, 256)]
    parts += [plain(v_ref, wa_ref, A_V, c) for c in range(0, W_MV, 256)]
    parts += [forget_input_gates]
    parts += [rotary(aq_ref, wq_ref, 0, c) for c in range(0, W_AQ, 256)]
    parts += [rotary(ak_ref, wr_ref, R_AK, 0), plain(av_ref, wr_ref, R_AV, 0)]
    parts += [gate(o_ref, wa_ref, A_O, c) for c in range(0, W_MV, 256)]
    parts += [gate(gm_ref, wr_ref, R_GM, c) for c in range(0, D_MODEL, 256)]
    parts += [gate(ga_ref, wr_ref, R_GA, c) for c in range(0, D_MODEL, 256)]
    return parts


def _proj_kernel(*refs):
    for part in _proj_parts(*refs):
        part()


def _proj(x, mod, per_row, rows_per_batch, chunks, w, cos_t, sin_t):
    m = x.shape[0]
    tm = ROW_TILE
    n_pos_tiles = cos_t.shape[0] // tm

    def tok(width):
        return pl.BlockSpec((tm, width), lambda i: (i, 0))

    widths = (W_MQ, W_MQ, W_MV, W_MV, W_AQ, W_AKV, W_AKV, D_MODEL, D_MODEL, LANES)
    dtypes = (BF16, BF16, BF16, BF16, BF16, F32, F32, BF16, BF16, F32)
    weights = (w["w_a"], w["w_aq"], w["w_r"], w["w_if"], w["bif"])
    return pl.pallas_call(
        _proj_kernel,
        grid=(m // tm,),
        in_specs=[tok(D_MODEL)] + _mod_specs(per_row, tm, rows_per_batch // tm, chunks)
        + [_resident(a.shape) for a in weights]
        + [pl.BlockSpec((tm, LANES), lambda i: (i % n_pos_tiles, 0)),
           pl.BlockSpec((tm, LANES), lambda i: (i % n_pos_tiles, 0))],
        out_specs=[tok(wd) for wd in widths],
        out_shape=[jax.ShapeDtypeStruct((m, wd), d) for wd, d in zip(widths, dtypes)],
        compiler_params=_params(),
        name="proj",
    )(x, mod, mod, *weights, cos_t, sin_t)


def _mlstm_kernel(*refs, nb, tpb, hps, zero_init):
    if zero_init:
        (q_ref, k_ref, v_ref, g_ref, mo_ref, ng_ref,
         out_ref, c_ref, n_ref, m_ref, gt_s, ct_s) = refs
    else:
        (q_ref, k_ref, v_ref, g_ref, mo_ref, ng_ref, c0_ref, n0_ref, m0_ref,
         out_ref, c_ref, n_ref, m_ref, gt_s, ct_s) = refs
    first_chunk = pl.program_id(2) == 0

    @pl.when(first_chunk)
    def _():
        if zero_init:
            c_ref[...] = jnp.zeros_like(c_ref)
            n_ref[...] = jnp.zeros_like(n_ref)
        else:
            c_ref[...] = c0_ref[...]
            n_ref[...] = n0_ref[...]

    @pl.when(first_chunk & (pl.program_id(1) == 0))
    def _():
        m_ref[...] = jnp.zeros_like(m_ref) if zero_init else m0_ref[...]

    head0 = 0 if hps == M_HEADS else pl.program_id(1) * hps
    _mlstm_chunk(q_ref, k_ref, v_ref, g_ref, mo_ref, ng_ref, out_ref, c_ref, n_ref, m_ref, gt_s, ct_s,
                 nb=nb, tpb=tpb, hps=hps, head0=head0)


def _mlstm_chunk(q_ref, k_ref, v_ref, g_ref, mo_ref, ng_ref, out_ref, c_ref, n_ref, m_ref, gt_s, ct_s,
                 *, nb, tpb, hps, head0, between=None):
    L = nb * tpb
    shift = tpb.bit_length() - 1
    row = lax.broadcasted_iota(jnp.int32, (L, L), 0)
    col = lax.broadcasted_iota(jnp.int32, (L, L), 1)
    same = (row >> shift) == (col >> shift)
    causal = same & (col <= row)
    lane = lax.broadcasted_iota(jnp.int32, (L, LANES), 1)
    row_seq = lax.broadcasted_iota(jnp.int32, (L, 1), 0) >> shift

    def lane_col(x, idx):
        return jnp.sum(jnp.where(lane == idx, x, 0.0), axis=1, keepdims=True)

    gates = g_ref[...]
    tri = jnp.where(causal, 1.0, 0.0).astype(BF16)
    g_hi = gates.astype(BF16)
    rem = gates - g_hi.astype(F32)
    g_mid = rem.astype(BF16)
    g_lo = (rem - g_mid.astype(F32)).astype(BF16)
    cum = (jnp.dot(tri, g_hi, preferred_element_type=F32)
           + jnp.dot(tri, g_mid, preferred_element_type=F32)
           + jnp.dot(tri, g_lo, preferred_element_type=F32))
    gt_s[...] = gates.T
    ct_s[...] = cum.T
    m_rows = m_ref[...]
    m_next = m_rows

    def head_stages(hl):
        head = head0 + hl
        qs = slice(hl * M_DQK, (hl + 1) * M_DQK)
        vs = slice(hl * M_DV, (hl + 1) * M_DV)
        st = {}

        def gate_stage():
            b_c = lane_col(cum, M_HEADS + head)
            m_p = lane_col(m_rows, head)
            i_r = gt_s[pl.ds(head, 1), :]
            b_r = ct_s[pl.ds(M_HEADS + head, 1), :]
            log_d = jnp.where(causal, b_c - b_r + i_r, -jnp.inf)
            m_t = jnp.maximum(b_c + m_p, jnp.max(log_d, axis=1, keepdims=True))
            dmat = jnp.exp(log_d - m_t)
            e_int = jnp.exp(b_c + m_p - m_t)
            if nb == 1:
                last = slice(L - 1, L)
                m_new = jnp.broadcast_to(m_t[last], (L, 1))
                e_c = jnp.broadcast_to(e_int[last], (L, 1))
                w_mat = jnp.broadcast_to(dmat[last], (L, L))
            else:
                b_last = jnp.min(jnp.where(same, b_r, jnp.inf), axis=1, keepdims=True)
                log_w = jnp.where(same, b_last - b_r + i_r, -jnp.inf)
                m_new = jnp.maximum(b_last + m_p, jnp.max(log_w, axis=1, keepdims=True))
                w_mat = jnp.exp(log_w - m_new)
                e_c = jnp.exp(b_last + m_p - m_new)
            st.update(m_t=m_t, dmat=dmat, e_int=e_int, m_new=m_new, e_c=e_c, w_mat=w_mat)

        def score_stage():
            q, k = q_ref[:, qs], k_ref[:, qs]
            st["s"] = lax.dot_general(q, k, _NT, preferred_element_type=F32) * st.pop("dmat")

        def state_stage():
            q, k = q_ref[:, qs], k_ref[:, qs]
            v_t = v_ref[:, vs].astype(F32).T
            w_mat = st["w_mat"]
            e_cb = jnp.broadcast_to(st["e_c"], (L, LANES))
            inter = jnp.zeros((L, M_DV), F32)
            for j in range(nb):
                first = slice(j * tpb, j * tpb + 1)
                c_j = c_ref[j, hl]
                q_j = q if nb == 1 else jnp.where(row_seq == j, q, jnp.zeros_like(q))
                inter = inter + lax.dot_general(q_j, c_j.astype(BF16), _NT, preferred_element_type=F32)
                lhs = (v_t * w_mat[first]).astype(BF16)
                c_ref[j, hl] = e_cb[first] * c_j + jnp.dot(lhs, k, preferred_element_type=F32)
            st["inter"] = inter

        def output_stage():
            q, k, v = q_ref[:, qs], k_ref[:, qs], v_ref[:, vs]
            s, e_int, m_t = st.pop("s"), st.pop("e_int"), st.pop("m_t")
            n_rows = n_ref[:, qs]
            qn = jnp.sum(q.astype(F32) * n_rows, axis=1, keepdims=True)
            num = jnp.dot(s.astype(BF16), v, preferred_element_type=F32) + e_int * st.pop("inter")
            den = jnp.sum(s, axis=1, keepdims=True) + e_int * qn
            hh = num / jnp.maximum(jnp.abs(den), jnp.exp(-m_t))
            mu = jnp.mean(hh, axis=1, keepdims=True)
            dlt = hh - mu
            var = jnp.mean(dlt * dlt, axis=1, keepdims=True)
            y = dlt * lax.rsqrt(var + HEAD_NORM_EPS)
            out_ref[:, vs] = (y * ng_ref[:, vs] * mo_ref[:, vs].astype(F32)).astype(BF16)
            n_ref[:, qs] = (st.pop("e_c") * n_rows
                            + jnp.dot(st.pop("w_mat").astype(BF16), k, preferred_element_type=F32))

        return (gate_stage, score_stage, state_stage, output_stage), st

    heads = [head_stages(hl) for hl in range(hps)]
    for stage in range(4):
        if between is not None:
            between()
        for stages, _ in heads:
            stages[stage]()
    for hl, (_, st) in enumerate(heads):
        m_next = jnp.where(lane == head0 + hl, st["m_new"], m_next)

    m_ref[...] = m_next


def _mlstm(q, k, v, gates, mo, norm_g, state, nb, tpb, n_chunks, hps):
    m = q.shape[0]
    L = nb * tpb
    n_blocks = m // (L * n_chunks)
    hd = M_HEADS * M_DQK

    def tok(width):
        return pl.BlockSpec((L, width), lambda b, g, c: (b * n_chunks + c, g))

    c_spec = pl.BlockSpec((nb, hps, M_DV, M_DQK), lambda b, g, c: (b, g, 0, 0))
    n_spec = pl.BlockSpec((L, hps * M_DQK), lambda b, g, c: (b, g))
    m_spec = pl.BlockSpec((L, LANES), lambda b, g, c: (b, 0))
    in_specs = [tok(hps * M_DQK), tok(hps * M_DQK), tok(hps * M_DV),
                pl.BlockSpec((L, LANES), lambda b, g, c: (b * n_chunks + c, 0)),
                tok(hps * M_DV), pl.BlockSpec((1, hps * M_DV), lambda b, g, c: (0, g))]
    args = [q, k, v, gates, mo, norm_g.reshape(1, M_HEADS * M_DV)]
    if state is not None:
        in_specs += [c_spec, n_spec, m_spec]
        args += list(state)
    return pl.pallas_call(
        functools.partial(_mlstm_kernel, nb=nb, tpb=tpb, hps=hps, zero_init=state is None),
        grid=(n_blocks, M_HEADS // hps, n_chunks),
        in_specs=in_specs,
        out_specs=[tok(hps * M_DV), c_spec, n_spec, m_spec],
        out_shape=[jax.ShapeDtypeStruct((m, M_HEADS * M_DV), BF16),
                   jax.ShapeDtypeStruct((n_blocks * nb, M_HEADS, M_DV, M_DQK), F32),
                   jax.ShapeDtypeStruct((n_blocks * L, hd), F32),
                   jax.ShapeDtypeStruct((n_blocks * L, LANES), F32)],
        scratch_shapes=[pltpu.VMEM((LANES, L), F32), pltpu.VMEM((LANES, L), F32)],
        compiler_params=_params(),
        name="mlstm",
    )(*args)


def _swa_kernel(*refs, nbb, tq, chained, n_new):
    if n_new:
        sink_ref, q_ref, kp_ref, kc_ref, vp_ref, vc_ref, o_ref, kn_ref, vn_ref = refs
    else:
        sink_ref, q_ref, kp_ref, kc_ref, vp_ref, vc_ref, o_ref = refs
    qi = lax.broadcasted_iota(jnp.int32, (tq, WINDOW), 0)
    kj = lax.broadcasted_iota(jnp.int32, (tq, WINDOW), 1)

    def pad_keys(x):
        if x.shape[0] == WINDOW:
            return x
        return jnp.concatenate([x, jnp.zeros((WINDOW - x.shape[0], x.shape[1]), x.dtype)], axis=0)

    blocks = []
    for jb in range(nbb):
        kc, vc = pad_keys(kc_ref[jb]), pad_keys(vc_ref[jb])
        valid_prev = kj > qi
        if chained and jb > 0:
            kp, vp = kc_ref[jb - 1], vc_ref[jb - 1]
        else:
            kp, vp = kp_ref[jb], vp_ref[jb]
            if chained:
                valid_prev = valid_prev & (pl.program_id(1) > 0)

        if n_new:
            keep_old = lax.broadcasted_iota(jnp.int32, (WINDOW, W_AKV), 0) < WINDOW - n_new
            kn_ref[jb] = jnp.where(keep_old, pltpu.roll(kp, WINDOW - n_new, 0),
                                   pltpu.roll(kc, WINDOW - n_new, 0))
            vn_ref[jb] = jnp.where(keep_old, pltpu.roll(vp, WINDOW - n_new, 0),
                                   pltpu.roll(vc, WINDOW - n_new, 0))

        blocks.append(_swa_stages(sink_ref, q_ref[jb], kp, vp, kc, vc, valid_prev, o_ref.at[jb]))

    for stage in range(3):
        for pairs in blocks:
            for pair in pairs:
                pair[stage]()


def _swa_stages(sink_ref, q, kp, vp, kc, vc, valid_prev, o_ref):
    tq = q.shape[0]
    reps = A_HEADS // A_KV_HEADS
    low_q = lax.broadcasted_iota(jnp.int32, (tq, LANES), 1) < A_DH
    low_k = lax.broadcasted_iota(jnp.int32, (WINDOW, LANES), 1) < A_DH
    key_row = lax.broadcasted_iota(jnp.int32, (WINDOW, LANES), 0)
    qi = lax.broadcasted_iota(jnp.int32, (tq, WINDOW), 0)
    kj = lax.broadcasted_iota(jnp.int32, (tq, WINDOW), 1)
    sink_lane = kj == 0
    valid_cur = kj <= qi
    scale = jnp.asarray(ATTN_SCALE, BF16)
    neg_inf = -jnp.inf

    def block_diag(x):
        return jnp.concatenate([jnp.where(low_k, x, 0.0), jnp.where(low_k, 0.0, x)], axis=0).astype(BF16)

    def pair_stages(p):
        ks = slice(p * LANES, (p + 1) * LANES)
        state = {}

        def scores():
            q4 = jnp.concatenate([q[:, (reps * p + r) * LANES:(reps * p + r + 1) * LANES] * scale
                                  for r in range(reps)], axis=0)
            state["sp"] = lax.dot_general(q4, block_diag(kp[:, ks]), _NT, preferred_element_type=F32)
            state["sc"] = lax.dot_general(q4, block_diag(kc[:, ks]), _NT, preferred_element_type=F32)

        def softmax():
            sp, sc = state.pop("sp"), state.pop("sc")
            pps, pcs, invs = [], [], []
            for r in range(reps):
                rows = slice(r * tq, (r + 1) * tq)
                pp_r, pc_r, inv_r = [], [], []
                for half in range(2):
                    cols = slice(half * WINDOW, (half + 1) * WINDOW)
                    head = 2 * reps * p + reps * half + r
                    sp_i = jnp.where(sink_lane, sink_ref[head], jnp.where(valid_prev, sp[rows, cols], neg_inf))
                    sc_i = jnp.where(valid_cur, sc[rows, cols], neg_inf)
                    mx = jnp.max(jnp.maximum(sp_i, sc_i), axis=1, keepdims=True)
                    pp = jnp.exp(sp_i - mx)
                    pc = jnp.exp(sc_i - mx)
                    inv_r.append(1.0 / jnp.sum(pp + pc, axis=1, keepdims=True))
                    pp_r.append(pp.astype(BF16))
                    pc_r.append(pc.astype(BF16))
                pps.append(jnp.concatenate(pp_r, axis=1))
                pcs.append(jnp.concatenate(pc_r, axis=1))
                invs.append(jnp.where(low_q, inv_r[0], inv_r[1]))
            state.update(pp=jnp.concatenate(pps, axis=0), pc=jnp.concatenate(pcs, axis=0), inv=invs)

        def values():
            v_prev = block_diag(jnp.where(key_row == 0, 0.0, vp[:, ks]))
            o4 = (jnp.dot(state.pop("pp"), v_prev, preferred_element_type=F32)
                  + jnp.dot(state.pop("pc"), block_diag(vc[:, ks]), preferred_element_type=F32))
            invs = state.pop("inv")
            for r in range(reps):
                blk = reps * p + r
                o_ref[:, blk * LANES:(blk + 1) * LANES] = (o4[r * tq:(r + 1) * tq] * invs[r]).astype(BF16)

        return scores, softmax, values

    return [pair_stages(p) for p in range(A_KV_HEADS // 2)]


def _swa_block(sink_ref, q, kp, vp, kc, vc, valid_prev, o_ref, between=None):
    for scores, softmax, values in _swa_stages(sink_ref, q, kp, vp, kc, vc, valid_prev, o_ref):
        if between is not None:
            between()
        scores()
        softmax()
        if between is not None:
            between()
        values()


def _swa(sinks, q, k_prev_src, k_cur, v_prev_src, v_cur, *, nbb, chained, n_new, blocks_per_seq):
    n, tq, _ = q.shape
    tk = k_cur.shape[1]
    if chained:
        steps = blocks_per_seq // nbb
        grid = (n // blocks_per_seq, steps)
        cur_index = lambda b, i: (b * steps + i, 0, 0)
        prev_spec = pl.BlockSpec((1, WINDOW, W_AKV),
                                 lambda b, i: (b * blocks_per_seq + jnp.maximum(i * nbb - 1, 0), 0, 0))
    else:
        grid = (n // nbb,)
        cur_index = lambda i: (i, 0, 0)
        prev_spec = pl.BlockSpec((nbb, WINDOW, W_AKV), cur_index)
    cur_spec = pl.BlockSpec((nbb, tk, W_AKV), cur_index)
    q_spec = pl.BlockSpec((nbb, tq, W_AQ), cur_index)
    out_specs = [q_spec]
    out_shape = [jax.ShapeDtypeStruct((n, tq, W_AQ), BF16)]
    if n_new:
        out_specs += [prev_spec, prev_spec]
        out_shape += [jax.ShapeDtypeStruct((n, WINDOW, W_AKV), F32)] * 2
    return pl.pallas_call(
        functools.partial(_swa_kernel, nbb=nbb, tq=tq, chained=chained, n_new=n_new),
        grid=grid,
        in_specs=[pl.BlockSpec(memory_space=pltpu.SMEM), q_spec, prev_spec, cur_spec, prev_spec, cur_spec],
        out_specs=out_specs,
        out_shape=out_shape,
        compiler_params=_params(),
        name="swa",
    )(sinks, q, k_prev_src, k_cur, v_prev_src, v_cur)


def _merge_kernel(x_ref, g_ref, hm_ref, oa_ref, gm_ref, ga_ref, wm_ref, wa_ref, wo_ref,
                  lg_ref, lb_ref, o_ref):
    ym = jnp.dot(hm_ref[...], wm_ref[...], preferred_element_type=F32)
    ya = jnp.dot(oa_ref[...], wa_ref[...], preferred_element_type=F32)
    mix = gm_ref[...].astype(F32) * ym + ga_ref[...].astype(F32) * ya
    t = jnp.dot(mix.astype(BF16), wo_ref[...], preferred_element_type=F32)
    y = DEEPNORM_ALPHA * x_ref[...] + (1.0 + _rows(g_ref)) * t
    o_ref[...] = _layer_norm(y, lg_ref[...], lb_ref[...], LN_EPS)


def _merge(x, mod, per_row, rows_per_batch, chunk, hm, oa, gm, ga, wm, wa, wo, ln_g, ln_b):
    m = x.shape[0]
    tm = ROW_TILE

    def tok():
        return pl.BlockSpec((tm, D_MODEL), lambda i: (i, 0))

    return pl.pallas_call(
        _merge_kernel,
        grid=(m // tm,),
        in_specs=[tok()] + _mod_specs(per_row, tm, rows_per_batch // tm, (chunk,))
        + [tok(), tok(), tok(), tok()]
        + [_resident((D_MODEL, D_MODEL))] * 3 + [_resident((1, D_MODEL))] * 2,
        out_specs=tok(),
        out_shape=jax.ShapeDtypeStruct((m, D_MODEL), F32),
        compiler_params=_params(),
        name="merge",
    )(x, mod, hm, oa, gm, ga, wm, wa, wo, ln_g.reshape(1, D_MODEL), ln_b.reshape(1, D_MODEL))


def _mixer_kernel(sink_ref, xa_ref, xb_ref, sh_ref, sc_ref, g_ref, wa_ref, wq_ref, wr_ref, wif_ref, bif_ref,
                  cos_ref, sin_ref, ng_ref, wm_ref, wba_ref, wo_ref, lg_ref, lb_ref,
                  y_ref, c_ref, n_ref, m_ref, kb_ref, vb_ref, *scratch, tiles_per_seq):
    n_staged = (len(scratch) - 6) // 2
    even, odd = scratch[:n_staged], scratch[n_staged:2 * n_staged]
    kprev_s, vprev_s, hm_s, oa_s, gtt_s, ctt_s = scratch[2 * n_staged:]
    tm = xa_ref.shape[0]
    s = pl.program_id(0)
    tile_b = jnp.maximum(s - 1, 0)
    seq_start = tile_b % tiles_per_seq == 0

    @pl.when(s == 0)
    def _():
        for ref in odd:
            ref[...] = jnp.zeros_like(ref)

    @pl.when(seq_start)
    def _():
        c_ref[...] = jnp.zeros_like(c_ref)
        n_ref[...] = jnp.zeros_like(n_ref)
        m_ref[...] = jnp.zeros_like(m_ref)
        kprev_s[...] = jnp.zeros_like(kprev_s)
        vprev_s[...] = jnp.zeros_like(vprev_s)

    def step(produce, consume):
        proj_parts = _proj_parts(xa_ref, sh_ref, sc_ref, wa_ref, wq_ref, wr_ref, wif_ref, bif_ref,
                                 cos_ref, sin_ref, *produce)
        pending = iter(proj_parts)

        def emit(count):
            def between():
                for _ in range(count):
                    part = next(pending, None)
                    if part is not None:
                        part()
            return between

        q_s, k_s, v_s, mo_s, aq_s, ak_s, av_s, gm_s, ga_s, gt_s = consume
        qi = lax.broadcasted_iota(jnp.int32, (WINDOW, WINDOW), 0)
        kj = lax.broadcasted_iota(jnp.int32, (WINDOW, WINDOW), 1)
        for j in range(tm // MLSTM_CHUNK):
            rows = pl.ds(j * MLSTM_CHUNK, MLSTM_CHUNK)
            _mlstm_chunk(q_s.at[rows], k_s.at[rows], v_s.at[rows], gt_s.at[rows], mo_s.at[rows], ng_ref,
                         hm_s.at[rows], c_ref, n_ref, m_ref, gtt_s, ctt_s,
                         nb=1, tpb=MLSTM_CHUNK, hps=M_HEADS, head0=0, between=emit(MIXER_PROJ_PER_HEAD))
            valid_prev = kj > qi
            if j == 0:
                valid_prev = valid_prev & jnp.logical_not(seq_start)
                k_prev, v_prev = kprev_s[...], vprev_s[...]
            else:
                before = pl.ds((j - 1) * WINDOW, WINDOW)
                k_prev, v_prev = ak_s[before], av_s[before]
            _swa_block(sink_ref, aq_s[rows], k_prev, v_prev, ak_s[rows], av_s[rows], valid_prev,
                       oa_s.at[rows], between=emit(MIXER_PROJ_PER_ATTN))

        for part in pending:
            part()
        _merge_kernel(xb_ref, g_ref, hm_s, oa_s, gm_s, ga_s, wm_ref, wba_ref, wo_ref, lg_ref, lb_ref, y_ref)

        last = pl.ds(tm - WINDOW, WINDOW)
        k_last, v_last = ak_s[last], av_s[last]
        kb_ref[0] = k_last
        vb_ref[0] = v_last
        kprev_s[...] = k_last
        vprev_s[...] = v_last

    @pl.when(s % 2 == 0)
    def _():
        step(even, odd)

    @pl.when(s % 2 == 1)
    def _():
        step(odd, even)


def _mixer(x1, mod, w, sinks, norm_g, cos_t, sin_t, bp, sp):
    tm = MIXER_TILE
    tps = sp // tm
    n_tiles = bp * tps
    hd = M_HEADS * M_DQK
    weights = (w["w_a"], w["w_aq"], w["w_r"], w["w_if"], w["bif"])

    def tile_a(s):
        return jnp.minimum(s, n_tiles - 1)

    def tile_b(s):
        return jnp.maximum(s - 1, 0)

    def mod_spec(tile, chunk):
        return pl.BlockSpec((1, 1, D_MODEL), lambda s: (tile(s) // tps, 0, chunk))

    def per_seq(*shape):
        return pl.BlockSpec((1,) + shape, lambda s: (tile_b(s) // tps,) + (0,) * len(shape))

    pos = pl.BlockSpec((tm, LANES), lambda s: (tile_a(s) % tps, 0))
    staged = [(W_MQ, BF16), (W_MQ, BF16), (W_MV, BF16), (W_MV, BF16), (W_AQ, BF16),
              (W_AKV, F32), (W_AKV, F32), (D_MODEL, BF16), (D_MODEL, BF16), (LANES, F32)]
    return pl.pallas_call(
        functools.partial(_mixer_kernel, tiles_per_seq=tps),
        grid=(n_tiles + 1,),
        in_specs=[pl.BlockSpec(memory_space=pltpu.SMEM),
                  pl.BlockSpec((tm, D_MODEL), lambda s: (tile_a(s), 0)),
                  pl.BlockSpec((tm, D_MODEL), lambda s: (tile_b(s), 0)),
                  mod_spec(tile_a, 3), mod_spec(tile_a, 4), mod_spec(tile_b, 5)]
        + [_resident(a.shape) for a in weights] + [pos, pos, _resident((1, M_HEADS * M_DV))]
        + [_resident((D_MODEL, D_MODEL))] * 3 + [_resident((1, D_MODEL))] * 2,
        out_specs=[pl.BlockSpec((tm, D_MODEL), lambda s: (tile_b(s), 0)),
                   per_seq(M_HEADS, M_DV, M_DQK),
                   pl.BlockSpec((MLSTM_CHUNK, hd), lambda s: (tile_b(s) // tps, 0)),
                   pl.BlockSpec((MLSTM_CHUNK, LANES), lambda s: (tile_b(s) // tps, 0)),
                   per_seq(WINDOW, W_AKV), per_seq(WINDOW, W_AKV)],
        out_shape=[jax.ShapeDtypeStruct((bp * sp, D_MODEL), F32),
                   jax.ShapeDtypeStruct((bp, M_HEADS, M_DV, M_DQK), F32),
                   jax.ShapeDtypeStruct((bp * MLSTM_CHUNK, hd), F32),
                   jax.ShapeDtypeStruct((bp * MLSTM_CHUNK, LANES), F32),
                   jax.ShapeDtypeStruct((bp, WINDOW, W_AKV), F32),
                   jax.ShapeDtypeStruct((bp, WINDOW, W_AKV), F32)],
        scratch_shapes=[pltpu.VMEM((tm, width), dtype) for width, dtype in staged + staged]
        + [pltpu.VMEM((WINDOW, W_AKV), F32), pltpu.VMEM((WINDOW, W_AKV), F32),
           pltpu.VMEM((tm, W_MV), BF16), pltpu.VMEM((tm, W_AQ), BF16),
           pltpu.VMEM((LANES, MLSTM_CHUNK), F32), pltpu.VMEM((LANES, MLSTM_CHUNK), F32)],
        compiler_params=_params(),
        name="mixer",
    )(sinks, x1, x1, mod, mod, mod, *weights, cos_t, sin_t, norm_g.reshape(1, M_HEADS * M_DV),
      w["wm"], w["wa"], w["wo"], w["ln2_g"].reshape(1, D_MODEL), w["ln2_b"].reshape(1, D_MODEL))


def _rope_tables(pos):
    half = A_DH // 2
    inv = ROPE_THETA ** (-jnp.arange(half, dtype=F32) / half)
    ang = pos.astype(F32)[:, None] * inv[None, :]
    cos, sin = jnp.cos(ang), jnp.sin(ang)
    return jnp.tile(cos, (1, 4)), jnp.concatenate([-sin, sin, -sin, sin], axis=1)


def _pair_heads(x, axis):
    shape = x.shape
    reps = A_HEADS // A_KV_HEADS
    x = x.reshape(shape[:axis] + (A_KV_HEADS // 2, 2, reps, A_DH) + shape[axis + 1:])
    return jnp.swapaxes(x, axis + 1, axis + 2).reshape(shape)


def _token_stage_1(x, mod, per_row, rows_per_batch, w, pos):
    x1 = _ffn(x, mod, per_row, rows_per_batch, (0, 1, 2), w["up1"], w["down1"], w["ln1_g"], w["ln1_b"])
    cos_t, sin_t = _rope_tables(pos)
    return x1, _proj(x1, mod, per_row, rows_per_batch, (3, 4), w, cos_t, sin_t)


def _token_stage_2(x1, mod, per_row, rows_per_batch, w, hm, oa, gm, ga):
    x2 = _merge(x1, mod, per_row, rows_per_batch, 5, hm, oa, gm, ga,
                w["wm"], w["wa"], w["wo"], w["ln2_g"], w["ln2_b"])
    return _ffn(x2, mod, per_row, rows_per_batch, (6, 7, 8), w["up2"], w["down2"], w["ln3_g"], w["ln3_b"])


def kernel(x_prompt, x_sample, state_mlstm_C, state_mlstm_n, state_mlstm_m, cache_swa_k, cache_swa_v, c_prompt, c_sample, w_ada, b_ada, w_ffn1_up, w_ffn1_down, ln1_g, ln1_b, w_in, b_igate, b_fgate, m_norm_g, sinks, w_branch_m, w_branch_a, w_out, ln2_g, ln2_b, w_ffn2_up, w_ffn2_down, ln3_g, ln3_b):
    assert w_ada.shape[0] == DEPTH == 1
    bp, sp, _ = x_prompt.shape
    bs, ts, _ = x_sample.shape

    win = w_in[0]
    w = dict(
        up1=w_ffn1_up[0].astype(BF16), down1=w_ffn1_down[0].astype(BF16),
        up2=w_ffn2_up[0].astype(BF16), down2=w_ffn2_down[0].astype(BF16),
        w_a=win[:, :IN_IF].astype(BF16),
        w_aq=_pair_heads(win[:, IN_AQ:IN_AK], 1).astype(BF16),
        w_r=win[:, IN_AK:IN_END].astype(BF16),
        w_if=jnp.pad(win[:, IN_IF:IN_AQ], ((0, 0), (0, LANES - 2 * M_HEADS))).astype(BF16),
        bif=jnp.concatenate([b_igate[0], b_fgate[0], jnp.zeros((LANES - 2 * M_HEADS,), F32)]).reshape(1, LANES),
        wm=w_branch_m[0].astype(BF16), wa=_pair_heads(w_branch_a[0], 0).astype(BF16),
        wo=w_out[0].astype(BF16),
        ln1_g=ln1_g[0], ln1_b=ln1_b[0], ln2_g=ln2_g[0], ln2_b=ln2_b[0], ln3_g=ln3_g[0], ln3_b=ln3_b[0],
    )

    ms = bs * ts
    c_all = jnp.concatenate([jnp.repeat(c_sample, ts, axis=0), c_prompt], axis=0)
    mod = _ada(c_all, w_ada[0], b_ada[0])
    mod_p = mod[ms:].reshape(bp, 1, ADA_CHUNKS * D_MODEL)

    mp = bp * sp
    x1p = _ffn(x_prompt.reshape(mp, D_MODEL), mod_p, False, sp, (0, 1, 2),
               w["up1"], w["down1"], w["ln1_g"], w["ln1_b"])
    cos_t, sin_t = _rope_tables(jnp.arange(sp))
    x2p, c_p, n_rows, m_rows, kb_p, vb_p = _mixer(x1p, mod_p, w, sinks[0], m_norm_g[0], cos_t, sin_t, bp, sp)
    y_p = _ffn(x2p, mod_p, False, sp, (6, 7, 8), w["up2"], w["down2"], w["ln3_g"], w["ln3_b"])
    n_p = n_rows.reshape(bp, MLSTM_CHUNK, M_HEADS, M_DQK)[:, 0]
    m_p = m_rows.reshape(bp, MLSTM_CHUNK, LANES)[:, 0, :M_HEADS]
    kb_p = kb_p.reshape(bp, WINDOW, A_KV_HEADS, A_DH)
    vb_p = vb_p.reshape(bp, WINDOW, A_KV_HEADS, A_DH)

    x1s, (qm, km, vm, mo, aq, ak, av, gm, ga, gt) = _token_stage_1(
        x_sample.reshape(ms, D_MODEL), mod, True, ms, w, PAST_LEN + jnp.arange(ms) % ts)
    seqs = MLSTM_CHUNK // ts
    n0_rows = jnp.repeat(state_mlstm_n[0].reshape(bs, M_HEADS * M_DQK), ts, axis=0)
    m0_rows = jnp.repeat(jnp.pad(state_mlstm_m[0], ((0, 0), (0, LANES - M_HEADS))), ts, axis=0)
    hm, c_s, n_rows, m_rows = _mlstm(qm, km, vm, gt, mo, m_norm_g[0],
                                     (state_mlstm_C[0], n0_rows, m0_rows), seqs, ts, 1, 1)
    n_s = n_rows.reshape(bs, ts, M_HEADS, M_DQK)[:, 0]
    m_s = m_rows.reshape(bs, ts, LANES)[:, 0, :M_HEADS]
    pad_t = ((0, 0), (0, SAMPLE_TQ - ts), (0, 0))
    oa, kb_s, vb_s = _swa(
        sinks[0], jnp.pad(aq.reshape(bs, ts, W_AQ), pad_t),
        cache_swa_k[0].reshape(bs, WINDOW, W_AKV), jnp.pad(ak.reshape(bs, ts, W_AKV), pad_t),
        cache_swa_v[0].reshape(bs, WINDOW, W_AKV), jnp.pad(av.reshape(bs, ts, W_AKV), pad_t),
        nbb=SWA_SAMPLE_SEQS, chained=False, n_new=ts, blocks_per_seq=1)
    y_s = _token_stage_2(x1s, mod, True, ms, w, hm, oa[:, :ts].reshape(ms, D_MODEL), gm, ga)
    kb_s = kb_s.reshape(bs, WINDOW, A_KV_HEADS, A_DH)
    vb_s = vb_s.reshape(bs, WINDOW, A_KV_HEADS, A_DH)

    return (y_p.reshape(bp, sp, D_MODEL), y_s.reshape(bs, ts, D_MODEL),
            c_p[None], n_p[None], m_p[None], kb_p[None], vb_p[None],
            c_s[None], n_s[None], m_s[None], kb_s[None], vb_s[None])
```

```python
import functools

import jax
import jax.numpy as jnp
from jax import lax
from jax.experimental import pallas as pl
from jax.experimental.pallas import tpu as pltpu

F32 = jnp.float32
BF16 = jnp.bfloat16

D_MODEL = 1024
D_FF = 2816
DEPTH = 1
M_HEADS = 4
M_DQK = 128
M_DV = 256
A_HEADS = 16
A_KV_HEADS = 4
A_DH = 64
WINDOW = 128
PAST_LEN = 8192
ROPE_THETA = 10000.0
ATTN_SCALE = A_DH ** -0.5
LN_EPS = 1e-5
HEAD_NORM_EPS = 1e-6
ADA_CHUNKS = 9
DEEPNORM_ALPHA = (2.0 * DEPTH) ** 0.25
K_SCALE = M_DQK ** -0.5

LANES = 128
BF16_SUBLANES = 16
VMEM_LIMIT_BYTES = 56 * 1024 * 1024

W_MQ = M_HEADS * M_DQK
W_MV = M_HEADS * M_DV
W_AQ = A_HEADS * A_DH
W_AKV = A_KV_HEADS * A_DH
IN_IF = 2 * W_MQ + 2 * W_MV
IN_AQ = IN_IF + 2 * M_HEADS
IN_AK = IN_AQ + W_AQ
IN_END = IN_AK + 2 * W_AKV + 2 * D_MODEL
A_Q, A_K, A_V, A_O = 0, W_MQ, 2 * W_MQ, 2 * W_MQ + W_MV
R_AK, R_AV, R_GM, R_GA = 0, W_AKV, 2 * W_AKV, 2 * W_AKV + D_MODEL

ROW_TILE = 512
MIXER_TILE = 256
MIXER_PROJ_PER_HEAD = 1
MIXER_PROJ_PER_ATTN = 3
ADA_TILE = 1536
FF_CHUNK = 256
FFN_EPILOGUE_PIECES = 4
MLSTM_CHUNK = 128
SAMPLE_TQ = BF16_SUBLANES
SWA_PROMPT_BLOCKS = 2
SWA_SAMPLE_SEQS = 16

_NT = (((1,), (1,)), ((), ()))


def _params():
    return pltpu.CompilerParams(vmem_limit_bytes=VMEM_LIMIT_BYTES)


def _resident(shape):
    return pl.BlockSpec(shape, lambda *_: (0,) * len(shape), pipeline_mode=pl.Buffered(1))


def _rows(ref):
    v = ref[...]
    return v.reshape(v.shape[-2], v.shape[-1])


def _layer_norm(y, g, b, eps):
    mu = jnp.mean(y, axis=-1, keepdims=True)
    d = y - mu
    var = jnp.mean(d * d, axis=-1, keepdims=True)
    return d * lax.rsqrt(var + eps) * g + b


def _sigmoid(x):
    return 1.0 / (1.0 + jnp.exp(-x))


def _mod_specs(per_row, tm, tiles_per_batch, chunks):
    if per_row:
        return [pl.BlockSpec((tm, D_MODEL), lambda i, c=c: (i, c)) for c in chunks]
    return [pl.BlockSpec((1, 1, D_MODEL), lambda i, c=c: (i // tiles_per_batch, 0, c)) for c in chunks]


def _ada_kernel(c_ref, w_ref, b_ref, o_ref):
    c = c_ref[...]
    s = (c * _sigmoid(c)).astype(BF16)
    o_ref[...] = jnp.dot(s, w_ref[...].astype(BF16), preferred_element_type=F32) + b_ref[...]


def _ada(c_all, w_ada, b_ada):
    rows = c_all.shape[0]
    n_out = w_ada.shape[1]
    tn = ADA_TILE
    return pl.pallas_call(
        _ada_kernel,
        grid=(n_out // tn,),
        in_specs=[pl.BlockSpec((rows, D_MODEL), lambda j: (0, 0)),
                  pl.BlockSpec((D_MODEL, tn), lambda j: (0, j)),
                  pl.BlockSpec((1, tn), lambda j: (0, j))],
        out_specs=pl.BlockSpec((rows, tn), lambda j: (0, j)),
        out_shape=jax.ShapeDtypeStruct((rows, n_out), F32),
        compiler_params=_params(),
        name="ada",
    )(c_all, w_ada, b_ada.reshape(1, n_out))


def _ffn_kernel(x_ref, xp_ref, sh_ref, sc_ref, g_ref, wup_ref, wdn_ref, lg_ref, lb_ref, *rest, n_tiles):
    n_cast = (len(rest) - 3) // 2
    cast_in, o_ref, cast_out = rest[:n_cast], rest[n_cast], rest[n_cast + 1:2 * n_cast + 1]
    act_ref, f_ref = rest[2 * n_cast + 1:]

    def cast_chunk(job):
        if job < n_cast:
            cast_out[job][...] = cast_in[job][...].astype(BF16)

    tm = x_ref.shape[0]
    s = pl.program_id(0)
    piece_rows = tm // FFN_EPILOGUE_PIECES

    @pl.when(s == 0)
    def _():
        f_ref[...] = jnp.zeros_like(f_ref)

    def epilogue(piece):
        rows = pl.ds(piece * piece_rows, piece_rows)
        g = _rows(g_ref)
        if g.shape[0] != 1:
            g = g[piece * piece_rows:(piece + 1) * piece_rows]
        y = DEEPNORM_ALPHA * xp_ref[rows, :] + (0.5 * (1.0 + g)) * f_ref[rows, :]
        o_ref[rows, :] = _layer_norm(y, lg_ref[...], lb_ref[...], LN_EPS)

    @pl.when(s < n_tiles)
    def _():
        h = (x_ref[...] * (1.0 + _rows(sc_ref)) + _rows(sh_ref)).astype(BF16)
        chunks = list(range(0, D_FF, FF_CHUNK))
        every = len(chunks) // FFN_EPILOGUE_PIECES
        piece = 0
        for i, c in enumerate(chunks):
            a = jnp.dot(h, wup_ref[:, c:c + FF_CHUNK], preferred_element_type=F32)
            u = jnp.dot(h, wup_ref[:, D_FF + c:D_FF + c + FF_CHUNK], preferred_element_type=F32)
            act_ref[:, c:c + FF_CHUNK] = (a * _sigmoid(a) * u).astype(BF16)
            cast_chunk(i)
            if i % every == every - 1 and piece < FFN_EPILOGUE_PIECES:
                epilogue(piece)
                piece += 1
        f_ref[...] = jnp.dot(act_ref[...], wdn_ref[...], preferred_element_type=F32)

    @pl.when(s == n_tiles)
    def _():
        for job in range(n_cast):
            cast_chunk(job)
        for piece in range(FFN_EPILOGUE_PIECES):
            epilogue(piece)


def _cast_job(array, block, axis, start=0, out_len=None, first_step=0, permute=None):
    n_blocks = (array.shape[axis] // block[axis] - start) if out_len is None else out_len

    def out_index(s):
        j = jnp.clip(s - first_step, 0, n_blocks - 1)
        return (j, 0) if axis == 0 else (0, j)

    def in_index(s):
        j = jnp.clip(s - first_step, 0, n_blocks - 1)
        j = start + (j if permute is None else permute(j))
        return (j, 0) if axis == 0 else (0, j)

    out_dims = list(array.shape)
    out_dims[axis] = n_blocks * block[axis]
    return dict(array=array, in_spec=pl.BlockSpec(block, in_index), out_spec=pl.BlockSpec(block, out_index),
                out_shape=jax.ShapeDtypeStruct(tuple(out_dims), BF16), steps=first_step + n_blocks)


def _ffn(x, mod, per_row, rows_per_batch, chunks, w_up, w_down, ln_g, ln_b, cast_jobs=()):
    m = x.shape[0]
    tm = ROW_TILE
    n_tiles = m // tm
    tpb = rows_per_batch // tm

    def cur(s):
        return jnp.minimum(s, n_tiles - 1)

    def prev(s):
        return jnp.maximum(s - 1, 0)

    def mod_spec(tile, chunk):
        if per_row:
            return pl.BlockSpec((tm, D_MODEL), lambda s: (tile(s), chunk))
        return pl.BlockSpec((1, 1, D_MODEL), lambda s: (tile(s) // tpb, 0, chunk))

    in_specs = [pl.BlockSpec((tm, D_MODEL), lambda s: (cur(s), 0)),
                pl.BlockSpec((tm, D_MODEL), lambda s: (prev(s), 0)),
                mod_spec(cur, chunks[0]), mod_spec(cur, chunks[1]), mod_spec(prev, chunks[2]),
                _resident((D_MODEL, 2 * D_FF)), _resident((D_FF, D_MODEL)),
                _resident((1, D_MODEL)), _resident((1, D_MODEL))]
    args = [x, x, mod, mod, mod, w_up, w_down, ln_g.reshape(1, D_MODEL), ln_b.reshape(1, D_MODEL)]
    out_specs = [pl.BlockSpec((tm, D_MODEL), lambda s: (prev(s), 0))]
    out_shape = [jax.ShapeDtypeStruct((m, D_MODEL), F32)]
    for job in cast_jobs:
        assert job["steps"] <= n_tiles + 1
        in_specs.append(job["in_spec"])
        args.append(job["array"])
        out_specs.append(job["out_spec"])
        out_shape.append(job["out_shape"])
    outs = pl.pallas_call(
        functools.partial(_ffn_kernel, n_tiles=n_tiles),
        grid=(n_tiles + 1,),
        in_specs=in_specs,
        out_specs=out_specs,
        out_shape=out_shape,
        scratch_shapes=[pltpu.VMEM((tm, D_FF), BF16), pltpu.VMEM((tm, D_MODEL), F32)],
        compiler_params=_params(),
        name="ffn",
    )(*args)
    return outs if cast_jobs else outs[0]


def _proj_parts(x_ref, sh_ref, sc_ref, wa_ref, wq_ref, wr_ref, wif_ref, bif_ref, cos_ref, sin_ref,
                q_ref, k_ref, v_ref, o_ref, aq_ref, ak_ref, av_ref, gm_ref, ga_ref, gt_ref):
    x = x_ref[...]
    tm = x.shape[0]
    h = (x * (1.0 + _rows(sc_ref)) + _rows(sh_ref)).astype(BF16)
    lane = lax.broadcasted_iota(jnp.int32, (tm, LANES), 1)

    def seg(w_ref, lo, width=256):
        return jnp.dot(h, w_ref[:, lo:lo + width], preferred_element_type=F32)

    def plain(dst_ref, w_ref, lo, c, scale=None):
        def run():
            z = seg(w_ref, lo + c)
            dst_ref[:, c:c + 256] = (z if scale is None else z * scale).astype(dst_ref.dtype)
        return run

    def gate(dst_ref, w_ref, lo, c):
        def run():
            dst_ref[:, c:c + 256] = _sigmoid(seg(w_ref, lo + c)).astype(BF16)
        return run

    def forget_input_gates():
        zg = jnp.dot(h, wif_ref[...], preferred_element_type=F32) + bif_ref[...]
        logsig = jnp.minimum(zg, 0.0) - jnp.log(1.0 + jnp.exp(-jnp.abs(zg)))
        gt_ref[...] = jnp.where(lane < M_HEADS, zg, logsig)

    def rotary(dst_ref, w_ref, lo, c):
        def run():
            cos = cos_ref[...]
            sin = sin_ref[...]
            low_half = (lane & (A_DH // 2)) == 0
            z = seg(w_ref, lo + c)
            for half in range(2):
                zh = z[:, half * LANES:(half + 1) * LANES]
                partner = jnp.where(low_half, pltpu.roll(zh, LANES - A_DH // 2, 1),
                                    pltpu.roll(zh, A_DH // 2, 1))
                dst_ref[:, c + half * LANES:c + (half + 1) * LANES] = (
                    zh * cos + partner * sin).astype(dst_ref.dtype)
        return run

    parts = [plain(q_ref, wa_ref, A_Q, c) for c in range(0, W_MQ, 256)]
    parts += [plain(k_ref, wa_ref, A_K, c, K_SCALE) for c in range(0, W_MQ, 256)]
    parts += [plain(v_ref, wa_ref, A_V, c) for c in range(0, W_MV, 256)]
    parts += [forget_input_gates]
    parts += [rotary(aq_ref, wq_ref, 0, c) for c in range(0, W_AQ, 256)]
    parts += [rotary(ak_ref, wr_ref, R_AK, 0), plain(av_ref, wr_ref, R_AV, 0)]
    parts += [gate(o_ref, wa_ref, A_O, c) for c in range(0, W_MV, 256)]
    parts += [gate(gm_ref, wr_ref, R_GM, c) for c in range(0, D_MODEL, 256)]
    parts += [gate(ga_ref, wr_ref, R_GA, c) for c in range(0, D_MODEL, 256)]
    return parts


def _proj_kernel(*refs):
    for part in _proj_parts(*refs):
        part()


def _proj(x, mod, per_row, rows_per_batch, chunks, w, cos_t, sin_t):
    m = x.shape[0]
    tm = ROW_TILE
    n_pos_tiles = cos_t.shape[0] // tm

    def tok(width):
        return pl.BlockSpec((tm, width), lambda i: (i, 0))

    widths = (W_MQ, W_MQ, W_MV, W_MV, W_AQ, W_AKV, W_AKV, D_MODEL, D_MODEL, LANES)
    dtypes = (BF16, BF16, BF16, BF16, BF16, F32, F32, BF16, BF16, F32)
    weights = (w["w_a"], w["w_aq"], w["w_r"], w["w_if"], w["bif"])
    return pl.pallas_call(
        _proj_kernel,
        grid=(m // tm,),
        in_specs=[tok(D_MODEL)] + _mod_specs(per_row, tm, rows_per_batch // tm, chunks)
        + [_resident(a.shape) for a in weights]
        + [pl.BlockSpec((tm, LANES), lambda i: (i % n_pos_tiles, 0)),
           pl.BlockSpec((tm, LANES), lambda i: (i % n_pos_tiles, 0))],
        out_specs=[tok(wd) for wd in widths],
        out_shape=[jax.ShapeDtypeStruct((m, wd), d) for wd, d in zip(widths, dtypes)],
        compiler_params=_params(),
        name="proj",
    )(x, mod, mod, *weights, cos_t, sin_t)


def _mlstm_kernel(*refs, nb, tpb, hps, zero_init):
    if zero_init:
        (q_ref, k_ref, v_ref, g_ref, mo_ref, ng_ref,
         out_ref, c_ref, n_ref, m_ref, gt_s, ct_s) = refs
    else:
        (q_ref, k_ref, v_ref, g_ref, mo_ref, ng_ref, c0_ref, n0_ref, m0_ref,
         out_ref, c_ref, n_ref, m_ref, gt_s, ct_s) = refs
    first_chunk = pl.program_id(2) == 0

    @pl.when(first_chunk)
    def _():
        if zero_init:
            c_ref[...] = jnp.zeros_like(c_ref)
            n_ref[...] = jnp.zeros_like(n_ref)
        else:
            c_ref[...] = c0_ref[...]
            n_ref[...] = n0_ref[...]

    @pl.when(first_chunk & (pl.program_id(1) == 0))
    def _():
        m_ref[...] = jnp.zeros_like(m_ref) if zero_init else m0_ref[...]

    head0 = 0 if hps == M_HEADS else pl.program_id(1) * hps
    _mlstm_chunk(q_ref, k_ref, v_ref, g_ref, mo_ref, ng_ref, out_ref, c_ref, n_ref, m_ref, gt_s, ct_s,
                 nb=nb, tpb=tpb, hps=hps, head0=head0)


def _mlstm_chunk(q_ref, k_ref, v_ref, g_ref, mo_ref, ng_ref, out_ref, c_ref, n_ref, m_ref, gt_s, ct_s,
                 *, nb, tpb, hps, head0, between=None):
    L = nb * tpb
    shift = tpb.bit_length() - 1
    row = lax.broadcasted_iota(jnp.int32, (L, L), 0)
    col = lax.broadcasted_iota(jnp.int32, (L, L), 1)
    same = (row >> shift) == (col >> shift)
    causal = same & (col <= row)
    lane = lax.broadcasted_iota(jnp.int32, (L, LANES), 1)
    row_seq = lax.broadcasted_iota(jnp.int32, (L, 1), 0) >> shift

    def lane_col(x, idx):
        return jnp.sum(jnp.where(lane == idx, x, 0.0), axis=1, keepdims=True)

    gates = g_ref[...]
    tri = jnp.where(causal, 1.0, 0.0).astype(BF16)
    g_hi = gates.astype(BF16)
    rem = gates - g_hi.astype(F32)
    g_mid = rem.astype(BF16)
    g_lo = (rem - g_mid.astype(F32)).astype(BF16)
    cum = (jnp.dot(tri, g_hi, preferred_element_type=F32)
           + jnp.dot(tri, g_mid, preferred_element_type=F32)
           + jnp.dot(tri, g_lo, preferred_element_type=F32))
    gt_s[...] = gates.T
    ct_s[...] = cum.T
    m_rows = m_ref[...]
    m_next = m_rows

    def head_stages(hl):
        head = head0 + hl
        qs = slice(hl * M_DQK, (hl + 1) * M_DQK)
        vs = slice(hl * M_DV, (hl + 1) * M_DV)
        st = {}

        def gate_stage():
            b_c = lane_col(cum, M_HEADS + head)
            m_p = lane_col(m_rows, head)
            i_r = gt_s[pl.ds(head, 1), :]
            b_r = ct_s[pl.ds(M_HEADS + head, 1), :]
            log_d = jnp.where(causal, b_c - b_r + i_r, -jnp.inf)
            m_t = jnp.maximum(b_c + m_p, jnp.max(log_d, axis=1, keepdims=True))
            dmat = jnp.exp(log_d - m_t)
            e_int = jnp.exp(b_c + m_p - m_t)
            if nb == 1:
                last = slice(L - 1, L)
                m_new = jnp.broadcast_to(m_t[last], (L, 1))
                e_c = jnp.broadcast_to(e_int[last], (L, 1))
                w_mat = jnp.broadcast_to(dmat[last], (L, L))
            else:
                b_last = jnp.min(jnp.where(same, b_r, jnp.inf), axis=1, keepdims=True)
                log_w = jnp.where(same, b_last - b_r + i_r, -jnp.inf)
                m_new = jnp.maximum(b_last + m_p, jnp.max(log_w, axis=1, keepdims=True))
                w_mat = jnp.exp(log_w - m_new)
                e_c = jnp.exp(b_last + m_p - m_new)
            st.update(m_t=m_t, dmat=dmat, e_int=e_int, m_new=m_new, e_c=e_c, w_mat=w_mat)

        def score_stage():
            q, k = q_ref[:, qs], k_ref[:, qs]
            st["s"] = lax.dot_general(q, k, _NT, preferred_element_type=F32) * st.pop("dmat")

        def state_stage():
            q, k = q_ref[:, qs], k_ref[:, qs]
            v_t = v_ref[:, vs].astype(F32).T
            w_mat = st["w_mat"]
            e_cb = jnp.broadcast_to(st["e_c"], (L, LANES))
            inter = jnp.zeros((L, M_DV), F32)
            for j in range(nb):
                first = slice(j * tpb, j * tpb + 1)
                c_j = c_ref[j, hl]
                q_j = q if nb == 1 else jnp.where(row_seq == j, q, jnp.zeros_like(q))
                inter = inter + lax.dot_general(q_j, c_j.astype(BF16), _NT, preferred_element_type=F32)
                lhs = (v_t * w_mat[first]).astype(BF16)
                c_ref[j, hl] = e_cb[first] * c_j + jnp.dot(lhs, k, preferred_element_type=F32)
            st["inter"] = inter

        def output_stage():
            q, k, v = q_ref[:, qs], k_ref[:, qs], v_ref[:, vs]
            s, e_int, m_t = st.pop("s"), st.pop("e_int"), st.pop("m_t")
            n_rows = n_ref[:, qs]
            qn = jnp.sum(q.astype(F32) * n_rows, axis=1, keepdims=True)
            num = jnp.dot(s.astype(BF16), v, preferred_element_type=F32) + e_int * st.pop("inter")
            den = jnp.sum(s, axis=1, keepdims=True) + e_int * qn
            hh = num / jnp.maximum(jnp.abs(den), jnp.exp(-m_t))
            mu = jnp.mean(hh, axis=1, keepdims=True)
            dlt = hh - mu
            var = jnp.mean(dlt * dlt, axis=1, keepdims=True)
            y = dlt * lax.rsqrt(var + HEAD_NORM_EPS)
            out_ref[:, vs] = (y * ng_ref[:, vs] * mo_ref[:, vs].astype(F32)).astype(BF16)
            n_ref[:, qs] = (st.pop("e_c") * n_rows
                            + jnp.dot(st.pop("w_mat").astype(BF16), k, preferred_element_type=F32))

        return (gate_stage, score_stage, state_stage, output_stage), st

    heads = [head_stages(hl) for hl in range(hps)]
    for stage in range(4):
        if between is not None:
            between()
        for stages, _ in heads:
            stages[stage]()
    for hl, (_, st) in enumerate(heads):
        m_next = jnp.where(lane == head0 + hl, st["m_new"], m_next)

    m_ref[...] = m_next


def _mlstm(q, k, v, gates, mo, norm_g, state, nb, tpb, n_chunks, hps):
    m = q.shape[0]
    L = nb * tpb
    n_blocks = m // (L * n_chunks)
    hd = M_HEADS * M_DQK

    def tok(width):
        return pl.BlockSpec((L, width), lambda b, g, c: (b * n_chunks + c, g))

    c_spec = pl.BlockSpec((nb, hps, M_DV, M_DQK), lambda b, g, c: (b, g, 0, 0))
    n_spec = pl.BlockSpec((L, hps * M_DQK), lambda b, g, c: (b, g))
    m_spec = pl.BlockSpec((L, LANES), lambda b, g, c: (b, 0))
    in_specs = [tok(hps * M_DQK), tok(hps * M_DQK), tok(hps * M_DV),
                pl.BlockSpec((L, LANES), lambda b, g, c: (b * n_chunks + c, 0)),
                tok(hps * M_DV), pl.BlockSpec((1, hps * M_DV), lambda b, g, c: (0, g))]
    args = [q, k, v, gates, mo, norm_g.reshape(1, M_HEADS * M_DV)]
    if state is not None:
        in_specs += [c_spec, n_spec, m_spec]
        args += list(state)
    return pl.pallas_call(
        functools.partial(_mlstm_kernel, nb=nb, tpb=tpb, hps=hps, zero_init=state is None),
        grid=(n_blocks, M_HEADS // hps, n_chunks),
        in_specs=in_specs,
        out_specs=[tok(hps * M_DV), c_spec, n_spec, m_spec],
        out_shape=[jax.ShapeDtypeStruct((m, M_HEADS * M_DV), BF16),
                   jax.ShapeDtypeStruct((n_blocks * nb, M_HEADS, M_DV, M_DQK), F32),
                   jax.ShapeDtypeStruct((n_blocks * L, hd), F32),
                   jax.ShapeDtypeStruct((n_blocks * L, LANES), F32)],
        scratch_shapes=[pltpu.VMEM((LANES, L), F32), pltpu.VMEM((LANES, L), F32)],
        compiler_params=_params(),
        name="mlstm",
    )(*args)


def _swa_kernel(*refs, nbb, tq, chained, n_new):
    if n_new:
        sink_ref, q_ref, kp_ref, kc_ref, vp_ref, vc_ref, o_ref, kn_ref, vn_ref = refs
    else:
        sink_ref, q_ref, kp_ref, kc_ref, vp_ref, vc_ref, o_ref = refs
    qi = lax.broadcasted_iota(jnp.int32, (tq, WINDOW), 0)
    kj = lax.broadcasted_iota(jnp.int32, (tq, WINDOW), 1)

    def pad_keys(x):
        if x.shape[0] == WINDOW:
            return x
        return jnp.concatenate([x, jnp.zeros((WINDOW - x.shape[0], x.shape[1]), x.dtype)], axis=0)

    blocks = []
    for jb in range(nbb):
        kc, vc = pad_keys(kc_ref[jb]), pad_keys(vc_ref[jb])
        valid_prev = kj > qi
        if chained and jb > 0:
            kp, vp = kc_ref[jb - 1], vc_ref[jb - 1]
        else:
            kp, vp = kp_ref[jb], vp_ref[jb]
            if chained:
                valid_prev = valid_prev & (pl.program_id(1) > 0)

        if n_new:
            keep_old = lax.broadcasted_iota(jnp.int32, (WINDOW, W_AKV), 0) < WINDOW - n_new
            kn_ref[jb] = jnp.where(keep_old, pltpu.roll(kp, WINDOW - n_new, 0),
                                   pltpu.roll(kc, WINDOW - n_new, 0))
            vn_ref[jb] = jnp.where(keep_old, pltpu.roll(vp, WINDOW - n_new, 0),
                                   pltpu.roll(vc, WINDOW - n_new, 0))

        blocks.append(_swa_stages(sink_ref, q_ref[jb], kp, vp, kc, vc, valid_prev, o_ref.at[jb]))

    for stage in range(3):
        for pairs in blocks:
            for pair in pairs:
                pair[stage]()


def _swa_stages(sink_ref, q, kp, vp, kc, vc, valid_prev, o_ref):
    tq = q.shape[0]
    reps = A_HEADS // A_KV_HEADS
    low_q = lax.broadcasted_iota(jnp.int32, (tq, LANES), 1) < A_DH
    low_k = lax.broadcasted_iota(jnp.int32, (WINDOW, LANES), 1) < A_DH
    key_row = lax.broadcasted_iota(jnp.int32, (WINDOW, LANES), 0)
    qi = lax.broadcasted_iota(jnp.int32, (tq, WINDOW), 0)
    kj = lax.broadcasted_iota(jnp.int32, (tq, WINDOW), 1)
    sink_lane = kj == 0
    valid_cur = kj <= qi
    scale = jnp.asarray(ATTN_SCALE, BF16)
    neg_inf = -jnp.inf

    def block_diag(x):
        return jnp.concatenate([jnp.where(low_k, x, 0.0), jnp.where(low_k, 0.0, x)], axis=0).astype(BF16)

    def pair_stages(p):
        ks = slice(p * LANES, (p + 1) * LANES)
        state = {}

        def scores():
            q4 = jnp.concatenate([q[:, (reps * p + r) * LANES:(reps * p + r + 1) * LANES] * scale
                                  for r in range(reps)], axis=0)
            state["sp"] = lax.dot_general(q4, block_diag(kp[:, ks]), _NT, preferred_element_type=F32)
            state["sc"] = lax.dot_general(q4, block_diag(kc[:, ks]), _NT, preferred_element_type=F32)

        def softmax():
            sp, sc = state.pop("sp"), state.pop("sc")
            pps, pcs, invs = [], [], []
            for r in range(reps):
                rows = slice(r * tq, (r + 1) * tq)
                pp_r, pc_r, inv_r = [], [], []
                for half in range(2):
                    cols = slice(half * WINDOW, (half + 1) * WINDOW)
                    head = 2 * reps * p + reps * half + r
                    sp_i = jnp.where(sink_lane, sink_ref[head], jnp.where(valid_prev, sp[rows, cols], neg_inf))
                    sc_i = jnp.where(valid_cur, sc[rows, cols], neg_inf)
                    mx = jnp.max(jnp.maximum(sp_i, sc_i), axis=1, keepdims=True)
                    pp = jnp.exp(sp_i - mx)
                    pc = jnp.exp(sc_i - mx)
                    inv_r.append(1.0 / jnp.sum(pp + pc, axis=1, keepdims=True))
                    pp_r.append(pp.astype(BF16))
                    pc_r.append(pc.astype(BF16))
                pps.append(jnp.concatenate(pp_r, axis=1))
                pcs.append(jnp.concatenate(pc_r, axis=1))
                invs.append(jnp.where(low_q, inv_r[0], inv_r[1]))
            state.update(pp=jnp.concatenate(pps, axis=0), pc=jnp.concatenate(pcs, axis=0), inv=invs)

        def values():
            v_prev = block_diag(jnp.where(key_row == 0, 0.0, vp[:, ks]))
            o4 = (jnp.dot(state.pop("pp"), v_prev, preferred_element_type=F32)
                  + jnp.dot(state.pop("pc"), block_diag(vc[:, ks]), preferred_element_type=F32))
            invs = state.pop("inv")
            for r in range(reps):
                blk = reps * p + r
                o_ref[:, blk * LANES:(blk + 1) * LANES] = (o4[r * tq:(r + 1) * tq] * invs[r]).astype(BF16)

        return scores, softmax, values

    return [pair_stages(p) for p in range(A_KV_HEADS // 2)]


def _swa_block(sink_ref, q, kp, vp, kc, vc, valid_prev, o_ref, between=None):
    for scores, softmax, values in _swa_stages(sink_ref, q, kp, vp, kc, vc, valid_prev, o_ref):
        if between is not None:
            between()
        scores()
        softmax()
        if between is not None:
            between()
        values()


def _swa(sinks, q, k_prev_src, k_cur, v_prev_src, v_cur, *, nbb, chained, n_new, blocks_per_seq):
    n, tq, _ = q.shape
    tk = k_cur.shape[1]
    if chained:
        steps = blocks_per_seq // nbb
        grid = (n // blocks_per_seq, steps)
        cur_index = lambda b, i: (b * steps + i, 0, 0)
        prev_spec = pl.BlockSpec((1, WINDOW, W_AKV),
                                 lambda b, i: (b * blocks_per_seq + jnp.maximum(i * nbb - 1, 0), 0, 0))
    else:
        grid = (n // nbb,)
        cur_index = lambda i: (i, 0, 0)
        prev_spec = pl.BlockSpec((nbb, WINDOW, W_AKV), cur_index)
    cur_spec = pl.BlockSpec((nbb, tk, W_AKV), cur_index)
    q_spec = pl.BlockSpec((nbb, tq, W_AQ), cur_index)
    out_specs = [q_spec]
    out_shape = [jax.ShapeDtypeStruct((n, tq, W_AQ), BF16)]
    if n_new:
        out_specs += [prev_spec, prev_spec]
        out_shape += [jax.ShapeDtypeStruct((n, WINDOW, W_AKV), F32)] * 2
    return pl.pallas_call(
        functools.partial(_swa_kernel, nbb=nbb, tq=tq, chained=chained, n_new=n_new),
        grid=grid,
        in_specs=[pl.BlockSpec(memory_space=pltpu.SMEM), q_spec, prev_spec, cur_spec, prev_spec, cur_spec],
        out_specs=out_specs,
        out_shape=out_shape,
        compiler_params=_params(),
        name="swa",
    )(sinks, q, k_prev_src, k_cur, v_prev_src, v_cur)


def _merge_kernel(x_ref, g_ref, hm_ref, oa_ref, gm_ref, ga_ref, wm_ref, wa_ref, wo_ref,
                  lg_ref, lb_ref, o_ref):
    ym = jnp.dot(hm_ref[...], wm_ref[...], preferred_element_type=F32)
    ya = jnp.dot(oa_ref[...], wa_ref[...], preferred_element_type=F32)
    mix = gm_ref[...].astype(F32) * ym + ga_ref[...].astype(F32) * ya
    t = jnp.dot(mix.astype(BF16), wo_ref[...], preferred_element_type=F32)
    y = DEEPNORM_ALPHA * x_ref[...] + (1.0 + _rows(g_ref)) * t
    o_ref[...] = _layer_norm(y, lg_ref[...], lb_ref[...], LN_EPS)


def _merge(x, mod, per_row, rows_per_batch, chunk, hm, oa, gm, ga, wm, wa, wo, ln_g, ln_b):
    m = x.shape[0]
    tm = ROW_TILE

    def tok():
        return pl.BlockSpec((tm, D_MODEL), lambda i: (i, 0))

    return pl.pallas_call(
        _merge_kernel,
        grid=(m // tm,),
        in_specs=[tok()] + _mod_specs(per_row, tm, rows_per_batch // tm, (chunk,))
        + [tok(), tok(), tok(), tok()]
        + [_resident((D_MODEL, D_MODEL))] * 3 + [_resident((1, D_MODEL))] * 2,
        out_specs=tok(),
        out_shape=jax.ShapeDtypeStruct((m, D_MODEL), F32),
        compiler_params=_params(),
        name="merge",
    )(x, mod, hm, oa, gm, ga, wm, wa, wo, ln_g.reshape(1, D_MODEL), ln_b.reshape(1, D_MODEL))


def _mixer_kernel(sink_ref, xa_ref, xb_ref, sh_ref, sc_ref, g_ref, wa_ref, wq_ref, wr_ref, wif_ref, bif_ref,
                  cos_ref, sin_ref, ng_ref, wm_ref, wba_ref, wo_ref, lg_ref, lb_ref,
                  y_ref, c_ref, n_ref, m_ref, kb_ref, vb_ref, *scratch, tiles_per_seq):
    n_staged = (len(scratch) - 6) // 2
    even, odd = scratch[:n_staged], scratch[n_staged:2 * n_staged]
    kprev_s, vprev_s, hm_s, oa_s, gtt_s, ctt_s = scratch[2 * n_staged:]
    tm = xa_ref.shape[0]
    s = pl.program_id(0)
    tile_b = jnp.maximum(s - 1, 0)
    seq_start = tile_b % tiles_per_seq == 0

    @pl.when(s == 0)
    def _():
        for ref in odd:
            ref[...] = jnp.zeros_like(ref)

    @pl.when(seq_start)
    def _():
        c_ref[...] = jnp.zeros_like(c_ref)
        n_ref[...] = jnp.zeros_like(n_ref)
        m_ref[...] = jnp.zeros_like(m_ref)
        kprev_s[...] = jnp.zeros_like(kprev_s)
        vprev_s[...] = jnp.zeros_like(vprev_s)

    def step(produce, consume):
        proj_parts = _proj_parts(xa_ref, sh_ref, sc_ref, wa_ref, wq_ref, wr_ref, wif_ref, bif_ref,
                                 cos_ref, sin_ref, *produce)
        pending = iter(proj_parts)

        def emit(count):
            def between():
                for _ in range(count):
                    part = next(pending, None)
                    if part is not None:
                        part()
            return between

        q_s, k_s, v_s, mo_s, aq_s, ak_s, av_s, gm_s, ga_s, gt_s = consume
        qi = lax.broadcasted_iota(jnp.int32, (WINDOW, WINDOW), 0)
        kj = lax.broadcasted_iota(jnp.int32, (WINDOW, WINDOW), 1)
        for j in range(tm // MLSTM_CHUNK):
            rows = pl.ds(j * MLSTM_CHUNK, MLSTM_CHUNK)
            _mlstm_chunk(q_s.at[rows], k_s.at[rows], v_s.at[rows], gt_s.at[rows], mo_s.at[rows], ng_ref,
                         hm_s.at[rows], c_ref, n_ref, m_ref, gtt_s, ctt_s,
                         nb=1, tpb=MLSTM_CHUNK, hps=M_HEADS, head0=0, between=emit(MIXER_PROJ_PER_HEAD))
            valid_prev = kj > qi
            if j == 0:
                valid_prev = valid_prev & jnp.logical_not(seq_start)
                k_prev, v_prev = kprev_s[...], vprev_s[...]
            else:
                before = pl.ds((j - 1) * WINDOW, WINDOW)
                k_prev, v_prev = ak_s[before], av_s[before]
            _swa_block(sink_ref, aq_s[rows], k_prev, v_prev, ak_s[rows], av_s[rows], valid_prev,
                       oa_s.at[rows], between=emit(MIXER_PROJ_PER_ATTN))

        for part in pending:
            part()
        _merge_kernel(xb_ref, g_ref, hm_s, oa_s, gm_s, ga_s, wm_ref, wba_ref, wo_ref, lg_ref, lb_ref, y_ref)

        last = pl.ds(tm - WINDOW, WINDOW)
        k_last, v_last = ak_s[last], av_s[last]
        kb_ref[0] = k_last
        vb_ref[0] = v_last
        kprev_s[...] = k_last
        vprev_s[...] = v_last

    @pl.when(s % 2 == 0)
    def _():
        step(even, odd)

    @pl.when(s % 2 == 1)
    def _():
        step(odd, even)


def _mixer(x1, mod, w, sinks, norm_g, cos_t, sin_t, bp, sp):
    tm = MIXER_TILE
    tps = sp // tm
    n_tiles = bp * tps
    hd = M_HEADS * M_DQK
    weights = (w["w_a"], w["w_aq"], w["w_r"], w["w_if"], w["bif"])

    def tile_a(s):
        return jnp.minimum(s, n_tiles - 1)

    def tile_b(s):
        return jnp.maximum(s - 1, 0)

    def mod_spec(tile, chunk):
        return pl.BlockSpec((1, 1, D_MODEL), lambda s: (tile(s) // tps, 0, chunk))

    def per_seq(*shape):
        return pl.BlockSpec((1,) + shape, lambda s: (tile_b(s) // tps,) + (0,) * len(shape))

    pos = pl.BlockSpec((tm, LANES), lambda s: (tile_a(s) % tps, 0))
    staged = [(W_MQ, BF16), (W_MQ, BF16), (W_MV, BF16), (W_MV, BF16), (W_AQ, BF16),
              (W_AKV, F32), (W_AKV, F32), (D_MODEL, BF16), (D_MODEL, BF16), (LANES, F32)]
    return pl.pallas_call(
        functools.partial(_mixer_kernel, tiles_per_seq=tps),
        grid=(n_tiles + 1,),
        in_specs=[pl.BlockSpec(memory_space=pltpu.SMEM),
                  pl.BlockSpec((tm, D_MODEL), lambda s: (tile_a(s), 0)),
                  pl.BlockSpec((tm, D_MODEL), lambda s: (tile_b(s), 0)),
                  mod_spec(tile_a, 3), mod_spec(tile_a, 4), mod_spec(tile_b, 5)]
        + [_resident(a.shape) for a in weights] + [pos, pos, _resident((1, M_HEADS * M_DV))]
        + [_resident((D_MODEL, D_MODEL))] * 3 + [_resident((1, D_MODEL))] * 2,
        out_specs=[pl.BlockSpec((tm, D_MODEL), lambda s: (tile_b(s), 0)),
                   per_seq(M_HEADS, M_DV, M_DQK),
                   pl.BlockSpec((MLSTM_CHUNK, hd), lambda s: (tile_b(s) // tps, 0)),
                   pl.BlockSpec((MLSTM_CHUNK, LANES), lambda s: (tile_b(s) // tps, 0)),
                   per_seq(WINDOW, W_AKV), per_seq(WINDOW, W_AKV)],
        out_shape=[jax.ShapeDtypeStruct((bp * sp, D_MODEL), F32),
                   jax.ShapeDtypeStruct((bp, M_HEADS, M_DV, M_DQK), F32),
                   jax.ShapeDtypeStruct((bp * MLSTM_CHUNK, hd), F32),
                   jax.ShapeDtypeStruct((bp * MLSTM_CHUNK, LANES), F32),
                   jax.ShapeDtypeStruct((bp, WINDOW, W_AKV), F32),
                   jax.ShapeDtypeStruct((bp, WINDOW, W_AKV), F32)],
        scratch_shapes=[pltpu.VMEM((tm, width), dtype) for width, dtype in staged + staged]
        + [pltpu.VMEM((WINDOW, W_AKV), F32), pltpu.VMEM((WINDOW, W_AKV), F32),
           pltpu.VMEM((tm, W_MV), BF16), pltpu.VMEM((tm, W_AQ), BF16),
           pltpu.VMEM((LANES, MLSTM_CHUNK), F32), pltpu.VMEM((LANES, MLSTM_CHUNK), F32)],
        compiler_params=_params(),
        name="mixer",
    )(sinks, x1, x1, mod, mod, mod, *weights, cos_t, sin_t, norm_g.reshape(1, M_HEADS * M_DV),
      w["wm"], w["wa"], w["wo"], w["ln2_g"].reshape(1, D_MODEL), w["ln2_b"].reshape(1, D_MODEL))


def _rope_tables(pos):
    half = A_DH // 2
    inv = ROPE_THETA ** (-jnp.arange(half, dtype=F32) / half)
    ang = pos.astype(F32)[:, None] * inv[None, :]
    cos, sin = jnp.cos(ang), jnp.sin(ang)
    return jnp.tile(cos, (1, 4)), jnp.concatenate([-sin, sin, -sin, sin], axis=1)


def _pair_heads(x, axis):
    shape = x.shape
    reps = A_HEADS // A_KV_HEADS
    x = x.reshape(shape[:axis] + (A_KV_HEADS // 2, 2, reps, A_DH) + shape[axis + 1:])
    return jnp.swapaxes(x, axis + 1, axis + 2).reshape(shape)


def _token_stage_1(x, mod, per_row, rows_per_batch, w, pos):
    x1 = _ffn(x, mod, per_row, rows_per_batch, (0, 1, 2), w["up1"], w["down1"], w["ln1_g"], w["ln1_b"])
    cos_t, sin_t = _rope_tables(pos)
    return x1, _proj(x1, mod, per_row, rows_per_batch, (3, 4), w, cos_t, sin_t)


def _token_stage_2(x1, mod, per_row, rows_per_batch, w, hm, oa, gm, ga):
    x2 = _merge(x1, mod, per_row, rows_per_batch, 5, hm, oa, gm, ga,
                w["wm"], w["wa"], w["wo"], w["ln2_g"], w["ln2_b"])
    return _ffn(x2, mod, per_row, rows_per_batch, (6, 7, 8), w["up2"], w["down2"], w["ln3_g"], w["ln3_b"])


def kernel(x_prompt, x_sample, state_mlstm_C, state_mlstm_n, state_mlstm_m, cache_swa_k, cache_swa_v, c_prompt, c_sample, w_ada, b_ada, w_ffn1_up, w_ffn1_down, ln1_g, ln1_b, w_in, b_igate, b_fgate, m_norm_g, sinks, w_branch_m, w_branch_a, w_out, ln2_g, ln2_b, w_ffn2_up, w_ffn2_down, ln3_g, ln3_b):
    assert w_ada.shape[0] == DEPTH == 1
    bp, sp, _ = x_prompt.shape
    bs, ts, _ = x_sample.shape

    win = w_in[0]
    w = dict(
        up1=w_ffn1_up[0].astype(BF16), down1=w_ffn1_down[0].astype(BF16),
        w_aq=_pair_heads(win[:, IN_AQ:IN_AK], 1).astype(BF16),
        w_r=win[:, IN_AK:IN_END].astype(BF16),
        w_if=jnp.pad(win[:, IN_IF:IN_AQ], ((0, 0), (0, LANES - 2 * M_HEADS))).astype(BF16),
        bif=jnp.concatenate([b_igate[0], b_fgate[0], jnp.zeros((LANES - 2 * M_HEADS,), F32)]).reshape(1, LANES),
        ln1_g=ln1_g[0], ln1_b=ln1_b[0], ln2_g=ln2_g[0], ln2_b=ln2_b[0], ln3_g=ln3_g[0], ln3_b=ln3_b[0],
    )

    ms = bs * ts
    c_all = jnp.concatenate([jnp.repeat(c_sample, ts, axis=0), c_prompt], axis=0)
    mod = _ada(c_all, w_ada[0], b_ada[0])
    mod_p = mod[ms:].reshape(bp, 1, ADA_CHUNKS * D_MODEL)

    mp = bp * sp
    reps = A_HEADS // A_KV_HEADS
    up_blocks = 2 * D_FF // FF_CHUNK
    jobs = (
        _cast_job(w_ffn2_up[0], (D_MODEL, FF_CHUNK), 1),
        _cast_job(w_ffn2_down[0], (FF_CHUNK, D_MODEL), 0, first_step=up_blocks),
        _cast_job(win, (D_MODEL, LANES), 1, out_len=IN_IF // LANES),
        _cast_job(w_branch_m[0], (D_MODEL, LANES), 1),
        _cast_job(w_out[0], (D_MODEL, LANES), 1),
        _cast_job(w_branch_a[0], (A_DH, D_MODEL), 0,
                  permute=lambda j: (j // (2 * reps)) * (2 * reps) + (j % 2) * reps + (j % (2 * reps)) // 2),
    )
    x1p, w["up2"], w["down2"], w["w_a"], w["wm"], w["wo"], w["wa"] = _ffn(
        x_prompt.reshape(mp, D_MODEL), mod_p, False, sp, (0, 1, 2),
        w["up1"], w["down1"], w["ln1_g"], w["ln1_b"], cast_jobs=jobs)
    cos_t, sin_t = _rope_tables(jnp.arange(sp))
    x2p, c_p, n_rows, m_rows, kb_p, vb_p = _mixer(x1p, mod_p, w, sinks[0], m_norm_g[0], cos_t, sin_t, bp, sp)
    y_p = _ffn(x2p, mod_p, False, sp, (6, 7, 8), w["up2"], w["down2"], w["ln3_g"], w["ln3_b"])
    n_p = n_rows.reshape(bp, MLSTM_CHUNK, M_HEADS, M_DQK)[:, 0]
    m_p = m_rows.reshape(bp, MLSTM_CHUNK, LANES)[:, 0, :M_HEADS]
    kb_p = kb_p.reshape(bp, WINDOW, A_KV_HEADS, A_DH)
    vb_p = vb_p.reshape(bp, WINDOW, A_KV_HEADS, A_DH)

    x1s, (qm, km, vm, mo, aq, ak, av, gm, ga, gt) = _token_stage_1(
        x_sample.reshape(ms, D_MODEL), mod, True, ms, w, PAST_LEN + jnp.arange(ms) % ts)
    seqs = MLSTM_CHUNK // ts
    n0_rows = jnp.repeat(state_mlstm_n[0].reshape(bs, M_HEADS * M_DQK), ts, axis=0)
    m0_rows = jnp.repeat(jnp.pad(state_mlstm_m[0], ((0, 0), (0, LANES - M_HEADS))), ts, axis=0)
    hm, c_s, n_rows, m_rows = _mlstm(qm, km, vm, gt, mo, m_norm_g[0],
                                     (state_mlstm_C[0], n0_rows, m0_rows), seqs, ts, 1, 1)
    n_s = n_rows.reshape(bs, ts, M_HEADS, M_DQK)[:, 0]
    m_s = m_rows.reshape(bs, ts, LANES)[:, 0, :M_HEADS]
    pad_t = ((0, 0), (0, SAMPLE_TQ - ts), (0, 0))
    oa, kb_s, vb_s = _swa(
        sinks[0], jnp.pad(aq.reshape(bs, ts, W_AQ), pad_t),
        cache_swa_k[0].reshape(bs, WINDOW, W_AKV), jnp.pad(ak.reshape(bs, ts, W_AKV), pad_t),
        cache_swa_v[0].reshape(bs, WINDOW, W_AKV), jnp.pad(av.reshape(bs, ts, W_AKV), pad_t),
        nbb=SWA_SAMPLE_SEQS, chained=False, n_new=ts, blocks_per_seq=1)
    y_s = _token_stage_2(x1s, mod, True, ms, w, hm, oa[:, :ts].reshape(ms, D_MODEL), gm, ga)
    kb_s = kb_s.reshape(bs, WINDOW, A_KV_HEADS, A_DH)
    vb_s = vb_s.reshape(bs, WINDOW, A_KV_HEADS, A_DH)

    return (y_p.reshape(bp, sp, D_MODEL), y_s.reshape(bs, ts, D_MODEL),
            c_p[None], n_p[None], m_p[None], kb_p[None], vb_p[None],
            c_s[None], n_s[None], m_s[None], kb_s[None], vb_s[None])
```

```python
import functools

import jax
import jax.numpy as jnp
from jax import lax
from jax.experimental import pallas as pl
from jax.experimental.pallas import tpu as pltpu

F32 = jnp.float32
BF16 = jnp.bfloat16

D_MODEL = 1024
D_FF = 2816
DEPTH = 1
M_HEADS = 4
M_DQK = 128
M_DV = 256
A_HEADS = 16
A_KV_HEADS = 4
A_DH = 64
WINDOW = 128
PAST_LEN = 8192
ROPE_THETA = 10000.0
ATTN_SCALE = A_DH ** -0.5
LN_EPS = 1e-5
HEAD_NORM_EPS = 1e-6
ADA_CHUNKS = 9
DEEPNORM_ALPHA = (2.0 * DEPTH) ** 0.25
K_SCALE = M_DQK ** -0.5

LANES = 128
BF16_SUBLANES = 16
VMEM_LIMIT_BYTES = 56 * 1024 * 1024

W_MQ = M_HEADS * M_DQK
W_MV = M_HEADS * M_DV
W_AQ = A_HEADS * A_DH
W_AKV = A_KV_HEADS * A_DH
IN_IF = 2 * W_MQ + 2 * W_MV
IN_AQ = IN_IF + 2 * M_HEADS
IN_AK = IN_AQ + W_AQ
IN_END = IN_AK + 2 * W_AKV + 2 * D_MODEL
A_Q, A_K, A_V, A_O = 0, W_MQ, 2 * W_MQ, 2 * W_MQ + W_MV
R_AK, R_AV, R_GM, R_GA = 0, W_AKV, 2 * W_AKV, 2 * W_AKV + D_MODEL

ROW_TILE = 512
MIXER_TILE = 256
MIXER_PROJ_PER_HEAD = 1
MIXER_PROJ_PER_ATTN = 3
ADA_TILE = 1536
FF_CHUNK = 256
FFN_EPILOGUE_PIECES = 4
MLSTM_CHUNK = 128
SAMPLE_TQ = BF16_SUBLANES
SWA_PROMPT_BLOCKS = 2
SWA_SAMPLE_SEQS = 16

_NT = (((1,), (1,)), ((), ()))


def _params():
    return pltpu.CompilerParams(vmem_limit_bytes=VMEM_LIMIT_BYTES)


def _resident(shape):
    return pl.BlockSpec(shape, lambda *_: (0,) * len(shape), pipeline_mode=pl.Buffered(1))


def _rows(ref):
    v = ref[...]
    return v.reshape(v.shape[-2], v.shape[-1])


def _layer_norm(y, g, b, eps):
    mu = jnp.mean(y, axis=-1, keepdims=True)
    d = y - mu
    var = jnp.mean(d * d, axis=-1, keepdims=True)
    return d * lax.rsqrt(var + eps) * g + b


def _sigmoid(x):
    return 1.0 / (1.0 + jnp.exp(-x))


def _mod_specs(per_row, tm, tiles_per_batch, chunks):
    if per_row:
        return [pl.BlockSpec((tm, D_MODEL), lambda i, c=c: (i, c)) for c in chunks]
    return [pl.BlockSpec((1, 1, D_MODEL), lambda i, c=c: (i // tiles_per_batch, 0, c)) for c in chunks]


def _ada_kernel(c_ref, w_ref, b_ref, o_ref):
    c = c_ref[...]
    s = (c * _sigmoid(c)).astype(BF16)
    o_ref[...] = jnp.dot(s, w_ref[...].astype(BF16), preferred_element_type=F32) + b_ref[...]


def _ada(c_all, w_ada, b_ada):
    rows = c_all.shape[0]
    n_out = w_ada.shape[1]
    tn = ADA_TILE
    return pl.pallas_call(
        _ada_kernel,
        grid=(n_out // tn,),
        in_specs=[pl.BlockSpec((rows, D_MODEL), lambda j: (0, 0)),
                  pl.BlockSpec((D_MODEL, tn), lambda j: (0, j)),
                  pl.BlockSpec((1, tn), lambda j: (0, j))],
        out_specs=pl.BlockSpec((rows, tn), lambda j: (0, j)),
        out_shape=jax.ShapeDtypeStruct((rows, n_out), F32),
        compiler_params=_params(),
        name="ada",
    )(c_all, w_ada, b_ada.reshape(1, n_out))


def _ffn_kernel(x_ref, xp_ref, sh_ref, sc_ref, g_ref, wup_ref, wdn_ref, lg_ref, lb_ref, *rest,
                n_tiles, cast_layout):
    n_cast = len(cast_layout)
    n_src = sum(layout[0] for layout in cast_layout)
    cast_in, o_ref, cast_out = rest[:n_src], rest[n_src], rest[n_src + 1:n_src + 1 + n_cast]
    act_ref, f_ref = rest[n_src + 1 + n_cast:]

    def cast_chunk(job):
        if job >= n_cast:
            return
        first = sum(layout[0] for layout in cast_layout[:job])
        n, transposed, shift = cast_layout[job]
        refs = cast_in[first:first + n]
        if shift:
            rows = refs[0].shape[0]
            parts = [jnp.concatenate([lo[...], hi[...]], axis=0)[shift:shift + rows]
                     for lo, hi in zip(refs[0::2], refs[1::2])]
        else:
            parts = [ref[...] for ref in refs]
        block = jnp.concatenate(parts, axis=0)
        cast_out[job][...] = (block.T if transposed else block).astype(BF16)

    tm = x_ref.shape[0]
    s = pl.program_id(0)
    piece_rows = tm // FFN_EPILOGUE_PIECES

    @pl.when(s == 0)
    def _():
        f_ref[...] = jnp.zeros_like(f_ref)

    def epilogue(piece):
        rows = pl.ds(piece * piece_rows, piece_rows)
        g = _rows(g_ref)
        if g.shape[0] != 1:
            g = g[piece * piece_rows:(piece + 1) * piece_rows]
        y = DEEPNORM_ALPHA * xp_ref[rows, :] + (0.5 * (1.0 + g)) * f_ref[rows, :]
        o_ref[rows, :] = _layer_norm(y, lg_ref[...], lb_ref[...], LN_EPS)

    @pl.when(s < n_tiles)
    def _():
        h = (x_ref[...] * (1.0 + _rows(sc_ref)) + _rows(sh_ref)).astype(BF16)
        chunks = list(range(0, D_FF, FF_CHUNK))
        every = len(chunks) // FFN_EPILOGUE_PIECES
        piece = 0
        for i, c in enumerate(chunks):
            a = jnp.dot(h, wup_ref[:, c:c + FF_CHUNK], preferred_element_type=F32)
            u = jnp.dot(h, wup_ref[:, D_FF + c:D_FF + c + FF_CHUNK], preferred_element_type=F32)
            act_ref[:, c:c + FF_CHUNK] = (a * _sigmoid(a) * u).astype(BF16)
            cast_chunk(i)
            if i % every == every - 1 and piece < FFN_EPILOGUE_PIECES:
                epilogue(piece)
                piece += 1
        f_ref[...] = jnp.dot(act_ref[...], wdn_ref[...], preferred_element_type=F32)

    @pl.when(s == n_tiles)
    def _():
        for job in range(n_cast):
            cast_chunk(job)
        for piece in range(FFN_EPILOGUE_PIECES):
            epilogue(piece)


def _cast_job(array, block, axis, first_step=0, sources=None, transposed=False, n_blocks=None, shift=0):
    sources = sources or (lambda j: j,)
    src_axis = 1 - axis if transposed else axis
    src_block = block[::-1] if transposed else block
    src_block = (src_block[0] // len(sources), src_block[1])
    if n_blocks is None:
        n_blocks = array.shape[src_axis] // block[axis]

    def step_block(s):
        return jnp.clip(s - first_step, 0, n_blocks - 1)

    def at(axis_, j):
        return (j, 0) if axis_ == 0 else (0, j)

    in_specs = [pl.BlockSpec(src_block, lambda s, f=f, d=d: at(src_axis, f(step_block(s)) + d))
                for f in sources for d in range(2 if shift else 1)]
    out_dims = list(array.shape[::-1] if transposed else array.shape)
    out_dims[axis] = n_blocks * block[axis]
    return dict(array=array, in_specs=in_specs, layout=(len(in_specs), transposed, shift),
                out_spec=pl.BlockSpec(block, lambda s: at(axis, step_block(s))),
                out_shape=jax.ShapeDtypeStruct(tuple(out_dims), BF16), steps=first_step + n_blocks)


def _ffn(x, mod, per_row, rows_per_batch, chunks, w_up, w_down, ln_g, ln_b, cast_jobs=()):
    m = x.shape[0]
    tm = ROW_TILE
    n_tiles = m // tm
    tpb = rows_per_batch // tm

    def cur(s):
        return jnp.minimum(s, n_tiles - 1)

    def prev(s):
        return jnp.maximum(s - 1, 0)

    def mod_spec(tile, chunk):
        if per_row:
            return pl.BlockSpec((tm, D_MODEL), lambda s: (tile(s), chunk))
        return pl.BlockSpec((1, 1, D_MODEL), lambda s: (tile(s) // tpb, 0, chunk))

    in_specs = [pl.BlockSpec((tm, D_MODEL), lambda s: (cur(s), 0)),
                pl.BlockSpec((tm, D_MODEL), lambda s: (prev(s), 0)),
                mod_spec(cur, chunks[0]), mod_spec(cur, chunks[1]), mod_spec(prev, chunks[2]),
                _resident((D_MODEL, 2 * D_FF)), _resident((D_FF, D_MODEL)),
                _resident((1, D_MODEL)), _resident((1, D_MODEL))]
    args = [x, x, mod, mod, mod, w_up, w_down, ln_g.reshape(1, D_MODEL), ln_b.reshape(1, D_MODEL)]
    out_specs = [pl.BlockSpec((tm, D_MODEL), lambda s: (prev(s), 0))]
    out_shape = [jax.ShapeDtypeStruct((m, D_MODEL), F32)]
    for job in cast_jobs:
        assert job["steps"] <= n_tiles + 1
        in_specs += job["in_specs"]
        args += [job["array"]] * len(job["in_specs"])
        out_specs.append(job["out_spec"])
        out_shape.append(job["out_shape"])
    outs = pl.pallas_call(
        functools.partial(_ffn_kernel, n_tiles=n_tiles, cast_layout=tuple(j["layout"] for j in cast_jobs)),
        grid=(n_tiles + 1,),
        in_specs=in_specs,
        out_specs=out_specs,
        out_shape=out_shape,
        scratch_shapes=[pltpu.VMEM((tm, D_FF), BF16), pltpu.VMEM((tm, D_MODEL), F32)],
        compiler_params=_params(),
        name="ffn",
    )(*args)
    return outs if cast_jobs else outs[0]


def _proj_parts(x_ref, sh_ref, sc_ref, wa_ref, wq_ref, wr_ref, wif_ref, bif_ref, cos_ref, sin_ref,
                q_ref, k_ref, v_ref, o_ref, aq_ref, ak_ref, av_ref, gm_ref, ga_ref, gt_ref):
    x = x_ref[...]
    tm = x.shape[0]
    h = (x * (1.0 + _rows(sc_ref)) + _rows(sh_ref)).astype(BF16)
    lane = lax.broadcasted_iota(jnp.int32, (tm, LANES), 1)

    def seg(w_ref, lo, width=256):
        return jnp.dot(h, w_ref[:, lo:lo + width], preferred_element_type=F32)

    def plain(dst_ref, w_ref, lo, c, scale=None):
        def run():
            z = seg(w_ref, lo + c)
            dst_ref[:, c:c + 256] = (z if scale is None else z * scale).astype(dst_ref.dtype)
        return run

    def gate(dst_ref, w_ref, lo, c):
        def run():
            dst_ref[:, c:c + 256] = _sigmoid(seg(w_ref, lo + c)).astype(BF16)
        return run

    def forget_input_gates():
        zg = jnp.dot(h, wif_ref[...], preferred_element_type=F32) + bif_ref[...]
        logsig = jnp.minimum(zg, 0.0) - jnp.log(1.0 + jnp.exp(-jnp.abs(zg)))
        gt_ref[...] = jnp.where(lane < M_HEADS, zg, logsig)

    def rotary(dst_ref, w_ref, lo, c):
        def run():
            cos = cos_ref[...]
            sin = sin_ref[...]
            low_half = (lane & (A_DH // 2)) == 0
            z = seg(w_ref, lo + c)
            for half in range(2):
                zh = z[:, half * LANES:(half + 1) * LANES]
                partner = jnp.where(low_half, pltpu.roll(zh, LANES - A_DH // 2, 1),
                                    pltpu.roll(zh, A_DH // 2, 1))
                dst_ref[:, c + half * LANES:c + (half + 1) * LANES] = (
                    zh * cos + partner * sin).astype(dst_ref.dtype)
        return run

    parts = [plain(q_ref, wa_ref, A_Q, c) for c in range(0, W_MQ, 256)]
    parts += [plain(k_ref, wa_ref, A_K, c, K_SCALE) for c in range(0, W_MQ, 256)]
    parts += [plain(v_ref, wa_ref, A_V, c) for c in range(0, W_MV, 256)]
    parts += [forget_input_gates]
    parts += [rotary(aq_ref, wq_ref, 0, c) for c in range(0, W_AQ, 256)]
    parts += [rotary(ak_ref, wr_ref, R_AK, 0), plain(av_ref, wr_ref, R_AV, 0)]
    parts += [gate(o_ref, wa_ref, A_O, c) for c in range(0, W_MV, 256)]
    parts += [gate(gm_ref, wr_ref, R_GM, c) for c in range(0, D_MODEL, 256)]
    parts += [gate(ga_ref, wr_ref, R_GA, c) for c in range(0, D_MODEL, 256)]
    return parts


def _proj_kernel(*refs):
    for part in _proj_parts(*refs):
        part()


def _proj(x, mod, per_row, rows_per_batch, chunks, w, cos_t, sin_t):
    m = x.shape[0]
    tm = ROW_TILE
    n_pos_tiles = cos_t.shape[0] // tm

    def tok(width):
        return pl.BlockSpec((tm, width), lambda i: (i, 0))

    widths = (W_MQ, W_MQ, W_MV, W_MV, W_AQ, W_AKV, W_AKV, D_MODEL, D_MODEL, LANES)
    dtypes = (BF16, BF16, BF16, BF16, BF16, F32, F32, BF16, BF16, F32)
    weights = (w["w_a"], w["w_aq"], w["w_r"], w["w_if"], w["bif"])
    return pl.pallas_call(
        _proj_kernel,
        grid=(m // tm,),
        in_specs=[tok(D_MODEL)] + _mod_specs(per_row, tm, rows_per_batch // tm, chunks)
        + [_resident(a.shape) for a in weights]
        + [pl.BlockSpec((tm, LANES), lambda i: (i % n_pos_tiles, 0)),
           pl.BlockSpec((tm, LANES), lambda i: (i % n_pos_tiles, 0))],
        out_specs=[tok(wd) for wd in widths],
        out_shape=[jax.ShapeDtypeStruct((m, wd), d) for wd, d in zip(widths, dtypes)],
        compiler_params=_params(),
        name="proj",
    )(x, mod, mod, *weights, cos_t, sin_t)


def _mlstm_kernel(*refs, nb, tpb, hps, zero_init):
    if zero_init:
        (q_ref, k_ref, v_ref, g_ref, mo_ref, ng_ref,
         out_ref, c_ref, n_ref, m_ref, gt_s, ct_s) = refs
    else:
        (q_ref, k_ref, v_ref, g_ref, mo_ref, ng_ref, c0_ref, n0_ref, m0_ref,
         out_ref, c_ref, n_ref, m_ref, gt_s, ct_s) = refs
    first_chunk = pl.program_id(2) == 0

    @pl.when(first_chunk)
    def _():
        if zero_init:
            c_ref[...] = jnp.zeros_like(c_ref)
            n_ref[...] = jnp.zeros_like(n_ref)
        else:
            c_ref[...] = c0_ref[...]
            n_ref[...] = n0_ref[...]

    @pl.when(first_chunk & (pl.program_id(1) == 0))
    def _():
        m_ref[...] = jnp.zeros_like(m_ref) if zero_init else m0_ref[...]

    head0 = 0 if hps == M_HEADS else pl.program_id(1) * hps
    _mlstm_chunk(q_ref, k_ref, v_ref, g_ref, mo_ref, ng_ref, out_ref, c_ref, n_ref, m_ref, gt_s, ct_s,
                 nb=nb, tpb=tpb, hps=hps, head0=head0)


def _mlstm_chunk(q_ref, k_ref, v_ref, g_ref, mo_ref, ng_ref, out_ref, c_ref, n_ref, m_ref, gt_s, ct_s,
                 *, nb, tpb, hps, head0, between=None):
    L = nb * tpb
    shift = tpb.bit_length() - 1
    row = lax.broadcasted_iota(jnp.int32, (L, L), 0)
    col = lax.broadcasted_iota(jnp.int32, (L, L), 1)
    same = (row >> shift) == (col >> shift)
    causal = same & (col <= row)
    lane = lax.broadcasted_iota(jnp.int32, (L, LANES), 1)
    row_seq = lax.broadcasted_iota(jnp.int32, (L, 1), 0) >> shift

    def lane_col(x, idx):
        return jnp.sum(jnp.where(lane == idx, x, 0.0), axis=1, keepdims=True)

    gates = g_ref[...]
    tri = jnp.where(causal, 1.0, 0.0).astype(BF16)
    g_hi = gates.astype(BF16)
    rem = gates - g_hi.astype(F32)
    g_mid = rem.astype(BF16)
    g_lo = (rem - g_mid.astype(F32)).astype(BF16)
    cum = (jnp.dot(tri, g_hi, preferred_element_type=F32)
           + jnp.dot(tri, g_mid, preferred_element_type=F32)
           + jnp.dot(tri, g_lo, preferred_element_type=F32))
    gt_s[...] = gates.T
    ct_s[...] = cum.T
    m_rows = m_ref[...]
    m_next = m_rows

    def head_stages(hl):
        head = head0 + hl
        qs = slice(hl * M_DQK, (hl + 1) * M_DQK)
        vs = slice(hl * M_DV, (hl + 1) * M_DV)
        st = {}

        def gate_stage():
            b_c = lane_col(cum, M_HEADS + head)
            m_p = lane_col(m_rows, head)
            i_r = gt_s[pl.ds(head, 1), :]
            b_r = ct_s[pl.ds(M_HEADS + head, 1), :]
            log_d = jnp.where(causal, b_c - b_r + i_r, -jnp.inf)
            m_t = jnp.maximum(b_c + m_p, jnp.max(log_d, axis=1, keepdims=True))
            dmat = jnp.exp(log_d - m_t)
            e_int = jnp.exp(b_c + m_p - m_t)
            if nb == 1:
                last = slice(L - 1, L)
                m_new = jnp.broadcast_to(m_t[last], (L, 1))
                e_c = jnp.broadcast_to(e_int[last], (L, 1))
                w_mat = jnp.broadcast_to(dmat[last], (L, L))
            else:
                b_last = jnp.min(jnp.where(same, b_r, jnp.inf), axis=1, keepdims=True)
                log_w = jnp.where(same, b_last - b_r + i_r, -jnp.inf)
                m_new = jnp.maximum(b_last + m_p, jnp.max(log_w, axis=1, keepdims=True))
                w_mat = jnp.exp(log_w - m_new)
                e_c = jnp.exp(b_last + m_p - m_new)
            st.update(m_t=m_t, dmat=dmat, e_int=e_int, m_new=m_new, e_c=e_c, w_mat=w_mat)

        def score_stage():
            q, k = q_ref[:, qs], k_ref[:, qs]
            st["s"] = lax.dot_general(q, k, _NT, preferred_element_type=F32) * st.pop("dmat")

        def state_stage():
            q, k = q_ref[:, qs], k_ref[:, qs]
            v_t = v_ref[:, vs].astype(F32).T
            w_mat = st["w_mat"]
            e_cb = jnp.broadcast_to(st["e_c"], (L, LANES))
            inter = jnp.zeros((L, M_DV), F32)
            for j in range(nb):
                first = slice(j * tpb, j * tpb + 1)
                c_j = c_ref[j, hl]
                q_j = q if nb == 1 else jnp.where(row_seq == j, q, jnp.zeros_like(q))
                inter = inter + lax.dot_general(q_j, c_j.astype(BF16), _NT, preferred_element_type=F32)
                lhs = (v_t * w_mat[first]).astype(BF16)
                c_ref[j, hl] = e_cb[first] * c_j + jnp.dot(lhs, k, preferred_element_type=F32)
            st["inter"] = inter

        def output_stage():
            q, k, v = q_ref[:, qs], k_ref[:, qs], v_ref[:, vs]
            s, e_int, m_t = st.pop("s"), st.pop("e_int"), st.pop("m_t")
            n_rows = n_ref[:, qs]
            qn = jnp.sum(q.astype(F32) * n_rows, axis=1, keepdims=True)
            num = jnp.dot(s.astype(BF16), v, preferred_element_type=F32) + e_int * st.pop("inter")
            den = jnp.sum(s, axis=1, keepdims=True) + e_int * qn
            hh = num / jnp.maximum(jnp.abs(den), jnp.exp(-m_t))
            mu = jnp.mean(hh, axis=1, keepdims=True)
            dlt = hh - mu
            var = jnp.mean(dlt * dlt, axis=1, keepdims=True)
            y = dlt * lax.rsqrt(var + HEAD_NORM_EPS)
            out_ref[:, vs] = (y * ng_ref[:, vs] * mo_ref[:, vs].astype(F32)).astype(BF16)
            n_ref[:, qs] = (st.pop("e_c") * n_rows
                            + jnp.dot(st.pop("w_mat").astype(BF16), k, preferred_element_type=F32))

        return (gate_stage, score_stage, state_stage, output_stage), st

    heads = [head_stages(hl) for hl in range(hps)]
    for stage in range(4):
        if between is not None:
            between()
        for stages, _ in heads:
            stages[stage]()
    for hl, (_, st) in enumerate(heads):
        m_next = jnp.where(lane == head0 + hl, st["m_new"], m_next)

    m_ref[...] = m_next


def _mlstm(q, k, v, gates, mo, norm_g, state, nb, tpb, n_chunks, hps):
    m = q.shape[0]
    L = nb * tpb
    n_blocks = m // (L * n_chunks)
    hd = M_HEADS * M_DQK

    def tok(width):
        return pl.BlockSpec((L, width), lambda b, g, c: (b * n_chunks + c, g))

    c_spec = pl.BlockSpec((nb, hps, M_DV, M_DQK), lambda b, g, c: (b, g, 0, 0))
    n_spec = pl.BlockSpec((L, hps * M_DQK), lambda b, g, c: (b, g))
    m_spec = pl.BlockSpec((L, LANES), lambda b, g, c: (b, 0))
    in_specs = [tok(hps * M_DQK), tok(hps * M_DQK), tok(hps * M_DV),
                pl.BlockSpec((L, LANES), lambda b, g, c: (b * n_chunks + c, 0)),
                tok(hps * M_DV), pl.BlockSpec((1, hps * M_DV), lambda b, g, c: (0, g))]
    args = [q, k, v, gates, mo, norm_g.reshape(1, M_HEADS * M_DV)]
    if state is not None:
        in_specs += [c_spec, n_spec, m_spec]
        args += list(state)
    return pl.pallas_call(
        functools.partial(_mlstm_kernel, nb=nb, tpb=tpb, hps=hps, zero_init=state is None),
        grid=(n_blocks, M_HEADS // hps, n_chunks),
        in_specs=in_specs,
        out_specs=[tok(hps * M_DV), c_spec, n_spec, m_spec],
        out_shape=[jax.ShapeDtypeStruct((m, M_HEADS * M_DV), BF16),
                   jax.ShapeDtypeStruct((n_blocks * nb, M_HEADS, M_DV, M_DQK), F32),
                   jax.ShapeDtypeStruct((n_blocks * L, hd), F32),
                   jax.ShapeDtypeStruct((n_blocks * L, LANES), F32)],
        scratch_shapes=[pltpu.VMEM((LANES, L), F32), pltpu.VMEM((LANES, L), F32)],
        compiler_params=_params(),
        name="mlstm",
    )(*args)


def _swa_kernel(*refs, nbb, tq, chained, n_new):
    if n_new:
        sink_ref, q_ref, kp_ref, kc_ref, vp_ref, vc_ref, o_ref, kn_ref, vn_ref = refs
    else:
        sink_ref, q_ref, kp_ref, kc_ref, vp_ref, vc_ref, o_ref = refs
    qi = lax.broadcasted_iota(jnp.int32, (tq, WINDOW), 0)
    kj = lax.broadcasted_iota(jnp.int32, (tq, WINDOW), 1)

    def pad_keys(x):
        if x.shape[0] == WINDOW:
            return x
        return jnp.concatenate([x, jnp.zeros((WINDOW - x.shape[0], x.shape[1]), x.dtype)], axis=0)

    blocks = []
    for jb in range(nbb):
        kc, vc = pad_keys(kc_ref[jb]), pad_keys(vc_ref[jb])
        valid_prev = kj > qi
        if chained and jb > 0:
            kp, vp = kc_ref[jb - 1], vc_ref[jb - 1]
        else:
            kp, vp = kp_ref[jb], vp_ref[jb]
            if chained:
                valid_prev = valid_prev & (pl.program_id(1) > 0)

        if n_new:
            keep_old = lax.broadcasted_iota(jnp.int32, (WINDOW, W_AKV), 0) < WINDOW - n_new
            kn_ref[jb] = jnp.where(keep_old, pltpu.roll(kp, WINDOW - n_new, 0),
                                   pltpu.roll(kc, WINDOW - n_new, 0))
            vn_ref[jb] = jnp.where(keep_old, pltpu.roll(vp, WINDOW - n_new, 0),
                                   pltpu.roll(vc, WINDOW - n_new, 0))

        blocks.append(_swa_stages(sink_ref, q_ref[jb], kp, vp, kc, vc, valid_prev, o_ref.at[jb]))

    for stage in range(3):
        for pairs in blocks:
            for pair in pairs:
                pair[stage]()


def _swa_stages(sink_ref, q, kp, vp, kc, vc, valid_prev, o_ref):
    tq = q.shape[0]
    reps = A_HEADS // A_KV_HEADS
    low_q = lax.broadcasted_iota(jnp.int32, (tq, LANES), 1) < A_DH
    low_k = lax.broadcasted_iota(jnp.int32, (WINDOW, LANES), 1) < A_DH
    key_row = lax.broadcasted_iota(jnp.int32, (WINDOW, LANES), 0)
    qi = lax.broadcasted_iota(jnp.int32, (tq, WINDOW), 0)
    kj = lax.broadcasted_iota(jnp.int32, (tq, WINDOW), 1)
    sink_lane = kj == 0
    valid_cur = kj <= qi
    scale = jnp.asarray(ATTN_SCALE, BF16)
    neg_inf = -jnp.inf

    def block_diag(x):
        return jnp.concatenate([jnp.where(low_k, x, 0.0), jnp.where(low_k, 0.0, x)], axis=0).astype(BF16)

    def pair_stages(p):
        ks = slice(p * LANES, (p + 1) * LANES)
        state = {}

        def scores():
            q4 = jnp.concatenate([q[:, (reps * p + r) * LANES:(reps * p + r + 1) * LANES] * scale
                                  for r in range(reps)], axis=0)
            state["sp"] = lax.dot_general(q4, block_diag(kp[:, ks]), _NT, preferred_element_type=F32)
            state["sc"] = lax.dot_general(q4, block_diag(kc[:, ks]), _NT, preferred_element_type=F32)

        def softmax():
            sp, sc = state.pop("sp"), state.pop("sc")
            pps, pcs, invs = [], [], []
            for r in range(reps):
                rows = slice(r * tq, (r + 1) * tq)
                pp_r, pc_r, inv_r = [], [], []
                for half in range(2):
                    cols = slice(half * WINDOW, (half + 1) * WINDOW)
                    head = 2 * reps * p + reps * half + r
                    sp_i = jnp.where(sink_lane, sink_ref[head], jnp.where(valid_prev, sp[rows, cols], neg_inf))
                    sc_i = jnp.where(valid_cur, sc[rows, cols], neg_inf)
                    mx = jnp.max(jnp.maximum(sp_i, sc_i), axis=1, keepdims=True)
                    pp = jnp.exp(sp_i - mx)
                    pc = jnp.exp(sc_i - mx)
                    inv_r.append(1.0 / jnp.sum(pp + pc, axis=1, keepdims=True))
                    pp_r.append(pp.astype(BF16))
                    pc_r.append(pc.astype(BF16))
                pps.append(jnp.concatenate(pp_r, axis=1))
                pcs.append(jnp.concatenate(pc_r, axis=1))
                invs.append(jnp.where(low_q, inv_r[0], inv_r[1]))
            state.update(pp=jnp.concatenate(pps, axis=0), pc=jnp.concatenate(pcs, axis=0), inv=invs)

        def values():
            v_prev = block_diag(jnp.where(key_row == 0, 0.0, vp[:, ks]))
            o4 = (jnp.dot(state.pop("pp"), v_prev, preferred_element_type=F32)
                  + jnp.dot(state.pop("pc"), block_diag(vc[:, ks]), preferred_element_type=F32))
            invs = state.pop("inv")
            for r in range(reps):
                blk = reps * p + r
                o_ref[:, blk * LANES:(blk + 1) * LANES] = (o4[r * tq:(r + 1) * tq] * invs[r]).astype(BF16)

        return scores, softmax, values

    return [pair_stages(p) for p in range(A_KV_HEADS // 2)]


def _swa_block(sink_ref, q, kp, vp, kc, vc, valid_prev, o_ref, between=None):
    for scores, softmax, values in _swa_stages(sink_ref, q, kp, vp, kc, vc, valid_prev, o_ref):
        if between is not None:
            between()
        scores()
        softmax()
        if between is not None:
            between()
        values()


def _swa(sinks, q, k_prev_src, k_cur, v_prev_src, v_cur, *, nbb, chained, n_new, blocks_per_seq):
    n, tq, _ = q.shape
    tk = k_cur.shape[1]
    if chained:
        steps = blocks_per_seq // nbb
        grid = (n // blocks_per_seq, steps)
        cur_index = lambda b, i: (b * steps + i, 0, 0)
        prev_spec = pl.BlockSpec((1, WINDOW, W_AKV),
                                 lambda b, i: (b * blocks_per_seq + jnp.maximum(i * nbb - 1, 0), 0, 0))
    else:
        grid = (n // nbb,)
        cur_index = lambda i: (i, 0, 0)
        prev_spec = pl.BlockSpec((nbb, WINDOW, W_AKV), cur_index)
    cur_spec = pl.BlockSpec((nbb, tk, W_AKV), cur_index)
    q_spec = pl.BlockSpec((nbb, tq, W_AQ), cur_index)
    out_specs = [q_spec]
    out_shape = [jax.ShapeDtypeStruct((n, tq, W_AQ), BF16)]
    if n_new:
        out_specs += [prev_spec, prev_spec]
        out_shape += [jax.ShapeDtypeStruct((n, WINDOW, W_AKV), F32)] * 2
    return pl.pallas_call(
        functools.partial(_swa_kernel, nbb=nbb, tq=tq, chained=chained, n_new=n_new),
        grid=grid,
        in_specs=[pl.BlockSpec(memory_space=pltpu.SMEM), q_spec, prev_spec, cur_spec, prev_spec, cur_spec],
        out_specs=out_specs,
        out_shape=out_shape,
        compiler_params=_params(),
        name="swa",
    )(sinks, q, k_prev_src, k_cur, v_prev_src, v_cur)


def _merge_kernel(x_ref, g_ref, hm_ref, oa_ref, gm_ref, ga_ref, wm_ref, wa_ref, wo_ref,
                  lg_ref, lb_ref, o_ref):
    ym = jnp.dot(hm_ref[...], wm_ref[...], preferred_element_type=F32)
    ya = jnp.dot(oa_ref[...], wa_ref[...], preferred_element_type=F32)
    mix = gm_ref[...].astype(F32) * ym + ga_ref[...].astype(F32) * ya
    t = jnp.dot(mix.astype(BF16), wo_ref[...], preferred_element_type=F32)
    y = DEEPNORM_ALPHA * x_ref[...] + (1.0 + _rows(g_ref)) * t
    o_ref[...] = _layer_norm(y, lg_ref[...], lb_ref[...], LN_EPS)


def _merge(x, mod, per_row, rows_per_batch, chunk, hm, oa, gm, ga, wm, wa, wo, ln_g, ln_b):
    m = x.shape[0]
    tm = ROW_TILE

    def tok():
        return pl.BlockSpec((tm, D_MODEL), lambda i: (i, 0))

    return pl.pallas_call(
        _merge_kernel,
        grid=(m // tm,),
        in_specs=[tok()] + _mod_specs(per_row, tm, rows_per_batch // tm, (chunk,))
        + [tok(), tok(), tok(), tok()]
        + [_resident((D_MODEL, D_MODEL))] * 3 + [_resident((1, D_MODEL))] * 2,
        out_specs=tok(),
        out_shape=jax.ShapeDtypeStruct((m, D_MODEL), F32),
        compiler_params=_params(),
        name="merge",
    )(x, mod, hm, oa, gm, ga, wm, wa, wo, ln_g.reshape(1, D_MODEL), ln_b.reshape(1, D_MODEL))


def _mixer_kernel(sink_ref, xa_ref, xb_ref, sh_ref, sc_ref, g_ref, wa_ref, wq_ref, wr_ref, wif_ref, bif_ref,
                  cos_ref, sin_ref, ng_ref, wm_ref, wba_ref, wo_ref, lg_ref, lb_ref,
                  y_ref, c_ref, n_ref, m_ref, kb_ref, vb_ref, *scratch, tiles_per_seq):
    n_staged = (len(scratch) - 6) // 2
    even, odd = scratch[:n_staged], scratch[n_staged:2 * n_staged]
    kprev_s, vprev_s, hm_s, oa_s, gtt_s, ctt_s = scratch[2 * n_staged:]
    tm = xa_ref.shape[0]
    s = pl.program_id(0)
    tile_b = jnp.maximum(s - 1, 0)
    seq_start = tile_b % tiles_per_seq == 0

    @pl.when(s == 0)
    def _():
        for ref in odd:
            ref[...] = jnp.zeros_like(ref)

    @pl.when(seq_start)
    def _():
        c_ref[...] = jnp.zeros_like(c_ref)
        n_ref[...] = jnp.zeros_like(n_ref)
        m_ref[...] = jnp.zeros_like(m_ref)
        kprev_s[...] = jnp.zeros_like(kprev_s)
        vprev_s[...] = jnp.zeros_like(vprev_s)

    def step(produce, consume):
        proj_parts = _proj_parts(xa_ref, sh_ref, sc_ref, wa_ref, wq_ref, wr_ref, wif_ref, bif_ref,
                                 cos_ref, sin_ref, *produce)
        pending = iter(proj_parts)

        def emit(count):
            def between():
                for _ in range(count):
                    part = next(pending, None)
                    if part is not None:
                        part()
            return between

        q_s, k_s, v_s, mo_s, aq_s, ak_s, av_s, gm_s, ga_s, gt_s = consume
        qi = lax.broadcasted_iota(jnp.int32, (WINDOW, WINDOW), 0)
        kj = lax.broadcasted_iota(jnp.int32, (WINDOW, WINDOW), 1)
        for j in range(tm // MLSTM_CHUNK):
            rows = pl.ds(j * MLSTM_CHUNK, MLSTM_CHUNK)
            _mlstm_chunk(q_s.at[rows], k_s.at[rows], v_s.at[rows], gt_s.at[rows], mo_s.at[rows], ng_ref,
                         hm_s.at[rows], c_ref, n_ref, m_ref, gtt_s, ctt_s,
                         nb=1, tpb=MLSTM_CHUNK, hps=M_HEADS, head0=0, between=emit(MIXER_PROJ_PER_HEAD))
            valid_prev = kj > qi
            if j == 0:
                valid_prev = valid_prev & jnp.logical_not(seq_start)
                k_prev, v_prev = kprev_s[...], vprev_s[...]
            else:
                before = pl.ds((j - 1) * WINDOW, WINDOW)
                k_prev, v_prev = ak_s[before], av_s[before]
            _swa_block(sink_ref, aq_s[rows], k_prev, v_prev, ak_s[rows], av_s[rows], valid_prev,
                       oa_s.at[rows], between=emit(MIXER_PROJ_PER_ATTN))

        for part in pending:
            part()
        _merge_kernel(xb_ref, g_ref, hm_s, oa_s, gm_s, ga_s, wm_ref, wba_ref, wo_ref, lg_ref, lb_ref, y_ref)

        last = pl.ds(tm - WINDOW, WINDOW)
        k_last, v_last = ak_s[last], av_s[last]
        kb_ref[0] = k_last
        vb_ref[0] = v_last
        kprev_s[...] = k_last
        vprev_s[...] = v_last

    @pl.when(s % 2 == 0)
    def _():
        step(even, odd)

    @pl.when(s % 2 == 1)
    def _():
        step(odd, even)


def _mixer(x1, mod, w, sinks, norm_g, cos_t, sin_t, bp, sp):
    tm = MIXER_TILE
    tps = sp // tm
    n_tiles = bp * tps
    hd = M_HEADS * M_DQK
    weights = (w["w_a"], w["w_aq"], w["w_r"], w["w_if"], w["bif"])

    def tile_a(s):
        return jnp.minimum(s, n_tiles - 1)

    def tile_b(s):
        return jnp.maximum(s - 1, 0)

    def mod_spec(tile, chunk):
        return pl.BlockSpec((1, 1, D_MODEL), lambda s: (tile(s) // tps, 0, chunk))

    def per_seq(*shape):
        return pl.BlockSpec((1,) + shape, lambda s: (tile_b(s) // tps,) + (0,) * len(shape))

    pos = pl.BlockSpec((tm, LANES), lambda s: (tile_a(s) % tps, 0))
    staged = [(W_MQ, BF16), (W_MQ, BF16), (W_MV, BF16), (W_MV, BF16), (W_AQ, BF16),
              (W_AKV, F32), (W_AKV, F32), (D_MODEL, BF16), (D_MODEL, BF16), (LANES, F32)]
    return pl.pallas_call(
        functools.partial(_mixer_kernel, tiles_per_seq=tps),
        grid=(n_tiles + 1,),
        in_specs=[pl.BlockSpec(memory_space=pltpu.SMEM),
                  pl.BlockSpec((tm, D_MODEL), lambda s: (tile_a(s), 0)),
                  pl.BlockSpec((tm, D_MODEL), lambda s: (tile_b(s), 0)),
                  mod_spec(tile_a, 3), mod_spec(tile_a, 4), mod_spec(tile_b, 5)]
        + [_resident(a.shape) for a in weights] + [pos, pos, _resident((1, M_HEADS * M_DV))]
        + [_resident((D_MODEL, D_MODEL))] * 3 + [_resident((1, D_MODEL))] * 2,
        out_specs=[pl.BlockSpec((tm, D_MODEL), lambda s: (tile_b(s), 0)),
                   per_seq(M_HEADS, M_DV, M_DQK),
                   pl.BlockSpec((MLSTM_CHUNK, hd), lambda s: (tile_b(s) // tps, 0)),
                   pl.BlockSpec((MLSTM_CHUNK, LANES), lambda s: (tile_b(s) // tps, 0)),
                   per_seq(WINDOW, W_AKV), per_seq(WINDOW, W_AKV)],
        out_shape=[jax.ShapeDtypeStruct((bp * sp, D_MODEL), F32),
                   jax.ShapeDtypeStruct((bp, M_HEADS, M_DV, M_DQK), F32),
                   jax.ShapeDtypeStruct((bp * MLSTM_CHUNK, hd), F32),
                   jax.ShapeDtypeStruct((bp * MLSTM_CHUNK, LANES), F32),
                   jax.ShapeDtypeStruct((bp, WINDOW, W_AKV), F32),
                   jax.ShapeDtypeStruct((bp, WINDOW, W_AKV), F32)],
        scratch_shapes=[pltpu.VMEM((tm, width), dtype) for width, dtype in staged + staged]
        + [pltpu.VMEM((WINDOW, W_AKV), F32), pltpu.VMEM((WINDOW, W_AKV), F32),
           pltpu.VMEM((tm, W_MV), BF16), pltpu.VMEM((tm, W_AQ), BF16),
           pltpu.VMEM((LANES, MLSTM_CHUNK), F32), pltpu.VMEM((LANES, MLSTM_CHUNK), F32)],
        compiler_params=_params(),
        name="mixer",
    )(sinks, x1, x1, mod, mod, mod, *weights, cos_t, sin_t, norm_g.reshape(1, M_HEADS * M_DV),
      w["wm"], w["wa"], w["wo"], w["ln2_g"].reshape(1, D_MODEL), w["ln2_b"].reshape(1, D_MODEL))


def _rope_tables(pos):
    half = A_DH // 2
    inv = ROPE_THETA ** (-jnp.arange(half, dtype=F32) / half)
    ang = pos.astype(F32)[:, None] * inv[None, :]
    cos, sin = jnp.cos(ang), jnp.sin(ang)
    return jnp.tile(cos, (1, 4)), jnp.concatenate([-sin, sin, -sin, sin], axis=1)


def _token_stage_1(x, mod, per_row, rows_per_batch, w, pos):
    x1 = _ffn(x, mod, per_row, rows_per_batch, (0, 1, 2), w["up1"], w["down1"], w["ln1_g"], w["ln1_b"])
    cos_t, sin_t = _rope_tables(pos)
    return x1, _proj(x1, mod, per_row, rows_per_batch, (3, 4), w, cos_t, sin_t)


def _token_stage_2(x1, mod, per_row, rows_per_batch, w, hm, oa, gm, ga):
    x2 = _merge(x1, mod, per_row, rows_per_batch, 5, hm, oa, gm, ga,
                w["wm"], w["wa"], w["wo"], w["ln2_g"], w["ln2_b"])
    return _ffn(x2, mod, per_row, rows_per_batch, (6, 7, 8), w["up2"], w["down2"], w["ln3_g"], w["ln3_b"])


def kernel(x_prompt, x_sample, state_mlstm_C, state_mlstm_n, state_mlstm_m, cache_swa_k, cache_swa_v, c_prompt, c_sample, w_ada, b_ada, w_ffn1_up, w_ffn1_down, ln1_g, ln1_b, w_in, b_igate, b_fgate, m_norm_g, sinks, w_branch_m, w_branch_a, w_out, ln2_g, ln2_b, w_ffn2_up, w_ffn2_down, ln3_g, ln3_b):
    assert w_ada.shape[0] == DEPTH == 1
    bp, sp, _ = x_prompt.shape
    bs, ts, _ = x_sample.shape

    w_in_t = jnp.transpose(w_in[0])
    w = dict(
        up1=w_ffn1_up[0].astype(BF16), down1=w_ffn1_down[0].astype(BF16),
        w_if=jnp.pad(jnp.transpose(w_in_t[IN_IF:IN_AQ]), ((0, 0), (0, LANES - 2 * M_HEADS))).astype(BF16),
        bif=jnp.concatenate([b_igate[0], b_fgate[0], jnp.zeros((LANES - 2 * M_HEADS,), F32)]).reshape(1, LANES),
        ln1_g=ln1_g[0], ln1_b=ln1_b[0], ln2_g=ln2_g[0], ln2_b=ln2_b[0], ln3_g=ln3_g[0], ln3_b=ln3_b[0],
    )

    ms = bs * ts
    c_all = jnp.concatenate([jnp.repeat(c_sample, ts, axis=0), c_prompt], axis=0)
    mod = _ada(c_all, w_ada[0], b_ada[0])
    mod_p = mod[ms:].reshape(bp, 1, ADA_CHUNKS * D_MODEL)

    mp = bp * sp
    reps = A_HEADS // A_KV_HEADS
    pair_low = lambda j: (j // reps) * (2 * reps) + j % reps
    pair_rows = lambda j: (j // (2 * reps)) * (2 * reps) + (j % 2) * reps + (j % (2 * reps)) // 2
    jobs = (
        _cast_job(w_ffn2_up[0], (D_MODEL, FF_CHUNK), 1),
        _cast_job(w_ffn2_down[0], (FF_CHUNK, D_MODEL), 0, first_step=2 * D_FF // FF_CHUNK),
        _cast_job(w_in_t, (D_MODEL, LANES), 1, transposed=True, n_blocks=IN_IF // LANES),
        _cast_job(w_in_t, (D_MODEL, LANES), 1, transposed=True, n_blocks=W_AQ // LANES, shift=IN_AQ - IN_IF,
                  sources=(lambda j: IN_IF // A_DH + pair_low(j), lambda j: IN_IF // A_DH + pair_low(j) + reps)),
        _cast_job(w_in_t, (D_MODEL, LANES), 1, transposed=True, n_blocks=(IN_END - IN_AK) // LANES,
                  shift=IN_AQ - IN_IF, sources=(lambda j: (IN_AK - (IN_AQ - IN_IF)) // LANES + j,)),
        _cast_job(w_branch_m[0], (D_MODEL, LANES), 1),
        _cast_job(w_out[0], (D_MODEL, LANES), 1),
        _cast_job(w_branch_a[0], (A_DH, D_MODEL), 0, sources=(pair_rows,)),
    )
    x1p, w["up2"], w["down2"], w["w_a"], w["w_aq"], w["w_r"], w["wm"], w["wo"], w["wa"] = _ffn(
        x_prompt.reshape(mp, D_MODEL), mod_p, False, sp, (0, 1, 2),
        w["up1"], w["down1"], w["ln1_g"], w["ln1_b"], cast_jobs=jobs)
    cos_t, sin_t = _rope_tables(jnp.arange(sp))
    x2p, c_p, n_rows, m_rows, kb_p, vb_p = _mixer(x1p, mod_p, w, sinks[0], m_norm_g[0], cos_t, sin_t, bp, sp)
    y_p = _ffn(x2p, mod_p, False, sp, (6, 7, 8), w["up2"], w["down2"], w["ln3_g"], w["ln3_b"])
    n_p = n_rows.reshape(bp, MLSTM_CHUNK, M_HEADS, M_DQK)[:, 0]
    m_p = m_rows.reshape(bp, MLSTM_CHUNK, LANES)[:, 0, :M_HEADS]
    kb_p = kb_p.reshape(bp, WINDOW, A_KV_HEADS, A_DH)
    vb_p = vb_p.reshape(bp, WINDOW, A_KV_HEADS, A_DH)

    x1s, (qm, km, vm, mo, aq, ak, av, gm, ga, gt) = _token_stage_1(
        x_sample.reshape(ms, D_MODEL), mod, True, ms, w, PAST_LEN + jnp.arange(ms) % ts)
    seqs = MLSTM_CHUNK // ts
    n0_rows = jnp.repeat(state_mlstm_n[0].reshape(bs, M_HEADS * M_DQK), ts, axis=0)
    m0_rows = jnp.repeat(jnp.pad(state_mlstm_m[0], ((0, 0), (0, LANES - M_HEADS))), ts, axis=0)
    hm, c_s, n_rows, m_rows = _mlstm(qm, km, vm, gt, mo, m_norm_g[0],
                                     (state_mlstm_C[0], n0_rows, m0_rows), seqs, ts, 1, 1)
    n_s = n_rows.reshape(bs, ts, M_HEADS, M_DQK)[:, 0]
    m_s = m_rows.reshape(bs, ts, LANES)[:, 0, :M_HEADS]
    pad_t = ((0, 0), (0, SAMPLE_TQ - ts), (0, 0))
    oa, kb_s, vb_s = _swa(
        sinks[0], jnp.pad(aq.reshape(bs, ts, W_AQ), pad_t),
        cache_swa_k[0].reshape(bs, WINDOW, W_AKV), jnp.pad(ak.reshape(bs, ts, W_AKV), pad_t),
        cache_swa_v[0].reshape(bs, WINDOW, W_AKV), jnp.pad(av.reshape(bs, ts, W_AKV), pad_t),
        nbb=SWA_SAMPLE_SEQS, chained=False, n_new=ts, blocks_per_seq=1)
    y_s = _token_stage_2(x1s, mod, True, ms, w, hm, oa[:, :ts].reshape(ms, D_MODEL), gm, ga)
    kb_s = kb_s.reshape(bs, WINDOW, A_KV_HEADS, A_DH)
    vb_s = vb_s.reshape(bs, WINDOW, A_KV_HEADS, A_DH)

    return (y_p.reshape(bp, sp, D_MODEL), y_s.reshape(bs, ts, D_MODEL),
            c_p[None], n_p[None], m_p[None], kb_p[None], vb_p[None],
            c_s[None], n_s[None], m_s[None], kb_s[None], vb_s[None])
```

```python
import functools

import jax
import jax.numpy as jnp
from jax import lax
from jax.experimental import pallas as pl
from jax.experimental.pallas import tpu as pltpu

F32 = jnp.float32
BF16 = jnp.bfloat16

D_MODEL = 1024
D_FF = 2816
DEPTH = 1
M_HEADS = 4
M_DQK = 128
M_DV = 256
A_HEADS = 16
A_KV_HEADS = 4
A_DH = 64
WINDOW = 128
PAST_LEN = 8192
ROPE_THETA = 10000.0
ATTN_SCALE = A_DH ** -0.5
LN_EPS = 1e-5
HEAD_NORM_EPS = 1e-6
ADA_CHUNKS = 9
DEEPNORM_ALPHA = (2.0 * DEPTH) ** 0.25
K_SCALE = M_DQK ** -0.5

LANES = 128
BF16_SUBLANES = 16
VMEM_LIMIT_BYTES = 56 * 1024 * 1024

W_MQ = M_HEADS * M_DQK
W_MV = M_HEADS * M_DV
W_AQ = A_HEADS * A_DH
W_AKV = A_KV_HEADS * A_DH
IN_IF = 2 * W_MQ + 2 * W_MV
IN_AQ = IN_IF + 2 * M_HEADS
IN_AK = IN_AQ + W_AQ
IN_END = IN_AK + 2 * W_AKV + 2 * D_MODEL
A_Q, A_K, A_V, A_O = 0, W_MQ, 2 * W_MQ, 2 * W_MQ + W_MV
R_AK, R_AV, R_GM, R_GA = 0, W_AKV, 2 * W_AKV, 2 * W_AKV + D_MODEL

ROW_TILE = 512
MIXER_TILE = 256
MIXER_PROJ_PER_HEAD = 1
MIXER_PROJ_PER_ATTN = 3
ADA_TILE = 1536
FF_CHUNK = 256
FFN_EPILOGUE_PIECES = 4
MLSTM_CHUNK = 128
SAMPLE_TQ = BF16_SUBLANES
SWA_SAMPLE_SEQS = 16

_NT = (((1,), (1,)), ((), ()))


def _params():
    return pltpu.CompilerParams(vmem_limit_bytes=VMEM_LIMIT_BYTES)


def _resident(shape):
    return pl.BlockSpec(shape, lambda *_: (0,) * len(shape), pipeline_mode=pl.Buffered(1))


def _rows(ref):
    v = ref[...]
    return v.reshape(v.shape[-2], v.shape[-1])


def _layer_norm(y, g, b, eps):
    mu = jnp.mean(y, axis=-1, keepdims=True)
    d = y - mu
    var = jnp.mean(d * d, axis=-1, keepdims=True)
    return d * lax.rsqrt(var + eps) * g + b


def _sigmoid(x):
    return 1.0 / (1.0 + jnp.exp(-x))


def _mod_specs(per_row, tm, tiles_per_batch, chunks):
    if per_row:
        return [pl.BlockSpec((tm, D_MODEL), lambda i, c=c: (i, c)) for c in chunks]
    return [pl.BlockSpec((1, 1, D_MODEL), lambda i, c=c: (i // tiles_per_batch, 0, c)) for c in chunks]


def _ada_kernel(c_ref, w_ref, b_ref, o_ref):
    c = c_ref[...]
    s = (c * _sigmoid(c)).astype(BF16)
    o_ref[...] = jnp.dot(s, w_ref[...].astype(BF16), preferred_element_type=F32) + b_ref[...]


def _ada(c_all, w_ada, b_ada):
    rows = c_all.shape[0]
    n_out = w_ada.shape[1]
    tn = ADA_TILE
    return pl.pallas_call(
        _ada_kernel,
        grid=(n_out // tn,),
        in_specs=[pl.BlockSpec((rows, D_MODEL), lambda j: (0, 0)),
                  pl.BlockSpec((D_MODEL, tn), lambda j: (0, j)),
                  pl.BlockSpec((1, tn), lambda j: (0, j))],
        out_specs=pl.BlockSpec((rows, tn), lambda j: (0, j)),
        out_shape=jax.ShapeDtypeStruct((rows, n_out), F32),
        compiler_params=_params(),
        name="ada",
    )(c_all, w_ada, b_ada.reshape(1, n_out))


def _ffn_kernel(x_ref, xp_ref, sh_ref, sc_ref, g_ref, wup_ref, wdn_ref, lg_ref, lb_ref, *rest,
                n_tiles, cast_layout):
    n_cast = len(cast_layout)
    n_src = sum(layout[0] for layout in cast_layout)
    cast_in, o_ref, cast_out = rest[:n_src], rest[n_src], rest[n_src + 1:n_src + 1 + n_cast]
    act_ref, f_ref = rest[n_src + 1 + n_cast:]

    def cast_chunk(job):
        if job >= n_cast:
            return
        first = sum(layout[0] for layout in cast_layout[:job])
        n, transposed, shift = cast_layout[job]
        refs = cast_in[first:first + n]
        if shift:
            rows = refs[0].shape[0]
            parts = [jnp.concatenate([lo[...], hi[...]], axis=0)[shift:shift + rows]
                     for lo, hi in zip(refs[0::2], refs[1::2])]
        else:
            parts = [ref[...] for ref in refs]
        block = jnp.concatenate(parts, axis=0)
        cast_out[job][...] = (block.T if transposed else block).astype(BF16)

    tm = x_ref.shape[0]
    s = pl.program_id(0)
    piece_rows = tm // FFN_EPILOGUE_PIECES

    @pl.when(s == 0)
    def _():
        f_ref[...] = jnp.zeros_like(f_ref)

    def epilogue(piece):
        rows = pl.ds(piece * piece_rows, piece_rows)
        g = _rows(g_ref)
        if g.shape[0] != 1:
            g = g[piece * piece_rows:(piece + 1) * piece_rows]
        y = DEEPNORM_ALPHA * xp_ref[rows, :] + (0.5 * (1.0 + g)) * f_ref[rows, :]
        o_ref[rows, :] = _layer_norm(y, lg_ref[...], lb_ref[...], LN_EPS)

    @pl.when(s < n_tiles)
    def _():
        h = (x_ref[...] * (1.0 + _rows(sc_ref)) + _rows(sh_ref)).astype(BF16)
        chunks = list(range(0, D_FF, FF_CHUNK))
        every = len(chunks) // FFN_EPILOGUE_PIECES
        piece = 0
        for i, c in enumerate(chunks):
            a = jnp.dot(h, wup_ref[:, c:c + FF_CHUNK], preferred_element_type=F32)
            u = jnp.dot(h, wup_ref[:, D_FF + c:D_FF + c + FF_CHUNK], preferred_element_type=F32)
            act_ref[:, c:c + FF_CHUNK] = (a * _sigmoid(a) * u).astype(BF16)
            cast_chunk(i)
            if i % every == every - 1 and piece < FFN_EPILOGUE_PIECES:
                epilogue(piece)
                piece += 1
        f_ref[...] = jnp.dot(act_ref[...], wdn_ref[...], preferred_element_type=F32)

    @pl.when(s == n_tiles)
    def _():
        for job in range(n_cast):
            cast_chunk(job)
        for piece in range(FFN_EPILOGUE_PIECES):
            epilogue(piece)


def _cast_job(array, block, axis, first_step=0, sources=None, transposed=False, n_blocks=None, shift=0):
    sources = sources or (lambda j: j,)
    src_axis = 1 - axis if transposed else axis
    src_block = block[::-1] if transposed else block
    src_block = (src_block[0] // len(sources), src_block[1])
    if n_blocks is None:
        n_blocks = array.shape[src_axis] // block[axis]

    def step_block(s):
        return jnp.clip(s - first_step, 0, n_blocks - 1)

    def at(axis_, j):
        return (j, 0) if axis_ == 0 else (0, j)

    in_specs = [pl.BlockSpec(src_block, lambda s, f=f, d=d: at(src_axis, f(step_block(s)) + d))
                for f in sources for d in range(2 if shift else 1)]
    out_dims = list(array.shape[::-1] if transposed else array.shape)
    out_dims[axis] = n_blocks * block[axis]
    return dict(array=array, in_specs=in_specs, layout=(len(in_specs), transposed, shift),
                out_spec=pl.BlockSpec(block, lambda s: at(axis, step_block(s))),
                out_shape=jax.ShapeDtypeStruct(tuple(out_dims), BF16), steps=first_step + n_blocks)


def _ffn(x, mod, per_row, rows_per_batch, chunks, w_up, w_down, ln_g, ln_b, cast_jobs=()):
    m = x.shape[0]
    tm = ROW_TILE
    n_tiles = m // tm
    tpb = rows_per_batch // tm

    def cur(s):
        return jnp.minimum(s, n_tiles - 1)

    def prev(s):
        return jnp.maximum(s - 1, 0)

    def mod_spec(tile, chunk):
        if per_row:
            return pl.BlockSpec((tm, D_MODEL), lambda s: (tile(s), chunk))
        return pl.BlockSpec((1, 1, D_MODEL), lambda s: (tile(s) // tpb, 0, chunk))

    in_specs = [pl.BlockSpec((tm, D_MODEL), lambda s: (cur(s), 0)),
                pl.BlockSpec((tm, D_MODEL), lambda s: (prev(s), 0)),
                mod_spec(cur, chunks[0]), mod_spec(cur, chunks[1]), mod_spec(prev, chunks[2]),
                _resident((D_MODEL, 2 * D_FF)), _resident((D_FF, D_MODEL)),
                _resident((1, D_MODEL)), _resident((1, D_MODEL))]
    args = [x, x, mod, mod, mod, w_up, w_down, ln_g.reshape(1, D_MODEL), ln_b.reshape(1, D_MODEL)]
    out_specs = [pl.BlockSpec((tm, D_MODEL), lambda s: (prev(s), 0))]
    out_shape = [jax.ShapeDtypeStruct((m, D_MODEL), F32)]
    for job in cast_jobs:
        assert job["steps"] <= n_tiles + 1
        in_specs += job["in_specs"]
        args += [job["array"]] * len(job["in_specs"])
        out_specs.append(job["out_spec"])
        out_shape.append(job["out_shape"])
    outs = pl.pallas_call(
        functools.partial(_ffn_kernel, n_tiles=n_tiles, cast_layout=tuple(j["layout"] for j in cast_jobs)),
        grid=(n_tiles + 1,),
        in_specs=in_specs,
        out_specs=out_specs,
        out_shape=out_shape,
        scratch_shapes=[pltpu.VMEM((tm, D_FF), BF16), pltpu.VMEM((tm, D_MODEL), F32)],
        compiler_params=_params(),
        name="ffn",
    )(*args)
    return outs if cast_jobs else outs[0]


def _proj_parts(x_ref, sh_ref, sc_ref, wa_ref, wq_ref, wr_ref, wif_ref, bif_ref, cos_ref, sin_ref,
                q_ref, k_ref, v_ref, o_ref, aq_ref, ak_ref, av_ref, gm_ref, ga_ref, gt_ref):
    x = x_ref[...]
    tm = x.shape[0]
    h = (x * (1.0 + _rows(sc_ref)) + _rows(sh_ref)).astype(BF16)
    lane = lax.broadcasted_iota(jnp.int32, (tm, LANES), 1)

    def seg(w_ref, lo, width=256):
        return jnp.dot(h, w_ref[:, lo:lo + width], preferred_element_type=F32)

    def plain(dst_ref, w_ref, lo, c, scale=None):
        def run():
            z = seg(w_ref, lo + c)
            dst_ref[:, c:c + 256] = (z if scale is None else z * scale).astype(dst_ref.dtype)
        return run

    def gate(dst_ref, w_ref, lo, c):
        def run():
            dst_ref[:, c:c + 256] = _sigmoid(seg(w_ref, lo + c)).astype(BF16)
        return run

    def forget_input_gates():
        zg = lax.dot_general(h, wif_ref[...], _NT, preferred_element_type=F32) + bif_ref[...]
        logsig = jnp.minimum(zg, 0.0) - jnp.log(1.0 + jnp.exp(-jnp.abs(zg)))
        gt_ref[...] = jnp.where(lane < M_HEADS, zg, logsig)

    def rotary(dst_ref, w_ref, lo, c):
        def run():
            cos = cos_ref[...]
            sin = sin_ref[...]
            low_half = (lane & (A_DH // 2)) == 0
            z = seg(w_ref, lo + c)
            for half in range(2):
                zh = z[:, half * LANES:(half + 1) * LANES]
                partner = jnp.where(low_half, pltpu.roll(zh, LANES - A_DH // 2, 1),
                                    pltpu.roll(zh, A_DH // 2, 1))
                dst_ref[:, c + half * LANES:c + (half + 1) * LANES] = (
                    zh * cos + partner * sin).astype(dst_ref.dtype)
        return run

    parts = [plain(q_ref, wa_ref, A_Q, c) for c in range(0, W_MQ, 256)]
    parts += [plain(k_ref, wa_ref, A_K, c, K_SCALE) for c in range(0, W_MQ, 256)]
    parts += [plain(v_ref, wa_ref, A_V, c) for c in range(0, W_MV, 256)]
    parts += [forget_input_gates]
    parts += [rotary(aq_ref, wq_ref, 0, c) for c in range(0, W_AQ, 256)]
    parts += [rotary(ak_ref, wr_ref, R_AK, 0), plain(av_ref, wr_ref, R_AV, 0)]
    parts += [gate(o_ref, wa_ref, A_O, c) for c in range(0, W_MV, 256)]
    parts += [gate(gm_ref, wr_ref, R_GM, c) for c in range(0, D_MODEL, 256)]
    parts += [gate(ga_ref, wr_ref, R_GA, c) for c in range(0, D_MODEL, 256)]
    return parts


def _proj_kernel(*refs):
    for part in _proj_parts(*refs):
        part()


def _proj(x, mod, per_row, rows_per_batch, chunks, w, cos_t, sin_t):
    m = x.shape[0]
    tm = ROW_TILE
    n_pos_tiles = cos_t.shape[0] // tm

    def tok(width):
        return pl.BlockSpec((tm, width), lambda i: (i, 0))

    widths = (W_MQ, W_MQ, W_MV, W_MV, W_AQ, W_AKV, W_AKV, D_MODEL, D_MODEL, LANES)
    dtypes = (BF16, BF16, BF16, BF16, BF16, F32, F32, BF16, BF16, F32)
    weights = (w["w_a"], w["w_aq"], w["w_r"], w["w_if"], w["bif"])
    return pl.pallas_call(
        _proj_kernel,
        grid=(m // tm,),
        in_specs=[tok(D_MODEL)] + _mod_specs(per_row, tm, rows_per_batch // tm, chunks)
        + [_resident(a.shape) for a in weights]
        + [pl.BlockSpec((tm, LANES), lambda i: (i % n_pos_tiles, 0)),
           pl.BlockSpec((tm, LANES), lambda i: (i % n_pos_tiles, 0))],
        out_specs=[tok(wd) for wd in widths],
        out_shape=[jax.ShapeDtypeStruct((m, wd), d) for wd, d in zip(widths, dtypes)],
        compiler_params=_params(),
        name="proj",
    )(x, mod, mod, *weights, cos_t, sin_t)


def _mlstm_kernel(*refs, nb, tpb, hps, zero_init):
    if zero_init:
        (q_ref, k_ref, v_ref, g_ref, mo_ref, ng_ref,
         out_ref, c_ref, n_ref, m_ref, gt_s, ct_s) = refs
    else:
        (q_ref, k_ref, v_ref, g_ref, mo_ref, ng_ref, c0_ref, n0_ref, m0_ref,
         out_ref, c_ref, n_ref, m_ref, gt_s, ct_s) = refs
    first_chunk = pl.program_id(2) == 0

    @pl.when(first_chunk)
    def _():
        if zero_init:
            c_ref[...] = jnp.zeros_like(c_ref)
            n_ref[...] = jnp.zeros_like(n_ref)
        else:
            c_ref[...] = c0_ref[...]
            n_ref[...] = n0_ref[...]

    @pl.when(first_chunk & (pl.program_id(1) == 0))
    def _():
        m_ref[...] = jnp.zeros_like(m_ref) if zero_init else m0_ref[...]

    head0 = 0 if hps == M_HEADS else pl.program_id(1) * hps
    _mlstm_chunk(q_ref, k_ref, v_ref, g_ref, mo_ref, ng_ref, out_ref, c_ref, n_ref, m_ref, gt_s, ct_s,
                 nb=nb, tpb=tpb, hps=hps, head0=head0)


def _mlstm_chunk(q_ref, k_ref, v_ref, g_ref, mo_ref, ng_ref, out_ref, c_ref, n_ref, m_ref, gt_s, ct_s,
                 *, nb, tpb, hps, head0, between=None):
    L = nb * tpb
    shift = tpb.bit_length() - 1
    row = lax.broadcasted_iota(jnp.int32, (L, L), 0)
    col = lax.broadcasted_iota(jnp.int32, (L, L), 1)
    same = (row >> shift) == (col >> shift)
    causal = same & (col <= row)
    lane = lax.broadcasted_iota(jnp.int32, (L, LANES), 1)
    row_seq = lax.broadcasted_iota(jnp.int32, (L, 1), 0) >> shift

    def lane_col(x, idx):
        return jnp.sum(jnp.where(lane == idx, x, 0.0), axis=1, keepdims=True)

    gates = g_ref[...]
    tri = jnp.where(causal, 1.0, 0.0).astype(BF16)
    g_hi = gates.astype(BF16)
    rem = gates - g_hi.astype(F32)
    g_mid = rem.astype(BF16)
    g_lo = (rem - g_mid.astype(F32)).astype(BF16)
    cum = (jnp.dot(tri, g_hi, preferred_element_type=F32)
           + jnp.dot(tri, g_mid, preferred_element_type=F32)
           + jnp.dot(tri, g_lo, preferred_element_type=F32))
    gt_s[...] = gates.T
    ct_s[...] = cum.T
    m_rows = m_ref[...]
    m_next = m_rows

    def head_stages(hl):
        head = head0 + hl
        qs = slice(hl * M_DQK, (hl + 1) * M_DQK)
        vs = slice(hl * M_DV, (hl + 1) * M_DV)
        st = {}

        def gate_stage():
            b_c = lane_col(cum, M_HEADS + head)
            m_p = lane_col(m_rows, head)
            i_r = gt_s[pl.ds(head, 1), :]
            b_r = ct_s[pl.ds(M_HEADS + head, 1), :]
            log_d = jnp.where(causal, b_c - b_r + i_r, -jnp.inf)
            m_t = jnp.maximum(b_c + m_p, jnp.max(log_d, axis=1, keepdims=True))
            dmat = jnp.exp(log_d - m_t)
            e_int = jnp.exp(b_c + m_p - m_t)
            if nb == 1:
                last = slice(L - 1, L)
                m_new = jnp.broadcast_to(m_t[last], (L, 1))
                e_c = jnp.broadcast_to(e_int[last], (L, 1))
                w_mat = jnp.broadcast_to(dmat[last], (L, L))
            else:
                b_last = jnp.min(jnp.where(same, b_r, jnp.inf), axis=1, keepdims=True)
                log_w = jnp.where(same, b_last - b_r + i_r, -jnp.inf)
                m_new = jnp.maximum(b_last + m_p, jnp.max(log_w, axis=1, keepdims=True))
                w_mat = jnp.exp(log_w - m_new)
                e_c = jnp.exp(b_last + m_p - m_new)
            st.update(m_t=m_t, dmat=dmat, e_int=e_int, m_new=m_new, e_c=e_c, w_mat=w_mat)

        def score_stage():
            q, k = q_ref[:, qs], k_ref[:, qs]
            st["s"] = lax.dot_general(q, k, _NT, preferred_element_type=F32) * st.pop("dmat")

        def state_stage():
            q, k = q_ref[:, qs], k_ref[:, qs]
            v_t = v_ref[:, vs].astype(F32).T
            w_mat = st["w_mat"]
            e_cb = jnp.broadcast_to(st["e_c"], (L, LANES))
            inter = jnp.zeros((L, M_DV), F32)
            for j in range(nb):
                first = slice(j * tpb, j * tpb + 1)
                c_j = c_ref[j, hl]
                q_j = q if nb == 1 else jnp.where(row_seq == j, q, jnp.zeros_like(q))
                inter = inter + lax.dot_general(q_j, c_j.astype(BF16), _NT, preferred_element_type=F32)
                lhs = (v_t * w_mat[first]).astype(BF16)
                c_ref[j, hl] = e_cb[first] * c_j + jnp.dot(lhs, k, preferred_element_type=F32)
            st["inter"] = inter

        def output_stage():
            q, k, v = q_ref[:, qs], k_ref[:, qs], v_ref[:, vs]
            s, e_int, m_t = st.pop("s"), st.pop("e_int"), st.pop("m_t")
            n_rows = n_ref[:, qs]
            qn = jnp.sum(q.astype(F32) * n_rows, axis=1, keepdims=True)
            num = jnp.dot(s.astype(BF16), v, preferred_element_type=F32) + e_int * st.pop("inter")
            den = jnp.sum(s, axis=1, keepdims=True) + e_int * qn
            hh = num / jnp.maximum(jnp.abs(den), jnp.exp(-m_t))
            mu = jnp.mean(hh, axis=1, keepdims=True)
            dlt = hh - mu
            var = jnp.mean(dlt * dlt, axis=1, keepdims=True)
            y = dlt * lax.rsqrt(var + HEAD_NORM_EPS)
            out_ref[:, vs] = (y * ng_ref[:, vs] * mo_ref[:, vs].astype(F32)).astype(BF16)
            n_ref[:, qs] = (st.pop("e_c") * n_rows
                            + jnp.dot(st.pop("w_mat").astype(BF16), k, preferred_element_type=F32))

        return (gate_stage, score_stage, state_stage, output_stage), st

    heads = [head_stages(hl) for hl in range(hps)]
    for stage in range(4):
        if between is not None:
            between()
        for stages, _ in heads:
            stages[stage]()
    for hl, (_, st) in enumerate(heads):
        m_next = jnp.where(lane == head0 + hl, st["m_new"], m_next)

    m_ref[...] = m_next


def _mlstm(q, k, v, gates, mo, norm_g, state, nb, tpb, n_chunks, hps):
    m = q.shape[0]
    L = nb * tpb
    n_blocks = m // (L * n_chunks)
    hd = M_HEADS * M_DQK

    def tok(width):
        return pl.BlockSpec((L, width), lambda b, g, c: (b * n_chunks + c, g))

    c_spec = pl.BlockSpec((nb, hps, M_DV, M_DQK), lambda b, g, c: (b, g, 0, 0))
    n_spec = pl.BlockSpec((L, hps * M_DQK), lambda b, g, c: (b, g))
    m_spec = pl.BlockSpec((L, LANES), lambda b, g, c: (b, 0))
    in_specs = [tok(hps * M_DQK), tok(hps * M_DQK), tok(hps * M_DV),
                pl.BlockSpec((L, LANES), lambda b, g, c: (b * n_chunks + c, 0)),
                tok(hps * M_DV), pl.BlockSpec((1, hps * M_DV), lambda b, g, c: (0, g))]
    args = [q, k, v, gates, mo, norm_g.reshape(1, M_HEADS * M_DV)]
    if state is not None:
        in_specs += [c_spec, n_spec, m_spec]
        args += list(state)
    return pl.pallas_call(
        functools.partial(_mlstm_kernel, nb=nb, tpb=tpb, hps=hps, zero_init=state is None),
        grid=(n_blocks, M_HEADS // hps, n_chunks),
        in_specs=in_specs,
        out_specs=[tok(hps * M_DV), c_spec, n_spec, m_spec],
        out_shape=[jax.ShapeDtypeStruct((m, M_HEADS * M_DV), BF16),
                   jax.ShapeDtypeStruct((n_blocks * nb, M_HEADS, M_DV, M_DQK), F32),
                   jax.ShapeDtypeStruct((n_blocks * L, hd), F32),
                   jax.ShapeDtypeStruct((n_blocks * L, LANES), F32)],
        scratch_shapes=[pltpu.VMEM((LANES, L), F32), pltpu.VMEM((LANES, L), F32)],
        compiler_params=_params(),
        name="mlstm",
    )(*args)


def _swa_kernel(sink_ref, q_ref, kp_ref, kc_ref, vp_ref, vc_ref, o_ref, kn_ref, vn_ref, *, nbb, n_new):
    tq = q_ref.shape[1]
    qi = lax.broadcasted_iota(jnp.int32, (tq, WINDOW), 0)
    kj = lax.broadcasted_iota(jnp.int32, (tq, WINDOW), 1)
    keep_old = lax.broadcasted_iota(jnp.int32, (WINDOW, W_AKV), 0) < WINDOW - n_new

    def pad_keys(x):
        return jnp.concatenate([x, jnp.zeros((WINDOW - x.shape[0], x.shape[1]), x.dtype)], axis=0)

    blocks = []
    for jb in range(nbb):
        kc, vc = pad_keys(kc_ref[jb]), pad_keys(vc_ref[jb])
        kp, vp = kp_ref[jb], vp_ref[jb]
        kn_ref[jb] = jnp.where(keep_old, pltpu.roll(kp, WINDOW - n_new, 0), pltpu.roll(kc, WINDOW - n_new, 0))
        vn_ref[jb] = jnp.where(keep_old, pltpu.roll(vp, WINDOW - n_new, 0), pltpu.roll(vc, WINDOW - n_new, 0))
        blocks.append(_swa_stages(sink_ref, q_ref[jb], kp, vp, kc, vc, kj > qi, o_ref.at[jb]))

    for stage in range(3):
        for pairs in blocks:
            for pair in pairs:
                pair[stage]()


def _swa_stages(sink_ref, q, kp, vp, kc, vc, valid_prev, o_ref):
    tq = q.shape[0]
    reps = A_HEADS // A_KV_HEADS
    low_q = lax.broadcasted_iota(jnp.int32, (tq, LANES), 1) < A_DH
    low_k = lax.broadcasted_iota(jnp.int32, (WINDOW, LANES), 1) < A_DH
    key_row = lax.broadcasted_iota(jnp.int32, (WINDOW, LANES), 0)
    qi = lax.broadcasted_iota(jnp.int32, (tq, WINDOW), 0)
    kj = lax.broadcasted_iota(jnp.int32, (tq, WINDOW), 1)
    sink_lane = kj == 0
    valid_cur = kj <= qi
    scale = jnp.asarray(ATTN_SCALE, BF16)
    neg_inf = -jnp.inf

    def block_diag(x):
        return jnp.concatenate([jnp.where(low_k, x, 0.0), jnp.where(low_k, 0.0, x)], axis=0).astype(BF16)

    def pair_stages(p):
        ks = slice(p * LANES, (p + 1) * LANES)
        state = {}

        def scores():
            q4 = jnp.concatenate([q[:, (reps * p + r) * LANES:(reps * p + r + 1) * LANES] * scale
                                  for r in range(reps)], axis=0)
            state["sp"] = lax.dot_general(q4, block_diag(kp[:, ks]), _NT, preferred_element_type=F32)
            state["sc"] = lax.dot_general(q4, block_diag(kc[:, ks]), _NT, preferred_element_type=F32)

        def softmax():
            sp, sc = state.pop("sp"), state.pop("sc")
            pps, pcs, invs = [], [], []
            for r in range(reps):
                rows = slice(r * tq, (r + 1) * tq)
                pp_r, pc_r, inv_r = [], [], []
                for half in range(2):
                    cols = slice(half * WINDOW, (half + 1) * WINDOW)
                    head = 2 * reps * p + reps * half + r
                    sp_i = jnp.where(sink_lane, sink_ref[head], jnp.where(valid_prev, sp[rows, cols], neg_inf))
                    sc_i = jnp.where(valid_cur, sc[rows, cols], neg_inf)
                    mx = jnp.max(jnp.maximum(sp_i, sc_i), axis=1, keepdims=True)
                    pp = jnp.exp(sp_i - mx)
                    pc = jnp.exp(sc_i - mx)
                    inv_r.append(1.0 / jnp.sum(pp + pc, axis=1, keepdims=True))
                    pp_r.append(pp.astype(BF16))
                    pc_r.append(pc.astype(BF16))
                pps.append(jnp.concatenate(pp_r, axis=1))
                pcs.append(jnp.concatenate(pc_r, axis=1))
                invs.append(jnp.where(low_q, inv_r[0], inv_r[1]))
            state.update(pp=jnp.concatenate(pps, axis=0), pc=jnp.concatenate(pcs, axis=0), inv=invs)

        def values():
            v_prev = block_diag(jnp.where(key_row == 0, 0.0, vp[:, ks]))
            o4 = (jnp.dot(state.pop("pp"), v_prev, preferred_element_type=F32)
                  + jnp.dot(state.pop("pc"), block_diag(vc[:, ks]), preferred_element_type=F32))
            invs = state.pop("inv")
            for r in range(reps):
                blk = reps * p + r
                o_ref[:, blk * LANES:(blk + 1) * LANES] = (o4[r * tq:(r + 1) * tq] * invs[r]).astype(BF16)

        return scores, softmax, values

    return [pair_stages(p) for p in range(A_KV_HEADS // 2)]


def _swa_block(sink_ref, q, kp, vp, kc, vc, valid_prev, o_ref, between=None):
    for scores, softmax, values in _swa_stages(sink_ref, q, kp, vp, kc, vc, valid_prev, o_ref):
        if between is not None:
            between()
        scores()
        softmax()
        if between is not None:
            between()
        values()


def _swa(sinks, q, k_cache, k_new, v_cache, v_new, *, nbb, n_new):
    n, tq, _ = q.shape
    tk = k_new.shape[1]
    index = lambda i: (i, 0, 0)
    cache_spec = pl.BlockSpec((nbb, WINDOW, W_AKV), index)
    new_spec = pl.BlockSpec((nbb, tk, W_AKV), index)
    q_spec = pl.BlockSpec((nbb, tq, W_AQ), index)
    return pl.pallas_call(
        functools.partial(_swa_kernel, nbb=nbb, n_new=n_new),
        grid=(n // nbb,),
        in_specs=[pl.BlockSpec(memory_space=pltpu.SMEM), q_spec, cache_spec, new_spec, cache_spec, new_spec],
        out_specs=[q_spec, cache_spec, cache_spec],
        out_shape=[jax.ShapeDtypeStruct((n, tq, W_AQ), BF16)]
        + [jax.ShapeDtypeStruct((n, WINDOW, W_AKV), F32)] * 2,
        compiler_params=_params(),
        name="swa",
    )(sinks, q, k_cache, k_new, v_cache, v_new)


def _merge_kernel(x_ref, g_ref, hm_ref, oa_ref, gm_ref, ga_ref, wm_ref, wa_ref, wo_ref,
                  lg_ref, lb_ref, o_ref):
    ym = jnp.dot(hm_ref[...], wm_ref[...], preferred_element_type=F32)
    ya = jnp.dot(oa_ref[...], wa_ref[...], preferred_element_type=F32)
    mix = gm_ref[...].astype(F32) * ym + ga_ref[...].astype(F32) * ya
    t = jnp.dot(mix.astype(BF16), wo_ref[...], preferred_element_type=F32)
    y = DEEPNORM_ALPHA * x_ref[...] + (1.0 + _rows(g_ref)) * t
    o_ref[...] = _layer_norm(y, lg_ref[...], lb_ref[...], LN_EPS)


def _merge(x, mod, per_row, rows_per_batch, chunk, hm, oa, gm, ga, wm, wa, wo, ln_g, ln_b):
    m = x.shape[0]
    tm = ROW_TILE

    def tok():
        return pl.BlockSpec((tm, D_MODEL), lambda i: (i, 0))

    return pl.pallas_call(
        _merge_kernel,
        grid=(m // tm,),
        in_specs=[tok()] + _mod_specs(per_row, tm, rows_per_batch // tm, (chunk,))
        + [tok(), tok(), tok(), tok()]
        + [_resident((D_MODEL, D_MODEL))] * 3 + [_resident((1, D_MODEL))] * 2,
        out_specs=tok(),
        out_shape=jax.ShapeDtypeStruct((m, D_MODEL), F32),
        compiler_params=_params(),
        name="merge",
    )(x, mod, hm, oa, gm, ga, wm, wa, wo, ln_g.reshape(1, D_MODEL), ln_b.reshape(1, D_MODEL))


def _mixer_kernel(sink_ref, xa_ref, xb_ref, sh_ref, sc_ref, g_ref, wa_ref, wq_ref, wr_ref, wif_ref, bif_ref,
                  cos_ref, sin_ref, ng_ref, wm_ref, wba_ref, wo_ref, lg_ref, lb_ref,
                  y_ref, c_ref, n_ref, m_ref, kb_ref, vb_ref, *scratch, tiles_per_seq):
    n_staged = (len(scratch) - 6) // 2
    even, odd = scratch[:n_staged], scratch[n_staged:2 * n_staged]
    kprev_s, vprev_s, hm_s, oa_s, gtt_s, ctt_s = scratch[2 * n_staged:]
    tm = xa_ref.shape[0]
    s = pl.program_id(0)
    tile_b = jnp.maximum(s - 1, 0)
    seq_start = tile_b % tiles_per_seq == 0

    @pl.when(s == 0)
    def _():
        for ref in odd:
            ref[...] = jnp.zeros_like(ref)

    @pl.when(seq_start)
    def _():
        c_ref[...] = jnp.zeros_like(c_ref)
        n_ref[...] = jnp.zeros_like(n_ref)
        m_ref[...] = jnp.zeros_like(m_ref)
        kprev_s[...] = jnp.zeros_like(kprev_s)
        vprev_s[...] = jnp.zeros_like(vprev_s)

    def step(produce, consume):
        proj_parts = _proj_parts(xa_ref, sh_ref, sc_ref, wa_ref, wq_ref, wr_ref, wif_ref, bif_ref,
                                 cos_ref, sin_ref, *produce)
        pending = iter(proj_parts)

        def emit(count):
            def between():
                for _ in range(count):
                    part = next(pending, None)
                    if part is not None:
                        part()
            return between

        q_s, k_s, v_s, mo_s, aq_s, ak_s, av_s, gm_s, ga_s, gt_s = consume
        qi = lax.broadcasted_iota(jnp.int32, (WINDOW, WINDOW), 0)
        kj = lax.broadcasted_iota(jnp.int32, (WINDOW, WINDOW), 1)
        for j in range(tm // MLSTM_CHUNK):
            rows = pl.ds(j * MLSTM_CHUNK, MLSTM_CHUNK)
            _mlstm_chunk(q_s.at[rows], k_s.at[rows], v_s.at[rows], gt_s.at[rows], mo_s.at[rows], ng_ref,
                         hm_s.at[rows], c_ref, n_ref, m_ref, gtt_s, ctt_s,
                         nb=1, tpb=MLSTM_CHUNK, hps=M_HEADS, head0=0, between=emit(MIXER_PROJ_PER_HEAD))
            valid_prev = kj > qi
            if j == 0:
                valid_prev = valid_prev & jnp.logical_not(seq_start)
                k_prev, v_prev = kprev_s[...], vprev_s[...]
            else:
                before = pl.ds((j - 1) * WINDOW, WINDOW)
                k_prev, v_prev = ak_s[before], av_s[before]
            _swa_block(sink_ref, aq_s[rows], k_prev, v_prev, ak_s[rows], av_s[rows], valid_prev,
                       oa_s.at[rows], between=emit(MIXER_PROJ_PER_ATTN))

        for part in pending:
            part()
        _merge_kernel(xb_ref, g_ref, hm_s, oa_s, gm_s, ga_s, wm_ref, wba_ref, wo_ref, lg_ref, lb_ref, y_ref)

        last = pl.ds(tm - WINDOW, WINDOW)
        k_last, v_last = ak_s[last], av_s[last]
        kb_ref[0] = k_last.T
        vb_ref[0] = v_last.T
        kprev_s[...] = k_last
        vprev_s[...] = v_last

    @pl.when(s % 2 == 0)
    def _():
        step(even, odd)

    @pl.when(s % 2 == 1)
    def _():
        step(odd, even)


def _mixer(x1, mod, w, sinks, norm_g, cos_t, sin_t, bp, sp):
    tm = MIXER_TILE
    tps = sp // tm
    n_tiles = bp * tps
    hd = M_HEADS * M_DQK
    weights = (w["w_a"], w["w_aq"], w["w_r"], w["w_if"], w["bif"])

    def tile_a(s):
        return jnp.minimum(s, n_tiles - 1)

    def tile_b(s):
        return jnp.maximum(s - 1, 0)

    def mod_spec(tile, chunk):
        return pl.BlockSpec((1, 1, D_MODEL), lambda s: (tile(s) // tps, 0, chunk))

    def per_seq(*shape):
        return pl.BlockSpec((1,) + shape, lambda s: (tile_b(s) // tps,) + (0,) * len(shape))

    pos = pl.BlockSpec((tm, LANES), lambda s: (tile_a(s) % tps, 0))
    staged = [(W_MQ, BF16), (W_MQ, BF16), (W_MV, BF16), (W_MV, BF16), (W_AQ, BF16),
              (W_AKV, F32), (W_AKV, F32), (D_MODEL, BF16), (D_MODEL, BF16), (LANES, F32)]
    return pl.pallas_call(
        functools.partial(_mixer_kernel, tiles_per_seq=tps),
        grid=(n_tiles + 1,),
        in_specs=[pl.BlockSpec(memory_space=pltpu.SMEM),
                  pl.BlockSpec((tm, D_MODEL), lambda s: (tile_a(s), 0)),
                  pl.BlockSpec((tm, D_MODEL), lambda s: (tile_b(s), 0)),
                  mod_spec(tile_a, 3), mod_spec(tile_a, 4), mod_spec(tile_b, 5)]
        + [_resident(a.shape) for a in weights] + [pos, pos, _resident((1, M_HEADS * M_DV))]
        + [_resident((D_MODEL, D_MODEL))] * 3 + [_resident((1, D_MODEL))] * 2,
        out_specs=[pl.BlockSpec((tm, D_MODEL), lambda s: (tile_b(s), 0)),
                   per_seq(M_HEADS, M_DV, M_DQK),
                   pl.BlockSpec((MLSTM_CHUNK, hd), lambda s: (tile_b(s) // tps, 0)),
                   pl.BlockSpec((MLSTM_CHUNK, LANES), lambda s: (tile_b(s) // tps, 0)),
                   per_seq(W_AKV, WINDOW), per_seq(W_AKV, WINDOW)],
        out_shape=[jax.ShapeDtypeStruct((bp * sp, D_MODEL), F32),
                   jax.ShapeDtypeStruct((bp, M_HEADS, M_DV, M_DQK), F32),
                   jax.ShapeDtypeStruct((bp * MLSTM_CHUNK, hd), F32),
                   jax.ShapeDtypeStruct((bp * MLSTM_CHUNK, LANES), F32),
                   jax.ShapeDtypeStruct((bp, W_AKV, WINDOW), F32),
                   jax.ShapeDtypeStruct((bp, W_AKV, WINDOW), F32)],
        scratch_shapes=[pltpu.VMEM((tm, width), dtype) for width, dtype in staged + staged]
        + [pltpu.VMEM((WINDOW, W_AKV), F32), pltpu.VMEM((WINDOW, W_AKV), F32),
           pltpu.VMEM((tm, W_MV), BF16), pltpu.VMEM((tm, W_AQ), BF16),
           pltpu.VMEM((LANES, MLSTM_CHUNK), F32), pltpu.VMEM((LANES, MLSTM_CHUNK), F32)],
        compiler_params=_params(),
        name="mixer",
    )(sinks, x1, x1, mod, mod, mod, *weights, cos_t, sin_t, norm_g.reshape(1, M_HEADS * M_DV),
      w["wm"], w["wa"], w["wo"], w["ln2_g"].reshape(1, D_MODEL), w["ln2_b"].reshape(1, D_MODEL))


def _rope_tables(pos):
    half = A_DH // 2
    inv = ROPE_THETA ** (-jnp.arange(half, dtype=F32) / half)
    ang = pos.astype(F32)[:, None] * inv[None, :]
    cos, sin = jnp.cos(ang), jnp.sin(ang)
    return jnp.tile(cos, (1, 4)), jnp.concatenate([-sin, sin, -sin, sin], axis=1)


def _from_stored(kv_t):
    return jnp.transpose(kv_t.reshape(kv_t.shape[0], A_KV_HEADS, A_DH, WINDOW), (0, 3, 1, 2))


def _token_stage_1(x, mod, per_row, rows_per_batch, w, pos):
    x1 = _ffn(x, mod, per_row, rows_per_batch, (0, 1, 2), w["up1"], w["down1"], w["ln1_g"], w["ln1_b"])
    cos_t, sin_t = _rope_tables(pos)
    return x1, _proj(x1, mod, per_row, rows_per_batch, (3, 4), w, cos_t, sin_t)


def _token_stage_2(x1, mod, per_row, rows_per_batch, w, hm, oa, gm, ga):
    x2 = _merge(x1, mod, per_row, rows_per_batch, 5, hm, oa, gm, ga,
                w["wm"], w["wa"], w["wo"], w["ln2_g"], w["ln2_b"])
    return _ffn(x2, mod, per_row, rows_per_batch, (6, 7, 8), w["up2"], w["down2"], w["ln3_g"], w["ln3_b"])


def kernel(x_prompt, x_sample, state_mlstm_C, state_mlstm_n, state_mlstm_m, cache_swa_k, cache_swa_v, c_prompt, c_sample, w_ada, b_ada, w_ffn1_up, w_ffn1_down, ln1_g, ln1_b, w_in, b_igate, b_fgate, m_norm_g, sinks, w_branch_m, w_branch_a, w_out, ln2_g, ln2_b, w_ffn2_up, w_ffn2_down, ln3_g, ln3_b):
    assert w_ada.shape[0] == DEPTH == 1
    bp, sp, _ = x_prompt.shape
    bs, ts, _ = x_sample.shape

    w_in_t = jnp.transpose(w_in[0])
    w = dict(
        up1=w_ffn1_up[0].astype(BF16), down1=w_ffn1_down[0].astype(BF16),
        bif=jnp.concatenate([b_igate[0], b_fgate[0], jnp.zeros((LANES - 2 * M_HEADS,), F32)]).reshape(1, LANES),
        ln1_g=ln1_g[0], ln1_b=ln1_b[0], ln2_g=ln2_g[0], ln2_b=ln2_b[0], ln3_g=ln3_g[0], ln3_b=ln3_b[0],
    )

    ms = bs * ts
    c_all = jnp.concatenate([jnp.repeat(c_sample, ts, axis=0), c_prompt], axis=0)
    mod = _ada(c_all, w_ada[0], b_ada[0])
    mod_p = mod[ms:].reshape(bp, 1, ADA_CHUNKS * D_MODEL)

    mp = bp * sp
    reps = A_HEADS // A_KV_HEADS
    pair_low = lambda j: (j // reps) * (2 * reps) + j % reps
    pair_rows = lambda j: (j // (2 * reps)) * (2 * reps) + (j % 2) * reps + (j % (2 * reps)) // 2
    jobs = (
        _cast_job(w_ffn2_up[0], (D_MODEL, FF_CHUNK), 1),
        _cast_job(w_ffn2_down[0], (FF_CHUNK, D_MODEL), 0, first_step=2 * D_FF // FF_CHUNK),
        _cast_job(w_in_t, (D_MODEL, LANES), 1, transposed=True, n_blocks=IN_IF // LANES),
        _cast_job(w_in_t, (D_MODEL, LANES), 1, transposed=True, n_blocks=W_AQ // LANES, shift=IN_AQ - IN_IF,
                  sources=(lambda j: IN_IF // A_DH + pair_low(j), lambda j: IN_IF // A_DH + pair_low(j) + reps)),
        _cast_job(w_in_t, (D_MODEL, LANES), 1, transposed=True, n_blocks=(IN_END - IN_AK) // LANES,
                  shift=IN_AQ - IN_IF, sources=(lambda j: (IN_AK - (IN_AQ - IN_IF)) // LANES + j,)),
        _cast_job(w_branch_m[0], (D_MODEL, LANES), 1),
        _cast_job(w_out[0], (D_MODEL, LANES), 1),
        _cast_job(w_branch_a[0], (A_DH, D_MODEL), 0, sources=(pair_rows,)),
        _cast_job(w_in_t, (LANES, D_MODEL), 0, n_blocks=1, sources=(lambda j: IN_IF // LANES + j,)),
    )
    x1p, w["up2"], w["down2"], w["w_a"], w["w_aq"], w["w_r"], w["wm"], w["wo"], w["wa"], w["w_if"] = _ffn(
        x_prompt.reshape(mp, D_MODEL), mod_p, False, sp, (0, 1, 2),
        w["up1"], w["down1"], w["ln1_g"], w["ln1_b"], cast_jobs=jobs)
    cos_t, sin_t = _rope_tables(jnp.arange(sp))
    x2p, c_p, n_rows, m_rows, kb_p, vb_p = _mixer(x1p, mod_p, w, sinks[0], m_norm_g[0], cos_t, sin_t, bp, sp)
    y_p = _ffn(x2p, mod_p, False, sp, (6, 7, 8), w["up2"], w["down2"], w["ln3_g"], w["ln3_b"])
    n_p = n_rows.reshape(bp, MLSTM_CHUNK, M_HEADS, M_DQK)[:, 0]
    m_p = m_rows.reshape(bp, MLSTM_CHUNK, LANES)[:, 0, :M_HEADS]
    kb_p, vb_p = _from_stored(kb_p), _from_stored(vb_p)

    x1s, (qm, km, vm, mo, aq, ak, av, gm, ga, gt) = _token_stage_1(
        x_sample.reshape(ms, D_MODEL), mod, True, ms, w, PAST_LEN + jnp.arange(ms) % ts)
    seqs = MLSTM_CHUNK // ts
    n0_rows = jnp.repeat(state_mlstm_n[0].reshape(bs, M_HEADS * M_DQK), ts, axis=0)
    m0_rows = jnp.repeat(jnp.pad(state_mlstm_m[0], ((0, 0), (0, LANES - M_HEADS))), ts, axis=0)
    hm, c_s, n_rows, m_rows = _mlstm(qm, km, vm, gt, mo, m_norm_g[0],
                                     (state_mlstm_C[0], n0_rows, m0_rows), seqs, ts, 1, 1)
    n_s = n_rows.reshape(bs, ts, M_HEADS, M_DQK)[:, 0]
    m_s = m_rows.reshape(bs, ts, LANES)[:, 0, :M_HEADS]
    pad_t = ((0, 0), (0, SAMPLE_TQ - ts), (0, 0))
    oa, kb_s, vb_s = _swa(
        sinks[0], jnp.pad(aq.reshape(bs, ts, W_AQ), pad_t),
        cache_swa_k[0].reshape(bs, WINDOW, W_AKV), jnp.pad(ak.reshape(bs, ts, W_AKV), pad_t),
        cache_swa_v[0].reshape(bs, WINDOW, W_AKV), jnp.pad(av.reshape(bs, ts, W_AKV), pad_t),
        nbb=SWA_SAMPLE_SEQS, n_new=ts)
    y_s = _token_stage_2(x1s, mod, True, ms, w, hm, oa[:, :ts].reshape(ms, D_MODEL), gm, ga)
    kb_s = kb_s.reshape(bs, WINDOW, A_KV_HEADS, A_DH)
    vb_s = vb_s.reshape(bs, WINDOW, A_KV_HEADS, A_DH)

    return (y_p.reshape(bp, sp, D_MODEL), y_s.reshape(bs, ts, D_MODEL),
            c_p[None], n_p[None], m_p[None], kb_p[None], vb_p[None],
            c_s[None], n_s[None], m_s[None], kb_s[None], vb_s[None])
```

```python
import functools

import jax
import jax.numpy as jnp
import numpy as np
from jax import lax
from jax.experimental import pallas as pl
from jax.experimental.pallas import tpu as pltpu

F32 = jnp.float32
BF16 = jnp.bfloat16

D_MODEL = 1024
D_FF = 2816
DEPTH = 1
M_HEADS = 4
M_DQK = 128
M_DV = 256
A_HEADS = 16
A_KV_HEADS = 4
A_DH = 64
WINDOW = 128
PAST_LEN = 8192
ROPE_THETA = 10000.0
ATTN_SCALE = A_DH ** -0.5
LN_EPS = 1e-5
HEAD_NORM_EPS = 1e-6
ADA_CHUNKS = 9
DEEPNORM_ALPHA = (2.0 * DEPTH) ** 0.25
K_SCALE = M_DQK ** -0.5

LANES = 128
BF16_SUBLANES = 16
VMEM_LIMIT_BYTES = 56 * 1024 * 1024

W_MQ = M_HEADS * M_DQK
W_MV = M_HEADS * M_DV
W_AQ = A_HEADS * A_DH
W_AKV = A_KV_HEADS * A_DH
IN_IF = 2 * W_MQ + 2 * W_MV
IN_AQ = IN_IF + 2 * M_HEADS
IN_AK = IN_AQ + W_AQ
IN_END = IN_AK + 2 * W_AKV + 2 * D_MODEL
A_Q, A_K, A_V, A_O = 0, W_MQ, 2 * W_MQ, 2 * W_MQ + W_MV
R_AK, R_AV, R_GM, R_GA = 0, W_AKV, 2 * W_AKV, 2 * W_AKV + D_MODEL

ROW_TILE = 512
MIXER_TILE = 256
MIXER_PROJ_PER_HEAD = 1
MIXER_PROJ_PER_ATTN = 2
ADA_TILE = 1536
FF_CHUNK = 256
FFN_EPILOGUE_PIECES = 4
MLSTM_CHUNK = 128
SAMPLE_TQ = BF16_SUBLANES
SWA_SAMPLE_SEQS = 16

_NT = (((1,), (1,)), ((), ()))


def _params():
    return pltpu.CompilerParams(vmem_limit_bytes=VMEM_LIMIT_BYTES)


def _resident(shape):
    return pl.BlockSpec(shape, lambda *_: (0,) * len(shape), pipeline_mode=pl.Buffered(1))


def _rows(ref):
    v = ref[...]
    return v.reshape(v.shape[-2], v.shape[-1])


def _layer_norm(y, g, b, eps):
    mu = jnp.mean(y, axis=-1, keepdims=True)
    d = y - mu
    var = jnp.mean(d * d, axis=-1, keepdims=True)
    return d * lax.rsqrt(var + eps) * g + b


def _sigmoid(x):
    return 1.0 / (1.0 + jnp.exp(-x))


def _mod_specs(per_row, tm, tiles_per_batch, chunks):
    if per_row:
        return [pl.BlockSpec((tm, D_MODEL), lambda i, c=c: (i, c)) for c in chunks]
    return [pl.BlockSpec((1, 1, D_MODEL), lambda i, c=c: (i // tiles_per_batch, 0, c)) for c in chunks]


def _ada_kernel(c_ref, w_ref, b_ref, o_ref):
    c = c_ref[...]
    s = (c * _sigmoid(c)).astype(BF16)
    o_ref[...] = jnp.dot(s, w_ref[...].astype(BF16), preferred_element_type=F32) + b_ref[...]


def _ada(c_all, w_ada, b_ada):
    rows = c_all.shape[0]
    n_out = w_ada.shape[1]
    tn = ADA_TILE
    return pl.pallas_call(
        _ada_kernel,
        grid=(n_out // tn,),
        in_specs=[pl.BlockSpec((rows, D_MODEL), lambda j: (0, 0)),
                  pl.BlockSpec((D_MODEL, tn), lambda j: (0, j)),
                  pl.BlockSpec((1, tn), lambda j: (0, j))],
        out_specs=pl.BlockSpec((rows, tn), lambda j: (0, j)),
        out_shape=jax.ShapeDtypeStruct((rows, n_out), F32),
        compiler_params=_params(),
        name="ada",
    )(c_all, w_ada, b_ada.reshape(1, n_out))


def _ffn_kernel(x_ref, xp_ref, sh_ref, sc_ref, g_ref, wup_ref, wdn_ref, lg_ref, lb_ref, *rest,
                n_tiles, cast_layout):
    n_cast = len(cast_layout)
    n_src = sum(layout[0] for layout in cast_layout)
    cast_in, o_ref, cast_out = rest[:n_src], rest[n_src], rest[n_src + 1:n_src + 1 + n_cast]
    act_ref, f_ref = rest[n_src + 1 + n_cast:]

    def cast_chunk(job):
        if job >= n_cast:
            return
        first = sum(layout[0] for layout in cast_layout[:job])
        n, transposed, shift = cast_layout[job]
        refs = cast_in[first:first + n]
        if shift:
            rows = refs[0].shape[0]
            parts = [jnp.concatenate([lo[...], hi[...]], axis=0)[shift:shift + rows]
                     for lo, hi in zip(refs[0::2], refs[1::2])]
        else:
            parts = [ref[...] for ref in refs]
        block = jnp.concatenate(parts, axis=0)
        cast_out[job][...] = (block.T if transposed else block).astype(BF16)

    tm = x_ref.shape[0]
    s = pl.program_id(0)
    piece_rows = tm // FFN_EPILOGUE_PIECES

    @pl.when(s == 0)
    def _():
        f_ref[...] = jnp.zeros_like(f_ref)

    def epilogue(piece):
        rows = pl.ds(piece * piece_rows, piece_rows)
        g = _rows(g_ref)
        if g.shape[0] != 1:
            g = g[piece * piece_rows:(piece + 1) * piece_rows]
        y = DEEPNORM_ALPHA * xp_ref[rows, :] + (0.5 * (1.0 + g)) * f_ref[rows, :]
        o_ref[rows, :] = _layer_norm(y, lg_ref[...], lb_ref[...], LN_EPS)

    @pl.when(s < n_tiles)
    def _():
        h = (x_ref[...] * (1.0 + _rows(sc_ref)) + _rows(sh_ref)).astype(BF16)
        chunks = list(range(0, D_FF, FF_CHUNK))
        every = len(chunks) // FFN_EPILOGUE_PIECES
        piece = 0
        for i, c in enumerate(chunks):
            a = jnp.dot(h, wup_ref[:, c:c + FF_CHUNK], preferred_element_type=F32)
            u = jnp.dot(h, wup_ref[:, D_FF + c:D_FF + c + FF_CHUNK], preferred_element_type=F32)
            act_ref[:, c:c + FF_CHUNK] = (a * _sigmoid(a) * u).astype(BF16)
            cast_chunk(i)
            if i % every == every - 1 and piece < FFN_EPILOGUE_PIECES:
                epilogue(piece)
                piece += 1
        f_ref[...] = jnp.dot(act_ref[...], wdn_ref[...], preferred_element_type=F32)

    @pl.when(s == n_tiles)
    def _():
        for job in range(n_cast):
            cast_chunk(job)
        for piece in range(FFN_EPILOGUE_PIECES):
            epilogue(piece)


def _cast_job(array, block, axis, first_step=0, sources=None, transposed=False, n_blocks=None, shift=0):
    sources = sources or (lambda j: j,)
    src_axis = 1 - axis if transposed else axis
    src_block = block[::-1] if transposed else block
    src_block = (src_block[0] // len(sources), src_block[1])
    if n_blocks is None:
        n_blocks = array.shape[src_axis] // block[axis]

    def step_block(s):
        return jnp.clip(s - first_step, 0, n_blocks - 1)

    def at(axis_, j):
        return (j, 0) if axis_ == 0 else (0, j)

    in_specs = [pl.BlockSpec(src_block, lambda s, f=f, d=d: at(src_axis, f(step_block(s)) + d))
                for f in sources for d in range(2 if shift else 1)]
    out_dims = list(array.shape[::-1] if transposed else array.shape)
    out_dims[axis] = n_blocks * block[axis]
    return dict(array=array, in_specs=in_specs, layout=(len(in_specs), transposed, shift),
                out_spec=pl.BlockSpec(block, lambda s: at(axis, step_block(s))),
                out_shape=jax.ShapeDtypeStruct(tuple(out_dims), BF16), steps=first_step + n_blocks)


def _ffn(x, mod, per_row, rows_per_batch, chunks, w_up, w_down, ln_g, ln_b, cast_jobs=()):
    m = x.shape[0]
    tm = ROW_TILE
    n_tiles = m // tm
    tpb = rows_per_batch // tm

    def cur(s):
        return jnp.minimum(s, n_tiles - 1)

    def prev(s):
        return jnp.maximum(s - 1, 0)

    def mod_spec(tile, chunk):
        if per_row:
            return pl.BlockSpec((tm, D_MODEL), lambda s: (tile(s), chunk))
        return pl.BlockSpec((1, 1, D_MODEL), lambda s: (tile(s) // tpb, 0, chunk))

    in_specs = [pl.BlockSpec((tm, D_MODEL), lambda s: (cur(s), 0)),
                pl.BlockSpec((tm, D_MODEL), lambda s: (prev(s), 0)),
                mod_spec(cur, chunks[0]), mod_spec(cur, chunks[1]), mod_spec(prev, chunks[2]),
                _resident((D_MODEL, 2 * D_FF)), _resident((D_FF, D_MODEL)),
                _resident((1, D_MODEL)), _resident((1, D_MODEL))]
    args = [x, x, mod, mod, mod, w_up, w_down, ln_g.reshape(1, D_MODEL), ln_b.reshape(1, D_MODEL)]
    out_specs = [pl.BlockSpec((tm, D_MODEL), lambda s: (prev(s), 0))]
    out_shape = [jax.ShapeDtypeStruct((m, D_MODEL), F32)]
    for job in cast_jobs:
        assert job["steps"] <= n_tiles + 1
        in_specs += job["in_specs"]
        args += [job["array"]] * len(job["in_specs"])
        out_specs.append(job["out_spec"])
        out_shape.append(job["out_shape"])
    outs = pl.pallas_call(
        functools.partial(_ffn_kernel, n_tiles=n_tiles, cast_layout=tuple(j["layout"] for j in cast_jobs)),
        grid=(n_tiles + 1,),
        in_specs=in_specs,
        out_specs=out_specs,
        out_shape=out_shape,
        scratch_shapes=[pltpu.VMEM((tm, D_FF), BF16), pltpu.VMEM((tm, D_MODEL), F32)],
        compiler_params=_params(),
        name="ffn",
    )(*args)
    return outs if cast_jobs else outs[0]


def _proj_parts(x_ref, sh_ref, sc_ref, wa_ref, wq_ref, wr_ref, wif_ref, bif_ref, cos_ref, sin_ref,
                q_ref, k_ref, v_ref, o_ref, aq_ref, ak_ref, av_ref, gm_ref, ga_ref, gt_ref):
    x = x_ref[...]
    tm = x.shape[0]
    h = (x * (1.0 + _rows(sc_ref)) + _rows(sh_ref)).astype(BF16)
    lane = lax.broadcasted_iota(jnp.int32, (tm, LANES), 1)

    def seg(w_ref, lo, width=256):
        return jnp.dot(h, w_ref[:, lo:lo + width], preferred_element_type=F32)

    def plain(dst_ref, w_ref, lo, c, scale=None):
        def run():
            z = seg(w_ref, lo + c)
            dst_ref[:, c:c + 256] = (z if scale is None else z * scale).astype(dst_ref.dtype)
        return run

    def gate(dst_ref, w_ref, lo, c):
        def run():
            dst_ref[:, c:c + 256] = _sigmoid(seg(w_ref, lo + c)).astype(BF16)
        return run

    def forget_input_gates():
        zg = lax.dot_general(h, wif_ref[...], _NT, preferred_element_type=F32) + bif_ref[...]
        logsig = jnp.minimum(zg, 0.0) - jnp.log(1.0 + jnp.exp(-jnp.abs(zg)))
        gt_ref[...] = jnp.where(lane < M_HEADS, zg, logsig)

    def rotary(dst_ref, w_ref, lo, c):
        def run():
            cos = cos_ref[...]
            sin = sin_ref[...]
            low_half = (lane & (A_DH // 2)) == 0
            z = seg(w_ref, lo + c)
            for half in range(2):
                zh = z[:, half * LANES:(half + 1) * LANES]
                partner = jnp.where(low_half, pltpu.roll(zh, LANES - A_DH // 2, 1),
                                    pltpu.roll(zh, A_DH // 2, 1))
                dst_ref[:, c + half * LANES:c + (half + 1) * LANES] = (
                    zh * cos + partner * sin).astype(dst_ref.dtype)
        return run

    parts = [plain(q_ref, wa_ref, A_Q, c) for c in range(0, W_MQ, 256)]
    parts += [plain(k_ref, wa_ref, A_K, c, K_SCALE) for c in range(0, W_MQ, 256)]
    parts += [plain(v_ref, wa_ref, A_V, c) for c in range(0, W_MV, 256)]
    parts += [forget_input_gates]
    parts += [rotary(aq_ref, wq_ref, 0, c) for c in range(0, W_AQ, 256)]
    parts += [rotary(ak_ref, wr_ref, R_AK, 0), plain(av_ref, wr_ref, R_AV, 0)]
    parts += [gate(o_ref, wa_ref, A_O, c) for c in range(0, W_MV, 256)]
    parts += [gate(gm_ref, wr_ref, R_GM, c) for c in range(0, D_MODEL, 256)]
    parts += [gate(ga_ref, wr_ref, R_GA, c) for c in range(0, D_MODEL, 256)]
    return parts


def _proj_kernel(*refs):
    for part in _proj_parts(*refs):
        part()


def _proj(x, mod, per_row, rows_per_batch, chunks, w, cos_t, sin_t):
    m = x.shape[0]
    tm = ROW_TILE
    n_pos_tiles = cos_t.shape[0] // tm

    def tok(width):
        return pl.BlockSpec((tm, width), lambda i: (i, 0))

    widths = (W_MQ, W_MQ, W_MV, W_MV, W_AQ, W_AKV, W_AKV, D_MODEL, D_MODEL, LANES)
    dtypes = (BF16, BF16, BF16, BF16, BF16, F32, F32, BF16, BF16, F32)
    weights = (w["w_a"], w["w_aq"], w["w_r"], w["w_if"], w["bif"])
    return pl.pallas_call(
        _proj_kernel,
        grid=(m // tm,),
        in_specs=[tok(D_MODEL)] + _mod_specs(per_row, tm, rows_per_batch // tm, chunks)
        + [_resident(a.shape) for a in weights]
        + [pl.BlockSpec((tm, LANES), lambda i: (i % n_pos_tiles, 0)),
           pl.BlockSpec((tm, LANES), lambda i: (i % n_pos_tiles, 0))],
        out_specs=[tok(wd) for wd in widths],
        out_shape=[jax.ShapeDtypeStruct((m, wd), d) for wd, d in zip(widths, dtypes)],
        compiler_params=_params(),
        name="proj",
    )(x, mod, mod, *weights, cos_t, sin_t)


def _mlstm_kernel(*refs, nb, tpb, hps, zero_init):
    if zero_init:
        (q_ref, k_ref, v_ref, g_ref, mo_ref, ng_ref,
         out_ref, c_ref, n_ref, m_ref, gt_s, ct_s) = refs
    else:
        (q_ref, k_ref, v_ref, g_ref, mo_ref, ng_ref, c0_ref, n0_ref, m0_ref,
         out_ref, c_ref, n_ref, m_ref, gt_s, ct_s) = refs
    first_chunk = pl.program_id(2) == 0

    @pl.when(first_chunk)
    def _():
        if zero_init:
            c_ref[...] = jnp.zeros_like(c_ref)
            n_ref[...] = jnp.zeros_like(n_ref)
        else:
            c_ref[...] = c0_ref[...]
            n_ref[...] = n0_ref[...]

    @pl.when(first_chunk & (pl.program_id(1) == 0))
    def _():
        m_ref[...] = jnp.zeros_like(m_ref) if zero_init else m0_ref[...]

    head0 = 0 if hps == M_HEADS else pl.program_id(1) * hps
    _mlstm_chunk(q_ref, k_ref, v_ref, g_ref, mo_ref, ng_ref, out_ref, c_ref, n_ref, m_ref, gt_s, ct_s,
                 nb=nb, tpb=tpb, hps=hps, head0=head0)


def _mlstm_chunk(q_ref, k_ref, v_ref, g_ref, mo_ref, ng_ref, out_ref, c_ref, n_ref, m_ref, gt_s, ct_s,
                 *, nb, tpb, hps, head0, between=None):
    L = nb * tpb
    shift = tpb.bit_length() - 1
    row = lax.broadcasted_iota(jnp.int32, (L, L), 0)
    col = lax.broadcasted_iota(jnp.int32, (L, L), 1)
    same = (row >> shift) == (col >> shift)
    causal = same & (col <= row)
    lane = lax.broadcasted_iota(jnp.int32, (L, LANES), 1)
    row_seq = lax.broadcasted_iota(jnp.int32, (L, 1), 0) >> shift

    def lane_col(x, idx):
        return jnp.sum(jnp.where(lane == idx, x, 0.0), axis=1, keepdims=True)

    gates = g_ref[...]
    tri = jnp.where(causal, 1.0, 0.0).astype(BF16)
    g_hi = gates.astype(BF16)
    rem = gates - g_hi.astype(F32)
    g_mid = rem.astype(BF16)
    g_lo = (rem - g_mid.astype(F32)).astype(BF16)
    cum = (jnp.dot(tri, g_hi, preferred_element_type=F32)
           + jnp.dot(tri, g_mid, preferred_element_type=F32)
           + jnp.dot(tri, g_lo, preferred_element_type=F32))
    gt_s[...] = gates.T
    ct_s[...] = cum.T
    m_rows = m_ref[...]
    m_next = m_rows

    def head_stages(hl):
        head = head0 + hl
        qs = slice(hl * M_DQK, (hl + 1) * M_DQK)
        vs = slice(hl * M_DV, (hl + 1) * M_DV)
        st = {}

        def gate_stage():
            b_c = lane_col(cum, M_HEADS + head)
            m_p = lane_col(m_rows, head)
            i_r = gt_s[pl.ds(head, 1), :]
            b_r = ct_s[pl.ds(M_HEADS + head, 1), :]
            log_d = jnp.where(causal, b_c - b_r + i_r, -jnp.inf)
            m_t = jnp.maximum(b_c + m_p, jnp.max(log_d, axis=1, keepdims=True))
            dmat = jnp.exp(log_d - m_t)
            e_int = jnp.exp(b_c + m_p - m_t)
            if nb == 1:
                last = slice(L - 1, L)
                m_new = jnp.broadcast_to(m_t[last], (L, 1))
                e_c = jnp.broadcast_to(e_int[last], (L, 1))
                w_mat = jnp.broadcast_to(dmat[last], (L, L))
            else:
                b_last = jnp.min(jnp.where(same, b_r, jnp.inf), axis=1, keepdims=True)
                log_w = jnp.where(same, b_last - b_r + i_r, -jnp.inf)
                m_new = jnp.maximum(b_last + m_p, jnp.max(log_w, axis=1, keepdims=True))
                w_mat = jnp.exp(log_w - m_new)
                e_c = jnp.exp(b_last + m_p - m_new)
            st.update(m_t=m_t, dmat=dmat, e_int=e_int, m_new=m_new, e_c=e_c, w_mat=w_mat)

        def score_stage():
            q, k = q_ref[:, qs], k_ref[:, qs]
            st["s"] = lax.dot_general(q, k, _NT, preferred_element_type=F32) * st.pop("dmat")

        def state_stage():
            q, k = q_ref[:, qs], k_ref[:, qs]
            v_t = v_ref[:, vs].astype(F32).T
            w_mat = st["w_mat"]
            e_cb = jnp.broadcast_to(st["e_c"], (L, LANES))
            inter = jnp.zeros((L, M_DV), F32)
            for j in range(nb):
                first = slice(j * tpb, j * tpb + 1)
                c_j = c_ref[j, hl]
                q_j = q if nb == 1 else jnp.where(row_seq == j, q, jnp.zeros_like(q))
                inter = inter + lax.dot_general(q_j, c_j.astype(BF16), _NT, preferred_element_type=F32)
                lhs = (v_t * w_mat[first]).astype(BF16)
                c_ref[j, hl] = e_cb[first] * c_j + jnp.dot(lhs, k, preferred_element_type=F32)
            st["inter"] = inter

        def output_stage():
            q, k, v = q_ref[:, qs], k_ref[:, qs], v_ref[:, vs]
            s, e_int, m_t = st.pop("s"), st.pop("e_int"), st.pop("m_t")
            n_rows = n_ref[:, qs]
            qn = jnp.sum(q.astype(F32) * n_rows, axis=1, keepdims=True)
            num = jnp.dot(s.astype(BF16), v, preferred_element_type=F32) + e_int * st.pop("inter")
            den = jnp.sum(s, axis=1, keepdims=True) + e_int * qn
            hh = num / jnp.maximum(jnp.abs(den), jnp.exp(-m_t))
            mu = jnp.mean(hh, axis=1, keepdims=True)
            dlt = hh - mu
            var = jnp.mean(dlt * dlt, axis=1, keepdims=True)
            y = dlt * lax.rsqrt(var + HEAD_NORM_EPS)
            out_ref[:, vs] = (y * ng_ref[:, vs] * mo_ref[:, vs].astype(F32)).astype(BF16)
            n_ref[:, qs] = (st.pop("e_c") * n_rows
                            + jnp.dot(st.pop("w_mat").astype(BF16), k, preferred_element_type=F32))

        return (gate_stage, score_stage, state_stage, output_stage), st

    heads = [head_stages(hl) for hl in range(hps)]
    for stage in range(4):
        if between is not None:
            between()
        for stages, _ in heads:
            stages[stage]()
    for hl, (_, st) in enumerate(heads):
        m_next = jnp.where(lane == head0 + hl, st["m_new"], m_next)

    m_ref[...] = m_next


def _mlstm(q, k, v, gates, mo, norm_g, state, nb, tpb, n_chunks, hps):
    m = q.shape[0]
    L = nb * tpb
    n_blocks = m // (L * n_chunks)
    hd = M_HEADS * M_DQK

    def tok(width):
        return pl.BlockSpec((L, width), lambda b, g, c: (b * n_chunks + c, g))

    c_spec = pl.BlockSpec((nb, hps, M_DV, M_DQK), lambda b, g, c: (b, g, 0, 0))
    n_spec = pl.BlockSpec((L, hps * M_DQK), lambda b, g, c: (b, g))
    m_spec = pl.BlockSpec((L, LANES), lambda b, g, c: (b, 0))
    in_specs = [tok(hps * M_DQK), tok(hps * M_DQK), tok(hps * M_DV),
                pl.BlockSpec((L, LANES), lambda b, g, c: (b * n_chunks + c, 0)),
                tok(hps * M_DV), pl.BlockSpec((1, hps * M_DV), lambda b, g, c: (0, g))]
    args = [q, k, v, gates, mo, norm_g.reshape(1, M_HEADS * M_DV)]
    if state is not None:
        in_specs += [c_spec, n_spec, m_spec]
        args += list(state)
    return pl.pallas_call(
        functools.partial(_mlstm_kernel, nb=nb, tpb=tpb, hps=hps, zero_init=state is None),
        grid=(n_blocks, M_HEADS // hps, n_chunks),
        in_specs=in_specs,
        out_specs=[tok(hps * M_DV), c_spec, n_spec, m_spec],
        out_shape=[jax.ShapeDtypeStruct((m, M_HEADS * M_DV), BF16),
                   jax.ShapeDtypeStruct((n_blocks * nb, M_HEADS, M_DV, M_DQK), F32),
                   jax.ShapeDtypeStruct((n_blocks * L, hd), F32),
                   jax.ShapeDtypeStruct((n_blocks * L, LANES), F32)],
        scratch_shapes=[pltpu.VMEM((LANES, L), F32), pltpu.VMEM((LANES, L), F32)],
        compiler_params=_params(),
        name="mlstm",
    )(*args)


def _swa_kernel(sink_ref, q_ref, kp_ref, kc_ref, vp_ref, vc_ref, o_ref, kn_ref, vn_ref, *, nbb, n_new):
    tq = q_ref.shape[1]
    qi = lax.broadcasted_iota(jnp.int32, (tq, WINDOW), 0)
    kj = lax.broadcasted_iota(jnp.int32, (tq, WINDOW), 1)
    keep_old = lax.broadcasted_iota(jnp.int32, (WINDOW, W_AKV), 0) < WINDOW - n_new

    def pad_keys(x):
        return jnp.concatenate([x, jnp.zeros((WINDOW - x.shape[0], x.shape[1]), x.dtype)], axis=0)

    blocks = []
    for jb in range(nbb):
        kc, vc = pad_keys(kc_ref[jb]), pad_keys(vc_ref[jb])
        kp, vp = kp_ref[jb], vp_ref[jb]
        kn_ref[jb] = jnp.where(keep_old, pltpu.roll(kp, WINDOW - n_new, 0), pltpu.roll(kc, WINDOW - n_new, 0))
        vn_ref[jb] = jnp.where(keep_old, pltpu.roll(vp, WINDOW - n_new, 0), pltpu.roll(vc, WINDOW - n_new, 0))
        blocks.append(_swa_stages(sink_ref, q_ref[jb], kp, vp, kc, vc, kj > qi, o_ref.at[jb]))

    for stage in range(3):
        for pairs in blocks:
            for pair in pairs:
                pair[stage]()


def _swa_stages(sink_ref, q, kp, vp, kc, vc, valid_prev, o_ref):
    tq = q.shape[0]
    reps = A_HEADS // A_KV_HEADS
    low_q = lax.broadcasted_iota(jnp.int32, (tq, LANES), 1) < A_DH
    low_k = lax.broadcasted_iota(jnp.int32, (WINDOW, LANES), 1) < A_DH
    key_row = lax.broadcasted_iota(jnp.int32, (WINDOW, LANES), 0)
    qi = lax.broadcasted_iota(jnp.int32, (tq, WINDOW), 0)
    kj = lax.broadcasted_iota(jnp.int32, (tq, WINDOW), 1)
    sink_lane = kj == 0
    valid_cur = kj <= qi
    scale = jnp.asarray(ATTN_SCALE, BF16)
    neg_inf = -jnp.inf

    def block_diag(x):
        return jnp.concatenate([jnp.where(low_k, x, 0.0), jnp.where(low_k, 0.0, x)], axis=0).astype(BF16)

    def pair_stages(p):
        ks = slice(p * LANES, (p + 1) * LANES)
        state = {}

        def scores():
            q4 = jnp.concatenate([q[:, (reps * p + r) * LANES:(reps * p + r + 1) * LANES] * scale
                                  for r in range(reps)], axis=0)
            state["sp"] = lax.dot_general(q4, block_diag(kp[:, ks]), _NT, preferred_element_type=F32)
            state["sc"] = lax.dot_general(q4, block_diag(kc[:, ks]), _NT, preferred_element_type=F32)

        def softmax():
            sp, sc = state.pop("sp"), state.pop("sc")
            pps, pcs, invs = [], [], []
            for r in range(reps):
                rows = slice(r * tq, (r + 1) * tq)
                pp_r, pc_r, inv_r = [], [], []
                for half in range(2):
                    cols = slice(half * WINDOW, (half + 1) * WINDOW)
                    head = 2 * reps * p + reps * half + r
                    sp_i = jnp.where(sink_lane, sink_ref[head], jnp.where(valid_prev, sp[rows, cols], neg_inf))
                    sc_i = jnp.where(valid_cur, sc[rows, cols], neg_inf)
                    mx = jnp.max(jnp.maximum(sp_i, sc_i), axis=1, keepdims=True)
                    pp = jnp.exp(sp_i - mx)
                    pc = jnp.exp(sc_i - mx)
                    inv_r.append(1.0 / jnp.sum(pp + pc, axis=1, keepdims=True))
                    pp_r.append(pp.astype(BF16))
                    pc_r.append(pc.astype(BF16))
                pps.append(jnp.concatenate(pp_r, axis=1))
                pcs.append(jnp.concatenate(pc_r, axis=1))
                invs.append(jnp.where(low_q, inv_r[0], inv_r[1]))
            state.update(pp=jnp.concatenate(pps, axis=0), pc=jnp.concatenate(pcs, axis=0), inv=invs)

        def values():
            v_prev = block_diag(jnp.where(key_row == 0, 0.0, vp[:, ks]))
            o4 = (jnp.dot(state.pop("pp"), v_prev, preferred_element_type=F32)
                  + jnp.dot(state.pop("pc"), block_diag(vc[:, ks]), preferred_element_type=F32))
            invs = state.pop("inv")
            for r in range(reps):
                blk = reps * p + r
                o_ref[:, blk * LANES:(blk + 1) * LANES] = (o4[r * tq:(r + 1) * tq] * invs[r]).astype(BF16)

        return scores, softmax, values

    return [pair_stages(p) for p in range(A_KV_HEADS // 2)]


def _swa_block(sink_ref, q, kp, vp, kc, vc, valid_prev, o_ref, between=None):
    for scores, softmax, values in _swa_stages(sink_ref, q, kp, vp, kc, vc, valid_prev, o_ref):
        if between is not None:
            between()
        scores()
        softmax()
        if between is not None:
            between()
        values()


def _swa(sinks, q, k_cache, k_new, v_cache, v_new, *, nbb, n_new):
    n, tq, _ = q.shape
    tk = k_new.shape[1]
    index = lambda i: (i, 0, 0)
    cache_spec = pl.BlockSpec((nbb, WINDOW, W_AKV), index)
    new_spec = pl.BlockSpec((nbb, tk, W_AKV), index)
    q_spec = pl.BlockSpec((nbb, tq, W_AQ), index)
    return pl.pallas_call(
        functools.partial(_swa_kernel, nbb=nbb, n_new=n_new),
        grid=(n // nbb,),
        in_specs=[pl.BlockSpec(memory_space=pltpu.SMEM), q_spec, cache_spec, new_spec, cache_spec, new_spec],
        out_specs=[q_spec, cache_spec, cache_spec],
        out_shape=[jax.ShapeDtypeStruct((n, tq, W_AQ), BF16)]
        + [jax.ShapeDtypeStruct((n, WINDOW, W_AKV), F32)] * 2,
        compiler_params=_params(),
        name="swa",
    )(sinks, q, k_cache, k_new, v_cache, v_new)


def _merge_kernel(x_ref, g_ref, hm_ref, oa_ref, gm_ref, ga_ref, wm_ref, wa_ref, wo_ref,
                  lg_ref, lb_ref, o_ref):
    ym = jnp.dot(hm_ref[...], wm_ref[...], preferred_element_type=F32)
    ya = jnp.dot(oa_ref[...], wa_ref[...], preferred_element_type=F32)
    mix = gm_ref[...].astype(F32) * ym + ga_ref[...].astype(F32) * ya
    t = jnp.dot(mix.astype(BF16), wo_ref[...], preferred_element_type=F32)
    y = DEEPNORM_ALPHA * x_ref[...] + (1.0 + _rows(g_ref)) * t
    o_ref[...] = _layer_norm(y, lg_ref[...], lb_ref[...], LN_EPS)


def _merge(x, mod, per_row, rows_per_batch, chunk, hm, oa, gm, ga, wm, wa, wo, ln_g, ln_b):
    m = x.shape[0]
    tm = ROW_TILE

    def tok():
        return pl.BlockSpec((tm, D_MODEL), lambda i: (i, 0))

    return pl.pallas_call(
        _merge_kernel,
        grid=(m // tm,),
        in_specs=[tok()] + _mod_specs(per_row, tm, rows_per_batch // tm, (chunk,))
        + [tok(), tok(), tok(), tok()]
        + [_resident((D_MODEL, D_MODEL))] * 3 + [_resident((1, D_MODEL))] * 2,
        out_specs=tok(),
        out_shape=jax.ShapeDtypeStruct((m, D_MODEL), F32),
        compiler_params=_params(),
        name="merge",
    )(x, mod, hm, oa, gm, ga, wm, wa, wo, ln_g.reshape(1, D_MODEL), ln_b.reshape(1, D_MODEL))


def _mixer_kernel(sink_ref, xa_ref, xb_ref, sh_ref, sc_ref, g_ref, wa_ref, wq_ref, wr_ref, wif_ref, bif_ref,
                  cos_ref, sin_ref, ng_ref, wm_ref, wba_ref, wo_ref, lg_ref, lb_ref,
                  y_ref, c_ref, n_ref, m_ref, kb_ref, vb_ref, *scratch, tiles_per_seq):
    n_staged = (len(scratch) - 6) // 2
    even, odd = scratch[:n_staged], scratch[n_staged:2 * n_staged]
    kprev_s, vprev_s, hm_s, oa_s, gtt_s, ctt_s = scratch[2 * n_staged:]
    tm = xa_ref.shape[0]
    s = pl.program_id(0)
    tile_b = jnp.maximum(s - 1, 0)
    seq_start = tile_b % tiles_per_seq == 0

    @pl.when(s == 0)
    def _():
        for ref in odd:
            ref[...] = jnp.zeros_like(ref)

    @pl.when(seq_start)
    def _():
        c_ref[...] = jnp.zeros_like(c_ref)
        n_ref[...] = jnp.zeros_like(n_ref)
        m_ref[...] = jnp.zeros_like(m_ref)
        kprev_s[...] = jnp.zeros_like(kprev_s)
        vprev_s[...] = jnp.zeros_like(vprev_s)

    def step(produce, consume):
        proj_parts = _proj_parts(xa_ref, sh_ref, sc_ref, wa_ref, wq_ref, wr_ref, wif_ref, bif_ref,
                                 cos_ref, sin_ref, *produce)
        pending = iter(proj_parts)

        def emit(count):
            def between():
                for _ in range(count):
                    part = next(pending, None)
                    if part is not None:
                        part()
            return between

        q_s, k_s, v_s, mo_s, aq_s, ak_s, av_s, gm_s, ga_s, gt_s = consume
        qi = lax.broadcasted_iota(jnp.int32, (WINDOW, WINDOW), 0)
        kj = lax.broadcasted_iota(jnp.int32, (WINDOW, WINDOW), 1)
        for j in range(tm // MLSTM_CHUNK):
            rows = pl.ds(j * MLSTM_CHUNK, MLSTM_CHUNK)
            _mlstm_chunk(q_s.at[rows], k_s.at[rows], v_s.at[rows], gt_s.at[rows], mo_s.at[rows], ng_ref,
                         hm_s.at[rows], c_ref, n_ref, m_ref, gtt_s, ctt_s,
                         nb=1, tpb=MLSTM_CHUNK, hps=M_HEADS, head0=0, between=emit(MIXER_PROJ_PER_HEAD))
            valid_prev = kj > qi
            if j == 0:
                valid_prev = valid_prev & jnp.logical_not(seq_start)
                k_prev, v_prev = kprev_s[...], vprev_s[...]
            else:
                before = pl.ds((j - 1) * WINDOW, WINDOW)
                k_prev, v_prev = ak_s[before], av_s[before]
            _swa_block(sink_ref, aq_s[rows], k_prev, v_prev, ak_s[rows], av_s[rows], valid_prev,
                       oa_s.at[rows], between=emit(MIXER_PROJ_PER_ATTN))

        for part in pending:
            part()
        _merge_kernel(xb_ref, g_ref, hm_s, oa_s, gm_s, ga_s, wm_ref, wba_ref, wo_ref, lg_ref, lb_ref, y_ref)

        last = pl.ds(tm - WINDOW, WINDOW)
        k_last, v_last = ak_s[last], av_s[last]
        kb_ref[0] = k_last.T
        vb_ref[0] = v_last.T
        kprev_s[...] = k_last
        vprev_s[...] = v_last

    @pl.when(s % 2 == 0)
    def _():
        step(even, odd)

    @pl.when(s % 2 == 1)
    def _():
        step(odd, even)


def _mixer(x1, mod, w, sinks, norm_g, cos_t, sin_t, bp, sp):
    tm = MIXER_TILE
    tps = sp // tm
    n_tiles = bp * tps
    hd = M_HEADS * M_DQK
    weights = (w["w_a"], w["w_aq"], w["w_r"], w["w_if"], w["bif"])

    def tile_a(s):
        return jnp.minimum(s, n_tiles - 1)

    def tile_b(s):
        return jnp.maximum(s - 1, 0)

    def mod_spec(tile, chunk):
        return pl.BlockSpec((1, 1, D_MODEL), lambda s: (tile(s) // tps, 0, chunk))

    def per_seq(*shape):
        return pl.BlockSpec((1,) + shape, lambda s: (tile_b(s) // tps,) + (0,) * len(shape))

    pos = pl.BlockSpec((tm, LANES), lambda s: (tile_a(s) % tps, 0))
    staged = [(W_MQ, BF16), (W_MQ, BF16), (W_MV, BF16), (W_MV, BF16), (W_AQ, BF16),
              (W_AKV, F32), (W_AKV, F32), (D_MODEL, BF16), (D_MODEL, BF16), (LANES, F32)]
    return pl.pallas_call(
        functools.partial(_mixer_kernel, tiles_per_seq=tps),
        grid=(n_tiles + 1,),
        in_specs=[pl.BlockSpec(memory_space=pltpu.SMEM),
                  pl.BlockSpec((tm, D_MODEL), lambda s: (tile_a(s), 0)),
                  pl.BlockSpec((tm, D_MODEL), lambda s: (tile_b(s), 0)),
                  mod_spec(tile_a, 3), mod_spec(tile_a, 4), mod_spec(tile_b, 5)]
        + [_resident(a.shape) for a in weights] + [pos, pos, _resident((1, M_HEADS * M_DV))]
        + [_resident((D_MODEL, D_MODEL))] * 3 + [_resident((1, D_MODEL))] * 2,
        out_specs=[pl.BlockSpec((tm, D_MODEL), lambda s: (tile_b(s), 0)),
                   per_seq(M_HEADS, M_DV, M_DQK),
                   pl.BlockSpec((MLSTM_CHUNK, hd), lambda s: (tile_b(s) // tps, 0)),
                   pl.BlockSpec((MLSTM_CHUNK, LANES), lambda s: (tile_b(s) // tps, 0)),
                   per_seq(W_AKV, WINDOW), per_seq(W_AKV, WINDOW)],
        out_shape=[jax.ShapeDtypeStruct((bp * sp, D_MODEL), F32),
                   jax.ShapeDtypeStruct((bp, M_HEADS, M_DV, M_DQK), F32),
                   jax.ShapeDtypeStruct((bp * MLSTM_CHUNK, hd), F32),
                   jax.ShapeDtypeStruct((bp * MLSTM_CHUNK, LANES), F32),
                   jax.ShapeDtypeStruct((bp, W_AKV, WINDOW), F32),
                   jax.ShapeDtypeStruct((bp, W_AKV, WINDOW), F32)],
        scratch_shapes=[pltpu.VMEM((tm, width), dtype) for width, dtype in staged + staged]
        + [pltpu.VMEM((WINDOW, W_AKV), F32), pltpu.VMEM((WINDOW, W_AKV), F32),
           pltpu.VMEM((tm, W_MV), BF16), pltpu.VMEM((tm, W_AQ), BF16),
           pltpu.VMEM((LANES, MLSTM_CHUNK), F32), pltpu.VMEM((LANES, MLSTM_CHUNK), F32)],
        compiler_params=_params(),
        name="mixer",
    )(sinks, x1, x1, mod, mod, mod, *weights, cos_t, sin_t, norm_g.reshape(1, M_HEADS * M_DV),
      w["wm"], w["wa"], w["wo"], w["ln2_g"].reshape(1, D_MODEL), w["ln2_b"].reshape(1, D_MODEL))


def _rope_tables(pos):
    half = A_DH // 2
    inv = np.float32(ROPE_THETA) ** (-np.arange(half, dtype=np.float32) / np.float32(half))
    ang = pos.astype(np.float32)[:, None] * inv[None, :]
    cos, sin = np.cos(ang), np.sin(ang)
    return jnp.asarray(np.tile(cos, (1, 4))), jnp.asarray(np.concatenate([-sin, sin, -sin, sin], axis=1))


def _from_stored(kv_t):
    return jnp.transpose(kv_t.reshape(kv_t.shape[0], A_KV_HEADS, A_DH, WINDOW), (0, 3, 1, 2))


def _token_stage_1(x, mod, per_row, rows_per_batch, w, pos):
    x1 = _ffn(x, mod, per_row, rows_per_batch, (0, 1, 2), w["up1"], w["down1"], w["ln1_g"], w["ln1_b"])
    cos_t, sin_t = _rope_tables(pos)
    return x1, _proj(x1, mod, per_row, rows_per_batch, (3, 4), w, cos_t, sin_t)


def _token_stage_2(x1, mod, per_row, rows_per_batch, w, hm, oa, gm, ga):
    x2 = _merge(x1, mod, per_row, rows_per_batch, 5, hm, oa, gm, ga,
                w["wm"], w["wa"], w["wo"], w["ln2_g"], w["ln2_b"])
    return _ffn(x2, mod, per_row, rows_per_batch, (6, 7, 8), w["up2"], w["down2"], w["ln3_g"], w["ln3_b"])


def kernel(x_prompt, x_sample, state_mlstm_C, state_mlstm_n, state_mlstm_m, cache_swa_k, cache_swa_v, c_prompt, c_sample, w_ada, b_ada, w_ffn1_up, w_ffn1_down, ln1_g, ln1_b, w_in, b_igate, b_fgate, m_norm_g, sinks, w_branch_m, w_branch_a, w_out, ln2_g, ln2_b, w_ffn2_up, w_ffn2_down, ln3_g, ln3_b):
    assert w_ada.shape[0] == DEPTH == 1
    bp, sp, _ = x_prompt.shape
    bs, ts, _ = x_sample.shape

    w_in_t = jnp.transpose(w_in[0])
    w = dict(
        up1=w_ffn1_up[0].astype(BF16), down1=w_ffn1_down[0].astype(BF16),
        bif=jnp.concatenate([b_igate[0], b_fgate[0], jnp.zeros((LANES - 2 * M_HEADS,), F32)]).reshape(1, LANES),
        ln1_g=ln1_g[0], ln1_b=ln1_b[0], ln2_g=ln2_g[0], ln2_b=ln2_b[0], ln3_g=ln3_g[0], ln3_b=ln3_b[0],
    )

    ms = bs * ts
    c_all = jnp.concatenate([jnp.repeat(c_sample, ts, axis=0), c_prompt], axis=0)
    mod = _ada(c_all, w_ada[0], b_ada[0])
    mod_p = mod[ms:].reshape(bp, 1, ADA_CHUNKS * D_MODEL)

    mp = bp * sp
    reps = A_HEADS // A_KV_HEADS
    pair_low = lambda j: (j // reps) * (2 * reps) + j % reps
    pair_rows = lambda j: (j // (2 * reps)) * (2 * reps) + (j % 2) * reps + (j % (2 * reps)) // 2
    jobs = (
        _cast_job(w_ffn2_up[0], (D_MODEL, FF_CHUNK), 1),
        _cast_job(w_ffn2_down[0], (FF_CHUNK, D_MODEL), 0, first_step=2 * D_FF // FF_CHUNK),
        _cast_job(w_in_t, (D_MODEL, LANES), 1, transposed=True, n_blocks=IN_IF // LANES),
        _cast_job(w_in_t, (D_MODEL, LANES), 1, transposed=True, n_blocks=W_AQ // LANES, shift=IN_AQ - IN_IF,
                  sources=(lambda j: IN_IF // A_DH + pair_low(j), lambda j: IN_IF // A_DH + pair_low(j) + reps)),
        _cast_job(w_in_t, (D_MODEL, LANES), 1, transposed=True, n_blocks=(IN_END - IN_AK) // LANES,
                  shift=IN_AQ - IN_IF, sources=(lambda j: (IN_AK - (IN_AQ - IN_IF)) // LANES + j,)),
        _cast_job(w_branch_m[0], (D_MODEL, LANES), 1),
        _cast_job(w_out[0], (D_MODEL, LANES), 1),
        _cast_job(w_branch_a[0], (A_DH, D_MODEL), 0, sources=(pair_rows,)),
        _cast_job(w_in_t, (LANES, D_MODEL), 0, n_blocks=1, sources=(lambda j: IN_IF // LANES + j,)),
    )
    x1p, w["up2"], w["down2"], w["w_a"], w["w_aq"], w["w_r"], w["wm"], w["wo"], w["wa"], w["w_if"] = _ffn(
        x_prompt.reshape(mp, D_MODEL), mod_p, False, sp, (0, 1, 2),
        w["up1"], w["down1"], w["ln1_g"], w["ln1_b"], cast_jobs=jobs)
    cos_t, sin_t = _rope_tables(np.arange(sp))
    x2p, c_p, n_rows, m_rows, kb_p, vb_p = _mixer(x1p, mod_p, w, sinks[0], m_norm_g[0], cos_t, sin_t, bp, sp)
    y_p = _ffn(x2p, mod_p, False, sp, (6, 7, 8), w["up2"], w["down2"], w["ln3_g"], w["ln3_b"])
    n_p = n_rows.reshape(bp, MLSTM_CHUNK, M_HEADS, M_DQK)[:, 0]
    m_p = m_rows.reshape(bp, MLSTM_CHUNK, LANES)[:, 0, :M_HEADS]
    kb_p, vb_p = _from_stored(kb_p), _from_stored(vb_p)

    x1s, (qm, km, vm, mo, aq, ak, av, gm, ga, gt) = _token_stage_1(
        x_sample.reshape(ms, D_MODEL), mod, True, ms, w, PAST_LEN + np.arange(ms) % ts)
    seqs = MLSTM_CHUNK // ts
    n0_rows = jnp.repeat(state_mlstm_n[0].reshape(bs, M_HEADS * M_DQK), ts, axis=0)
    m0_rows = jnp.repeat(jnp.pad(state_mlstm_m[0], ((0, 0), (0, LANES - M_HEADS))), ts, axis=0)
    hm, c_s, n_rows, m_rows = _mlstm(qm, km, vm, gt, mo, m_norm_g[0],
                                     (state_mlstm_C[0], n0_rows, m0_rows), seqs, ts, 1, 1)
    n_s = n_rows.reshape(bs, ts, M_HEADS, M_DQK)[:, 0]
    m_s = m_rows.reshape(bs, ts, LANES)[:, 0, :M_HEADS]
    pad_t = ((0, 0), (0, SAMPLE_TQ - ts), (0, 0))
    oa, kb_s, vb_s = _swa(
        sinks[0], jnp.pad(aq.reshape(bs, ts, W_AQ), pad_t),
        cache_swa_k[0].reshape(bs, WINDOW, W_AKV), jnp.pad(ak.reshape(bs, ts, W_AKV), pad_t),
        cache_swa_v[0].reshape(bs, WINDOW, W_AKV), jnp.pad(av.reshape(bs, ts, W_AKV), pad_t),
        nbb=SWA_SAMPLE_SEQS, n_new=ts)
    y_s = _token_stage_2(x1s, mod, True, ms, w, hm, oa[:, :ts].reshape(ms, D_MODEL), gm, ga)
    kb_s = kb_s.reshape(bs, WINDOW, A_KV_HEADS, A_DH)
    vb_s = vb_s.reshape(bs, WINDOW, A_KV_HEADS, A_DH)

    return (y_p.reshape(bp, sp, D_MODEL), y_s.reshape(bs, ts, D_MODEL),
            c_p[None], n_p[None], m_p[None], kb_p[None], vb_p[None],
            c_s[None], n_s[None], m_s[None], kb_s[None], vb_s[None])
```

```python
import functools

import jax
import jax.numpy as jnp
import numpy as np
from jax import lax
from jax.experimental import pallas as pl
from jax.experimental.pallas import tpu as pltpu

F32 = jnp.float32
BF16 = jnp.bfloat16

D_MODEL = 1024
D_FF = 2816
DEPTH = 1
M_HEADS = 4
M_DQK = 128
M_DV = 256
A_HEADS = 16
A_KV_HEADS = 4
A_DH = 64
WINDOW = 128
PAST_LEN = 8192
ROPE_THETA = 10000.0
ATTN_SCALE = A_DH ** -0.5
LN_EPS = 1e-5
HEAD_NORM_EPS = 1e-6
ADA_CHUNKS = 9
DEEPNORM_ALPHA = (2.0 * DEPTH) ** 0.25
K_SCALE = M_DQK ** -0.5

LANES = 128
BF16_SUBLANES = 16
VMEM_LIMIT_BYTES = 56 * 1024 * 1024

W_MQ = M_HEADS * M_DQK
W_MV = M_HEADS * M_DV
W_AQ = A_HEADS * A_DH
W_AKV = A_KV_HEADS * A_DH
IN_IF = 2 * W_MQ + 2 * W_MV
IN_AQ = IN_IF + 2 * M_HEADS
IN_AK = IN_AQ + W_AQ
IN_END = IN_AK + 2 * W_AKV + 2 * D_MODEL
A_Q, A_K, A_V, A_O = 0, W_MQ, 2 * W_MQ, 2 * W_MQ + W_MV
R_AK, R_AV, R_GM, R_GA = 0, W_AKV, 2 * W_AKV, 2 * W_AKV + D_MODEL

ROW_TILE = 512
MIXER_TILE = 256
MIXER_PROJ_SCHEDULE = (1, 1, 1, 1, 2, 2, 2, 2) * 2
ADA_TILE = 1536
FF_CHUNK = 256
PROJ_CHUNK = 2 * LANES
FFN_EPILOGUE_PIECES = 4
MLSTM_CHUNK = 128
SAMPLE_TQ = BF16_SUBLANES
SWA_SAMPLE_SEQS = 16

_NT = (((1,), (1,)), ((), ()))


def _params():
    return pltpu.CompilerParams(vmem_limit_bytes=VMEM_LIMIT_BYTES)


def _resident(shape):
    return pl.BlockSpec(shape, lambda *_: (0,) * len(shape), pipeline_mode=pl.Buffered(1))


def _rows(ref):
    v = ref[...]
    return v.reshape(v.shape[-2], v.shape[-1])


def _layer_norm(y, g, b, eps):
    mu = jnp.mean(y, axis=-1, keepdims=True)
    d = y - mu
    var = jnp.mean(d * d, axis=-1, keepdims=True)
    return d * lax.rsqrt(var + eps) * g + b


def _sigmoid(x):
    return 1.0 / (1.0 + jnp.exp(-x))


def _mod_specs(per_row, tm, tiles_per_batch, chunks):
    if per_row:
        return [pl.BlockSpec((tm, D_MODEL), lambda i, c=c: (i, c)) for c in chunks]
    return [pl.BlockSpec((1, 1, D_MODEL), lambda i, c=c: (i // tiles_per_batch, 0, c)) for c in chunks]


def _ada_kernel(c_ref, w_ref, b_ref, o_ref):
    c = c_ref[...]
    s = (c * _sigmoid(c)).astype(BF16)
    o_ref[...] = jnp.dot(s, w_ref[...].astype(BF16), preferred_element_type=F32) + b_ref[...]


def _ada(c_all, w_ada, b_ada):
    rows = c_all.shape[0]
    n_out = w_ada.shape[1]
    tn = ADA_TILE
    return pl.pallas_call(
        _ada_kernel,
        grid=(n_out // tn,),
        in_specs=[pl.BlockSpec((rows, D_MODEL), lambda j: (0, 0)),
                  pl.BlockSpec((D_MODEL, tn), lambda j: (0, j)),
                  pl.BlockSpec((1, tn), lambda j: (0, j))],
        out_specs=pl.BlockSpec((rows, tn), lambda j: (0, j)),
        out_shape=jax.ShapeDtypeStruct((rows, n_out), F32),
        compiler_params=_params(),
        name="ada",
    )(c_all, w_ada, b_ada.reshape(1, n_out))


def _ffn_kernel(x_ref, xp_ref, sh_ref, sc_ref, g_ref, wup_ref, wdn_ref, lg_ref, lb_ref, *rest,
                n_tiles, cast_layout):
    n_cast = len(cast_layout)
    n_src = sum(layout[0] for layout in cast_layout)
    cast_in, o_ref, cast_out = rest[:n_src], rest[n_src], rest[n_src + 1:n_src + 1 + n_cast]
    act_ref, f_ref = rest[n_src + 1 + n_cast:]

    def cast_chunk(job):
        if job >= n_cast:
            return
        first = sum(layout[0] for layout in cast_layout[:job])
        n, transposed, shift = cast_layout[job]
        refs = cast_in[first:first + n]
        if shift:
            rows = refs[0].shape[0]
            parts = [jnp.concatenate([lo[...], hi[...]], axis=0)[shift:shift + rows]
                     for lo, hi in zip(refs[0::2], refs[1::2])]
        else:
            parts = [ref[...] for ref in refs]
        block = jnp.concatenate(parts, axis=0)
        cast_out[job][...] = (block.T if transposed else block).astype(BF16)

    tm = x_ref.shape[0]
    s = pl.program_id(0)
    piece_rows = tm // FFN_EPILOGUE_PIECES

    @pl.when(s == 0)
    def _():
        f_ref[...] = jnp.zeros_like(f_ref)

    def epilogue(piece):
        rows = pl.ds(piece * piece_rows, piece_rows)
        g = _rows(g_ref)
        if g.shape[0] != 1:
            g = g[piece * piece_rows:(piece + 1) * piece_rows]
        y = DEEPNORM_ALPHA * xp_ref[rows, :] + (0.5 * (1.0 + g)) * f_ref[rows, :]
        o_ref[rows, :] = _layer_norm(y, lg_ref[...], lb_ref[...], LN_EPS)

    @pl.when(s < n_tiles)
    def _():
        h = (x_ref[...] * (1.0 + _rows(sc_ref)) + _rows(sh_ref)).astype(BF16)
        chunks = list(range(0, D_FF, FF_CHUNK))
        every = len(chunks) // FFN_EPILOGUE_PIECES
        piece = 0
        for i, c in enumerate(chunks):
            a = jnp.dot(h, wup_ref[:, c:c + FF_CHUNK], preferred_element_type=F32)
            u = jnp.dot(h, wup_ref[:, D_FF + c:D_FF + c + FF_CHUNK], preferred_element_type=F32)
            act_ref[:, c:c + FF_CHUNK] = (a * _sigmoid(a) * u).astype(BF16)
            cast_chunk(i)
            if i % every == every - 1 and piece < FFN_EPILOGUE_PIECES:
                epilogue(piece)
                piece += 1
        f_ref[...] = jnp.dot(act_ref[...], wdn_ref[...], preferred_element_type=F32)

    @pl.when(s == n_tiles)
    def _():
        for job in range(n_cast):
            cast_chunk(job)
        for piece in range(FFN_EPILOGUE_PIECES):
            epilogue(piece)


def _cast_job(array, block, axis, first_step=0, sources=None, transposed=False, n_blocks=None, shift=0):
    sources = sources or (lambda j: j,)
    src_axis = 1 - axis if transposed else axis
    src_block = block[::-1] if transposed else block
    src_block = (src_block[0] // len(sources), src_block[1])
    if n_blocks is None:
        n_blocks = array.shape[src_axis] // block[axis]

    def step_block(s):
        return jnp.clip(s - first_step, 0, n_blocks - 1)

    def at(axis_, j):
        return (j, 0) if axis_ == 0 else (0, j)

    in_specs = [pl.BlockSpec(src_block, lambda s, f=f, d=d: at(src_axis, f(step_block(s)) + d))
                for f in sources for d in range(2 if shift else 1)]
    out_dims = list(array.shape[::-1] if transposed else array.shape)
    out_dims[axis] = n_blocks * block[axis]
    return dict(array=array, in_specs=in_specs, layout=(len(in_specs), transposed, shift),
                out_spec=pl.BlockSpec(block, lambda s: at(axis, step_block(s))),
                out_shape=jax.ShapeDtypeStruct(tuple(out_dims), BF16), steps=first_step + n_blocks)


def _ffn(x, mod, per_row, rows_per_batch, chunks, w_up, w_down, ln_g, ln_b, cast_jobs=()):
    m = x.shape[0]
    tm = ROW_TILE
    n_tiles = m // tm
    tpb = rows_per_batch // tm

    def cur(s):
        return jnp.minimum(s, n_tiles - 1)

    def prev(s):
        return jnp.maximum(s - 1, 0)

    def mod_spec(tile, chunk):
        if per_row:
            return pl.BlockSpec((tm, D_MODEL), lambda s: (tile(s), chunk))
        return pl.BlockSpec((1, 1, D_MODEL), lambda s: (tile(s) // tpb, 0, chunk))

    in_specs = [pl.BlockSpec((tm, D_MODEL), lambda s: (cur(s), 0)),
                pl.BlockSpec((tm, D_MODEL), lambda s: (prev(s), 0)),
                mod_spec(cur, chunks[0]), mod_spec(cur, chunks[1]), mod_spec(prev, chunks[2]),
                _resident((D_MODEL, 2 * D_FF)), _resident((D_FF, D_MODEL)),
                _resident((1, D_MODEL)), _resident((1, D_MODEL))]
    args = [x, x, mod, mod, mod, w_up, w_down, ln_g.reshape(1, D_MODEL), ln_b.reshape(1, D_MODEL)]
    out_specs = [pl.BlockSpec((tm, D_MODEL), lambda s: (prev(s), 0))]
    out_shape = [jax.ShapeDtypeStruct((m, D_MODEL), F32)]
    for job in cast_jobs:
        assert job["steps"] <= n_tiles + 1
        in_specs += job["in_specs"]
        args += [job["array"]] * len(job["in_specs"])
        out_specs.append(job["out_spec"])
        out_shape.append(job["out_shape"])
    outs = pl.pallas_call(
        functools.partial(_ffn_kernel, n_tiles=n_tiles, cast_layout=tuple(j["layout"] for j in cast_jobs)),
        grid=(n_tiles + 1,),
        in_specs=in_specs,
        out_specs=out_specs,
        out_shape=out_shape,
        scratch_shapes=[pltpu.VMEM((tm, D_FF), BF16), pltpu.VMEM((tm, D_MODEL), F32)],
        compiler_params=_params(),
        name="ffn",
    )(*args)
    return outs if cast_jobs else outs[0]


def _proj_parts(x_ref, sh_ref, sc_ref, wa_ref, wq_ref, wr_ref, wif_ref, bif_ref, cos_ref, sin_ref,
                q_ref, k_ref, v_ref, o_ref, aq_ref, ak_ref, av_ref, gm_ref, ga_ref, gt_ref):
    x = x_ref[...]
    tm = x.shape[0]
    h = (x * (1.0 + _rows(sc_ref)) + _rows(sh_ref)).astype(BF16)
    lane = lax.broadcasted_iota(jnp.int32, (tm, LANES), 1)
    pc = PROJ_CHUNK

    def seg(w_ref, lo):
        return jnp.dot(h, w_ref[:, lo:lo + pc], preferred_element_type=F32)

    def plain(dst_ref, w_ref, lo, c, scale=None):
        def run():
            z = seg(w_ref, lo + c)
            dst_ref[:, c:c + pc] = (z if scale is None else z * scale).astype(dst_ref.dtype)
        return run

    def gate(dst_ref, w_ref, lo, c):
        def run():
            dst_ref[:, c:c + pc] = _sigmoid(seg(w_ref, lo + c)).astype(BF16)
        return run

    def forget_input_gates():
        zg = lax.dot_general(h, wif_ref[...], _NT, preferred_element_type=F32) + bif_ref[...]
        logsig = jnp.minimum(zg, 0.0) - jnp.log(1.0 + jnp.exp(-jnp.abs(zg)))
        gt_ref[...] = jnp.where(lane < M_HEADS, zg, logsig)

    def rotary(dst_ref, w_ref, lo, c):
        def run():
            cos = cos_ref[...]
            sin = sin_ref[...]
            low_half = (lane & (A_DH // 2)) == 0
            z = seg(w_ref, lo + c)
            for half in range(2):
                zh = z[:, half * LANES:(half + 1) * LANES]
                partner = jnp.where(low_half, pltpu.roll(zh, LANES - A_DH // 2, 1),
                                    pltpu.roll(zh, A_DH // 2, 1))
                dst_ref[:, c + half * LANES:c + (half + 1) * LANES] = (
                    zh * cos + partner * sin).astype(dst_ref.dtype)
        return run

    parts = [plain(q_ref, wa_ref, A_Q, c) for c in range(0, W_MQ, pc)]
    parts += [plain(k_ref, wa_ref, A_K, c, K_SCALE) for c in range(0, W_MQ, pc)]
    parts += [plain(v_ref, wa_ref, A_V, c) for c in range(0, W_MV, pc)]
    parts += [forget_input_gates]
    parts += [rotary(aq_ref, wq_ref, 0, c) for c in range(0, W_AQ, pc)]
    parts += [rotary(ak_ref, wr_ref, R_AK, 0), plain(av_ref, wr_ref, R_AV, 0)]
    parts += [gate(o_ref, wa_ref, A_O, c) for c in range(0, W_MV, pc)]
    parts += [gate(gm_ref, wr_ref, R_GM, c) for c in range(0, D_MODEL, pc)]
    parts += [gate(ga_ref, wr_ref, R_GA, c) for c in range(0, D_MODEL, pc)]
    return parts


def _proj_kernel(*refs):
    for part in _proj_parts(*refs):
        part()


def _proj(x, mod, per_row, rows_per_batch, chunks, w, cos_t, sin_t):
    m = x.shape[0]
    tm = ROW_TILE
    n_pos_tiles = cos_t.shape[0] // tm

    def tok(width):
        return pl.BlockSpec((tm, width), lambda i: (i, 0))

    widths = (W_MQ, W_MQ, W_MV, W_MV, W_AQ, W_AKV, W_AKV, D_MODEL, D_MODEL, LANES)
    dtypes = (BF16, BF16, BF16, BF16, BF16, F32, F32, BF16, BF16, F32)
    weights = (w["w_a"], w["w_aq"], w["w_r"], w["w_if"], w["bif"])
    return pl.pallas_call(
        _proj_kernel,
        grid=(m // tm,),
        in_specs=[tok(D_MODEL)] + _mod_specs(per_row, tm, rows_per_batch // tm, chunks)
        + [_resident(a.shape) for a in weights]
        + [pl.BlockSpec((tm, LANES), lambda i: (i % n_pos_tiles, 0)),
           pl.BlockSpec((tm, LANES), lambda i: (i % n_pos_tiles, 0))],
        out_specs=[tok(wd) for wd in widths],
        out_shape=[jax.ShapeDtypeStruct((m, wd), d) for wd, d in zip(widths, dtypes)],
        compiler_params=_params(),
        name="proj",
    )(x, mod, mod, *weights, cos_t, sin_t)


def _mlstm_kernel(q_ref, k_ref, v_ref, g_ref, mo_ref, ng_ref, c0_ref, n0_ref, m0_ref,
                  out_ref, c_ref, n_ref, m_ref, gt_s, ct_s, *, nb, tpb, hps):
    first_chunk = pl.program_id(2) == 0

    @pl.when(first_chunk)
    def _():
        c_ref[...] = c0_ref[...]
        n_ref[...] = n0_ref[...]

    @pl.when(first_chunk & (pl.program_id(1) == 0))
    def _():
        m_ref[...] = m0_ref[...]

    head0 = 0 if hps == M_HEADS else pl.program_id(1) * hps
    _mlstm_chunk(q_ref, k_ref, v_ref, g_ref, mo_ref, ng_ref, out_ref, c_ref, n_ref, m_ref, gt_s, ct_s,
                 nb=nb, tpb=tpb, hps=hps, head0=head0)


def _mlstm_chunk(q_ref, k_ref, v_ref, g_ref, mo_ref, ng_ref, out_ref, c_ref, n_ref, m_ref, gt_s, ct_s,
                 *, nb, tpb, hps, head0, between=None):
    L = nb * tpb
    shift = tpb.bit_length() - 1
    row = lax.broadcasted_iota(jnp.int32, (L, L), 0)
    col = lax.broadcasted_iota(jnp.int32, (L, L), 1)
    same = (row >> shift) == (col >> shift)
    causal = same & (col <= row)
    lane = lax.broadcasted_iota(jnp.int32, (L, LANES), 1)
    row_seq = lax.broadcasted_iota(jnp.int32, (L, 1), 0) >> shift

    def lane_col(x, idx):
        return jnp.sum(jnp.where(lane == idx, x, 0.0), axis=1, keepdims=True)

    gates = g_ref[...]
    tri = jnp.where(causal, 1.0, 0.0).astype(BF16)
    g_hi = gates.astype(BF16)
    rem = gates - g_hi.astype(F32)
    g_mid = rem.astype(BF16)
    g_lo = (rem - g_mid.astype(F32)).astype(BF16)
    cum = (jnp.dot(tri, g_hi, preferred_element_type=F32)
           + jnp.dot(tri, g_mid, preferred_element_type=F32)
           + jnp.dot(tri, g_lo, preferred_element_type=F32))
    gt_s[...] = gates.T
    ct_s[...] = cum.T
    m_rows = m_ref[...]
    m_next = m_rows

    def head_stages(hl):
        head = head0 + hl
        qs = slice(hl * M_DQK, (hl + 1) * M_DQK)
        vs = slice(hl * M_DV, (hl + 1) * M_DV)
        st = {}

        def gate_stage():
            b_c = lane_col(cum, M_HEADS + head)
            m_p = lane_col(m_rows, head)
            i_r = gt_s[pl.ds(head, 1), :]
            b_r = ct_s[pl.ds(M_HEADS + head, 1), :]
            log_d = jnp.where(causal, b_c - b_r + i_r, -jnp.inf)
            m_t = jnp.maximum(b_c + m_p, jnp.max(log_d, axis=1, keepdims=True))
            dmat = jnp.exp(log_d - m_t)
            e_int = jnp.exp(b_c + m_p - m_t)
            if nb == 1:
                last = slice(L - 1, L)
                m_new = jnp.broadcast_to(m_t[last], (L, 1))
                e_c = jnp.broadcast_to(e_int[last], (L, 1))
                w_mat = jnp.broadcast_to(dmat[last], (L, L))
            else:
                b_last = jnp.min(jnp.where(same, b_r, jnp.inf), axis=1, keepdims=True)
                log_w = jnp.where(same, b_last - b_r + i_r, -jnp.inf)
                m_new = jnp.maximum(b_last + m_p, jnp.max(log_w, axis=1, keepdims=True))
                w_mat = jnp.exp(log_w - m_new)
                e_c = jnp.exp(b_last + m_p - m_new)
            st.update(m_t=m_t, dmat=dmat, e_int=e_int, m_new=m_new, e_c=e_c, w_mat=w_mat)

        def score_stage():
            q, k = q_ref[:, qs], k_ref[:, qs]
            st["s"] = lax.dot_general(q, k, _NT, preferred_element_type=F32) * st.pop("dmat")

        def state_stage():
            q, k = q_ref[:, qs], k_ref[:, qs]
            v_t = v_ref[:, vs].astype(F32).T
            w_mat = st["w_mat"]
            e_cb = jnp.broadcast_to(st["e_c"], (L, LANES))
            inter = jnp.zeros((L, M_DV), F32)
            for j in range(nb):
                first = slice(j * tpb, j * tpb + 1)
                c_j = c_ref[j, hl]
                q_j = q if nb == 1 else jnp.where(row_seq == j, q, jnp.zeros_like(q))
                inter = inter + lax.dot_general(q_j, c_j.astype(BF16), _NT, preferred_element_type=F32)
                lhs = (v_t * w_mat[first]).astype(BF16)
                c_ref[j, hl] = e_cb[first] * c_j + jnp.dot(lhs, k, preferred_element_type=F32)
            st["inter"] = inter

        def output_stage():
            q, k, v = q_ref[:, qs], k_ref[:, qs], v_ref[:, vs]
            s, e_int, m_t = st.pop("s"), st.pop("e_int"), st.pop("m_t")
            n_rows = n_ref[:, qs]
            qn = jnp.sum(q.astype(F32) * n_rows, axis=1, keepdims=True)
            num = jnp.dot(s.astype(BF16), v, preferred_element_type=F32) + e_int * st.pop("inter")
            den = jnp.sum(s, axis=1, keepdims=True) + e_int * qn
            hh = num / jnp.maximum(jnp.abs(den), jnp.exp(-m_t))
            mu = jnp.mean(hh, axis=1, keepdims=True)
            dlt = hh - mu
            var = jnp.mean(dlt * dlt, axis=1, keepdims=True)
            y = dlt * lax.rsqrt(var + HEAD_NORM_EPS)
            out_ref[:, vs] = (y * ng_ref[:, vs] * mo_ref[:, vs].astype(F32)).astype(BF16)
            n_ref[:, qs] = (st.pop("e_c") * n_rows
                            + jnp.dot(st.pop("w_mat").astype(BF16), k, preferred_element_type=F32))

        return (gate_stage, score_stage, state_stage, output_stage), st

    heads = [head_stages(hl) for hl in range(hps)]
    for stage in range(4):
        if between is not None:
            between()
        for stages, _ in heads:
            stages[stage]()
    for hl, (_, st) in enumerate(heads):
        m_next = jnp.where(lane == head0 + hl, st["m_new"], m_next)

    m_ref[...] = m_next


def _mlstm(q, k, v, gates, mo, norm_g, state, nb, tpb, n_chunks, hps):
    m = q.shape[0]
    L = nb * tpb
    n_blocks = m // (L * n_chunks)
    hd = M_HEADS * M_DQK

    def tok(width):
        return pl.BlockSpec((L, width), lambda b, g, c: (b * n_chunks + c, g))

    c_spec = pl.BlockSpec((nb, hps, M_DV, M_DQK), lambda b, g, c: (b, g, 0, 0))
    n_spec = pl.BlockSpec((L, hps * M_DQK), lambda b, g, c: (b, g))
    m_spec = pl.BlockSpec((L, LANES), lambda b, g, c: (b, 0))
    in_specs = [tok(hps * M_DQK), tok(hps * M_DQK), tok(hps * M_DV),
                pl.BlockSpec((L, LANES), lambda b, g, c: (b * n_chunks + c, 0)),
                tok(hps * M_DV), pl.BlockSpec((1, hps * M_DV), lambda b, g, c: (0, g)),
                c_spec, n_spec, m_spec]
    args = [q, k, v, gates, mo, norm_g.reshape(1, M_HEADS * M_DV), *state]
    return pl.pallas_call(
        functools.partial(_mlstm_kernel, nb=nb, tpb=tpb, hps=hps),
        grid=(n_blocks, M_HEADS // hps, n_chunks),
        in_specs=in_specs,
        out_specs=[tok(hps * M_DV), c_spec, n_spec, m_spec],
        out_shape=[jax.ShapeDtypeStruct((m, M_HEADS * M_DV), BF16),
                   jax.ShapeDtypeStruct((n_blocks * nb, M_HEADS, M_DV, M_DQK), F32),
                   jax.ShapeDtypeStruct((n_blocks * L, hd), F32),
                   jax.ShapeDtypeStruct((n_blocks * L, LANES), F32)],
        scratch_shapes=[pltpu.VMEM((LANES, L), F32), pltpu.VMEM((LANES, L), F32)],
        compiler_params=_params(),
        name="mlstm",
    )(*args)


def _swa_kernel(sink_ref, q_ref, kp_ref, kc_ref, vp_ref, vc_ref, o_ref, kn_ref, vn_ref, *, nbb, n_new):
    tq = q_ref.shape[1]
    qi = lax.broadcasted_iota(jnp.int32, (tq, WINDOW), 0)
    kj = lax.broadcasted_iota(jnp.int32, (tq, WINDOW), 1)
    keep_old = lax.broadcasted_iota(jnp.int32, (WINDOW, W_AKV), 0) < WINDOW - n_new

    def pad_keys(x):
        return jnp.concatenate([x, jnp.zeros((WINDOW - x.shape[0], x.shape[1]), x.dtype)], axis=0)

    blocks = []
    for jb in range(nbb):
        kc, vc = pad_keys(kc_ref[jb]), pad_keys(vc_ref[jb])
        kp, vp = kp_ref[jb], vp_ref[jb]
        kn_ref[jb] = jnp.where(keep_old, pltpu.roll(kp, WINDOW - n_new, 0), pltpu.roll(kc, WINDOW - n_new, 0))
        vn_ref[jb] = jnp.where(keep_old, pltpu.roll(vp, WINDOW - n_new, 0), pltpu.roll(vc, WINDOW - n_new, 0))
        blocks.append(_swa_stages(sink_ref, q_ref[jb], kp, vp, kc, vc, kj > qi, o_ref.at[jb]))

    for stage in range(3):
        for pairs in blocks:
            for pair in pairs:
                pair[stage]()


def _swa_stages(sink_ref, q, kp, vp, kc, vc, valid_prev, o_ref):
    tq = q.shape[0]
    reps = A_HEADS // A_KV_HEADS
    low_q = lax.broadcasted_iota(jnp.int32, (tq, LANES), 1) < A_DH
    low_k = lax.broadcasted_iota(jnp.int32, (WINDOW, LANES), 1) < A_DH
    key_row = lax.broadcasted_iota(jnp.int32, (WINDOW, LANES), 0)
    qi = lax.broadcasted_iota(jnp.int32, (tq, WINDOW), 0)
    kj = lax.broadcasted_iota(jnp.int32, (tq, WINDOW), 1)
    sink_lane = kj == 0
    valid_cur = kj <= qi
    scale = jnp.asarray(ATTN_SCALE, BF16)
    neg_inf = -jnp.inf

    def block_diag(x):
        return jnp.concatenate([jnp.where(low_k, x, 0.0), jnp.where(low_k, 0.0, x)], axis=0).astype(BF16)

    def pair_stages(p):
        ks = slice(p * LANES, (p + 1) * LANES)
        state = {}

        def scores():
            q4 = jnp.concatenate([q[:, (reps * p + r) * LANES:(reps * p + r + 1) * LANES] * scale
                                  for r in range(reps)], axis=0)
            state["sp"] = lax.dot_general(q4, block_diag(kp[:, ks]), _NT, preferred_element_type=F32)
            state["sc"] = lax.dot_general(q4, block_diag(kc[:, ks]), _NT, preferred_element_type=F32)

        def softmax():
            sp, sc = state.pop("sp"), state.pop("sc")
            pps, pcs, invs = [], [], []
            for r in range(reps):
                rows = slice(r * tq, (r + 1) * tq)
                pp_r, pc_r, inv_r = [], [], []
                for half in range(2):
                    cols = slice(half * WINDOW, (half + 1) * WINDOW)
                    head = 2 * reps * p + reps * half + r
                    sp_i = jnp.where(sink_lane, sink_ref[head], jnp.where(valid_prev, sp[rows, cols], neg_inf))
                    sc_i = jnp.where(valid_cur, sc[rows, cols], neg_inf)
                    mx = jnp.max(jnp.maximum(sp_i, sc_i), axis=1, keepdims=True)
                    pp = jnp.exp(sp_i - mx)
                    pc = jnp.exp(sc_i - mx)
                    inv_r.append(1.0 / jnp.sum(pp + pc, axis=1, keepdims=True))
                    pp_r.append(pp.astype(BF16))
                    pc_r.append(pc.astype(BF16))
                pps.append(jnp.concatenate(pp_r, axis=1))
                pcs.append(jnp.concatenate(pc_r, axis=1))
                invs.append(jnp.where(low_q, inv_r[0], inv_r[1]))
            state.update(pp=jnp.concatenate(pps, axis=0), pc=jnp.concatenate(pcs, axis=0), inv=invs)

        def values():
            v_prev = block_diag(jnp.where(key_row == 0, 0.0, vp[:, ks]))
            o4 = (jnp.dot(state.pop("pp"), v_prev, preferred_element_type=F32)
                  + jnp.dot(state.pop("pc"), block_diag(vc[:, ks]), preferred_element_type=F32))
            invs = state.pop("inv")
            for r in range(reps):
                blk = reps * p + r
                o_ref[:, blk * LANES:(blk + 1) * LANES] = (o4[r * tq:(r + 1) * tq] * invs[r]).astype(BF16)

        return scores, softmax, values

    return [pair_stages(p) for p in range(A_KV_HEADS // 2)]


def _swa_block(sink_ref, q, kp, vp, kc, vc, valid_prev, o_ref, between=None):
    for scores, softmax, values in _swa_stages(sink_ref, q, kp, vp, kc, vc, valid_prev, o_ref):
        if between is not None:
            between()
        scores()
        softmax()
        if between is not None:
            between()
        values()


def _swa(sinks, q, k_cache, k_new, v_cache, v_new, *, nbb, n_new):
    n, tq, _ = q.shape
    tk = k_new.shape[1]
    index = lambda i: (i, 0, 0)
    cache_spec = pl.BlockSpec((nbb, WINDOW, W_AKV), index)
    new_spec = pl.BlockSpec((nbb, tk, W_AKV), index)
    q_spec = pl.BlockSpec((nbb, tq, W_AQ), index)
    return pl.pallas_call(
        functools.partial(_swa_kernel, nbb=nbb, n_new=n_new),
        grid=(n // nbb,),
        in_specs=[pl.BlockSpec(memory_space=pltpu.SMEM), q_spec, cache_spec, new_spec, cache_spec, new_spec],
        out_specs=[q_spec, cache_spec, cache_spec],
        out_shape=[jax.ShapeDtypeStruct((n, tq, W_AQ), BF16)]
        + [jax.ShapeDtypeStruct((n, WINDOW, W_AKV), F32)] * 2,
        compiler_params=_params(),
        name="swa",
    )(sinks, q, k_cache, k_new, v_cache, v_new)


def _merge_kernel(x_ref, g_ref, hm_ref, oa_ref, gm_ref, ga_ref, wm_ref, wa_ref, wo_ref,
                  lg_ref, lb_ref, o_ref):
    ym = jnp.dot(hm_ref[...], wm_ref[...], preferred_element_type=F32)
    ya = jnp.dot(oa_ref[...], wa_ref[...], preferred_element_type=F32)
    mix = gm_ref[...].astype(F32) * ym + ga_ref[...].astype(F32) * ya
    t = jnp.dot(mix.astype(BF16), wo_ref[...], preferred_element_type=F32)
    y = DEEPNORM_ALPHA * x_ref[...] + (1.0 + _rows(g_ref)) * t
    o_ref[...] = _layer_norm(y, lg_ref[...], lb_ref[...], LN_EPS)


def _merge(x, mod, per_row, rows_per_batch, chunk, hm, oa, gm, ga, wm, wa, wo, ln_g, ln_b):
    m = x.shape[0]
    tm = ROW_TILE

    def tok():
        return pl.BlockSpec((tm, D_MODEL), lambda i: (i, 0))

    return pl.pallas_call(
        _merge_kernel,
        grid=(m // tm,),
        in_specs=[tok()] + _mod_specs(per_row, tm, rows_per_batch // tm, (chunk,))
        + [tok(), tok(), tok(), tok()]
        + [_resident((D_MODEL, D_MODEL))] * 3 + [_resident((1, D_MODEL))] * 2,
        out_specs=tok(),
        out_shape=jax.ShapeDtypeStruct((m, D_MODEL), F32),
        compiler_params=_params(),
        name="merge",
    )(x, mod, hm, oa, gm, ga, wm, wa, wo, ln_g.reshape(1, D_MODEL), ln_b.reshape(1, D_MODEL))


def _mixer_kernel(sink_ref, xa_ref, xb_ref, sh_ref, sc_ref, g_ref, wa_ref, wq_ref, wr_ref, wif_ref, bif_ref,
                  cos_ref, sin_ref, ng_ref, wm_ref, wba_ref, wo_ref, lg_ref, lb_ref,
                  y_ref, c_ref, n_ref, m_ref, kb_ref, vb_ref, *scratch, tiles_per_seq):
    n_staged = (len(scratch) - 6) // 2
    even, odd = scratch[:n_staged], scratch[n_staged:2 * n_staged]
    kprev_s, vprev_s, hm_s, oa_s, gtt_s, ctt_s = scratch[2 * n_staged:]
    tm = xa_ref.shape[0]
    s = pl.program_id(0)
    tile_b = jnp.maximum(s - 1, 0)
    seq_start = tile_b % tiles_per_seq == 0

    @pl.when(s == 0)
    def _():
        for ref in odd:
            ref[...] = jnp.zeros_like(ref)

    @pl.when(seq_start)
    def _():
        c_ref[...] = jnp.zeros_like(c_ref)
        n_ref[...] = jnp.zeros_like(n_ref)
        m_ref[...] = jnp.zeros_like(m_ref)
        kprev_s[...] = jnp.zeros_like(kprev_s)
        vprev_s[...] = jnp.zeros_like(vprev_s)

    def step(produce, consume):
        proj_parts = _proj_parts(xa_ref, sh_ref, sc_ref, wa_ref, wq_ref, wr_ref, wif_ref, bif_ref,
                                 cos_ref, sin_ref, *produce)
        pending = iter(proj_parts)

        counts = iter(MIXER_PROJ_SCHEDULE)

        def between():
            for _ in range(next(counts)):
                part = next(pending, None)
                if part is not None:
                    part()

        q_s, k_s, v_s, mo_s, aq_s, ak_s, av_s, gm_s, ga_s, gt_s = consume
        qi = lax.broadcasted_iota(jnp.int32, (WINDOW, WINDOW), 0)
        kj = lax.broadcasted_iota(jnp.int32, (WINDOW, WINDOW), 1)
        for j in range(tm // MLSTM_CHUNK):
            rows = pl.ds(j * MLSTM_CHUNK, MLSTM_CHUNK)
            _mlstm_chunk(q_s.at[rows], k_s.at[rows], v_s.at[rows], gt_s.at[rows], mo_s.at[rows], ng_ref,
                         hm_s.at[rows], c_ref, n_ref, m_ref, gtt_s, ctt_s,
                         nb=1, tpb=MLSTM_CHUNK, hps=M_HEADS, head0=0, between=between)
            valid_prev = kj > qi
            if j == 0:
                valid_prev = valid_prev & jnp.logical_not(seq_start)
                k_prev, v_prev = kprev_s[...], vprev_s[...]
            else:
                before = pl.ds((j - 1) * WINDOW, WINDOW)
                k_prev, v_prev = ak_s[before], av_s[before]
            _swa_block(sink_ref, aq_s[rows], k_prev, v_prev, ak_s[rows], av_s[rows], valid_prev,
                       oa_s.at[rows], between=between)

        for part in pending:
            part()
        _merge_kernel(xb_ref, g_ref, hm_s, oa_s, gm_s, ga_s, wm_ref, wba_ref, wo_ref, lg_ref, lb_ref, y_ref)

        last = pl.ds(tm - WINDOW, WINDOW)
        k_last, v_last = ak_s[last], av_s[last]
        kb_ref[0] = k_last.T
        vb_ref[0] = v_last.T
        kprev_s[...] = k_last
        vprev_s[...] = v_last

    @pl.when(s % 2 == 0)
    def _():
        step(even, odd)

    @pl.when(s % 2 == 1)
    def _():
        step(odd, even)


def _mixer(x1, mod, w, sinks, norm_g, cos_t, sin_t, bp, sp):
    tm = MIXER_TILE
    tps = sp // tm
    n_tiles = bp * tps
    hd = M_HEADS * M_DQK
    weights = (w["w_a"], w["w_aq"], w["w_r"], w["w_if"], w["bif"])

    def tile_a(s):
        return jnp.minimum(s, n_tiles - 1)

    def tile_b(s):
        return jnp.maximum(s - 1, 0)

    def mod_spec(tile, chunk):
        return pl.BlockSpec((1, 1, D_MODEL), lambda s: (tile(s) // tps, 0, chunk))

    def per_seq(*shape):
        return pl.BlockSpec((1,) + shape, lambda s: (tile_b(s) // tps,) + (0,) * len(shape))

    pos = pl.BlockSpec((tm, LANES), lambda s: (tile_a(s) % tps, 0))
    staged = [(W_MQ, BF16), (W_MQ, BF16), (W_MV, BF16), (W_MV, BF16), (W_AQ, BF16),
              (W_AKV, F32), (W_AKV, F32), (D_MODEL, BF16), (D_MODEL, BF16), (LANES, F32)]
    return pl.pallas_call(
        functools.partial(_mixer_kernel, tiles_per_seq=tps),
        grid=(n_tiles + 1,),
        in_specs=[pl.BlockSpec(memory_space=pltpu.SMEM),
                  pl.BlockSpec((tm, D_MODEL), lambda s: (tile_a(s), 0)),
                  pl.BlockSpec((tm, D_MODEL), lambda s: (tile_b(s), 0)),
                  mod_spec(tile_a, 3), mod_spec(tile_a, 4), mod_spec(tile_b, 5)]
        + [_resident(a.shape) for a in weights] + [pos, pos, _resident((1, M_HEADS * M_DV))]
        + [_resident((D_MODEL, D_MODEL))] * 3 + [_resident((1, D_MODEL))] * 2,
        out_specs=[pl.BlockSpec((tm, D_MODEL), lambda s: (tile_b(s), 0)),
                   per_seq(M_HEADS, M_DV, M_DQK),
                   pl.BlockSpec((MLSTM_CHUNK, hd), lambda s: (tile_b(s) // tps, 0)),
                   pl.BlockSpec((MLSTM_CHUNK, LANES), lambda s: (tile_b(s) // tps, 0)),
                   per_seq(W_AKV, WINDOW), per_seq(W_AKV, WINDOW)],
        out_shape=[jax.ShapeDtypeStruct((bp * sp, D_MODEL), F32),
                   jax.ShapeDtypeStruct((bp, M_HEADS, M_DV, M_DQK), F32),
                   jax.ShapeDtypeStruct((bp * MLSTM_CHUNK, hd), F32),
                   jax.ShapeDtypeStruct((bp * MLSTM_CHUNK, LANES), F32),
                   jax.ShapeDtypeStruct((bp, W_AKV, WINDOW), F32),
                   jax.ShapeDtypeStruct((bp, W_AKV, WINDOW), F32)],
        scratch_shapes=[pltpu.VMEM((tm, width), dtype) for width, dtype in staged + staged]
        + [pltpu.VMEM((WINDOW, W_AKV), F32), pltpu.VMEM((WINDOW, W_AKV), F32),
           pltpu.VMEM((tm, W_MV), BF16), pltpu.VMEM((tm, W_AQ), BF16),
           pltpu.VMEM((LANES, MLSTM_CHUNK), F32), pltpu.VMEM((LANES, MLSTM_CHUNK), F32)],
        compiler_params=_params(),
        name="mixer",
    )(sinks, x1, x1, mod, mod, mod, *weights, cos_t, sin_t, norm_g.reshape(1, M_HEADS * M_DV),
      w["wm"], w["wa"], w["wo"], w["ln2_g"].reshape(1, D_MODEL), w["ln2_b"].reshape(1, D_MODEL))


def _rope_tables(pos):
    half = A_DH // 2
    inv = np.float32(ROPE_THETA) ** (-np.arange(half, dtype=np.float32) / np.float32(half))
    ang = pos.astype(np.float32)[:, None] * inv[None, :]
    cos, sin = np.cos(ang), np.sin(ang)
    return jnp.asarray(np.tile(cos, (1, 4))), jnp.asarray(np.concatenate([-sin, sin, -sin, sin], axis=1))


def _from_stored(kv_t):
    return jnp.transpose(kv_t.reshape(kv_t.shape[0], A_KV_HEADS, A_DH, WINDOW), (0, 3, 1, 2))


def _token_stage_1(x, mod, per_row, rows_per_batch, w, pos):
    x1 = _ffn(x, mod, per_row, rows_per_batch, (0, 1, 2), w["up1"], w["down1"], w["ln1_g"], w["ln1_b"])
    cos_t, sin_t = _rope_tables(pos)
    return x1, _proj(x1, mod, per_row, rows_per_batch, (3, 4), w, cos_t, sin_t)


def _token_stage_2(x1, mod, per_row, rows_per_batch, w, hm, oa, gm, ga):
    x2 = _merge(x1, mod, per_row, rows_per_batch, 5, hm, oa, gm, ga,
                w["wm"], w["wa"], w["wo"], w["ln2_g"], w["ln2_b"])
    return _ffn(x2, mod, per_row, rows_per_batch, (6, 7, 8), w["up2"], w["down2"], w["ln3_g"], w["ln3_b"])


def kernel(x_prompt, x_sample, state_mlstm_C, state_mlstm_n, state_mlstm_m, cache_swa_k, cache_swa_v, c_prompt, c_sample, w_ada, b_ada, w_ffn1_up, w_ffn1_down, ln1_g, ln1_b, w_in, b_igate, b_fgate, m_norm_g, sinks, w_branch_m, w_branch_a, w_out, ln2_g, ln2_b, w_ffn2_up, w_ffn2_down, ln3_g, ln3_b):
    assert w_ada.shape[0] == DEPTH == 1
    bp, sp, _ = x_prompt.shape
    bs, ts, _ = x_sample.shape

    w_in_t = jnp.transpose(w_in[0])
    w = dict(
        up1=w_ffn1_up[0].astype(BF16), down1=w_ffn1_down[0].astype(BF16),
        bif=jnp.concatenate([b_igate[0], b_fgate[0], jnp.zeros((LANES - 2 * M_HEADS,), F32)]).reshape(1, LANES),
        ln1_g=ln1_g[0], ln1_b=ln1_b[0], ln2_g=ln2_g[0], ln2_b=ln2_b[0], ln3_g=ln3_g[0], ln3_b=ln3_b[0],
    )

    ms = bs * ts
    c_all = jnp.concatenate([jnp.repeat(c_sample, ts, axis=0), c_prompt], axis=0)
    mod = _ada(c_all, w_ada[0], b_ada[0])
    mod_p = mod[ms:].reshape(bp, 1, ADA_CHUNKS * D_MODEL)

    mp = bp * sp
    reps = A_HEADS // A_KV_HEADS
    pair_low = lambda j: (j // reps) * (2 * reps) + j % reps
    pair_rows = lambda j: (j // (2 * reps)) * (2 * reps) + (j % 2) * reps + (j % (2 * reps)) // 2
    jobs = (
        _cast_job(w_ffn2_up[0], (D_MODEL, FF_CHUNK), 1),
        _cast_job(w_ffn2_down[0], (FF_CHUNK, D_MODEL), 0, first_step=2 * D_FF // FF_CHUNK),
        _cast_job(w_in_t, (D_MODEL, LANES), 1, transposed=True, n_blocks=IN_IF // LANES),
        _cast_job(w_in_t, (D_MODEL, LANES), 1, transposed=True, n_blocks=W_AQ // LANES, shift=IN_AQ - IN_IF,
                  sources=(lambda j: IN_IF // A_DH + pair_low(j), lambda j: IN_IF // A_DH + pair_low(j) + reps)),
        _cast_job(w_in_t, (D_MODEL, LANES), 1, transposed=True, n_blocks=(IN_END - IN_AK) // LANES,
                  shift=IN_AQ - IN_IF, sources=(lambda j: (IN_AK - (IN_AQ - IN_IF)) // LANES + j,)),
        _cast_job(w_branch_m[0], (D_MODEL, LANES), 1),
        _cast_job(w_out[0], (D_MODEL, LANES), 1),
        _cast_job(w_branch_a[0], (A_DH, D_MODEL), 0, sources=(pair_rows,)),
        _cast_job(w_in_t, (LANES, D_MODEL), 0, n_blocks=1, sources=(lambda j: IN_IF // LANES + j,)),
    )
    x1p, w["up2"], w["down2"], w["w_a"], w["w_aq"], w["w_r"], w["wm"], w["wo"], w["wa"], w["w_if"] = _ffn(
        x_prompt.reshape(mp, D_MODEL), mod_p, False, sp, (0, 1, 2),
        w["up1"], w["down1"], w["ln1_g"], w["ln1_b"], cast_jobs=jobs)
    cos_t, sin_t = _rope_tables(np.arange(sp))
    x2p, c_p, n_rows, m_rows, kb_p, vb_p = _mixer(x1p, mod_p, w, sinks[0], m_norm_g[0], cos_t, sin_t, bp, sp)
    y_p = _ffn(x2p, mod_p, False, sp, (6, 7, 8), w["up2"], w["down2"], w["ln3_g"], w["ln3_b"])
    n_p = n_rows.reshape(bp, MLSTM_CHUNK, M_HEADS, M_DQK)[:, 0]
    m_p = m_rows.reshape(bp, MLSTM_CHUNK, LANES)[:, 0, :M_HEADS]
    kb_p, vb_p = _from_stored(kb_p), _from_stored(vb_p)

    x1s, (qm, km, vm, mo, aq, ak, av, gm, ga, gt) = _token_stage_1(
        x_sample.reshape(ms, D_MODEL), mod, True, ms, w, PAST_LEN + np.arange(ms) % ts)
    seqs = MLSTM_CHUNK // ts
    n0_rows = jnp.repeat(state_mlstm_n[0].reshape(bs, M_HEADS * M_DQK), ts, axis=0)
    m0_rows = jnp.repeat(jnp.pad(state_mlstm_m[0], ((0, 0), (0, LANES - M_HEADS))), ts, axis=0)
    hm, c_s, n_rows, m_rows = _mlstm(qm, km, vm, gt, mo, m_norm_g[0],
                                     (state_mlstm_C[0], n0_rows, m0_rows), seqs, ts, 1, 2)
    n_s = n_rows.reshape(bs, ts, M_HEADS, M_DQK)[:, 0]
    m_s = m_rows.reshape(bs, ts, LANES)[:, 0, :M_HEADS]
    pad_t = ((0, 0), (0, SAMPLE_TQ - ts), (0, 0))
    oa, kb_s, vb_s = _swa(
        sinks[0], jnp.pad(aq.reshape(bs, ts, W_AQ), pad_t),
        cache_swa_k[0].reshape(bs, WINDOW, W_AKV), jnp.pad(ak.reshape(bs, ts, W_AKV), pad_t),
        cache_swa_v[0].reshape(bs, WINDOW, W_AKV), jnp.pad(av.reshape(bs, ts, W_AKV), pad_t),
        nbb=SWA_SAMPLE_SEQS, n_new=ts)
    y_s = _token_stage_2(x1s, mod, True, ms, w, hm, oa[:, :ts].reshape(ms, D_MODEL), gm, ga)
    kb_s = kb_s.reshape(bs, WINDOW, A_KV_HEADS, A_DH)
    vb_s = vb_s.reshape(bs, WINDOW, A_KV_HEADS, A_DH)

    return (y_p.reshape(bp, sp, D_MODEL), y_s.reshape(bs, ts, D_MODEL),
            c_p[None], n_p[None], m_p[None], kb_p[None], vb_p[None],
            c_s[None], n_s[None], m_s[None], kb_s[None], vb_s[None])
```

```python
import functools

import jax
import jax.numpy as jnp
import numpy as np
from jax import lax
from jax.experimental import pallas as pl
from jax.experimental.pallas import tpu as pltpu

F32 = jnp.float32
BF16 = jnp.bfloat16

D_MODEL = 1024
D_FF = 2816
DEPTH = 1
M_HEADS = 4
M_DQK = 128
M_DV = 256
A_HEADS = 16
A_KV_HEADS = 4
A_DH = 64
WINDOW = 128
PAST_LEN = 8192
ROPE_THETA = 10000.0
ATTN_SCALE = A_DH ** -0.5
LN_EPS = 1e-5
HEAD_NORM_EPS = 1e-6
ADA_CHUNKS = 9
DEEPNORM_ALPHA = (2.0 * DEPTH) ** 0.25
K_SCALE = M_DQK ** -0.5

LANES = 128
BF16_SUBLANES = 16
VMEM_LIMIT_BYTES = 56 * 1024 * 1024

W_MQ = M_HEADS * M_DQK
W_MV = M_HEADS * M_DV
W_AQ = A_HEADS * A_DH
W_AKV = A_KV_HEADS * A_DH
IN_IF = 2 * W_MQ + 2 * W_MV
IN_AQ = IN_IF + 2 * M_HEADS
IN_AK = IN_AQ + W_AQ
IN_END = IN_AK + 2 * W_AKV + 2 * D_MODEL
A_Q, A_K, A_V, A_O = 0, W_MQ, 2 * W_MQ, 2 * W_MQ + W_MV
R_AK, R_AV, R_GM, R_GA = 0, W_AKV, 2 * W_AKV, 2 * W_AKV + D_MODEL

ROW_TILE = 512
MIXER_TILE = 256
MIXER_PROJ_SCHEDULE = (1, 1, 1, 1, 2, 2, 2, 2) * 2
ADA_TILE = 1536
FF_CHUNK = 256
PROJ_CHUNK = 2 * LANES
FFN_EPILOGUE_PIECES = 4
MLSTM_CHUNK = 128
SAMPLE_TQ = BF16_SUBLANES
SWA_SAMPLE_SEQS = 16

_NT = (((1,), (1,)), ((), ()))


def _params():
    return pltpu.CompilerParams(vmem_limit_bytes=VMEM_LIMIT_BYTES)


def _resident(shape):
    return pl.BlockSpec(shape, lambda *_: (0,) * len(shape), pipeline_mode=pl.Buffered(1))


def _rows(ref):
    v = ref[...]
    return v.reshape(v.shape[-2], v.shape[-1])


def _layer_norm(y, g, b, eps):
    mu = jnp.mean(y, axis=-1, keepdims=True)
    d = y - mu
    var = jnp.mean(d * d, axis=-1, keepdims=True)
    return d * lax.rsqrt(var + eps) * g + b


def _sigmoid(x):
    return 1.0 / (1.0 + jnp.exp(-x))


def _mod_specs(per_row, tm, tiles_per_batch, chunks):
    if per_row:
        return [pl.BlockSpec((tm, D_MODEL), lambda i, c=c: (i, c)) for c in chunks]
    return [pl.BlockSpec((1, 1, D_MODEL), lambda i, c=c: (i // tiles_per_batch, 0, c)) for c in chunks]


def _ada_kernel(c_ref, w_ref, b_ref, o_ref):
    c = c_ref[...]
    s = (c * _sigmoid(c)).astype(BF16)
    o_ref[...] = jnp.dot(s, w_ref[...].astype(BF16), preferred_element_type=F32) + b_ref[...]


def _ada(c_all, w_ada, b_ada):
    rows = c_all.shape[0]
    n_out = w_ada.shape[1]
    tn = ADA_TILE
    return pl.pallas_call(
        _ada_kernel,
        grid=(n_out // tn,),
        in_specs=[pl.BlockSpec((rows, D_MODEL), lambda j: (0, 0)),
                  pl.BlockSpec((D_MODEL, tn), lambda j: (0, j)),
                  pl.BlockSpec((1, tn), lambda j: (0, j))],
        out_specs=pl.BlockSpec((rows, tn), lambda j: (0, j)),
        out_shape=jax.ShapeDtypeStruct((rows, n_out), F32),
        compiler_params=_params(),
        name="ada",
    )(c_all, w_ada, b_ada.reshape(1, n_out))


def _ffn_kernel(x_ref, xp_ref, sh_ref, sc_ref, g_ref, wup_ref, wdn_ref, lg_ref, lb_ref, *rest,
                n_tiles, cast_layout):
    n_cast = len(cast_layout)
    n_src = sum(layout[0] for layout in cast_layout)
    cast_in, o_ref, cast_out = rest[:n_src], rest[n_src], rest[n_src + 1:n_src + 1 + n_cast]
    act_ref, f_ref = rest[n_src + 1 + n_cast:]

    def cast_chunk(job):
        if job >= n_cast:
            return
        first = sum(layout[0] for layout in cast_layout[:job])
        n, transposed, shift = cast_layout[job]
        refs = cast_in[first:first + n]
        if shift:
            rows = refs[0].shape[0]
            parts = [jnp.concatenate([lo[...], hi[...]], axis=0)[shift:shift + rows]
                     for lo, hi in zip(refs[0::2], refs[1::2])]
        else:
            parts = [ref[...] for ref in refs]
        block = jnp.concatenate(parts, axis=0)
        cast_out[job][...] = (block.T if transposed else block).astype(BF16)

    tm = x_ref.shape[0]
    s = pl.program_id(0)
    piece_rows = tm // FFN_EPILOGUE_PIECES

    def epilogue(piece):
        rows = pl.ds(piece * piece_rows, piece_rows)
        g = _rows(g_ref)
        if g.shape[0] != 1:
            g = g[piece * piece_rows:(piece + 1) * piece_rows]
        y = DEEPNORM_ALPHA * xp_ref[rows, :] + (0.5 * (1.0 + g)) * f_ref[rows, :]
        o_ref[rows, :] = _layer_norm(y, lg_ref[...], lb_ref[...], LN_EPS)

    def matmuls(previous_epilogue):
        h = (x_ref[...] * (1.0 + _rows(sc_ref)) + _rows(sh_ref)).astype(BF16)
        chunks = list(range(0, D_FF, FF_CHUNK))
        every = len(chunks) // FFN_EPILOGUE_PIECES
        piece = 0
        for i, c in enumerate(chunks):
            a = jnp.dot(h, wup_ref[:, c:c + FF_CHUNK], preferred_element_type=F32)
            u = jnp.dot(h, wup_ref[:, D_FF + c:D_FF + c + FF_CHUNK], preferred_element_type=F32)
            act_ref[:, c:c + FF_CHUNK] = (a * _sigmoid(a) * u).astype(BF16)
            cast_chunk(i)
            if previous_epilogue and i % every == every - 1 and piece < FFN_EPILOGUE_PIECES:
                epilogue(piece)
                piece += 1
        f_ref[...] = jnp.dot(act_ref[...], wdn_ref[...], preferred_element_type=F32)

    if n_tiles == 1:
        matmuls(False)
        for piece in range(FFN_EPILOGUE_PIECES):
            epilogue(piece)
        return

    @pl.when(s == 0)
    def _():
        f_ref[...] = jnp.zeros_like(f_ref)

    @pl.when(s < n_tiles)
    def _():
        matmuls(True)

    @pl.when(s == n_tiles)
    def _():
        for job in range(n_cast):
            cast_chunk(job)
        for piece in range(FFN_EPILOGUE_PIECES):
            epilogue(piece)


def _cast_job(array, block, axis, first_step=0, sources=None, transposed=False, n_blocks=None, shift=0):
    sources = sources or (lambda j: j,)
    src_axis = 1 - axis if transposed else axis
    src_block = block[::-1] if transposed else block
    src_block = (src_block[0] // len(sources), src_block[1])
    if n_blocks is None:
        n_blocks = array.shape[src_axis] // block[axis]

    def step_block(s):
        return jnp.clip(s - first_step, 0, n_blocks - 1)

    def at(axis_, j):
        return (j, 0) if axis_ == 0 else (0, j)

    in_specs = [pl.BlockSpec(src_block, lambda s, f=f, d=d: at(src_axis, f(step_block(s)) + d))
                for f in sources for d in range(2 if shift else 1)]
    out_dims = list(array.shape[::-1] if transposed else array.shape)
    out_dims[axis] = n_blocks * block[axis]
    return dict(array=array, in_specs=in_specs, layout=(len(in_specs), transposed, shift),
                out_spec=pl.BlockSpec(block, lambda s: at(axis, step_block(s))),
                out_shape=jax.ShapeDtypeStruct(tuple(out_dims), BF16), steps=first_step + n_blocks)


def _ffn(x, mod, per_row, rows_per_batch, chunks, w_up, w_down, ln_g, ln_b, cast_jobs=()):
    m = x.shape[0]
    tm = ROW_TILE
    n_tiles = m // tm
    tpb = rows_per_batch // tm

    def cur(s):
        return jnp.minimum(s, n_tiles - 1)

    def prev(s):
        return jnp.maximum(s - 1, 0)

    def mod_spec(tile, chunk):
        if per_row:
            return pl.BlockSpec((tm, D_MODEL), lambda s: (tile(s), chunk))
        return pl.BlockSpec((1, 1, D_MODEL), lambda s: (tile(s) // tpb, 0, chunk))

    in_specs = [pl.BlockSpec((tm, D_MODEL), lambda s: (cur(s), 0)),
                pl.BlockSpec((tm, D_MODEL), lambda s: (prev(s), 0)),
                mod_spec(cur, chunks[0]), mod_spec(cur, chunks[1]), mod_spec(prev, chunks[2]),
                _resident((D_MODEL, 2 * D_FF)), _resident((D_FF, D_MODEL)),
                _resident((1, D_MODEL)), _resident((1, D_MODEL))]
    args = [x, x, mod, mod, mod, w_up, w_down, ln_g.reshape(1, D_MODEL), ln_b.reshape(1, D_MODEL)]
    out_specs = [pl.BlockSpec((tm, D_MODEL), lambda s: (prev(s), 0))]
    out_shape = [jax.ShapeDtypeStruct((m, D_MODEL), F32)]
    n_steps = 1 if n_tiles == 1 else n_tiles + 1
    for job in cast_jobs:
        assert job["steps"] <= n_steps
        in_specs += job["in_specs"]
        args += [job["array"]] * len(job["in_specs"])
        out_specs.append(job["out_spec"])
        out_shape.append(job["out_shape"])
    outs = pl.pallas_call(
        functools.partial(_ffn_kernel, n_tiles=n_tiles, cast_layout=tuple(j["layout"] for j in cast_jobs)),
        grid=(n_steps,),
        in_specs=in_specs,
        out_specs=out_specs,
        out_shape=out_shape,
        scratch_shapes=[pltpu.VMEM((tm, D_FF), BF16), pltpu.VMEM((tm, D_MODEL), F32)],
        compiler_params=_params(),
        name="ffn",
    )(*args)
    return outs if cast_jobs else outs[0]


def _proj_parts(x_ref, sh_ref, sc_ref, wa_ref, wq_ref, wr_ref, wif_ref, bif_ref, cos_ref, sin_ref,
                q_ref, k_ref, v_ref, o_ref, aq_ref, ak_ref, av_ref, gm_ref, ga_ref, gt_ref):
    x = x_ref[...]
    tm = x.shape[0]
    h = (x * (1.0 + _rows(sc_ref)) + _rows(sh_ref)).astype(BF16)
    lane = lax.broadcasted_iota(jnp.int32, (tm, LANES), 1)
    pc = PROJ_CHUNK

    def seg(w_ref, lo):
        return jnp.dot(h, w_ref[:, lo:lo + pc], preferred_element_type=F32)

    def plain(dst_ref, w_ref, lo, c, scale=None):
        def run():
            z = seg(w_ref, lo + c)
            dst_ref[:, c:c + pc] = (z if scale is None else z * scale).astype(dst_ref.dtype)
        return run

    def gate(dst_ref, w_ref, lo, c):
        def run():
            dst_ref[:, c:c + pc] = _sigmoid(seg(w_ref, lo + c)).astype(BF16)
        return run

    def forget_input_gates():
        zg = lax.dot_general(h, wif_ref[...], _NT, preferred_element_type=F32) + bif_ref[...]
        logsig = jnp.minimum(zg, 0.0) - jnp.log(1.0 + jnp.exp(-jnp.abs(zg)))
        gt_ref[...] = jnp.where(lane < M_HEADS, zg, logsig)

    def rotary(dst_ref, w_ref, lo, c):
        def run():
            cos = cos_ref[...]
            sin = sin_ref[...]
            low_half = (lane & (A_DH // 2)) == 0
            z = seg(w_ref, lo + c)
            for half in range(2):
                zh = z[:, half * LANES:(half + 1) * LANES]
                partner = jnp.where(low_half, pltpu.roll(zh, LANES - A_DH // 2, 1),
                                    pltpu.roll(zh, A_DH // 2, 1))
                dst_ref[:, c + half * LANES:c + (half + 1) * LANES] = (
                    zh * cos + partner * sin).astype(dst_ref.dtype)
        return run

    parts = [plain(q_ref, wa_ref, A_Q, c) for c in range(0, W_MQ, pc)]
    parts += [plain(k_ref, wa_ref, A_K, c, K_SCALE) for c in range(0, W_MQ, pc)]
    parts += [plain(v_ref, wa_ref, A_V, c) for c in range(0, W_MV, pc)]
    parts += [forget_input_gates]
    parts += [rotary(aq_ref, wq_ref, 0, c) for c in range(0, W_AQ, pc)]
    parts += [rotary(ak_ref, wr_ref, R_AK, 0), plain(av_ref, wr_ref, R_AV, 0)]
    parts += [gate(o_ref, wa_ref, A_O, c) for c in range(0, W_MV, pc)]
    parts += [gate(gm_ref, wr_ref, R_GM, c) for c in range(0, D_MODEL, pc)]
    parts += [gate(ga_ref, wr_ref, R_GA, c) for c in range(0, D_MODEL, pc)]
    return parts


def _proj_kernel(*refs):
    for part in _proj_parts(*refs):
        part()


def _proj(x, mod, per_row, rows_per_batch, chunks, w, cos_t, sin_t):
    m = x.shape[0]
    tm = ROW_TILE
    n_pos_tiles = cos_t.shape[0] // tm

    def tok(width):
        return pl.BlockSpec((tm, width), lambda i: (i, 0))

    widths = (W_MQ, W_MQ, W_MV, W_MV, W_AQ, W_AKV, W_AKV, D_MODEL, D_MODEL, LANES)
    dtypes = (BF16, BF16, BF16, BF16, BF16, F32, F32, BF16, BF16, F32)
    weights = (w["w_a"], w["w_aq"], w["w_r"], w["w_if"], w["bif"])
    return pl.pallas_call(
        _proj_kernel,
        grid=(m // tm,),
        in_specs=[tok(D_MODEL)] + _mod_specs(per_row, tm, rows_per_batch // tm, chunks)
        + [_resident(a.shape) for a in weights]
        + [pl.BlockSpec((tm, LANES), lambda i: (i % n_pos_tiles, 0)),
           pl.BlockSpec((tm, LANES), lambda i: (i % n_pos_tiles, 0))],
        out_specs=[tok(wd) for wd in widths],
        out_shape=[jax.ShapeDtypeStruct((m, wd), d) for wd, d in zip(widths, dtypes)],
        compiler_params=_params(),
        name="proj",
    )(x, mod, mod, *weights, cos_t, sin_t)


def _mlstm_kernel(q_ref, k_ref, v_ref, g_ref, mo_ref, ng_ref, c0_ref, n0_ref, m0_ref,
                  out_ref, c_ref, n_ref, m_ref, gt_s, ct_s, *, nb, tpb, hps):
    first_chunk = pl.program_id(2) == 0

    @pl.when(first_chunk)
    def _():
        c_ref[...] = c0_ref[...]
        n_ref[...] = n0_ref[...]

    @pl.when(first_chunk & (pl.program_id(1) == 0))
    def _():
        m_ref[...] = m0_ref[...]

    head0 = 0 if hps == M_HEADS else pl.program_id(1) * hps
    _mlstm_chunk(q_ref, k_ref, v_ref, g_ref, mo_ref, ng_ref, out_ref, c_ref, n_ref, m_ref, gt_s, ct_s,
                 nb=nb, tpb=tpb, hps=hps, head0=head0)


def _mlstm_chunk(q_ref, k_ref, v_ref, g_ref, mo_ref, ng_ref, out_ref, c_ref, n_ref, m_ref, gt_s, ct_s,
                 *, nb, tpb, hps, head0, between=None):
    L = nb * tpb
    shift = tpb.bit_length() - 1
    row = lax.broadcasted_iota(jnp.int32, (L, L), 0)
    col = lax.broadcasted_iota(jnp.int32, (L, L), 1)
    same = (row >> shift) == (col >> shift)
    causal = same & (col <= row)
    lane = lax.broadcasted_iota(jnp.int32, (L, LANES), 1)
    row_seq = lax.broadcasted_iota(jnp.int32, (L, 1), 0) >> shift

    def lane_col(x, idx):
        return jnp.sum(jnp.where(lane == idx, x, 0.0), axis=1, keepdims=True)

    gates = g_ref[...]
    tri = jnp.where(causal, 1.0, 0.0).astype(BF16)
    g_hi = gates.astype(BF16)
    rem = gates - g_hi.astype(F32)
    g_mid = rem.astype(BF16)
    g_lo = (rem - g_mid.astype(F32)).astype(BF16)
    cum = (jnp.dot(tri, g_hi, preferred_element_type=F32)
           + jnp.dot(tri, g_mid, preferred_element_type=F32)
           + jnp.dot(tri, g_lo, preferred_element_type=F32))
    gt_s[...] = gates.T
    ct_s[...] = cum.T
    m_rows = m_ref[...]
    m_next = m_rows

    def head_stages(hl):
        head = head0 + hl
        qs = slice(hl * M_DQK, (hl + 1) * M_DQK)
        vs = slice(hl * M_DV, (hl + 1) * M_DV)
        st = {}

        def gate_stage():
            b_c = lane_col(cum, M_HEADS + head)
            m_p = lane_col(m_rows, head)
            i_r = gt_s[pl.ds(head, 1), :]
            b_r = ct_s[pl.ds(M_HEADS + head, 1), :]
            log_d = jnp.where(causal, b_c - b_r + i_r, -jnp.inf)
            m_t = jnp.maximum(b_c + m_p, jnp.max(log_d, axis=1, keepdims=True))
            dmat = jnp.exp(log_d - m_t)
            e_int = jnp.exp(b_c + m_p - m_t)
            if nb == 1:
                last = slice(L - 1, L)
                m_new = jnp.broadcast_to(m_t[last], (L, 1))
                e_c = jnp.broadcast_to(e_int[last], (L, 1))
                w_mat = jnp.broadcast_to(dmat[last], (L, L))
            else:
                b_last = jnp.min(jnp.where(same, b_r, jnp.inf), axis=1, keepdims=True)
                log_w = jnp.where(same, b_last - b_r + i_r, -jnp.inf)
                m_new = jnp.maximum(b_last + m_p, jnp.max(log_w, axis=1, keepdims=True))
                w_mat = jnp.exp(log_w - m_new)
                e_c = jnp.exp(b_last + m_p - m_new)
            st.update(m_t=m_t, dmat=dmat, e_int=e_int, m_new=m_new, e_c=e_c, w_mat=w_mat)

        def score_stage():
            q, k = q_ref[:, qs], k_ref[:, qs]
            st["s"] = lax.dot_general(q, k, _NT, preferred_element_type=F32) * st.pop("dmat")

        def state_stage():
            q, k = q_ref[:, qs], k_ref[:, qs]
            v_t = v_ref[:, vs].astype(F32).T
            w_mat = st["w_mat"]
            e_cb = jnp.broadcast_to(st["e_c"], (L, LANES))
            inter = jnp.zeros((L, M_DV), F32)
            for j in range(nb):
                first = slice(j * tpb, j * tpb + 1)
                c_j = c_ref[j, hl]
                q_j = q if nb == 1 else jnp.where(row_seq == j, q, jnp.zeros_like(q))
                inter = inter + lax.dot_general(q_j, c_j.astype(BF16), _NT, preferred_element_type=F32)
                lhs = (v_t * w_mat[first]).astype(BF16)
                c_ref[j, hl] = e_cb[first] * c_j + jnp.dot(lhs, k, preferred_element_type=F32)
            st["inter"] = inter

        def output_stage():
            q, k, v = q_ref[:, qs], k_ref[:, qs], v_ref[:, vs]
            s, e_int, m_t = st.pop("s"), st.pop("e_int"), st.pop("m_t")
            n_rows = n_ref[:, qs]
            qn = jnp.sum(q.astype(F32) * n_rows, axis=1, keepdims=True)
            num = jnp.dot(s.astype(BF16), v, preferred_element_type=F32) + e_int * st.pop("inter")
            den = jnp.sum(s, axis=1, keepdims=True) + e_int * qn
            hh = num / jnp.maximum(jnp.abs(den), jnp.exp(-m_t))
            mu = jnp.mean(hh, axis=1, keepdims=True)
            dlt = hh - mu
            var = jnp.mean(dlt * dlt, axis=1, keepdims=True)
            y = dlt * lax.rsqrt(var + HEAD_NORM_EPS)
            out_ref[:, vs] = (y * ng_ref[:, vs] * mo_ref[:, vs].astype(F32)).astype(BF16)
            n_ref[:, qs] = (st.pop("e_c") * n_rows
                            + jnp.dot(st.pop("w_mat").astype(BF16), k, preferred_element_type=F32))

        return (gate_stage, score_stage, state_stage, output_stage), st

    heads = [head_stages(hl) for hl in range(hps)]
    for stage in range(4):
        if between is not None:
            between()
        for stages, _ in heads:
            stages[stage]()
    for hl, (_, st) in enumerate(heads):
        m_next = jnp.where(lane == head0 + hl, st["m_new"], m_next)

    m_ref[...] = m_next


def _mlstm(q, k, v, gates, mo, norm_g, state, nb, tpb, n_chunks, hps):
    m = q.shape[0]
    L = nb * tpb
    n_blocks = m // (L * n_chunks)
    hd = M_HEADS * M_DQK

    def tok(width):
        return pl.BlockSpec((L, width), lambda b, g, c: (b * n_chunks + c, g))

    c_spec = pl.BlockSpec((nb, hps, M_DV, M_DQK), lambda b, g, c: (b, g, 0, 0))
    n_spec = pl.BlockSpec((L, hps * M_DQK), lambda b, g, c: (b, g))
    m_spec = pl.BlockSpec((L, LANES), lambda b, g, c: (b, 0))
    in_specs = [tok(hps * M_DQK), tok(hps * M_DQK), tok(hps * M_DV),
                pl.BlockSpec((L, LANES), lambda b, g, c: (b * n_chunks + c, 0)),
                tok(hps * M_DV), pl.BlockSpec((1, hps * M_DV), lambda b, g, c: (0, g)),
                c_spec, n_spec, m_spec]
    args = [q, k, v, gates, mo, norm_g.reshape(1, M_HEADS * M_DV), *state]
    return pl.pallas_call(
        functools.partial(_mlstm_kernel, nb=nb, tpb=tpb, hps=hps),
        grid=(n_blocks, M_HEADS // hps, n_chunks),
        in_specs=in_specs,
        out_specs=[tok(hps * M_DV), c_spec, n_spec, m_spec],
        out_shape=[jax.ShapeDtypeStruct((m, M_HEADS * M_DV), BF16),
                   jax.ShapeDtypeStruct((n_blocks * nb, M_HEADS, M_DV, M_DQK), F32),
                   jax.ShapeDtypeStruct((n_blocks * L, hd), F32),
                   jax.ShapeDtypeStruct((n_blocks * L, LANES), F32)],
        scratch_shapes=[pltpu.VMEM((LANES, L), F32), pltpu.VMEM((LANES, L), F32)],
        compiler_params=_params(),
        name="mlstm",
    )(*args)


def _swa_kernel(sink_ref, q_ref, kp_ref, kc_ref, vp_ref, vc_ref, o_ref, kn_ref, vn_ref, *, nbb, n_new):
    tq = q_ref.shape[1]
    qi = lax.broadcasted_iota(jnp.int32, (tq, WINDOW), 0)
    kj = lax.broadcasted_iota(jnp.int32, (tq, WINDOW), 1)
    keep_old = lax.broadcasted_iota(jnp.int32, (WINDOW, W_AKV), 0) < WINDOW - n_new

    def pad_keys(x):
        return jnp.concatenate([x, jnp.zeros((WINDOW - x.shape[0], x.shape[1]), x.dtype)], axis=0)

    blocks = []
    for jb in range(nbb):
        kc, vc = pad_keys(kc_ref[jb]), pad_keys(vc_ref[jb])
        kp, vp = kp_ref[jb], vp_ref[jb]
        kn_ref[jb] = jnp.where(keep_old, pltpu.roll(kp, WINDOW - n_new, 0), pltpu.roll(kc, WINDOW - n_new, 0))
        vn_ref[jb] = jnp.where(keep_old, pltpu.roll(vp, WINDOW - n_new, 0), pltpu.roll(vc, WINDOW - n_new, 0))
        blocks.append(_swa_stages(sink_ref, q_ref[jb], kp, vp, kc, vc, kj > qi, o_ref.at[jb]))

    for stage in range(3):
        for pairs in blocks:
            for pair in pairs:
                pair[stage]()


def _swa_stages(sink_ref, q, kp, vp, kc, vc, valid_prev, o_ref):
    tq = q.shape[0]
    reps = A_HEADS // A_KV_HEADS
    low_q = lax.broadcasted_iota(jnp.int32, (tq, LANES), 1) < A_DH
    low_k = lax.broadcasted_iota(jnp.int32, (WINDOW, LANES), 1) < A_DH
    key_row = lax.broadcasted_iota(jnp.int32, (WINDOW, LANES), 0)
    qi = lax.broadcasted_iota(jnp.int32, (tq, WINDOW), 0)
    kj = lax.broadcasted_iota(jnp.int32, (tq, WINDOW), 1)
    sink_lane = kj == 0
    valid_cur = kj <= qi
    scale = jnp.asarray(ATTN_SCALE, BF16)
    neg_inf = -jnp.inf

    def block_diag(x):
        return jnp.concatenate([jnp.where(low_k, x, 0.0), jnp.where(low_k, 0.0, x)], axis=0).astype(BF16)

    def pair_stages(p):
        ks = slice(p * LANES, (p + 1) * LANES)
        state = {}

        def scores():
            q4 = jnp.concatenate([q[:, (reps * p + r) * LANES:(reps * p + r + 1) * LANES] * scale
                                  for r in range(reps)], axis=0)
            state["sp"] = lax.dot_general(q4, block_diag(kp[:, ks]), _NT, preferred_element_type=F32)
            state["sc"] = lax.dot_general(q4, block_diag(kc[:, ks]), _NT, preferred_element_type=F32)

        def softmax():
            sp, sc = state.pop("sp"), state.pop("sc")
            pps, pcs, invs = [], [], []
            for r in range(reps):
                rows = slice(r * tq, (r + 1) * tq)
                pp_r, pc_r, inv_r = [], [], []
                for half in range(2):
                    cols = slice(half * WINDOW, (half + 1) * WINDOW)
                    head = 2 * reps * p + reps * half + r
                    sp_i = jnp.where(sink_lane, sink_ref[head], jnp.where(valid_prev, sp[rows, cols], neg_inf))
                    sc_i = jnp.where(valid_cur, sc[rows, cols], neg_inf)
                    mx = jnp.max(jnp.maximum(sp_i, sc_i), axis=1, keepdims=True)
                    pp = jnp.exp(sp_i - mx)
                    pc = jnp.exp(sc_i - mx)
                    inv_r.append(1.0 / jnp.sum(pp + pc, axis=1, keepdims=True))
                    pp_r.append(pp.astype(BF16))
                    pc_r.append(pc.astype(BF16))
                pps.append(jnp.concatenate(pp_r, axis=1))
                pcs.append(jnp.concatenate(pc_r, axis=1))
                invs.append(jnp.where(low_q, inv_r[0], inv_r[1]))
            state.update(pp=jnp.concatenate(pps, axis=0), pc=jnp.concatenate(pcs, axis=0), inv=invs)

        def values():
            v_prev = block_diag(jnp.where(key_row == 0, 0.0, vp[:, ks]))
            o4 = (jnp.dot(state.pop("pp"), v_prev, preferred_element_type=F32)
                  + jnp.dot(state.pop("pc"), block_diag(vc[:, ks]), preferred_element_type=F32))
            invs = state.pop("inv")
            for r in range(reps):
                blk = reps * p + r
                o_ref[:, blk * LANES:(blk + 1) * LANES] = (o4[r * tq:(r + 1) * tq] * invs[r]).astype(BF16)

        return scores, softmax, values

    return [pair_stages(p) for p in range(A_KV_HEADS // 2)]


def _swa_block(sink_ref, q, kp, vp, kc, vc, valid_prev, o_ref, between=None):
    for scores, softmax, values in _swa_stages(sink_ref, q, kp, vp, kc, vc, valid_prev, o_ref):
        if between is not None:
            between()
        scores()
        softmax()
        if between is not None:
            between()
        values()


def _swa(sinks, q, k_cache, k_new, v_cache, v_new, *, nbb, n_new):
    n, tq, _ = q.shape
    tk = k_new.shape[1]
    index = lambda i: (i, 0, 0)
    cache_spec = pl.BlockSpec((nbb, WINDOW, W_AKV), index)
    new_spec = pl.BlockSpec((nbb, tk, W_AKV), index)
    q_spec = pl.BlockSpec((nbb, tq, W_AQ), index)
    return pl.pallas_call(
        functools.partial(_swa_kernel, nbb=nbb, n_new=n_new),
        grid=(n // nbb,),
        in_specs=[pl.BlockSpec(memory_space=pltpu.SMEM), q_spec, cache_spec, new_spec, cache_spec, new_spec],
        out_specs=[q_spec, cache_spec, cache_spec],
        out_shape=[jax.ShapeDtypeStruct((n, tq, W_AQ), BF16)]
        + [jax.ShapeDtypeStruct((n, WINDOW, W_AKV), F32)] * 2,
        compiler_params=_params(),
        name="swa",
    )(sinks, q, k_cache, k_new, v_cache, v_new)


def _merge_kernel(x_ref, g_ref, hm_ref, oa_ref, gm_ref, ga_ref, wm_ref, wa_ref, wo_ref,
                  lg_ref, lb_ref, o_ref):
    ym = jnp.dot(hm_ref[...], wm_ref[...], preferred_element_type=F32)
    ya = jnp.dot(oa_ref[...], wa_ref[...], preferred_element_type=F32)
    mix = gm_ref[...].astype(F32) * ym + ga_ref[...].astype(F32) * ya
    t = jnp.dot(mix.astype(BF16), wo_ref[...], preferred_element_type=F32)
    y = DEEPNORM_ALPHA * x_ref[...] + (1.0 + _rows(g_ref)) * t
    o_ref[...] = _layer_norm(y, lg_ref[...], lb_ref[...], LN_EPS)


def _merge(x, mod, per_row, rows_per_batch, chunk, hm, oa, gm, ga, wm, wa, wo, ln_g, ln_b):
    m = x.shape[0]
    tm = ROW_TILE

    def tok():
        return pl.BlockSpec((tm, D_MODEL), lambda i: (i, 0))

    return pl.pallas_call(
        _merge_kernel,
        grid=(m // tm,),
        in_specs=[tok()] + _mod_specs(per_row, tm, rows_per_batch // tm, (chunk,))
        + [tok(), tok(), tok(), tok()]
        + [_resident((D_MODEL, D_MODEL))] * 3 + [_resident((1, D_MODEL))] * 2,
        out_specs=tok(),
        out_shape=jax.ShapeDtypeStruct((m, D_MODEL), F32),
        compiler_params=_params(),
        name="merge",
    )(x, mod, hm, oa, gm, ga, wm, wa, wo, ln_g.reshape(1, D_MODEL), ln_b.reshape(1, D_MODEL))


def _mixer_kernel(sink_ref, xa_ref, xb_ref, sh_ref, sc_ref, g_ref, wa_ref, wq_ref, wr_ref, wif_ref, bif_ref,
                  cos_ref, sin_ref, ng_ref, wm_ref, wba_ref, wo_ref, lg_ref, lb_ref,
                  y_ref, c_ref, n_ref, m_ref, kb_ref, vb_ref, *scratch, tiles_per_seq):
    n_staged = (len(scratch) - 6) // 2
    even, odd = scratch[:n_staged], scratch[n_staged:2 * n_staged]
    kprev_s, vprev_s, hm_s, oa_s, gtt_s, ctt_s = scratch[2 * n_staged:]
    tm = xa_ref.shape[0]
    s = pl.program_id(0)
    tile_b = jnp.maximum(s - 1, 0)
    seq_start = tile_b % tiles_per_seq == 0

    @pl.when(s == 0)
    def _():
        for ref in odd:
            ref[...] = jnp.zeros_like(ref)

    @pl.when(seq_start)
    def _():
        c_ref[...] = jnp.zeros_like(c_ref)
        n_ref[...] = jnp.zeros_like(n_ref)
        m_ref[...] = jnp.zeros_like(m_ref)
        kprev_s[...] = jnp.zeros_like(kprev_s)
        vprev_s[...] = jnp.zeros_like(vprev_s)

    def step(produce, consume):
        proj_parts = _proj_parts(xa_ref, sh_ref, sc_ref, wa_ref, wq_ref, wr_ref, wif_ref, bif_ref,
                                 cos_ref, sin_ref, *produce)
        pending = iter(proj_parts)

        counts = iter(MIXER_PROJ_SCHEDULE)

        def between():
            for _ in range(next(counts)):
                part = next(pending, None)
                if part is not None:
                    part()

        q_s, k_s, v_s, mo_s, aq_s, ak_s, av_s, gm_s, ga_s, gt_s = consume
        qi = lax.broadcasted_iota(jnp.int32, (WINDOW, WINDOW), 0)
        kj = lax.broadcasted_iota(jnp.int32, (WINDOW, WINDOW), 1)
        for j in range(tm // MLSTM_CHUNK):
            rows = pl.ds(j * MLSTM_CHUNK, MLSTM_CHUNK)
            _mlstm_chunk(q_s.at[rows], k_s.at[rows], v_s.at[rows], gt_s.at[rows], mo_s.at[rows], ng_ref,
                         hm_s.at[rows], c_ref, n_ref, m_ref, gtt_s, ctt_s,
                         nb=1, tpb=MLSTM_CHUNK, hps=M_HEADS, head0=0, between=between)
            valid_prev = kj > qi
            if j == 0:
                valid_prev = valid_prev & jnp.logical_not(seq_start)
                k_prev, v_prev = kprev_s[...], vprev_s[...]
            else:
                before = pl.ds((j - 1) * WINDOW, WINDOW)
                k_prev, v_prev = ak_s[before], av_s[before]
            _swa_block(sink_ref, aq_s[rows], k_prev, v_prev, ak_s[rows], av_s[rows], valid_prev,
                       oa_s.at[rows], between=between)

        for part in pending:
            part()
        _merge_kernel(xb_ref, g_ref, hm_s, oa_s, gm_s, ga_s, wm_ref, wba_ref, wo_ref, lg_ref, lb_ref, y_ref)

        last = pl.ds(tm - WINDOW, WINDOW)
        k_last, v_last = ak_s[last], av_s[last]
        kb_ref[0] = k_last.T
        vb_ref[0] = v_last.T
        kprev_s[...] = k_last
        vprev_s[...] = v_last

    @pl.when(s % 2 == 0)
    def _():
        step(even, odd)

    @pl.when(s % 2 == 1)
    def _():
        step(odd, even)


def _mixer(x1, mod, w, sinks, norm_g, cos_t, sin_t, bp, sp):
    tm = MIXER_TILE
    tps = sp // tm
    n_tiles = bp * tps
    hd = M_HEADS * M_DQK
    weights = (w["w_a"], w["w_aq"], w["w_r"], w["w_if"], w["bif"])

    def tile_a(s):
        return jnp.minimum(s, n_tiles - 1)

    def tile_b(s):
        return jnp.maximum(s - 1, 0)

    def mod_spec(tile, chunk):
        return pl.BlockSpec((1, 1, D_MODEL), lambda s: (tile(s) // tps, 0, chunk))

    def per_seq(*shape):
        return pl.BlockSpec((1,) + shape, lambda s: (tile_b(s) // tps,) + (0,) * len(shape))

    pos = pl.BlockSpec((tm, LANES), lambda s: (tile_a(s) % tps, 0))
    staged = [(W_MQ, BF16), (W_MQ, BF16), (W_MV, BF16), (W_MV, BF16), (W_AQ, BF16),
              (W_AKV, F32), (W_AKV, F32), (D_MODEL, BF16), (D_MODEL, BF16), (LANES, F32)]
    return pl.pallas_call(
        functools.partial(_mixer_kernel, tiles_per_seq=tps),
        grid=(n_tiles + 1,),
        in_specs=[pl.BlockSpec(memory_space=pltpu.SMEM),
                  pl.BlockSpec((tm, D_MODEL), lambda s: (tile_a(s), 0)),
                  pl.BlockSpec((tm, D_MODEL), lambda s: (tile_b(s), 0)),
                  mod_spec(tile_a, 3), mod_spec(tile_a, 4), mod_spec(tile_b, 5)]
        + [_resident(a.shape) for a in weights] + [pos, pos, _resident((1, M_HEADS * M_DV))]
        + [_resident((D_MODEL, D_MODEL))] * 3 + [_resident((1, D_MODEL))] * 2,
        out_specs=[pl.BlockSpec((tm, D_MODEL), lambda s: (tile_b(s), 0)),
                   per_seq(M_HEADS, M_DV, M_DQK),
                   pl.BlockSpec((MLSTM_CHUNK, hd), lambda s: (tile_b(s) // tps, 0)),
                   pl.BlockSpec((MLSTM_CHUNK, LANES), lambda s: (tile_b(s) // tps, 0)),
                   per_seq(W_AKV, WINDOW), per_seq(W_AKV, WINDOW)],
        out_shape=[jax.ShapeDtypeStruct((bp * sp, D_MODEL), F32),
                   jax.ShapeDtypeStruct((bp, M_HEADS, M_DV, M_DQK), F32),
                   jax.ShapeDtypeStruct((bp * MLSTM_CHUNK, hd), F32),
                   jax.ShapeDtypeStruct((bp * MLSTM_CHUNK, LANES), F32),
                   jax.ShapeDtypeStruct((bp, W_AKV, WINDOW), F32),
                   jax.ShapeDtypeStruct((bp, W_AKV, WINDOW), F32)],
        scratch_shapes=[pltpu.VMEM((tm, width), dtype) for width, dtype in staged + staged]
        + [pltpu.VMEM((WINDOW, W_AKV), F32), pltpu.VMEM((WINDOW, W_AKV), F32),
           pltpu.VMEM((tm, W_MV), BF16), pltpu.VMEM((tm, W_AQ), BF16),
           pltpu.VMEM((LANES, MLSTM_CHUNK), F32), pltpu.VMEM((LANES, MLSTM_CHUNK), F32)],
        compiler_params=_params(),
        name="mixer",
    )(sinks, x1, x1, mod, mod, mod, *weights, cos_t, sin_t, norm_g.reshape(1, M_HEADS * M_DV),
      w["wm"], w["wa"], w["wo"], w["ln2_g"].reshape(1, D_MODEL), w["ln2_b"].reshape(1, D_MODEL))


def _rope_tables(pos):
    half = A_DH // 2
    inv = np.float32(ROPE_THETA) ** (-np.arange(half, dtype=np.float32) / np.float32(half))
    ang = pos.astype(np.float32)[:, None] * inv[None, :]
    cos, sin = np.cos(ang), np.sin(ang)
    return jnp.asarray(np.tile(cos, (1, 4))), jnp.asarray(np.concatenate([-sin, sin, -sin, sin], axis=1))


def _from_stored(kv_t):
    return jnp.transpose(kv_t.reshape(kv_t.shape[0], A_KV_HEADS, A_DH, WINDOW), (0, 3, 1, 2))


def _token_stage_1(x, mod, per_row, rows_per_batch, w, pos):
    x1 = _ffn(x, mod, per_row, rows_per_batch, (0, 1, 2), w["up1"], w["down1"], w["ln1_g"], w["ln1_b"])
    cos_t, sin_t = _rope_tables(pos)
    return x1, _proj(x1, mod, per_row, rows_per_batch, (3, 4), w, cos_t, sin_t)


def _token_stage_2(x1, mod, per_row, rows_per_batch, w, hm, oa, gm, ga):
    x2 = _merge(x1, mod, per_row, rows_per_batch, 5, hm, oa, gm, ga,
                w["wm"], w["wa"], w["wo"], w["ln2_g"], w["ln2_b"])
    return _ffn(x2, mod, per_row, rows_per_batch, (6, 7, 8), w["up2"], w["down2"], w["ln3_g"], w["ln3_b"])


def kernel(x_prompt, x_sample, state_mlstm_C, state_mlstm_n, state_mlstm_m, cache_swa_k, cache_swa_v, c_prompt, c_sample, w_ada, b_ada, w_ffn1_up, w_ffn1_down, ln1_g, ln1_b, w_in, b_igate, b_fgate, m_norm_g, sinks, w_branch_m, w_branch_a, w_out, ln2_g, ln2_b, w_ffn2_up, w_ffn2_down, ln3_g, ln3_b):
    assert w_ada.shape[0] == DEPTH == 1
    bp, sp, _ = x_prompt.shape
    bs, ts, _ = x_sample.shape

    w_in_t = jnp.transpose(w_in[0])
    w = dict(
        up1=w_ffn1_up[0].astype(BF16), down1=w_ffn1_down[0].astype(BF16),
        bif=jnp.concatenate([b_igate[0], b_fgate[0], jnp.zeros((LANES - 2 * M_HEADS,), F32)]).reshape(1, LANES),
        ln1_g=ln1_g[0], ln1_b=ln1_b[0], ln2_g=ln2_g[0], ln2_b=ln2_b[0], ln3_g=ln3_g[0], ln3_b=ln3_b[0],
    )

    ms = bs * ts
    c_all = jnp.concatenate([jnp.repeat(c_sample, ts, axis=0), c_prompt], axis=0)
    mod = _ada(c_all, w_ada[0], b_ada[0])
    mod_p = mod[ms:].reshape(bp, 1, ADA_CHUNKS * D_MODEL)

    mp = bp * sp
    reps = A_HEADS // A_KV_HEADS
    pair_low = lambda j: (j // reps) * (2 * reps) + j % reps
    pair_rows = lambda j: (j // (2 * reps)) * (2 * reps) + (j % 2) * reps + (j % (2 * reps)) // 2
    jobs = (
        _cast_job(w_ffn2_up[0], (D_MODEL, FF_CHUNK), 1),
        _cast_job(w_ffn2_down[0], (FF_CHUNK, D_MODEL), 0, first_step=2 * D_FF // FF_CHUNK),
        _cast_job(w_in_t, (D_MODEL, LANES), 1, transposed=True, n_blocks=IN_IF // LANES),
        _cast_job(w_in_t, (D_MODEL, LANES), 1, transposed=True, n_blocks=W_AQ // LANES, shift=IN_AQ - IN_IF,
                  sources=(lambda j: IN_IF // A_DH + pair_low(j), lambda j: IN_IF // A_DH + pair_low(j) + reps)),
        _cast_job(w_in_t, (D_MODEL, LANES), 1, transposed=True, n_blocks=(IN_END - IN_AK) // LANES,
                  shift=IN_AQ - IN_IF, sources=(lambda j: (IN_AK - (IN_AQ - IN_IF)) // LANES + j,)),
        _cast_job(w_branch_m[0], (D_MODEL, LANES), 1),
        _cast_job(w_out[0], (D_MODEL, LANES), 1),
        _cast_job(w_branch_a[0], (A_DH, D_MODEL), 0, sources=(pair_rows,)),
        _cast_job(w_in_t, (LANES, D_MODEL), 0, n_blocks=1, sources=(lambda j: IN_IF // LANES + j,)),
    )
    x1p, w["up2"], w["down2"], w["w_a"], w["w_aq"], w["w_r"], w["wm"], w["wo"], w["wa"], w["w_if"] = _ffn(
        x_prompt.reshape(mp, D_MODEL), mod_p, False, sp, (0, 1, 2),
        w["up1"], w["down1"], w["ln1_g"], w["ln1_b"], cast_jobs=jobs)
    cos_t, sin_t = _rope_tables(np.arange(sp))
    x2p, c_p, n_rows, m_rows, kb_p, vb_p = _mixer(x1p, mod_p, w, sinks[0], m_norm_g[0], cos_t, sin_t, bp, sp)
    y_p = _ffn(x2p, mod_p, False, sp, (6, 7, 8), w["up2"], w["down2"], w["ln3_g"], w["ln3_b"])
    n_p = n_rows.reshape(bp, MLSTM_CHUNK, M_HEADS, M_DQK)[:, 0]
    m_p = m_rows.reshape(bp, MLSTM_CHUNK, LANES)[:, 0, :M_HEADS]
    kb_p, vb_p = _from_stored(kb_p), _from_stored(vb_p)

    x1s, (qm, km, vm, mo, aq, ak, av, gm, ga, gt) = _token_stage_1(
        x_sample.reshape(ms, D_MODEL), mod, True, ms, w, PAST_LEN + np.arange(ms) % ts)
    seqs = MLSTM_CHUNK // ts
    n0_rows = jnp.repeat(state_mlstm_n[0].reshape(bs, M_HEADS * M_DQK), ts, axis=0)
    m0_rows = jnp.repeat(jnp.pad(state_mlstm_m[0], ((0, 0), (0, LANES - M_HEADS))), ts, axis=0)
    hm, c_s, n_rows, m_rows = _mlstm(qm, km, vm, gt, mo, m_norm_g[0],
                                     (state_mlstm_C[0], n0_rows, m0_rows), seqs, ts, 1, 2)
    n_s = n_rows.reshape(bs, ts, M_HEADS, M_DQK)[:, 0]
    m_s = m_rows.reshape(bs, ts, LANES)[:, 0, :M_HEADS]
    pad_t = ((0, 0), (0, SAMPLE_TQ - ts), (0, 0))
    oa, kb_s, vb_s = _swa(
        sinks[0], jnp.pad(aq.reshape(bs, ts, W_AQ), pad_t),
        cache_swa_k[0].reshape(bs, WINDOW, W_AKV), jnp.pad(ak.reshape(bs, ts, W_AKV), pad_t),
        cache_swa_v[0].reshape(bs, WINDOW, W_AKV), jnp.pad(av.reshape(bs, ts, W_AKV), pad_t),
        nbb=SWA_SAMPLE_SEQS, n_new=ts)
    y_s = _token_stage_2(x1s, mod, True, ms, w, hm, oa[:, :ts].reshape(ms, D_MODEL), gm, ga)
    kb_s = kb_s.reshape(bs, WINDOW, A_KV_HEADS, A_DH)
    vb_s = vb_s.reshape(bs, WINDOW, A_KV_HEADS, A_DH)

    return (y_p.reshape(bp, sp, D_MODEL), y_s.reshape(bs, ts, D_MODEL),
            c_p[None], n_p[None], m_p[None], kb_p[None], vb_p[None],
            c_s[None], n_s[None], m_s[None], kb_s[None], vb_s[None])
```

```python
import functools

import jax
import jax.numpy as jnp
import numpy as np
from jax import lax
from jax.experimental import pallas as pl
from jax.experimental.pallas import tpu as pltpu

F32 = jnp.float32
BF16 = jnp.bfloat16

D_MODEL = 1024
D_FF = 2816
DEPTH = 1
M_HEADS = 4
M_DQK = 128
M_DV = 256
A_HEADS = 16
A_KV_HEADS = 4
A_DH = 64
WINDOW = 128
PAST_LEN = 8192
ROPE_THETA = 10000.0
ATTN_SCALE = A_DH ** -0.5
LN_EPS = 1e-5
HEAD_NORM_EPS = 1e-6
ADA_CHUNKS = 9
DEEPNORM_ALPHA = (2.0 * DEPTH) ** 0.25
K_SCALE = M_DQK ** -0.5

LANES = 128
BF16_SUBLANES = 16
VMEM_LIMIT_BYTES = 56 * 1024 * 1024

W_MQ = M_HEADS * M_DQK
W_MV = M_HEADS * M_DV
W_AQ = A_HEADS * A_DH
W_AKV = A_KV_HEADS * A_DH
IN_IF = 2 * W_MQ + 2 * W_MV
IN_AQ = IN_IF + 2 * M_HEADS
IN_AK = IN_AQ + W_AQ
IN_END = IN_AK + 2 * W_AKV + 2 * D_MODEL
A_Q, A_K, A_V, A_O = 0, W_MQ, 2 * W_MQ, 2 * W_MQ + W_MV
R_AK, R_AV, R_GM, R_GA = 0, W_AKV, 2 * W_AKV, 2 * W_AKV + D_MODEL

ROW_TILE = 512
MIXER_TILE = 256
MIXER_PROJ_SCHEDULE = (1, 1, 1, 1, 2, 2, 2, 2) * 2
ADA_TILE = 1536
FF_CHUNK = 256
PROJ_CHUNK = 2 * LANES
FFN_EPILOGUE_PIECES = 4
MLSTM_CHUNK = 128
SAMPLE_TQ = BF16_SUBLANES
SWA_SAMPLE_SEQS = 16

_NT = (((1,), (1,)), ((), ()))


def _params():
    return pltpu.CompilerParams(vmem_limit_bytes=VMEM_LIMIT_BYTES)


def _resident(shape):
    return pl.BlockSpec(shape, lambda *_: (0,) * len(shape), pipeline_mode=pl.Buffered(1))


def _rows(ref):
    v = ref[...]
    return v.reshape(v.shape[-2], v.shape[-1])


def _layer_norm(y, g, b, eps):
    mu = jnp.mean(y, axis=-1, keepdims=True)
    d = y - mu
    var = jnp.mean(d * d, axis=-1, keepdims=True)
    return d * lax.rsqrt(var + eps) * g + b


def _sigmoid(x):
    return 1.0 / (1.0 + jnp.exp(-x))


def _mod_specs(per_row, tm, tiles_per_batch, chunks):
    if per_row:
        return [pl.BlockSpec((tm, D_MODEL), lambda i, c=c: (i, c)) for c in chunks]
    return [pl.BlockSpec((1, 1, D_MODEL), lambda i, c=c: (i // tiles_per_batch, 0, c)) for c in chunks]


def _ada_kernel(c_ref, w_ref, b_ref, o_ref):
    c = c_ref[...]
    s = (c * _sigmoid(c)).astype(BF16)
    o_ref[...] = jnp.dot(s, w_ref[...].astype(BF16), preferred_element_type=F32) + b_ref[...]


def _ada(c_all, w_ada, b_ada):
    rows = c_all.shape[0]
    n_out = w_ada.shape[1]
    tn = ADA_TILE
    return pl.pallas_call(
        _ada_kernel,
        grid=(n_out // tn,),
        in_specs=[pl.BlockSpec((rows, D_MODEL), lambda j: (0, 0)),
                  pl.BlockSpec((D_MODEL, tn), lambda j: (0, j)),
                  pl.BlockSpec((1, tn), lambda j: (0, j))],
        out_specs=pl.BlockSpec((rows, tn), lambda j: (0, j)),
        out_shape=jax.ShapeDtypeStruct((rows, n_out), F32),
        compiler_params=_params(),
        name="ada",
    )(c_all, w_ada, b_ada.reshape(1, n_out))


def _ffn_kernel(x_ref, xp_ref, sh_ref, sc_ref, g_ref, wup_ref, wdn_ref, lg_ref, lb_ref, *rest,
                n_tiles, cast_layout):
    n_cast = len(cast_layout)
    n_src = sum(layout[0] for layout in cast_layout)
    cast_in, o_ref, cast_out = rest[:n_src], rest[n_src], rest[n_src + 1:n_src + 1 + n_cast]
    act_ref, f_ref = rest[n_src + 1 + n_cast:]

    def cast_chunk(job):
        if job >= n_cast:
            return
        first = sum(layout[0] for layout in cast_layout[:job])
        n, transposed, shift = cast_layout[job]
        refs = cast_in[first:first + n]
        if shift:
            rows = refs[0].shape[0]
            parts = [jnp.concatenate([lo[...], hi[...]], axis=0)[shift:shift + rows]
                     for lo, hi in zip(refs[0::2], refs[1::2])]
        else:
            parts = [ref[...] for ref in refs]
        block = jnp.concatenate(parts, axis=0)
        cast_out[job][...] = (block.T if transposed else block).astype(BF16)

    tm = x_ref.shape[0]
    s = pl.program_id(0)
    piece_rows = tm // FFN_EPILOGUE_PIECES

    def epilogue(piece):
        rows = pl.ds(piece * piece_rows, piece_rows)
        g = _rows(g_ref)
        if g.shape[0] != 1:
            g = g[piece * piece_rows:(piece + 1) * piece_rows]
        y = DEEPNORM_ALPHA * xp_ref[rows, :] + (0.5 * (1.0 + g)) * f_ref[rows, :]
        o_ref[rows, :] = _layer_norm(y, lg_ref[...], lb_ref[...], LN_EPS)

    def matmuls(previous_epilogue):
        h = (x_ref[...] * (1.0 + _rows(sc_ref)) + _rows(sh_ref)).astype(BF16)
        chunks = list(range(0, D_FF, FF_CHUNK))
        every = len(chunks) // FFN_EPILOGUE_PIECES
        piece = 0
        for i, c in enumerate(chunks):
            a = jnp.dot(h, wup_ref[:, c:c + FF_CHUNK], preferred_element_type=F32)
            u = jnp.dot(h, wup_ref[:, D_FF + c:D_FF + c + FF_CHUNK], preferred_element_type=F32)
            act_ref[:, c:c + FF_CHUNK] = (a * _sigmoid(a) * u).astype(BF16)
            cast_chunk(i)
            if previous_epilogue and i % every == every - 1 and piece < FFN_EPILOGUE_PIECES:
                epilogue(piece)
                piece += 1
        f_ref[...] = jnp.dot(act_ref[...], wdn_ref[...], preferred_element_type=F32)

    if n_tiles == 1:
        matmuls(False)
        for piece in range(FFN_EPILOGUE_PIECES):
            epilogue(piece)
        return

    @pl.when(s == 0)
    def _():
        f_ref[...] = jnp.zeros_like(f_ref)

    @pl.when(s < n_tiles)
    def _():
        matmuls(True)

    @pl.when(s == n_tiles)
    def _():
        for job in range(n_cast):
            cast_chunk(job)
        for piece in range(FFN_EPILOGUE_PIECES):
            epilogue(piece)


def _cast_job(array, block, axis, first_step=0, sources=None, transposed=False, n_blocks=None, shift=0):
    sources = sources or (lambda j: j,)
    src_axis = 1 - axis if transposed else axis
    src_block = block[::-1] if transposed else block
    src_block = (src_block[0] // len(sources), src_block[1])
    if n_blocks is None:
        n_blocks = array.shape[src_axis] // block[axis]

    def step_block(s):
        return jnp.clip(s - first_step, 0, n_blocks - 1)

    def at(axis_, j):
        return (j, 0) if axis_ == 0 else (0, j)

    in_specs = [pl.BlockSpec(src_block, lambda s, f=f, d=d: at(src_axis, f(step_block(s)) + d))
                for f in sources for d in range(2 if shift else 1)]
    out_dims = list(array.shape[::-1] if transposed else array.shape)
    out_dims[axis] = n_blocks * block[axis]
    return dict(array=array, in_specs=in_specs, layout=(len(in_specs), transposed, shift),
                out_spec=pl.BlockSpec(block, lambda s: at(axis, step_block(s))),
                out_shape=jax.ShapeDtypeStruct(tuple(out_dims), BF16), steps=first_step + n_blocks)


def _ffn(x, mod, per_row, rows_per_batch, chunks, w_up, w_down, ln_g, ln_b, cast_jobs=()):
    m = x.shape[0]
    tm = ROW_TILE
    n_tiles = m // tm
    tpb = rows_per_batch // tm

    def cur(s):
        return jnp.minimum(s, n_tiles - 1)

    def prev(s):
        return jnp.maximum(s - 1, 0)

    def mod_spec(tile, chunk):
        if per_row:
            return pl.BlockSpec((tm, D_MODEL), lambda s: (tile(s), chunk))
        return pl.BlockSpec((1, 1, D_MODEL), lambda s: (tile(s) // tpb, 0, chunk))

    in_specs = [pl.BlockSpec((tm, D_MODEL), lambda s: (cur(s), 0)),
                pl.BlockSpec((tm, D_MODEL), lambda s: (prev(s), 0)),
                mod_spec(cur, chunks[0]), mod_spec(cur, chunks[1]), mod_spec(prev, chunks[2]),
                _resident((D_MODEL, 2 * D_FF)), _resident((D_FF, D_MODEL)),
                _resident((1, D_MODEL)), _resident((1, D_MODEL))]
    args = [x, x, mod, mod, mod, w_up, w_down, ln_g.reshape(1, D_MODEL), ln_b.reshape(1, D_MODEL)]
    out_specs = [pl.BlockSpec((tm, D_MODEL), lambda s: (prev(s), 0))]
    out_shape = [jax.ShapeDtypeStruct((m, D_MODEL), F32)]
    n_steps = 1 if n_tiles == 1 else n_tiles + 1
    for job in cast_jobs:
        assert job["steps"] <= n_steps
        in_specs += job["in_specs"]
        args += [job["array"]] * len(job["in_specs"])
        out_specs.append(job["out_spec"])
        out_shape.append(job["out_shape"])
    outs = pl.pallas_call(
        functools.partial(_ffn_kernel, n_tiles=n_tiles, cast_layout=tuple(j["layout"] for j in cast_jobs)),
        grid=(n_steps,),
        in_specs=in_specs,
        out_specs=out_specs,
        out_shape=out_shape,
        scratch_shapes=[pltpu.VMEM((tm, D_FF), BF16), pltpu.VMEM((tm, D_MODEL), F32)],
        compiler_params=_params(),
        name="ffn",
    )(*args)
    return outs if cast_jobs else outs[0]


def _proj_parts(x_ref, sh_ref, sc_ref, wa_ref, wq_ref, wr_ref, wif_ref, bif_ref, cos_ref, sin_ref,
                q_ref, k_ref, v_ref, o_ref, aq_ref, ak_ref, av_ref, gm_ref, ga_ref, gt_ref):
    x = x_ref[...]
    tm = x.shape[0]
    h = (x * (1.0 + _rows(sc_ref)) + _rows(sh_ref)).astype(BF16)
    lane = lax.broadcasted_iota(jnp.int32, (tm, LANES), 1)
    pc = PROJ_CHUNK

    def seg(w_ref, lo):
        return jnp.dot(h, w_ref[:, lo:lo + pc], preferred_element_type=F32)

    def plain(dst_ref, w_ref, lo, c, scale=None):
        def run():
            z = seg(w_ref, lo + c)
            dst_ref[:, c:c + pc] = (z if scale is None else z * scale).astype(dst_ref.dtype)
        return run

    def gate(dst_ref, w_ref, lo, c):
        def run():
            dst_ref[:, c:c + pc] = _sigmoid(seg(w_ref, lo + c)).astype(BF16)
        return run

    def forget_input_gates():
        zg = lax.dot_general(h, wif_ref[...], _NT, preferred_element_type=F32) + bif_ref[...]
        logsig = jnp.minimum(zg, 0.0) - jnp.log(1.0 + jnp.exp(-jnp.abs(zg)))
        gt_ref[...] = jnp.where(lane < M_HEADS, zg, logsig)

    def rotary(dst_ref, w_ref, lo, c):
        def run():
            cos = cos_ref[...]
            sin = sin_ref[...]
            low_half = (lane & (A_DH // 2)) == 0
            z = seg(w_ref, lo + c)
            for half in range(2):
                zh = z[:, half * LANES:(half + 1) * LANES]
                partner = jnp.where(low_half, pltpu.roll(zh, LANES - A_DH // 2, 1),
                                    pltpu.roll(zh, A_DH // 2, 1))
                dst_ref[:, c + half * LANES:c + (half + 1) * LANES] = (
                    zh * cos + partner * sin).astype(dst_ref.dtype)
        return run

    parts = [plain(q_ref, wa_ref, A_Q, c) for c in range(0, W_MQ, pc)]
    parts += [plain(k_ref, wa_ref, A_K, c, K_SCALE) for c in range(0, W_MQ, pc)]
    parts += [plain(v_ref, wa_ref, A_V, c) for c in range(0, W_MV, pc)]
    parts += [forget_input_gates]
    parts += [rotary(aq_ref, wq_ref, 0, c) for c in range(0, W_AQ, pc)]
    parts += [rotary(ak_ref, wr_ref, R_AK, 0), plain(av_ref, wr_ref, R_AV, 0)]
    parts += [gate(o_ref, wa_ref, A_O, c) for c in range(0, W_MV, pc)]
    parts += [gate(gm_ref, wr_ref, R_GM, c) for c in range(0, D_MODEL, pc)]
    parts += [gate(ga_ref, wr_ref, R_GA, c) for c in range(0, D_MODEL, pc)]
    return parts


def _proj_kernel(*refs):
    for part in _proj_parts(*refs):
        part()


def _proj(x, mod, per_row, rows_per_batch, chunks, w, cos_t, sin_t):
    m = x.shape[0]
    tm = ROW_TILE
    n_pos_tiles = cos_t.shape[0] // tm

    def tok(width):
        return pl.BlockSpec((tm, width), lambda i: (i, 0))

    widths = (W_MQ, W_MQ, W_MV, W_MV, W_AQ, W_AKV, W_AKV, D_MODEL, D_MODEL, LANES)
    dtypes = (BF16, BF16, BF16, BF16, F32, F32, F32, BF16, BF16, F32)
    weights = (w["w_a"], w["w_aq"], w["w_r"], w["w_if"], w["bif"])
    return pl.pallas_call(
        _proj_kernel,
        grid=(m // tm,),
        in_specs=[tok(D_MODEL)] + _mod_specs(per_row, tm, rows_per_batch // tm, chunks)
        + [_resident(a.shape) for a in weights]
        + [pl.BlockSpec((tm, LANES), lambda i: (i % n_pos_tiles, 0)),
           pl.BlockSpec((tm, LANES), lambda i: (i % n_pos_tiles, 0))],
        out_specs=[tok(wd) for wd in widths],
        out_shape=[jax.ShapeDtypeStruct((m, wd), d) for wd, d in zip(widths, dtypes)],
        compiler_params=_params(),
        name="proj",
    )(x, mod, mod, *weights, cos_t, sin_t)


def _mlstm_kernel(q_ref, k_ref, v_ref, g_ref, mo_ref, ng_ref, c0_ref, n0_ref, m0_ref,
                  out_ref, c_ref, n_ref, m_ref, gt_s, ct_s, *, nb, tpb, hps):
    first_chunk = pl.program_id(2) == 0

    @pl.when(first_chunk)
    def _():
        c_ref[...] = c0_ref[...]
        n_ref[...] = n0_ref[...]

    @pl.when(first_chunk & (pl.program_id(1) == 0))
    def _():
        m_ref[...] = m0_ref[...]

    head0 = 0 if hps == M_HEADS else pl.program_id(1) * hps
    _mlstm_chunk(q_ref, k_ref, v_ref, g_ref, mo_ref, ng_ref, out_ref, c_ref, n_ref, m_ref, gt_s, ct_s,
                 nb=nb, tpb=tpb, hps=hps, head0=head0)


def _mlstm_chunk(q_ref, k_ref, v_ref, g_ref, mo_ref, ng_ref, out_ref, c_ref, n_ref, m_ref, gt_s, ct_s,
                 *, nb, tpb, hps, head0, between=None):
    L = nb * tpb
    shift = tpb.bit_length() - 1
    row = lax.broadcasted_iota(jnp.int32, (L, L), 0)
    col = lax.broadcasted_iota(jnp.int32, (L, L), 1)
    same = (row >> shift) == (col >> shift)
    causal = same & (col <= row)
    lane = lax.broadcasted_iota(jnp.int32, (L, LANES), 1)
    row_seq = lax.broadcasted_iota(jnp.int32, (L, 1), 0) >> shift

    def lane_col(x, idx):
        return jnp.sum(jnp.where(lane == idx, x, 0.0), axis=1, keepdims=True)

    gates = g_ref[...]
    tri = jnp.where(causal, 1.0, 0.0).astype(BF16)
    g_hi = gates.astype(BF16)
    rem = gates - g_hi.astype(F32)
    g_mid = rem.astype(BF16)
    g_lo = (rem - g_mid.astype(F32)).astype(BF16)
    cum = (jnp.dot(tri, g_hi, preferred_element_type=F32)
           + jnp.dot(tri, g_mid, preferred_element_type=F32)
           + jnp.dot(tri, g_lo, preferred_element_type=F32))
    gt_s[...] = gates.T
    ct_s[...] = cum.T
    m_rows = m_ref[...]
    m_next = m_rows

    def head_stages(hl):
        head = head0 + hl
        qs = slice(hl * M_DQK, (hl + 1) * M_DQK)
        vs = slice(hl * M_DV, (hl + 1) * M_DV)
        st = {}

        def gate_stage():
            b_c = lane_col(cum, M_HEADS + head)
            m_p = lane_col(m_rows, head)
            i_r = gt_s[pl.ds(head, 1), :]
            b_r = ct_s[pl.ds(M_HEADS + head, 1), :]
            log_d = jnp.where(causal, b_c - b_r + i_r, -jnp.inf)
            m_t = jnp.maximum(b_c + m_p, jnp.max(log_d, axis=1, keepdims=True))
            dmat = jnp.exp(log_d - m_t)
            e_int = jnp.exp(b_c + m_p - m_t)
            if nb == 1:
                last = slice(L - 1, L)
                m_new = jnp.broadcast_to(m_t[last], (L, 1))
                e_c = jnp.broadcast_to(e_int[last], (L, 1))
                w_mat = jnp.broadcast_to(dmat[last], (L, L))
            else:
                b_last = jnp.min(jnp.where(same, b_r, jnp.inf), axis=1, keepdims=True)
                log_w = jnp.where(same, b_last - b_r + i_r, -jnp.inf)
                m_new = jnp.maximum(b_last + m_p, jnp.max(log_w, axis=1, keepdims=True))
                w_mat = jnp.exp(log_w - m_new)
                e_c = jnp.exp(b_last + m_p - m_new)
            st.update(m_t=m_t, dmat=dmat, e_int=e_int, m_new=m_new, e_c=e_c, w_mat=w_mat)

        def score_stage():
            q, k = q_ref[:, qs], k_ref[:, qs]
            st["s"] = lax.dot_general(q, k, _NT, preferred_element_type=F32) * st.pop("dmat")

        def state_stage():
            q, k = q_ref[:, qs], k_ref[:, qs]
            v_t = v_ref[:, vs].astype(F32).T
            w_mat = st["w_mat"]
            e_cb = jnp.broadcast_to(st["e_c"], (L, LANES))
            inter = jnp.zeros((L, M_DV), F32)
            for j in range(nb):
                first = slice(j * tpb, j * tpb + 1)
                c_j = c_ref[j, hl]
                q_j = q if nb == 1 else jnp.where(row_seq == j, q, jnp.zeros_like(q))
                inter = inter + lax.dot_general(q_j, c_j.astype(BF16), _NT, preferred_element_type=F32)
                lhs = (v_t * w_mat[first]).astype(BF16)
                c_ref[j, hl] = e_cb[first] * c_j + jnp.dot(lhs, k, preferred_element_type=F32)
            st["inter"] = inter

        def output_stage():
            q, k, v = q_ref[:, qs], k_ref[:, qs], v_ref[:, vs]
            s, e_int, m_t = st.pop("s"), st.pop("e_int"), st.pop("m_t")
            n_rows = n_ref[:, qs]
            qn = jnp.sum(q.astype(F32) * n_rows, axis=1, keepdims=True)
            num = jnp.dot(s.astype(BF16), v, preferred_element_type=F32) + e_int * st.pop("inter")
            den = jnp.sum(s, axis=1, keepdims=True) + e_int * qn
            hh = num / jnp.maximum(jnp.abs(den), jnp.exp(-m_t))
            mu = jnp.mean(hh, axis=1, keepdims=True)
            dlt = hh - mu
            var = jnp.mean(dlt * dlt, axis=1, keepdims=True)
            y = dlt * lax.rsqrt(var + HEAD_NORM_EPS)
            out_ref[:, vs] = (y * ng_ref[:, vs] * mo_ref[:, vs].astype(F32)).astype(BF16)
            n_ref[:, qs] = (st.pop("e_c") * n_rows
                            + jnp.dot(st.pop("w_mat").astype(BF16), k, preferred_element_type=F32))

        return (gate_stage, score_stage, state_stage, output_stage), st

    heads = [head_stages(hl) for hl in range(hps)]
    for stage in range(4):
        if between is not None:
            between()
        for stages, _ in heads:
            stages[stage]()
    for hl, (_, st) in enumerate(heads):
        m_next = jnp.where(lane == head0 + hl, st["m_new"], m_next)

    m_ref[...] = m_next


def _mlstm(q, k, v, gates, mo, norm_g, state, nb, tpb, n_chunks, hps):
    m = q.shape[0]
    L = nb * tpb
    n_blocks = m // (L * n_chunks)
    hd = M_HEADS * M_DQK

    def tok(width):
        return pl.BlockSpec((L, width), lambda b, g, c: (b * n_chunks + c, g))

    c_spec = pl.BlockSpec((nb, hps, M_DV, M_DQK), lambda b, g, c: (b, g, 0, 0))
    n_spec = pl.BlockSpec((L, hps * M_DQK), lambda b, g, c: (b, g))
    m_spec = pl.BlockSpec((L, LANES), lambda b, g, c: (b, 0))
    in_specs = [tok(hps * M_DQK), tok(hps * M_DQK), tok(hps * M_DV),
                pl.BlockSpec((L, LANES), lambda b, g, c: (b * n_chunks + c, 0)),
                tok(hps * M_DV), pl.BlockSpec((1, hps * M_DV), lambda b, g, c: (0, g)),
                c_spec, n_spec, m_spec]
    args = [q, k, v, gates, mo, norm_g.reshape(1, M_HEADS * M_DV), *state]
    return pl.pallas_call(
        functools.partial(_mlstm_kernel, nb=nb, tpb=tpb, hps=hps),
        grid=(n_blocks, M_HEADS // hps, n_chunks),
        in_specs=in_specs,
        out_specs=[tok(hps * M_DV), c_spec, n_spec, m_spec],
        out_shape=[jax.ShapeDtypeStruct((m, M_HEADS * M_DV), BF16),
                   jax.ShapeDtypeStruct((n_blocks * nb, M_HEADS, M_DV, M_DQK), F32),
                   jax.ShapeDtypeStruct((n_blocks * L, hd), F32),
                   jax.ShapeDtypeStruct((n_blocks * L, LANES), F32)],
        scratch_shapes=[pltpu.VMEM((LANES, L), F32), pltpu.VMEM((LANES, L), F32)],
        compiler_params=_params(),
        name="mlstm",
    )(*args)


def _swa_kernel(sink_ref, q_ref, kp_ref, kc_ref, vp_ref, vc_ref, o_ref, kn_ref, vn_ref, *, nbb, n_new):
    tq = SAMPLE_TQ
    qi = lax.broadcasted_iota(jnp.int32, (tq, WINDOW), 0)
    kj = lax.broadcasted_iota(jnp.int32, (tq, WINDOW), 1)
    keep_old = lax.broadcasted_iota(jnp.int32, (WINDOW, W_AKV), 0) < WINDOW - n_new

    def new_rows(ref, jb, rows):
        x = ref[jb * n_new:(jb + 1) * n_new, :]
        return jnp.concatenate([x, jnp.zeros((rows - n_new, x.shape[1]), x.dtype)], axis=0)

    blocks = []
    for jb in range(nbb):
        kc, vc = new_rows(kc_ref, jb, WINDOW), new_rows(vc_ref, jb, WINDOW)
        kp, vp = kp_ref[jb], vp_ref[jb]
        kn_ref[jb] = jnp.where(keep_old, pltpu.roll(kp, WINDOW - n_new, 0), pltpu.roll(kc, WINDOW - n_new, 0))
        vn_ref[jb] = jnp.where(keep_old, pltpu.roll(vp, WINDOW - n_new, 0), pltpu.roll(vc, WINDOW - n_new, 0))
        q = new_rows(q_ref, jb, tq).astype(BF16)
        blocks.append(_swa_stages(sink_ref, q, kp, vp, kc, vc, kj > qi, o_ref, jb * n_new, n_new))

    for stage in range(3):
        for pairs in blocks:
            for pair in pairs:
                pair[stage]()


def _swa_stages(sink_ref, q, kp, vp, kc, vc, valid_prev, o_ref, o_row0=0, o_rows=None):
    tq = q.shape[0]
    o_rows = tq if o_rows is None else o_rows
    reps = A_HEADS // A_KV_HEADS
    low_q =lax.broadcasted_iota(jnp.int32, (tq, LANES), 1) < A_DH
    low_k = lax.broadcasted_iota(jnp.int32, (WINDOW, LANES), 1) < A_DH
    key_row = lax.broadcasted_iota(jnp.int32, (WINDOW, LANES), 0)
    qi = lax.broadcasted_iota(jnp.int32, (tq, WINDOW), 0)
    kj = lax.broadcasted_iota(jnp.int32, (tq, WINDOW), 1)
    sink_lane = kj == 0
    valid_cur = kj <= qi
    scale = jnp.asarray(ATTN_SCALE, BF16)
    neg_inf = -jnp.inf

    def block_diag(x):
        return jnp.concatenate([jnp.where(low_k, x, 0.0), jnp.where(low_k, 0.0, x)], axis=0).astype(BF16)

    def pair_stages(p):
        ks = slice(p * LANES, (p + 1) * LANES)
        state = {}

        def scores():
            q4 = jnp.concatenate([q[:, (reps * p + r) * LANES:(reps * p + r + 1) * LANES] * scale
                                  for r in range(reps)], axis=0)
            state["sp"] = lax.dot_general(q4, block_diag(kp[:, ks]), _NT, preferred_element_type=F32)
            state["sc"] = lax.dot_general(q4, block_diag(kc[:, ks]), _NT, preferred_element_type=F32)

        def softmax():
            sp, sc = state.pop("sp"), state.pop("sc")
            pps, pcs, invs = [], [], []
            for r in range(reps):
                rows = slice(r * tq, (r + 1) * tq)
                pp_r, pc_r, inv_r = [], [], []
                for half in range(2):
                    cols = slice(half * WINDOW, (half + 1) * WINDOW)
                    head = 2 * reps * p + reps * half + r
                    sp_i = jnp.where(sink_lane, sink_ref[head], jnp.where(valid_prev, sp[rows, cols], neg_inf))
                    sc_i = jnp.where(valid_cur, sc[rows, cols], neg_inf)
                    mx = jnp.max(jnp.maximum(sp_i, sc_i), axis=1, keepdims=True)
                    pp = jnp.exp(sp_i - mx)
                    pc = jnp.exp(sc_i - mx)
                    inv_r.append(1.0 / jnp.sum(pp + pc, axis=1, keepdims=True))
                    pp_r.append(pp.astype(BF16))
                    pc_r.append(pc.astype(BF16))
                pps.append(jnp.concatenate(pp_r, axis=1))
                pcs.append(jnp.concatenate(pc_r, axis=1))
                invs.append(jnp.where(low_q, inv_r[0], inv_r[1]))
            state.update(pp=jnp.concatenate(pps, axis=0), pc=jnp.concatenate(pcs, axis=0), inv=invs)

        def values():
            v_prev = block_diag(jnp.where(key_row == 0, 0.0, vp[:, ks]))
            o4 = (jnp.dot(state.pop("pp"), v_prev, preferred_element_type=F32)
                  + jnp.dot(state.pop("pc"), block_diag(vc[:, ks]), preferred_element_type=F32))
            invs = state.pop("inv")
            for r in range(reps):
                blk = reps * p + r
                o_ref[o_row0:o_row0 + o_rows, blk * LANES:(blk + 1) * LANES] = (
                    o4[r * tq:r * tq + o_rows] * invs[r][:o_rows]).astype(o_ref.dtype)

        return scores, softmax, values

    return [pair_stages(p) for p in range(A_KV_HEADS // 2)]


def _swa_block(sink_ref, q, kp, vp, kc, vc, valid_prev, o_ref, between=None):
    for scores, softmax, values in _swa_stages(sink_ref, q, kp, vp, kc, vc, valid_prev, o_ref):
        if between is not None:
            between()
        scores()
        softmax()
        if between is not None:
            between()
        values()


def _swa(sinks, q, k_cache, k_new, v_cache, v_new, *, nbb, n_new):
    n = k_cache.shape[0]
    assert n_new <= SAMPLE_TQ and q.shape[0] == n * n_new
    cache_spec = pl.BlockSpec((nbb, WINDOW, W_AKV), lambda i: (i, 0, 0))
    new_spec = pl.BlockSpec((nbb * n_new, W_AKV), lambda i: (i, 0))
    q_spec = pl.BlockSpec((nbb * n_new, W_AQ), lambda i: (i, 0))
    return pl.pallas_call(
        functools.partial(_swa_kernel, nbb=nbb, n_new=n_new),
        grid=(n // nbb,),
        in_specs=[pl.BlockSpec(memory_space=pltpu.SMEM), q_spec, cache_spec, new_spec, cache_spec, new_spec],
        out_specs=[q_spec, cache_spec, cache_spec],
        out_shape=[jax.ShapeDtypeStruct((n * n_new, W_AQ), F32)]
        + [jax.ShapeDtypeStruct((n, WINDOW, W_AKV), F32)] * 2,
        compiler_params=_params(),
        name="swa",
    )(sinks, q, k_cache, k_new, v_cache, v_new)


def _merge_kernel(x_ref, g_ref, hm_ref, oa_ref, gm_ref, ga_ref, wm_ref, wa_ref, wo_ref,
                  lg_ref, lb_ref, o_ref):
    ym = jnp.dot(hm_ref[...], wm_ref[...], preferred_element_type=F32)
    ya = jnp.dot(oa_ref[...].astype(BF16), wa_ref[...], preferred_element_type=F32)
    mix = gm_ref[...].astype(F32) * ym + ga_ref[...].astype(F32) * ya
    t = jnp.dot(mix.astype(BF16), wo_ref[...], preferred_element_type=F32)
    y = DEEPNORM_ALPHA * x_ref[...] + (1.0 + _rows(g_ref)) * t
    o_ref[...] = _layer_norm(y, lg_ref[...], lb_ref[...], LN_EPS)


def _merge(x, mod, per_row, rows_per_batch, chunk, hm, oa, gm, ga, wm, wa, wo, ln_g, ln_b):
    m = x.shape[0]
    tm = ROW_TILE

    def tok():
        return pl.BlockSpec((tm, D_MODEL), lambda i: (i, 0))

    return pl.pallas_call(
        _merge_kernel,
        grid=(m // tm,),
        in_specs=[tok()] + _mod_specs(per_row, tm, rows_per_batch // tm, (chunk,))
        + [tok(), tok(), tok(), tok()]
        + [_resident((D_MODEL, D_MODEL))] * 3 + [_resident((1, D_MODEL))] * 2,
        out_specs=tok(),
        out_shape=jax.ShapeDtypeStruct((m, D_MODEL), F32),
        compiler_params=_params(),
        name="merge",
    )(x, mod, hm, oa, gm, ga, wm, wa, wo, ln_g.reshape(1, D_MODEL), ln_b.reshape(1, D_MODEL))


def _mixer_kernel(sink_ref, xa_ref, xb_ref, sh_ref, sc_ref, g_ref, wa_ref, wq_ref, wr_ref, wif_ref, bif_ref,
                  cos_ref, sin_ref, ng_ref, wm_ref, wba_ref, wo_ref, lg_ref, lb_ref,
                  y_ref, c_ref, n_ref, m_ref, kb_ref, vb_ref, *scratch, tiles_per_seq):
    n_staged = (len(scratch) - 6) // 2
    even, odd = scratch[:n_staged], scratch[n_staged:2 * n_staged]
    kprev_s, vprev_s, hm_s, oa_s, gtt_s, ctt_s = scratch[2 * n_staged:]
    tm = xa_ref.shape[0]
    s = pl.program_id(0)
    tile_b = jnp.maximum(s - 1, 0)
    seq_start = tile_b % tiles_per_seq == 0

    @pl.when(s == 0)
    def _():
        for ref in odd:
            ref[...] = jnp.zeros_like(ref)

    @pl.when(seq_start)
    def _():
        c_ref[...] = jnp.zeros_like(c_ref)
        n_ref[...] = jnp.zeros_like(n_ref)
        m_ref[...] = jnp.zeros_like(m_ref)
        kprev_s[...] = jnp.zeros_like(kprev_s)
        vprev_s[...] = jnp.zeros_like(vprev_s)

    def step(produce, consume):
        proj_parts = _proj_parts(xa_ref, sh_ref, sc_ref, wa_ref, wq_ref, wr_ref, wif_ref, bif_ref,
                                 cos_ref, sin_ref, *produce)
        pending = iter(proj_parts)

        counts = iter(MIXER_PROJ_SCHEDULE)

        def between():
            for _ in range(next(counts)):
                part = next(pending, None)
                if part is not None:
                    part()

        q_s, k_s, v_s, mo_s, aq_s, ak_s, av_s, gm_s, ga_s, gt_s = consume
        qi = lax.broadcasted_iota(jnp.int32, (WINDOW, WINDOW), 0)
        kj = lax.broadcasted_iota(jnp.int32, (WINDOW, WINDOW), 1)
        for j in range(tm // MLSTM_CHUNK):
            rows = pl.ds(j * MLSTM_CHUNK, MLSTM_CHUNK)
            _mlstm_chunk(q_s.at[rows], k_s.at[rows], v_s.at[rows], gt_s.at[rows], mo_s.at[rows], ng_ref,
                         hm_s.at[rows], c_ref, n_ref, m_ref, gtt_s, ctt_s,
                         nb=1, tpb=MLSTM_CHUNK, hps=M_HEADS, head0=0, between=between)
            valid_prev = kj > qi
            if j == 0:
                valid_prev = valid_prev & jnp.logical_not(seq_start)
                k_prev, v_prev = kprev_s[...], vprev_s[...]
            else:
                before = pl.ds((j - 1) * WINDOW, WINDOW)
                k_prev, v_prev = ak_s[before], av_s[before]
            _swa_block(sink_ref, aq_s[rows], k_prev, v_prev, ak_s[rows], av_s[rows], valid_prev,
                       oa_s.at[rows], between=between)

        for part in pending:
            part()
        _merge_kernel(xb_ref, g_ref, hm_s, oa_s, gm_s, ga_s, wm_ref, wba_ref, wo_ref, lg_ref, lb_ref, y_ref)

        last = pl.ds(tm - WINDOW, WINDOW)
        k_last, v_last = ak_s[last], av_s[last]
        kb_ref[0] = k_last.T
        vb_ref[0] = v_last.T
        kprev_s[...] = k_last
        vprev_s[...] = v_last

    @pl.when(s % 2 == 0)
    def _():
        step(even, odd)

    @pl.when(s % 2 == 1)
    def _():
        step(odd, even)


def _mixer(x1, mod, w, sinks, norm_g, cos_t, sin_t, bp, sp):
    tm = MIXER_TILE
    tps = sp // tm
    n_tiles = bp * tps
    hd = M_HEADS * M_DQK
    weights = (w["w_a"], w["w_aq"], w["w_r"], w["w_if"], w["bif"])

    def tile_a(s):
        return jnp.minimum(s, n_tiles - 1)

    def tile_b(s):
        return jnp.maximum(s - 1, 0)

    def mod_spec(tile, chunk):
        return pl.BlockSpec((1, 1, D_MODEL), lambda s: (tile(s) // tps, 0, chunk))

    def per_seq(*shape):
        return pl.BlockSpec((1,) + shape, lambda s: (tile_b(s) // tps,) + (0,) * len(shape))

    pos = pl.BlockSpec((tm, LANES), lambda s: (tile_a(s) % tps, 0))
    staged = [(W_MQ, BF16), (W_MQ, BF16), (W_MV, BF16), (W_MV, BF16), (W_AQ, BF16),
              (W_AKV, F32), (W_AKV, F32), (D_MODEL, BF16), (D_MODEL, BF16), (LANES, F32)]
    return pl.pallas_call(
        functools.partial(_mixer_kernel, tiles_per_seq=tps),
        grid=(n_tiles + 1,),
        in_specs=[pl.BlockSpec(memory_space=pltpu.SMEM),
                  pl.BlockSpec((tm, D_MODEL), lambda s: (tile_a(s), 0)),
                  pl.BlockSpec((tm, D_MODEL), lambda s: (tile_b(s), 0)),
                  mod_spec(tile_a, 3), mod_spec(tile_a, 4), mod_spec(tile_b, 5)]
        + [_resident(a.shape) for a in weights] + [pos, pos, _resident((1, M_HEADS * M_DV))]
        + [_resident((D_MODEL, D_MODEL))] * 3 + [_resident((1, D_MODEL))] * 2,
        out_specs=[pl.BlockSpec((tm, D_MODEL), lambda s: (tile_b(s), 0)),
                   per_seq(M_HEADS, M_DV, M_DQK),
                   pl.BlockSpec((MLSTM_CHUNK, hd), lambda s: (tile_b(s) // tps, 0)),
                   pl.BlockSpec((MLSTM_CHUNK, LANES), lambda s: (tile_b(s) // tps, 0)),
                   per_seq(W_AKV, WINDOW), per_seq(W_AKV, WINDOW)],
        out_shape=[jax.ShapeDtypeStruct((bp * sp, D_MODEL), F32),
                   jax.ShapeDtypeStruct((bp, M_HEADS, M_DV, M_DQK), F32),
                   jax.ShapeDtypeStruct((bp * MLSTM_CHUNK, hd), F32),
                   jax.ShapeDtypeStruct((bp * MLSTM_CHUNK, LANES), F32),
                   jax.ShapeDtypeStruct((bp, W_AKV, WINDOW), F32),
                   jax.ShapeDtypeStruct((bp, W_AKV, WINDOW), F32)],
        scratch_shapes=[pltpu.VMEM((tm, width), dtype) for width, dtype in staged + staged]
        + [pltpu.VMEM((WINDOW, W_AKV), F32), pltpu.VMEM((WINDOW, W_AKV), F32),
           pltpu.VMEM((tm, W_MV), BF16), pltpu.VMEM((tm, W_AQ), BF16),
           pltpu.VMEM((LANES, MLSTM_CHUNK), F32), pltpu.VMEM((LANES, MLSTM_CHUNK), F32)],
        compiler_params=_params(),
        name="mixer",
    )(sinks, x1, x1, mod, mod, mod, *weights, cos_t, sin_t, norm_g.reshape(1, M_HEADS * M_DV),
      w["wm"], w["wa"], w["wo"], w["ln2_g"].reshape(1, D_MODEL), w["ln2_b"].reshape(1, D_MODEL))


def _rope_tables(pos):
    half = A_DH // 2
    inv = np.float32(ROPE_THETA) ** (-np.arange(half, dtype=np.float32) / np.float32(half))
    ang = pos.astype(np.float32)[:, None] * inv[None, :]
    cos, sin = np.cos(ang), np.sin(ang)
    return jnp.asarray(np.tile(cos, (1, 4))), jnp.asarray(np.concatenate([-sin, sin, -sin, sin], axis=1))


def _from_stored(kv_t):
    return jnp.transpose(kv_t.reshape(kv_t.shape[0], A_KV_HEADS, A_DH, WINDOW), (0, 3, 1, 2))


def _token_stage_1(x, mod, per_row, rows_per_batch, w, pos):
    x1 = _ffn(x, mod, per_row, rows_per_batch, (0, 1, 2), w["up1"], w["down1"], w["ln1_g"], w["ln1_b"])
    cos_t, sin_t = _rope_tables(pos)
    return x1, _proj(x1, mod, per_row, rows_per_batch, (3, 4), w, cos_t, sin_t)


def _token_stage_2(x1, mod, per_row, rows_per_batch, w, hm, oa, gm, ga):
    x2 = _merge(x1, mod, per_row, rows_per_batch, 5, hm, oa, gm, ga,
                w["wm"], w["wa"], w["wo"], w["ln2_g"], w["ln2_b"])
    return _ffn(x2, mod, per_row, rows_per_batch, (6, 7, 8), w["up2"], w["down2"], w["ln3_g"], w["ln3_b"])


def kernel(x_prompt, x_sample, state_mlstm_C, state_mlstm_n, state_mlstm_m, cache_swa_k, cache_swa_v, c_prompt, c_sample, w_ada, b_ada, w_ffn1_up, w_ffn1_down, ln1_g, ln1_b, w_in, b_igate, b_fgate, m_norm_g, sinks, w_branch_m, w_branch_a, w_out, ln2_g, ln2_b, w_ffn2_up, w_ffn2_down, ln3_g, ln3_b):
    assert w_ada.shape[0] == DEPTH == 1
    bp, sp, _ = x_prompt.shape
    bs, ts, _ = x_sample.shape

    w_in_t = jnp.transpose(w_in[0])
    w = dict(
        up1=w_ffn1_up[0].astype(BF16), down1=w_ffn1_down[0].astype(BF16),
        bif=jnp.concatenate([b_igate[0], b_fgate[0], jnp.zeros((LANES - 2 * M_HEADS,), F32)]).reshape(1, LANES),
        ln1_g=ln1_g[0], ln1_b=ln1_b[0], ln2_g=ln2_g[0], ln2_b=ln2_b[0], ln3_g=ln3_g[0], ln3_b=ln3_b[0],
    )

    ms = bs * ts
    c_all = jnp.concatenate([jnp.repeat(c_sample, ts, axis=0), c_prompt], axis=0)
    mod = _ada(c_all, w_ada[0], b_ada[0])
    mod_p = mod[ms:].reshape(bp, 1, ADA_CHUNKS * D_MODEL)

    mp = bp * sp
    reps = A_HEADS // A_KV_HEADS
    pair_low = lambda j: (j // reps) * (2 * reps) + j % reps
    pair_rows = lambda j: (j // (2 * reps)) * (2 * reps) + (j % 2) * reps + (j % (2 * reps)) // 2
    jobs = (
        _cast_job(w_ffn2_up[0], (D_MODEL, FF_CHUNK), 1),
        _cast_job(w_ffn2_down[0], (FF_CHUNK, D_MODEL), 0, first_step=2 * D_FF // FF_CHUNK),
        _cast_job(w_in_t, (D_MODEL, LANES), 1, transposed=True, n_blocks=IN_IF // LANES),
        _cast_job(w_in_t, (D_MODEL, LANES), 1, transposed=True, n_blocks=W_AQ // LANES, shift=IN_AQ - IN_IF,
                  sources=(lambda j: IN_IF // A_DH + pair_low(j), lambda j: IN_IF // A_DH + pair_low(j) + reps)),
        _cast_job(w_in_t, (D_MODEL, LANES), 1, transposed=True, n_blocks=(IN_END - IN_AK) // LANES,
                  shift=IN_AQ - IN_IF, sources=(lambda j: (IN_AK - (IN_AQ - IN_IF)) // LANES + j,)),
        _cast_job(w_branch_m[0], (D_MODEL, LANES), 1),
        _cast_job(w_out[0], (D_MODEL, LANES), 1),
        _cast_job(w_branch_a[0], (A_DH, D_MODEL), 0, sources=(pair_rows,)),
        _cast_job(w_in_t, (LANES, D_MODEL), 0, n_blocks=1, sources=(lambda j: IN_IF // LANES + j,)),
    )
    x1p, w["up2"], w["down2"], w["w_a"], w["w_aq"], w["w_r"], w["wm"], w["wo"], w["wa"], w["w_if"] = _ffn(
        x_prompt.reshape(mp, D_MODEL), mod_p, False, sp, (0, 1, 2),
        w["up1"], w["down1"], w["ln1_g"], w["ln1_b"], cast_jobs=jobs)
    cos_t, sin_t = _rope_tables(np.arange(sp))
    x2p, c_p, n_rows, m_rows, kb_p, vb_p = _mixer(x1p, mod_p, w, sinks[0], m_norm_g[0], cos_t, sin_t, bp, sp)
    y_p = _ffn(x2p, mod_p, False, sp, (6, 7, 8), w["up2"], w["down2"], w["ln3_g"], w["ln3_b"])
    n_p = n_rows.reshape(bp, MLSTM_CHUNK, M_HEADS, M_DQK)[:, 0]
    m_p = m_rows.reshape(bp, MLSTM_CHUNK, LANES)[:, 0, :M_HEADS]
    kb_p, vb_p = _from_stored(kb_p), _from_stored(vb_p)

    x1s, (qm, km, vm, mo, aq, ak, av, gm, ga, gt) = _token_stage_1(
        x_sample.reshape(ms, D_MODEL), mod, True, ms, w, PAST_LEN + np.arange(ms) % ts)
    seqs = MLSTM_CHUNK // ts
    n0_rows = jnp.repeat(state_mlstm_n[0].reshape(bs, M_HEADS * M_DQK), ts, axis=0)
    m0_rows = jnp.repeat(jnp.pad(state_mlstm_m[0], ((0, 0), (0, LANES - M_HEADS))), ts, axis=0)
    hm, c_s, n_rows, m_rows = _mlstm(qm, km, vm, gt, mo, m_norm_g[0],
                                     (state_mlstm_C[0], n0_rows, m0_rows), seqs, ts, 1, 2)
    n_s = n_rows.reshape(bs, ts, M_HEADS, M_DQK)[:, 0]
    m_s = m_rows.reshape(bs, ts, LANES)[:, 0, :M_HEADS]
    oa, kb_s, vb_s = _swa(
        sinks[0], aq, cache_swa_k[0].reshape(bs, WINDOW, W_AKV), ak,
        cache_swa_v[0].reshape(bs, WINDOW, W_AKV), av, nbb=SWA_SAMPLE_SEQS, n_new=ts)
    y_s = _token_stage_2(x1s, mod, True, ms, w, hm, oa, gm, ga)
    kb_s = kb_s.reshape(bs, WINDOW, A_KV_HEADS, A_DH)
    vb_s = vb_s.reshape(bs, WINDOW, A_KV_HEADS, A_DH)

    return (y_p.reshape(bp, sp, D_MODEL), y_s.reshape(bs, ts, D_MODEL),
            c_p[None], n_p[None], m_p[None], kb_p[None], vb_p[None],
            c_s[None], n_s[None], m_s[None], kb_s[None], vb_s[None])
```

```python
import functools

import jax
import jax.numpy as jnp
import numpy as np
from jax import lax
from jax.experimental import pallas as pl
from jax.experimental.pallas import tpu as pltpu

F32 = jnp.float32
BF16 = jnp.bfloat16

D_MODEL = 1024
D_FF = 2816
DEPTH = 1
M_HEADS = 4
M_DQK = 128
M_DV = 256
A_HEADS = 16
A_KV_HEADS = 4
A_DH = 64
WINDOW = 128
PAST_LEN = 8192
ROPE_THETA = 10000.0
ATTN_SCALE = A_DH ** -0.5
LN_EPS = 1e-5
HEAD_NORM_EPS = 1e-6
ADA_CHUNKS = 9
DEEPNORM_ALPHA = (2.0 * DEPTH) ** 0.25
K_SCALE = M_DQK ** -0.5

LANES = 128
BF16_SUBLANES = 16
VMEM_LIMIT_BYTES = 56 * 1024 * 1024

W_MQ = M_HEADS * M_DQK
W_MV = M_HEADS * M_DV
W_AQ = A_HEADS * A_DH
W_AKV = A_KV_HEADS * A_DH
IN_IF = 2 * W_MQ + 2 * W_MV
IN_AQ = IN_IF + 2 * M_HEADS
IN_AK = IN_AQ + W_AQ
IN_END = IN_AK + 2 * W_AKV + 2 * D_MODEL
A_Q, A_K, A_V, A_O = 0, W_MQ, 2 * W_MQ, 2 * W_MQ + W_MV
R_AK, R_AV, R_GM, R_GA = 0, W_AKV, 2 * W_AKV, 2 * W_AKV + D_MODEL

ROW_TILE = 512
MIXER_TILE = 256
MIXER_PROJ_SCHEDULE = (1, 1, 1, 1, 2, 2, 2, 2) * 2
ADA_TILE = 1536
FF_CHUNK = 256
PROJ_CHUNK = 2 * LANES
FFN_EPILOGUE_PIECES = 4
MLSTM_CHUNK = 128
SAMPLE_TQ = BF16_SUBLANES
SWA_SAMPLE_SEQS = 16

_NT = (((1,), (1,)), ((), ()))


def _params():
    return pltpu.CompilerParams(vmem_limit_bytes=VMEM_LIMIT_BYTES)


def _resident(shape):
    return pl.BlockSpec(shape, lambda *_: (0,) * len(shape), pipeline_mode=pl.Buffered(1))


def _rows(ref):
    v = ref[...]
    return v.reshape(v.shape[-2], v.shape[-1])


def _layer_norm(y, g, b, eps):
    mu = jnp.mean(y, axis=-1, keepdims=True)
    d = y - mu
    var = jnp.mean(d * d, axis=-1, keepdims=True)
    return d * lax.rsqrt(var + eps) * g + b


def _sigmoid(x):
    return 1.0 / (1.0 + jnp.exp(-x))


def _mod_specs(per_row, tm, tiles_per_batch, chunks):
    if per_row:
        return [pl.BlockSpec((tm, D_MODEL), lambda i, c=c: (i, c)) for c in chunks]
    return [pl.BlockSpec((1, 1, D_MODEL), lambda i, c=c: (i // tiles_per_batch, 0, c)) for c in chunks]


def _ada_kernel(c_ref, w_ref, b_ref, o_ref):
    c = c_ref[...]
    s = (c * _sigmoid(c)).astype(BF16)
    o_ref[...] = jnp.dot(s, w_ref[...].astype(BF16), preferred_element_type=F32) + b_ref[...]


def _ada(c_all, w_ada, b_ada):
    rows = c_all.shape[0]
    n_out = w_ada.shape[1]
    tn = ADA_TILE
    return pl.pallas_call(
        _ada_kernel,
        grid=(n_out // tn,),
        in_specs=[pl.BlockSpec((rows, D_MODEL), lambda j: (0, 0)),
                  pl.BlockSpec((D_MODEL, tn), lambda j: (0, j)),
                  pl.BlockSpec((1, tn), lambda j: (0, j))],
        out_specs=pl.BlockSpec((rows, tn), lambda j: (0, j)),
        out_shape=jax.ShapeDtypeStruct((rows, n_out), F32),
        compiler_params=_params(),
        name="ada",
    )(c_all, w_ada, b_ada.reshape(1, n_out))


def _ffn_kernel(x_ref, xp_ref, sh_ref, sc_ref, g_ref, wup_ref, wdn_ref, lg_ref, lb_ref, *rest,
                n_tiles, cast_layout):
    n_cast = len(cast_layout)
    n_src = sum(layout[0] for layout in cast_layout)
    cast_in, o_ref, cast_out = rest[:n_src], rest[n_src], rest[n_src + 1:n_src + 1 + n_cast]
    act_ref, f_ref = rest[n_src + 1 + n_cast:]

    def cast_chunk(job):
        if job >= n_cast:
            return
        first = sum(layout[0] for layout in cast_layout[:job])
        n, transposed, shift = cast_layout[job]
        refs = cast_in[first:first + n]
        if shift:
            rows = refs[0].shape[0]
            parts = [jnp.concatenate([lo[...], hi[...]], axis=0)[shift:shift + rows]
                     for lo, hi in zip(refs[0::2], refs[1::2])]
        else:
            parts = [ref[...] for ref in refs]
        block = jnp.concatenate(parts, axis=0)
        cast_out[job][...] = (block.T if transposed else block).astype(BF16)

    tm = x_ref.shape[0]
    s = pl.program_id(0)
    piece_rows = tm // FFN_EPILOGUE_PIECES

    def epilogue(piece):
        rows = pl.ds(piece * piece_rows, piece_rows)
        g = _rows(g_ref)
        if g.shape[0] != 1:
            g = g[piece * piece_rows:(piece + 1) * piece_rows]
        y = DEEPNORM_ALPHA * xp_ref[rows, :] + (0.5 * (1.0 + g)) * f_ref[rows, :]
        o_ref[rows, :] = _layer_norm(y, lg_ref[...], lb_ref[...], LN_EPS)

    def matmuls(previous_epilogue):
        h = (x_ref[...] * (1.0 + _rows(sc_ref)) + _rows(sh_ref)).astype(BF16)
        chunks = list(range(0, D_FF, FF_CHUNK))
        every = len(chunks) // FFN_EPILOGUE_PIECES
        piece = 0
        for i, c in enumerate(chunks):
            a = jnp.dot(h, wup_ref[:, c:c + FF_CHUNK], preferred_element_type=F32)
            u = jnp.dot(h, wup_ref[:, D_FF + c:D_FF + c + FF_CHUNK], preferred_element_type=F32)
            act_ref[:, c:c + FF_CHUNK] = (a * _sigmoid(a) * u).astype(BF16)
            cast_chunk(i)
            if previous_epilogue and i % every == every - 1 and piece < FFN_EPILOGUE_PIECES:
                epilogue(piece)
                piece += 1
        f_ref[...] = jnp.dot(act_ref[...], wdn_ref[...], preferred_element_type=F32)

    if n_tiles == 1:
        matmuls(False)
        for piece in range(FFN_EPILOGUE_PIECES):
            epilogue(piece)
        return

    @pl.when(s == 0)
    def _():
        f_ref[...] = jnp.zeros_like(f_ref)

    @pl.when(s < n_tiles)
    def _():
        matmuls(True)

    @pl.when(s == n_tiles)
    def _():
        for job in range(n_cast):
            cast_chunk(job)
        for piece in range(FFN_EPILOGUE_PIECES):
            epilogue(piece)


def _cast_job(array, block, axis, first_step=0, sources=None, transposed=False, n_blocks=None, shift=0):
    sources = sources or (lambda j: j,)
    src_axis = 1 - axis if transposed else axis
    src_block = block[::-1] if transposed else block
    src_block = (src_block[0] // len(sources), src_block[1])
    if n_blocks is None:
        n_blocks = array.shape[src_axis] // block[axis]

    def step_block(s):
        return jnp.clip(s - first_step, 0, n_blocks - 1)

    def at(axis_, j):
        return (j, 0) if axis_ == 0 else (0, j)

    in_specs = [pl.BlockSpec(src_block, lambda s, f=f, d=d: at(src_axis, f(step_block(s)) + d))
                for f in sources for d in range(2 if shift else 1)]
    out_dims = list(array.shape[::-1] if transposed else array.shape)
    out_dims[axis] = n_blocks * block[axis]
    return dict(array=array, in_specs=in_specs, layout=(len(in_specs), transposed, shift),
                out_spec=pl.BlockSpec(block, lambda s: at(axis, step_block(s))),
                out_shape=jax.ShapeDtypeStruct(tuple(out_dims), BF16), steps=first_step + n_blocks)


def _ffn(x, mod, per_row, rows_per_batch, chunks, w_up, w_down, ln_g, ln_b, cast_jobs=()):
    m = x.shape[0]
    tm = ROW_TILE
    n_tiles = m // tm
    tpb = rows_per_batch // tm

    def cur(s):
        return jnp.minimum(s, n_tiles - 1)

    def prev(s):
        return jnp.maximum(s - 1, 0)

    def mod_spec(tile, chunk):
        if per_row:
            return pl.BlockSpec((tm, D_MODEL), lambda s: (tile(s), chunk))
        return pl.BlockSpec((1, 1, D_MODEL), lambda s: (tile(s) // tpb, 0, chunk))

    in_specs = [pl.BlockSpec((tm, D_MODEL), lambda s: (cur(s), 0)),
                pl.BlockSpec((tm, D_MODEL), lambda s: (prev(s), 0)),
                mod_spec(cur, chunks[0]), mod_spec(cur, chunks[1]), mod_spec(prev, chunks[2]),
                _resident((D_MODEL, 2 * D_FF)), _resident((D_FF, D_MODEL)),
                _resident((1, D_MODEL)), _resident((1, D_MODEL))]
    args = [x, x, mod, mod, mod, w_up, w_down, ln_g.reshape(1, D_MODEL), ln_b.reshape(1, D_MODEL)]
    out_specs = [pl.BlockSpec((tm, D_MODEL), lambda s: (prev(s), 0))]
    out_shape = [jax.ShapeDtypeStruct((m, D_MODEL), F32)]
    n_steps = 1 if n_tiles == 1 else n_tiles + 1
    for job in cast_jobs:
        assert job["steps"] <= n_steps
        in_specs += job["in_specs"]
        args += [job["array"]] * len(job["in_specs"])
        out_specs.append(job["out_spec"])
        out_shape.append(job["out_shape"])
    outs = pl.pallas_call(
        functools.partial(_ffn_kernel, n_tiles=n_tiles, cast_layout=tuple(j["layout"] for j in cast_jobs)),
        grid=(n_steps,),
        in_specs=in_specs,
        out_specs=out_specs,
        out_shape=out_shape,
        scratch_shapes=[pltpu.VMEM((tm, D_FF), BF16), pltpu.VMEM((tm, D_MODEL), F32)],
        compiler_params=_params(),
        name="ffn",
    )(*args)
    return outs if cast_jobs else outs[0]


def _proj_parts(x_ref, sh_ref, sc_ref, wa_ref, wq_ref, wr_ref, wif_ref, bif_ref, cos_ref, sin_ref,
                q_ref, k_ref, v_ref, o_ref, aq_ref, ak_ref, av_ref, gm_ref, ga_ref, gt_ref):
    x = x_ref[...]
    tm = x.shape[0]
    h = (x * (1.0 + _rows(sc_ref)) + _rows(sh_ref)).astype(BF16)
    lane = lax.broadcasted_iota(jnp.int32, (tm, LANES), 1)
    pc = PROJ_CHUNK

    def seg(w_ref, lo):
        return jnp.dot(h, w_ref[:, lo:lo + pc], preferred_element_type=F32)

    def plain(dst_ref, w_ref, lo, c, scale=None):
        def run():
            z = seg(w_ref, lo + c)
            dst_ref[:, c:c + pc] = (z if scale is None else z * scale).astype(dst_ref.dtype)
        return run

    def gate(dst_ref, w_ref, lo, c):
        def run():
            dst_ref[:, c:c + pc] = _sigmoid(seg(w_ref, lo + c)).astype(BF16)
        return run

    def forget_input_gates():
        zg = lax.dot_general(h, wif_ref[...], _NT, preferred_element_type=F32) + bif_ref[...]
        logsig = jnp.minimum(zg, 0.0) - jnp.log(1.0 + jnp.exp(-jnp.abs(zg)))
        gt_ref[...] = jnp.where(lane < M_HEADS, zg, logsig)

    def rotary(dst_ref, w_ref, lo, c):
        def run():
            cos = cos_ref[...]
            sin = sin_ref[...]
            low_half = (lane & (A_DH // 2)) == 0
            z = seg(w_ref, lo + c)
            for half in range(2):
                zh = z[:, half * LANES:(half + 1) * LANES]
                partner = jnp.where(low_half, pltpu.roll(zh, LANES - A_DH // 2, 1),
                                    pltpu.roll(zh, A_DH // 2, 1))
                dst_ref[:, c + half * LANES:c + (half + 1) * LANES] = (
                    zh * cos + partner * sin).astype(dst_ref.dtype)
        return run

    parts = [plain(q_ref, wa_ref, A_Q, c) for c in range(0, W_MQ, pc)]
    parts += [plain(k_ref, wa_ref, A_K, c, K_SCALE) for c in range(0, W_MQ, pc)]
    parts += [plain(v_ref, wa_ref, A_V, c) for c in range(0, W_MV, pc)]
    parts += [forget_input_gates]
    parts += [rotary(aq_ref, wq_ref, 0, c) for c in range(0, W_AQ, pc)]
    parts += [rotary(ak_ref, wr_ref, R_AK, 0), plain(av_ref, wr_ref, R_AV, 0)]
    parts += [gate(o_ref, wa_ref, A_O, c) for c in range(0, W_MV, pc)]
    parts += [gate(gm_ref, wr_ref, R_GM, c) for c in range(0, D_MODEL, pc)]
    parts += [gate(ga_ref, wr_ref, R_GA, c) for c in range(0, D_MODEL, pc)]
    return parts


def _proj_kernel(*refs):
    for part in _proj_parts(*refs):
        part()


def _proj(x, mod, per_row, rows_per_batch, chunks, w, cos_t, sin_t):
    m = x.shape[0]
    tm = ROW_TILE
    n_pos_tiles = cos_t.shape[0] // tm

    def tok(width):
        return pl.BlockSpec((tm, width), lambda i: (i, 0))

    widths = (W_MQ, W_MQ, W_MV, W_MV, W_AQ, W_AKV, W_AKV, D_MODEL, D_MODEL, LANES)
    dtypes = (BF16, BF16, BF16, BF16, F32, F32, F32, BF16, BF16, F32)
    weights = (w["w_a"], w["w_aq"], w["w_r"], w["w_if"], w["bif"])
    return pl.pallas_call(
        _proj_kernel,
        grid=(m // tm,),
        in_specs=[tok(D_MODEL)] + _mod_specs(per_row, tm, rows_per_batch // tm, chunks)
        + [_resident(a.shape) for a in weights]
        + [pl.BlockSpec((tm, LANES), lambda i: (i % n_pos_tiles, 0)),
           pl.BlockSpec((tm, LANES), lambda i: (i % n_pos_tiles, 0))],
        out_specs=[tok(wd) for wd in widths],
        out_shape=[jax.ShapeDtypeStruct((m, wd), d) for wd, d in zip(widths, dtypes)],
        compiler_params=_params(),
        name="proj",
    )(x, mod, mod, *weights, cos_t, sin_t)


def _mlstm_kernel(q_ref, k_ref, v_ref, g_ref, mo_ref, ng_ref, c0_ref, n0_ref, m0_ref,
                  out_ref, c_ref, n_ref, m_ref, gt_s, ct_s, *, nb, tpb, hps):
    first_chunk = pl.program_id(2) == 0

    @pl.when(first_chunk)
    def _():
        c_ref[...] = c0_ref[...]
        n_ref[...] = n0_ref[...]

    @pl.when(first_chunk & (pl.program_id(1) == 0))
    def _():
        m_ref[...] = m0_ref[...]

    head0 = 0 if hps == M_HEADS else pl.program_id(1) * hps
    _mlstm_chunk(q_ref, k_ref, v_ref, g_ref, mo_ref, ng_ref, out_ref, c_ref, n_ref, m_ref, gt_s, ct_s,
                 nb=nb, tpb=tpb, hps=hps, head0=head0)


def _mlstm_chunk(q_ref, k_ref, v_ref, g_ref, mo_ref, ng_ref, out_ref, c_ref, n_ref, m_ref, gt_s, ct_s,
                 *, nb, tpb, hps, head0, between=None):
    L = nb * tpb
    shift = tpb.bit_length() - 1
    row = lax.broadcasted_iota(jnp.int32, (L, L), 0)
    col = lax.broadcasted_iota(jnp.int32, (L, L), 1)
    same = (row >> shift) == (col >> shift)
    causal = same & (col <= row)
    lane = lax.broadcasted_iota(jnp.int32, (L, LANES), 1)
    row_seq = lax.broadcasted_iota(jnp.int32, (L, 1), 0) >> shift

    def lane_col(x, idx):
        return jnp.sum(jnp.where(lane == idx, x, 0.0), axis=1, keepdims=True)

    gates = g_ref[...]
    tri = jnp.where(causal, 1.0, 0.0).astype(BF16)
    g_hi = gates.astype(BF16)
    rem = gates - g_hi.astype(F32)
    g_mid = rem.astype(BF16)
    g_lo = (rem - g_mid.astype(F32)).astype(BF16)
    cum = (jnp.dot(tri, g_hi, preferred_element_type=F32)
           + jnp.dot(tri, g_mid, preferred_element_type=F32)
           + jnp.dot(tri, g_lo, preferred_element_type=F32))
    gt_s[...] = gates.T
    ct_s[...] = cum.T
    m_rows = m_ref[...]
    m_next = m_rows

    def head_stages(hl):
        head = head0 + hl
        qs = slice(hl * M_DQK, (hl + 1) * M_DQK)
        vs = slice(hl * M_DV, (hl + 1) * M_DV)
        st = {}

        def gate_stage():
            b_c = lane_col(cum, M_HEADS + head)
            m_p = lane_col(m_rows, head)
            i_r = gt_s[pl.ds(head, 1), :]
            b_r = ct_s[pl.ds(M_HEADS + head, 1), :]
            log_d = jnp.where(causal, b_c - b_r + i_r, -jnp.inf)
            m_t = jnp.maximum(b_c + m_p, jnp.max(log_d, axis=1, keepdims=True))
            dmat = jnp.exp(log_d - m_t)
            e_int = jnp.exp(b_c + m_p - m_t)
            if nb == 1:
                last = slice(L - 1, L)
                m_new = jnp.broadcast_to(m_t[last], (L, 1))
                e_c = jnp.broadcast_to(e_int[last], (L, 1))
                w_mat = jnp.broadcast_to(dmat[last], (L, L))
            else:
                b_last = jnp.min(jnp.where(same, b_r, jnp.inf), axis=1, keepdims=True)
                log_w = jnp.where(same, b_last - b_r + i_r, -jnp.inf)
                m_new = jnp.maximum(b_last + m_p, jnp.max(log_w, axis=1, keepdims=True))
                w_mat = jnp.exp(log_w - m_new)
                e_c = jnp.exp(b_last + m_p - m_new)
            st.update(m_t=m_t, dmat=dmat, e_int=e_int, m_new=m_new, e_c=e_c, w_mat=w_mat)

        def score_stage():
            q, k = q_ref[:, qs], k_ref[:, qs]
            st["s"] = lax.dot_general(q, k, _NT, preferred_element_type=F32) * st.pop("dmat")

        def state_stage():
            q, k = q_ref[:, qs], k_ref[:, qs]
            v_t = v_ref[:, vs].astype(F32).T
            w_mat = st["w_mat"]
            e_cb = jnp.broadcast_to(st["e_c"], (L, LANES))
            inter = jnp.zeros((L, M_DV), F32)
            for j in range(nb):
                first = slice(j * tpb, j * tpb + 1)
                c_j = c_ref[j, hl]
                q_j = q if nb == 1 else jnp.where(row_seq == j, q, jnp.zeros_like(q))
                inter = inter + lax.dot_general(q_j, c_j.astype(BF16), _NT, preferred_element_type=F32)
                lhs = (v_t * w_mat[first]).astype(BF16)
                c_ref[j, hl] = e_cb[first] * c_j + jnp.dot(lhs, k, preferred_element_type=F32)
            st["inter"] = inter

        def output_stage():
            q, k, v = q_ref[:, qs], k_ref[:, qs], v_ref[:, vs]
            s, e_int, m_t = st.pop("s"), st.pop("e_int"), st.pop("m_t")
            n_rows = n_ref[:, qs]
            qn = jnp.sum(q.astype(F32) * n_rows, axis=1, keepdims=True)
            num = jnp.dot(s.astype(BF16), v, preferred_element_type=F32) + e_int * st.pop("inter")
            den = jnp.sum(s, axis=1, keepdims=True) + e_int * qn
            hh = num / jnp.maximum(jnp.abs(den), jnp.exp(-m_t))
            mu = jnp.mean(hh, axis=1, keepdims=True)
            dlt = hh - mu
            var = jnp.mean(dlt * dlt, axis=1, keepdims=True)
            y = dlt * lax.rsqrt(var + HEAD_NORM_EPS)
            out_ref[:, vs] = (y * ng_ref[:, vs] * mo_ref[:, vs].astype(F32)).astype(BF16)
            n_ref[:, qs] = (st.pop("e_c") * n_rows
                            + jnp.dot(st.pop("w_mat").astype(BF16), k, preferred_element_type=F32))

        return (gate_stage, score_stage, state_stage, output_stage), st

    heads = [head_stages(hl) for hl in range(hps)]
    for stage in range(4):
        if between is not None:
            between()
        for stages, _ in heads:
            stages[stage]()
    for hl, (_, st) in enumerate(heads):
        m_next = jnp.where(lane == head0 + hl, st["m_new"], m_next)

    m_ref[...] = m_next


def _mlstm(q, k, v, gates, mo, norm_g, state, nb, tpb, n_chunks, hps):
    m = q.shape[0]
    L = nb * tpb
    n_blocks = m // (L * n_chunks)
    hd = M_HEADS * M_DQK

    def tok(width):
        return pl.BlockSpec((L, width), lambda b, g, c: (b * n_chunks + c, g))

    c_spec = pl.BlockSpec((nb, hps, M_DV, M_DQK), lambda b, g, c: (b, g, 0, 0))
    n_spec = pl.BlockSpec((L, hps * M_DQK), lambda b, g, c: (b, g))
    m_spec = pl.BlockSpec((L, LANES), lambda b, g, c: (b, 0))
    in_specs = [tok(hps * M_DQK), tok(hps * M_DQK), tok(hps * M_DV),
                pl.BlockSpec((L, LANES), lambda b, g, c: (b * n_chunks + c, 0)),
                tok(hps * M_DV), pl.BlockSpec((1, hps * M_DV), lambda b, g, c: (0, g)),
                c_spec, n_spec, m_spec]
    args = [q, k, v, gates, mo, norm_g.reshape(1, M_HEADS * M_DV), *state]
    return pl.pallas_call(
        functools.partial(_mlstm_kernel, nb=nb, tpb=tpb, hps=hps),
        grid=(n_blocks, M_HEADS // hps, n_chunks),
        in_specs=in_specs,
        out_specs=[tok(hps * M_DV), c_spec, n_spec, m_spec],
        out_shape=[jax.ShapeDtypeStruct((m, M_HEADS * M_DV), BF16),
                   jax.ShapeDtypeStruct((n_blocks * nb, M_HEADS, M_DV, M_DQK), F32),
                   jax.ShapeDtypeStruct((n_blocks * L, hd), F32),
                   jax.ShapeDtypeStruct((n_blocks * L, LANES), F32)],
        scratch_shapes=[pltpu.VMEM((LANES, L), F32), pltpu.VMEM((LANES, L), F32)],
        compiler_params=_params(),
        name="mlstm",
    )(*args)


def _swa_kernel(sink_ref, q_ref, kp_ref, kc_ref, vp_ref, vc_ref, o_ref, kn_ref, vn_ref, *, nbb, n_new):
    tq = SAMPLE_TQ
    qi = lax.broadcasted_iota(jnp.int32, (tq, WINDOW), 0)
    kj = lax.broadcasted_iota(jnp.int32, (tq, WINDOW), 1)
    keep_old = lax.broadcasted_iota(jnp.int32, (WINDOW, W_AKV), 0) < WINDOW - n_new

    def new_rows(ref, jb, rows):
        x = ref[jb * n_new:(jb + 1) * n_new, :]
        return jnp.concatenate([x, jnp.zeros((rows - n_new, x.shape[1]), x.dtype)], axis=0)

    blocks = []
    for jb in range(nbb):
        kc, vc = new_rows(kc_ref, jb, WINDOW), new_rows(vc_ref, jb, WINDOW)
        kp, vp = kp_ref[jb], vp_ref[jb]
        kn_ref[jb] = jnp.where(keep_old, pltpu.roll(kp, WINDOW - n_new, 0), pltpu.roll(kc, WINDOW - n_new, 0))
        vn_ref[jb] = jnp.where(keep_old, pltpu.roll(vp, WINDOW - n_new, 0), pltpu.roll(vc, WINDOW - n_new, 0))
        q = new_rows(q_ref, jb, tq).astype(BF16)
        blocks.append(_swa_stages(sink_ref, q, kp, vp, kc, vc, kj > qi, o_ref, jb * n_new, n_new))

    for stage in range(3):
        for pairs in blocks:
            for pair in pairs:
                pair[stage]()


def _swa_stages(sink_ref, q, kp, vp, kc, vc, valid_prev, o_ref, o_row0=0, o_rows=None):
    tq = q.shape[0]
    o_rows = tq if o_rows is None else o_rows
    reps = A_HEADS // A_KV_HEADS
    low_q =lax.broadcasted_iota(jnp.int32, (tq, LANES), 1) < A_DH
    low_k = lax.broadcasted_iota(jnp.int32, (WINDOW, LANES), 1) < A_DH
    key_row = lax.broadcasted_iota(jnp.int32, (WINDOW, LANES), 0)
    qi = lax.broadcasted_iota(jnp.int32, (tq, WINDOW), 0)
    kj = lax.broadcasted_iota(jnp.int32, (tq, WINDOW), 1)
    sink_lane = kj == 0
    valid_cur = kj <= qi
    scale = jnp.asarray(ATTN_SCALE, BF16)
    neg_inf = -jnp.inf

    def block_diag(x):
        return jnp.concatenate([jnp.where(low_k, x, 0.0), jnp.where(low_k, 0.0, x)], axis=0).astype(BF16)

    def pair_stages(p):
        ks = slice(p * LANES, (p + 1) * LANES)
        state = {}

        def scores():
            q4 = jnp.concatenate([q[:, (reps * p + r) * LANES:(reps * p + r + 1) * LANES] * scale
                                  for r in range(reps)], axis=0)
            state["sp"] = lax.dot_general(q4, block_diag(kp[:, ks]), _NT, preferred_element_type=F32)
            state["sc"] = lax.dot_general(q4, block_diag(kc[:, ks]), _NT, preferred_element_type=F32)

        def softmax():
            sp, sc = state.pop("sp"), state.pop("sc")
            pps, pcs, invs = [], [], []
            for r in range(reps):
                rows = slice(r * tq, (r + 1) * tq)
                pp_r, pc_r, inv_r = [], [], []
                for half in range(2):
                    cols = slice(half * WINDOW, (half + 1) * WINDOW)
                    head = 2 * reps * p + reps * half + r
                    sp_i = jnp.where(sink_lane, sink_ref[head], jnp.where(valid_prev, sp[rows, cols], neg_inf))
                    sc_i = jnp.where(valid_cur, sc[rows, cols], neg_inf)
                    mx = jnp.max(jnp.maximum(sp_i, sc_i), axis=1, keepdims=True)
                    pp = jnp.exp(sp_i - mx)
                    pc = jnp.exp(sc_i - mx)
                    inv_r.append(1.0 / jnp.sum(pp + pc, axis=1, keepdims=True))
                    pp_r.append(pp.astype(BF16))
                    pc_r.append(pc.astype(BF16))
                pps.append(jnp.concatenate(pp_r, axis=1))
                pcs.append(jnp.concatenate(pc_r, axis=1))
                invs.append(jnp.where(low_q, inv_r[0], inv_r[1]))
            state.update(pp=jnp.concatenate(pps, axis=0), pc=jnp.concatenate(pcs, axis=0), inv=invs)

        def values():
            v_prev = block_diag(jnp.where(key_row == 0, 0.0, vp[:, ks]))
            o4 = (jnp.dot(state.pop("pp"), v_prev, preferred_element_type=F32)
                  + jnp.dot(state.pop("pc"), block_diag(vc[:, ks]), preferred_element_type=F32))
            invs = state.pop("inv")
            for r in range(reps):
                blk = reps * p + r
                o_ref[o_row0:o_row0 + o_rows, blk * LANES:(blk + 1) * LANES] = (
                    o4[r * tq:r * tq + o_rows] * invs[r][:o_rows]).astype(o_ref.dtype)

        return scores, softmax, values

    return [pair_stages(p) for p in range(A_KV_HEADS // 2)]


def _swa_block(sink_ref, q, kp, vp, kc, vc, valid_prev, o_ref, between=None):
    for scores, softmax, values in _swa_stages(sink_ref, q, kp, vp, kc, vc, valid_prev, o_ref):
        if between is not None:
            between()
        scores()
        softmax()
        if between is not None:
            between()
        values()


def _swa(sinks, q, k_cache, k_new, v_cache, v_new, *, nbb, n_new):
    n = k_cache.shape[0]
    assert n_new <= SAMPLE_TQ and q.shape[0] == n * n_new
    cache_spec = pl.BlockSpec((nbb, WINDOW, W_AKV), lambda i: (i, 0, 0))
    new_spec = pl.BlockSpec((nbb * n_new, W_AKV), lambda i: (i, 0))
    q_spec = pl.BlockSpec((nbb * n_new, W_AQ), lambda i: (i, 0))
    return pl.pallas_call(
        functools.partial(_swa_kernel, nbb=nbb, n_new=n_new),
        grid=(n // nbb,),
        in_specs=[pl.BlockSpec(memory_space=pltpu.SMEM), q_spec, cache_spec, new_spec, cache_spec, new_spec],
        out_specs=[q_spec, cache_spec, cache_spec],
        out_shape=[jax.ShapeDtypeStruct((n * n_new, W_AQ), F32)]
        + [jax.ShapeDtypeStruct((n, WINDOW, W_AKV), F32)] * 2,
        compiler_params=_params(),
        name="swa",
    )(sinks, q, k_cache, k_new, v_cache, v_new)


def _merge_kernel(x_ref, g_ref, hm_ref, oa_ref, gm_ref, ga_ref, wm_ref, wa_ref, wo_ref,
                  lg_ref, lb_ref, o_ref):
    ym = jnp.dot(hm_ref[...], wm_ref[...], preferred_element_type=F32)
    ya = jnp.dot(oa_ref[...].astype(BF16), wa_ref[...], preferred_element_type=F32)
    mix = gm_ref[...].astype(F32) * ym + ga_ref[...].astype(F32) * ya
    t = jnp.dot(mix.astype(BF16), wo_ref[...], preferred_element_type=F32)
    y = DEEPNORM_ALPHA * x_ref[...] + (1.0 + _rows(g_ref)) * t
    o_ref[...] = _layer_norm(y, lg_ref[...], lb_ref[...], LN_EPS)


def _merge(x, mod, per_row, rows_per_batch, chunk, hm, oa, gm, ga, wm, wa, wo, ln_g, ln_b):
    m = x.shape[0]
    tm = ROW_TILE

    def tok():
        return pl.BlockSpec((tm, D_MODEL), lambda i: (i, 0))

    return pl.pallas_call(
        _merge_kernel,
        grid=(m // tm,),
        in_specs=[tok()] + _mod_specs(per_row, tm, rows_per_batch // tm, (chunk,))
        + [tok(), tok(), tok(), tok()]
        + [_resident((D_MODEL, D_MODEL))] * 3 + [_resident((1, D_MODEL))] * 2,
        out_specs=tok(),
        out_shape=jax.ShapeDtypeStruct((m, D_MODEL), F32),
        compiler_params=_params(),
        name="merge",
    )(x, mod, hm, oa, gm, ga, wm, wa, wo, ln_g.reshape(1, D_MODEL), ln_b.reshape(1, D_MODEL))


def _mixer_kernel(sink_ref, xa_ref, xb_ref, sh_ref, sc_ref, g_ref, wa_ref, wq_ref, wr_ref, wif_ref, bif_ref,
                  cos_ref, sin_ref, ng_ref, wm_ref, wba_ref, wo_ref, lg_ref, lb_ref,
                  y_ref, c_ref, n_ref, m_ref, kb_ref, vb_ref, *scratch, tiles_per_seq):
    n_staged = (len(scratch) - 6) // 2
    even, odd = scratch[:n_staged], scratch[n_staged:2 * n_staged]
    kprev_s, vprev_s, hm_s, oa_s, gtt_s, ctt_s = scratch[2 * n_staged:]
    tm = xa_ref.shape[0]
    s = pl.program_id(0)
    tile_b = jnp.maximum(s - 1, 0)
    seq_start = tile_b % tiles_per_seq == 0

    @pl.when(s == 0)
    def _():
        for ref in odd:
            ref[...] = jnp.zeros_like(ref)

    @pl.when(seq_start)
    def _():
        c_ref[...] = jnp.zeros_like(c_ref)
        n_ref[...] = jnp.zeros_like(n_ref)
        m_ref[...] = jnp.zeros_like(m_ref)
        kprev_s[...] = jnp.zeros_like(kprev_s)
        vprev_s[...] = jnp.zeros_like(vprev_s)

    def step(produce, consume):
        proj_parts = _proj_parts(xa_ref, sh_ref, sc_ref, wa_ref, wq_ref, wr_ref, wif_ref, bif_ref,
                                 cos_ref, sin_ref, *produce)
        pending = iter(proj_parts)

        counts = iter(MIXER_PROJ_SCHEDULE)

        def between():
            for _ in range(next(counts)):
                part = next(pending, None)
                if part is not None:
                    part()

        q_s, k_s, v_s, mo_s, aq_s, ak_s, av_s, gm_s, ga_s, gt_s = consume
        qi = lax.broadcasted_iota(jnp.int32, (WINDOW, WINDOW), 0)
        kj = lax.broadcasted_iota(jnp.int32, (WINDOW, WINDOW), 1)
        for j in range(tm // MLSTM_CHUNK):
            rows = pl.ds(j * MLSTM_CHUNK, MLSTM_CHUNK)
            _mlstm_chunk(q_s.at[rows], k_s.at[rows], v_s.at[rows], gt_s.at[rows], mo_s.at[rows], ng_ref,
                         hm_s.at[rows], c_ref, n_ref, m_ref, gtt_s, ctt_s,
                         nb=1, tpb=MLSTM_CHUNK, hps=M_HEADS, head0=0, between=between)
            valid_prev = kj > qi
            if j == 0:
                valid_prev = valid_prev & jnp.logical_not(seq_start)
                k_prev, v_prev = kprev_s[...], vprev_s[...]
            else:
                before = pl.ds((j - 1) * WINDOW, WINDOW)
                k_prev, v_prev = ak_s[before], av_s[before]
            _swa_block(sink_ref, aq_s[rows], k_prev, v_prev, ak_s[rows], av_s[rows], valid_prev,
                       oa_s.at[rows], between=between)

        for part in pending:
            part()
        _merge_kernel(xb_ref, g_ref, hm_s, oa_s, gm_s, ga_s, wm_ref, wba_ref, wo_ref, lg_ref, lb_ref, y_ref)

        last = pl.ds(tm - WINDOW, WINDOW)
        k_last, v_last = ak_s[last], av_s[last]
        kb_ref[0] = k_last.T
        vb_ref[0] = v_last.T
        kprev_s[...] = k_last
        vprev_s[...] = v_last

    @pl.when(s % 2 == 0)
    def _():
        step(even, odd)

    @pl.when(s % 2 == 1)
    def _():
        step(odd, even)


def _mixer(x1, mod, w, sinks, norm_g, cos_t, sin_t, bp, sp):
    tm = MIXER_TILE
    tps = sp // tm
    n_tiles = bp * tps
    hd = M_HEADS * M_DQK
    weights = (w["w_a"], w["w_aq"], w["w_r"], w["w_if"], w["bif"])

    def tile_a(s):
        return jnp.minimum(s, n_tiles - 1)

    def tile_b(s):
        return jnp.maximum(s - 1, 0)

    def mod_spec(tile, chunk):
        return pl.BlockSpec((1, 1, D_MODEL), lambda s: (tile(s) // tps, 0, chunk))

    def per_seq(*shape):
        return pl.BlockSpec((1,) + shape, lambda s: (tile_b(s) // tps,) + (0,) * len(shape))

    pos = pl.BlockSpec((tm, LANES), lambda s: (tile_a(s) % tps, 0))
    staged = [(W_MQ, BF16), (W_MQ, BF16), (W_MV, BF16), (W_MV, BF16), (W_AQ, BF16),
              (W_AKV, F32), (W_AKV, F32), (D_MODEL, BF16), (D_MODEL, BF16), (LANES, F32)]
    return pl.pallas_call(
        functools.partial(_mixer_kernel, tiles_per_seq=tps),
        grid=(n_tiles + 1,),
        in_specs=[pl.BlockSpec(memory_space=pltpu.SMEM),
                  pl.BlockSpec((tm, D_MODEL), lambda s: (tile_a(s), 0)),
                  pl.BlockSpec((tm, D_MODEL), lambda s: (tile_b(s), 0)),
                  mod_spec(tile_a, 3), mod_spec(tile_a, 4), mod_spec(tile_b, 5)]
        + [_resident(a.shape) for a in weights] + [pos, pos, _resident((1, M_HEADS * M_DV))]
        + [_resident((D_MODEL, D_MODEL))] * 3 + [_resident((1, D_MODEL))] * 2,
        out_specs=[pl.BlockSpec((tm, D_MODEL), lambda s: (tile_b(s), 0)),
                   per_seq(M_HEADS, M_DV, M_DQK),
                   pl.BlockSpec((MLSTM_CHUNK, hd), lambda s: (tile_b(s) // tps, 0)),
                   pl.BlockSpec((MLSTM_CHUNK, LANES), lambda s: (tile_b(s) // tps, 0)),
                   per_seq(W_AKV, WINDOW), per_seq(W_AKV, WINDOW)],
        out_shape=[jax.ShapeDtypeStruct((bp * sp, D_MODEL), F32),
                   jax.ShapeDtypeStruct((bp, M_HEADS, M_DV, M_DQK), F32),
                   jax.ShapeDtypeStruct((bp * MLSTM_CHUNK, hd), F32),
                   jax.ShapeDtypeStruct((bp * MLSTM_CHUNK, LANES), F32),
                   jax.ShapeDtypeStruct((bp, W_AKV, WINDOW), F32),
                   jax.ShapeDtypeStruct((bp, W_AKV, WINDOW), F32)],
        scratch_shapes=[pltpu.VMEM((tm, width), dtype) for width, dtype in staged + staged]
        + [pltpu.VMEM((WINDOW, W_AKV), F32), pltpu.VMEM((WINDOW, W_AKV), F32),
           pltpu.VMEM((tm, W_MV), BF16), pltpu.VMEM((tm, W_AQ), BF16),
           pltpu.VMEM((LANES, MLSTM_CHUNK), F32), pltpu.VMEM((LANES, MLSTM_CHUNK), F32)],
        compiler_params=_params(),
        name="mixer",
    )(sinks, x1, x1, mod, mod, mod, *weights, cos_t, sin_t, norm_g.reshape(1, M_HEADS * M_DV),
      w["wm"], w["wa"], w["wo"], w["ln2_g"].reshape(1, D_MODEL), w["ln2_b"].reshape(1, D_MODEL))


def _rope_tables(pos):
    half = A_DH // 2
    inv = np.float32(ROPE_THETA) ** (-np.arange(half, dtype=np.float32) / np.float32(half))
    ang = pos.astype(np.float32)[:, None] * inv[None, :]
    cos, sin = np.cos(ang), np.sin(ang)
    return jnp.asarray(np.tile(cos, (1, 4))), jnp.asarray(np.concatenate([-sin, sin, -sin, sin], axis=1))


def _from_stored(kv_t):
    return jnp.transpose(kv_t.reshape(kv_t.shape[0], A_KV_HEADS, A_DH, WINDOW), (0, 3, 1, 2))


def _token_stage_1(x, mod, per_row, rows_per_batch, w, pos):
    x1 = _ffn(x, mod, per_row, rows_per_batch, (0, 1, 2), w["up1"], w["down1"], w["ln1_g"], w["ln1_b"])
    cos_t, sin_t = _rope_tables(pos)
    return x1, _proj(x1, mod, per_row, rows_per_batch, (3, 4), w, cos_t, sin_t)


def _token_stage_2(x1, mod, per_row, rows_per_batch, w, hm, oa, gm, ga):
    x2 = _merge(x1, mod, per_row, rows_per_batch, 5, hm, oa, gm, ga,
                w["wm"], w["wa"], w["wo"], w["ln2_g"], w["ln2_b"])
    return _ffn(x2, mod, per_row, rows_per_batch, (6, 7, 8), w["up2"], w["down2"], w["ln3_g"], w["ln3_b"])


def kernel(x_prompt, x_sample, state_mlstm_C, state_mlstm_n, state_mlstm_m, cache_swa_k, cache_swa_v, c_prompt, c_sample, w_ada, b_ada, w_ffn1_up, w_ffn1_down, ln1_g, ln1_b, w_in, b_igate, b_fgate, m_norm_g, sinks, w_branch_m, w_branch_a, w_out, ln2_g, ln2_b, w_ffn2_up, w_ffn2_down, ln3_g, ln3_b):
    assert w_ada.shape[0] == DEPTH == 1
    bp, sp, _ = x_prompt.shape
    bs, ts, _ = x_sample.shape

    w_in_t = jnp.transpose(w_in[0])
    w = dict(
        up1=w_ffn1_up[0].astype(BF16), down1=w_ffn1_down[0].astype(BF16),
        bif=jnp.concatenate([b_igate[0], b_fgate[0], jnp.zeros((LANES - 2 * M_HEADS,), F32)]).reshape(1, LANES),
        ln1_g=ln1_g[0], ln1_b=ln1_b[0], ln2_g=ln2_g[0], ln2_b=ln2_b[0], ln3_g=ln3_g[0], ln3_b=ln3_b[0],
    )

    ms = bs * ts
    c_all = jnp.concatenate([jnp.repeat(c_sample, ts, axis=0), c_prompt], axis=0)
    mod = _ada(c_all, w_ada[0], b_ada[0])
    mod_p = mod[ms:].reshape(bp, 1, ADA_CHUNKS * D_MODEL)

    mp = bp * sp
    reps = A_HEADS // A_KV_HEADS
    pair_low = lambda j: (j // reps) * (2 * reps) + j % reps
    pair_rows = lambda j: (j // (2 * reps)) * (2 * reps) + (j % 2) * reps + (j % (2 * reps)) // 2
    jobs = (
        _cast_job(w_ffn2_up[0], (D_MODEL, FF_CHUNK), 1),
        _cast_job(w_ffn2_down[0], (FF_CHUNK, D_MODEL), 0, first_step=2 * D_FF // FF_CHUNK),
        _cast_job(w_in_t, (D_MODEL, LANES), 1, transposed=True, n_blocks=IN_IF // LANES),
        _cast_job(w_in_t, (D_MODEL, LANES), 1, transposed=True, n_blocks=W_AQ // LANES, shift=IN_AQ - IN_IF,
                  sources=(lambda j: IN_IF // A_DH + pair_low(j), lambda j: IN_IF // A_DH + pair_low(j) + reps)),
        _cast_job(w_in_t, (D_MODEL, LANES), 1, transposed=True, n_blocks=(IN_END - IN_AK) // LANES,
                  shift=IN_AQ - IN_IF, sources=(lambda j: (IN_AK - (IN_AQ - IN_IF)) // LANES + j,)),
        _cast_job(w_branch_m[0], (D_MODEL, LANES), 1),
        _cast_job(w_out[0], (D_MODEL, LANES), 1),
        _cast_job(w_branch_a[0], (A_DH, D_MODEL), 0, sources=(pair_rows,)),
        _cast_job(w_in_t, (LANES, D_MODEL), 0, n_blocks=1, sources=(lambda j: IN_IF // LANES + j,)),
    )
    x1p, w["up2"], w["down2"], w["w_a"], w["w_aq"], w["w_r"], w["wm"], w["wo"], w["wa"], w["w_if"] = _ffn(
        x_prompt.reshape(mp, D_MODEL), mod_p, False, sp, (0, 1, 2),
        w["up1"], w["down1"], w["ln1_g"], w["ln1_b"], cast_jobs=jobs)
    cos_t, sin_t = _rope_tables(np.arange(sp))
    x2p, c_p, n_rows, m_rows, kb_p, vb_p = _mixer(x1p, mod_p, w, sinks[0], m_norm_g[0], cos_t, sin_t, bp, sp)
    y_p = _ffn(x2p, mod_p, False, sp, (6, 7, 8), w["up2"], w["down2"], w["ln3_g"], w["ln3_b"])
    n_p = n_rows.reshape(bp, MLSTM_CHUNK, M_HEADS * M_DQK)[:, 0].reshape(bp, M_HEADS, M_DQK)
    m_p = m_rows.reshape(bp, MLSTM_CHUNK, LANES)[:, 0, :M_HEADS]
    kb_p, vb_p = _from_stored(kb_p), _from_stored(vb_p)

    x1s, (qm, km, vm, mo, aq, ak, av, gm, ga, gt) = _token_stage_1(
        x_sample.reshape(ms, D_MODEL), mod, True, ms, w, PAST_LEN + np.arange(ms) % ts)
    seqs = MLSTM_CHUNK // ts
    n0_rows = jnp.repeat(state_mlstm_n[0].reshape(bs, M_HEADS * M_DQK), ts, axis=0)
    m0_rows = jnp.repeat(jnp.pad(state_mlstm_m[0], ((0, 0), (0, LANES - M_HEADS))), ts, axis=0)
    hm, c_s, n_rows, m_rows = _mlstm(qm, km, vm, gt, mo, m_norm_g[0],
                                     (state_mlstm_C[0], n0_rows, m0_rows), seqs, ts, 1, 2)
    n_s = n_rows.reshape(bs, ts, M_HEADS * M_DQK)[:, 0].reshape(bs, M_HEADS, M_DQK)
    m_s = m_rows.reshape(bs, ts, LANES)[:, 0, :M_HEADS]
    oa, kb_s, vb_s = _swa(
        sinks[0], aq, cache_swa_k[0].reshape(bs, WINDOW, W_AKV), ak,
        cache_swa_v[0].reshape(bs, WINDOW, W_AKV), av, nbb=SWA_SAMPLE_SEQS, n_new=ts)
    y_s = _token_stage_2(x1s, mod, True, ms, w, hm, oa, gm, ga)
    kb_s = kb_s.reshape(bs, WINDOW, A_KV_HEADS, A_DH)
    vb_s = vb_s.reshape(bs, WINDOW, A_KV_HEADS, A_DH)

    return (y_p.reshape(bp, sp, D_MODEL), y_s.reshape(bs, ts, D_MODEL),
            c_p[None], n_p[None], m_p[None], kb_p[None], vb_p[None],
            c_s[None], n_s[None], m_s[None], kb_s[None], vb_s[None])
```

```python
import functools

import jax
import jax.numpy as jnp
import numpy as np
from jax import lax
from jax.experimental import pallas as pl
from jax.experimental.pallas import tpu as pltpu

F32 = jnp.float32
BF16 = jnp.bfloat16

D_MODEL = 1024
D_FF = 2816
DEPTH = 1
M_HEADS = 4
M_DQK = 128
M_DV = 256
A_HEADS = 16
A_KV_HEADS = 4
A_DH = 64
WINDOW = 128
PAST_LEN = 8192
ROPE_THETA = 10000.0
ATTN_SCALE = A_DH ** -0.5
LN_EPS = 1e-5
HEAD_NORM_EPS = 1e-6
ADA_CHUNKS = 9
DEEPNORM_ALPHA = (2.0 * DEPTH) ** 0.25
K_SCALE = M_DQK ** -0.5

LANES = 128
BF16_SUBLANES = 16
VMEM_LIMIT_BYTES = 56 * 1024 * 1024

W_MQ = M_HEADS * M_DQK
W_MV = M_HEADS * M_DV
W_AQ = A_HEADS * A_DH
W_AKV = A_KV_HEADS * A_DH
IN_IF = 2 * W_MQ + 2 * W_MV
IN_AQ = IN_IF + 2 * M_HEADS
IN_AK = IN_AQ + W_AQ
IN_END = IN_AK + 2 * W_AKV + 2 * D_MODEL
A_Q, A_K, A_V, A_O = 0, W_MQ, 2 * W_MQ, 2 * W_MQ + W_MV
R_AK, R_AV, R_GM, R_GA = 0, W_AKV, 2 * W_AKV, 2 * W_AKV + D_MODEL

ROW_TILE = 512
MIXER_TILE = 256
MIXER_PROJ_SCHEDULE = (1, 1, 1, 1, 2, 2, 2, 2) * 2
ADA_TILE = 1536
FF_CHUNK = 256
PROJ_CHUNK = 2 * LANES
FFN_EPILOGUE_PIECES = 4
MLSTM_CHUNK = 128
SAMPLE_TQ = BF16_SUBLANES
SWA_SAMPLE_SEQS = 16

_NT = (((1,), (1,)), ((), ()))


def _params():
    return pltpu.CompilerParams(vmem_limit_bytes=VMEM_LIMIT_BYTES)


def _resident(shape):
    return pl.BlockSpec(shape, lambda *_: (0,) * len(shape), pipeline_mode=pl.Buffered(1))


def _rows(ref):
    v = ref[...]
    return v.reshape(v.shape[-2], v.shape[-1])


def _layer_norm(y, g, b, eps):
    mu = jnp.mean(y, axis=-1, keepdims=True)
    d = y - mu
    var = jnp.mean(d * d, axis=-1, keepdims=True)
    return d * lax.rsqrt(var + eps) * g + b


def _sigmoid(x):
    return 1.0 / (1.0 + jnp.exp(-x))


def _mod_specs(per_row, tm, tiles_per_batch, chunks):
    if per_row:
        return [pl.BlockSpec((tm, D_MODEL), lambda i, c=c: (i, c)) for c in chunks]
    return [pl.BlockSpec((1, 1, D_MODEL), lambda i, c=c: (i // tiles_per_batch, 0, c)) for c in chunks]


def _ada_kernel(c_ref, w_ref, b_ref, o_ref):
    c = c_ref[...]
    s = (c * _sigmoid(c)).astype(BF16)
    o_ref[...] = jnp.dot(s, w_ref[...].astype(BF16), preferred_element_type=F32) + b_ref[...]


def _ada(c_all, w_ada, b_ada):
    rows = c_all.shape[0]
    n_out = w_ada.shape[1]
    tn = ADA_TILE
    return pl.pallas_call(
        _ada_kernel,
        grid=(n_out // tn,),
        in_specs=[pl.BlockSpec((rows, D_MODEL), lambda j: (0, 0)),
                  pl.BlockSpec((D_MODEL, tn), lambda j: (0, j)),
                  pl.BlockSpec((1, tn), lambda j: (0, j))],
        out_specs=pl.BlockSpec((rows, tn), lambda j: (0, j)),
        out_shape=jax.ShapeDtypeStruct((rows, n_out), F32),
        compiler_params=_params(),
        name="ada",
    )(c_all, w_ada, b_ada.reshape(1, n_out))


def _ffn_kernel(x_ref, xp_ref, sh_ref, sc_ref, g_ref, wup_ref, wdn_ref, lg_ref, lb_ref, *rest,
                n_tiles, cast_layout):
    n_cast = len(cast_layout)
    n_src = sum(layout[0] for layout in cast_layout)
    cast_in, o_ref, cast_out = rest[:n_src], rest[n_src], rest[n_src + 1:n_src + 1 + n_cast]
    act_ref, f_ref = rest[n_src + 1 + n_cast:]

    def cast_chunk(job):
        if job >= n_cast:
            return
        first = sum(layout[0] for layout in cast_layout[:job])
        n, transposed, shift = cast_layout[job]
        refs = cast_in[first:first + n]
        if shift:
            rows = refs[0].shape[0]
            parts = [jnp.concatenate([lo[...], hi[...]], axis=0)[shift:shift + rows]
                     for lo, hi in zip(refs[0::2], refs[1::2])]
        else:
            parts = [ref[...] for ref in refs]
        block = jnp.concatenate(parts, axis=0)
        cast_out[job][...] = (block.T if transposed else block).astype(BF16)

    tm = x_ref.shape[0]
    s = pl.program_id(0)
    piece_rows = tm // FFN_EPILOGUE_PIECES

    def epilogue(piece):
        rows = pl.ds(piece * piece_rows, piece_rows)
        g = _rows(g_ref)
        if g.shape[0] != 1:
            g = g[piece * piece_rows:(piece + 1) * piece_rows]
        y = DEEPNORM_ALPHA * xp_ref[rows, :] + (0.5 * (1.0 + g)) * f_ref[rows, :]
        o_ref[rows, :] = _layer_norm(y, lg_ref[...], lb_ref[...], LN_EPS)

    def matmuls(previous_epilogue):
        h = (x_ref[...] * (1.0 + _rows(sc_ref)) + _rows(sh_ref)).astype(BF16)
        chunks = list(range(0, D_FF, FF_CHUNK))
        every = len(chunks) // FFN_EPILOGUE_PIECES
        piece = 0
        for i, c in enumerate(chunks):
            a = jnp.dot(h, wup_ref[:, c:c + FF_CHUNK], preferred_element_type=F32)
            u = jnp.dot(h, wup_ref[:, D_FF + c:D_FF + c + FF_CHUNK], preferred_element_type=F32)
            act_ref[:, c:c + FF_CHUNK] = (a * _sigmoid(a) * u).astype(BF16)
            cast_chunk(i)
            if previous_epilogue and i % every == every - 1 and piece < FFN_EPILOGUE_PIECES:
                epilogue(piece)
                piece += 1
        f_ref[...] = jnp.dot(act_ref[...], wdn_ref[...], preferred_element_type=F32)

    if n_tiles == 1:
        matmuls(False)
        for piece in range(FFN_EPILOGUE_PIECES):
            epilogue(piece)
        return

    @pl.when(s == 0)
    def _():
        f_ref[...] = jnp.zeros_like(f_ref)

    @pl.when(s < n_tiles)
    def _():
        matmuls(True)

    @pl.when(s == n_tiles)
    def _():
        for job in range(n_cast):
            cast_chunk(job)
        for piece in range(FFN_EPILOGUE_PIECES):
            epilogue(piece)


def _cast_job(array, block, axis, first_step=0, sources=None, transposed=False, n_blocks=None, shift=0):
    sources = sources or (lambda j: j,)
    src_axis = 1 - axis if transposed else axis
    src_block = block[::-1] if transposed else block
    src_block = (src_block[0] // len(sources), src_block[1])
    if n_blocks is None:
        n_blocks = array.shape[src_axis] // block[axis]

    def step_block(s):
        return jnp.clip(s - first_step, 0, n_blocks - 1)

    def at(axis_, j):
        return (j, 0) if axis_ == 0 else (0, j)

    in_specs = [pl.BlockSpec(src_block, lambda s, f=f, d=d: at(src_axis, f(step_block(s)) + d))
                for f in sources for d in range(2 if shift else 1)]
    out_dims = list(array.shape[::-1] if transposed else array.shape)
    out_dims[axis] = n_blocks * block[axis]
    return dict(array=array, in_specs=in_specs, layout=(len(in_specs), transposed, shift),
                out_spec=pl.BlockSpec(block, lambda s: at(axis, step_block(s))),
                out_shape=jax.ShapeDtypeStruct(tuple(out_dims), BF16), steps=first_step + n_blocks)


def _ffn(x, mod, per_row, rows_per_batch, chunks, w_up, w_down, ln_g, ln_b, cast_jobs=()):
    m = x.shape[0]
    tm = ROW_TILE
    n_tiles = m // tm
    tpb = rows_per_batch // tm

    def cur(s):
        return jnp.minimum(s, n_tiles - 1)

    def prev(s):
        return jnp.maximum(s - 1, 0)

    def mod_spec(tile, chunk):
        if per_row:
            return pl.BlockSpec((tm, D_MODEL), lambda s: (tile(s), chunk))
        return pl.BlockSpec((1, 1, D_MODEL), lambda s: (tile(s) // tpb, 0, chunk))

    in_specs = [pl.BlockSpec((tm, D_MODEL), lambda s: (cur(s), 0)),
                pl.BlockSpec((tm, D_MODEL), lambda s: (prev(s), 0)),
                mod_spec(cur, chunks[0]), mod_spec(cur, chunks[1]), mod_spec(prev, chunks[2]),
                _resident((D_MODEL, 2 * D_FF)), _resident((D_FF, D_MODEL)),
                _resident((1, D_MODEL)), _resident((1, D_MODEL))]
    args = [x, x, mod, mod, mod, w_up, w_down, ln_g.reshape(1, D_MODEL), ln_b.reshape(1, D_MODEL)]
    out_specs = [pl.BlockSpec((tm, D_MODEL), lambda s: (prev(s), 0))]
    out_shape = [jax.ShapeDtypeStruct((m, D_MODEL), F32)]
    n_steps = 1 if n_tiles == 1 else n_tiles + 1
    for job in cast_jobs:
        assert job["steps"] <= n_steps
        in_specs += job["in_specs"]
        args += [job["array"]] * len(job["in_specs"])
        out_specs.append(job["out_spec"])
        out_shape.append(job["out_shape"])
    outs = pl.pallas_call(
        functools.partial(_ffn_kernel, n_tiles=n_tiles, cast_layout=tuple(j["layout"] for j in cast_jobs)),
        grid=(n_steps,),
        in_specs=in_specs,
        out_specs=out_specs,
        out_shape=out_shape,
        scratch_shapes=[pltpu.VMEM((tm, D_FF), BF16), pltpu.VMEM((tm, D_MODEL), F32)],
        compiler_params=_params(),
        name="ffn",
    )(*args)
    return outs if cast_jobs else outs[0]


def _ffn_round_kernel(x_ref, sh_ref, sc_ref, g_ref, wup_ref, wdn_ref, lg_ref, lb_ref,
                      o_ref, up_ref, dn_ref, h_ref, a_ref, f_ref, *, n_chunks):
    k = pl.program_id(0)
    up = wup_ref[...].astype(BF16)
    up_ref[...] = up

    @pl.when(k == 0)
    def _():
        h_ref[...] = (x_ref[...] * (1.0 + _rows(sc_ref)) + _rows(sh_ref)).astype(BF16)
        f_ref[...] = jnp.zeros_like(f_ref)

    z = jnp.dot(h_ref[...], up, preferred_element_type=F32)

    @pl.when(k < n_chunks)
    def _():
        a_ref[k] = z

    @pl.when(k >= n_chunks)
    def _():
        dn = wdn_ref[...].astype(BF16)
        dn_ref[...] = dn
        a = a_ref[k - n_chunks]
        f_ref[...] += jnp.dot((a * _sigmoid(a) * z).astype(BF16), dn, preferred_element_type=F32)

    @pl.when(k == 2 * n_chunks - 1)
    def _():
        y = DEEPNORM_ALPHA * x_ref[...] + (0.5 * (1.0 + _rows(g_ref))) * f_ref[...]
        o_ref[...] = _layer_norm(y, lg_ref[...], lb_ref[...], LN_EPS)


def _ffn_round(x, mod, chunks, w_up, w_down, ln_g, ln_b):
    tm = x.shape[0]
    assert tm == ROW_TILE
    n_chunks = D_FF // FF_CHUNK

    def tok(chunk):
        return pl.BlockSpec((tm, D_MODEL), lambda k: (0, chunk))

    def down_block(k):
        return (jnp.maximum(k - n_chunks, 0), 0)

    return pl.pallas_call(
        functools.partial(_ffn_round_kernel, n_chunks=n_chunks),
        grid=(2 * n_chunks,),
        in_specs=[tok(0), tok(chunks[0]), tok(chunks[1]), tok(chunks[2]),
                  pl.BlockSpec((D_MODEL, FF_CHUNK), lambda k: (0, k)),
                  pl.BlockSpec((FF_CHUNK, D_MODEL), down_block),
                  _resident((1, D_MODEL)), _resident((1, D_MODEL))],
        out_specs=[tok(0),
                   pl.BlockSpec((D_MODEL, FF_CHUNK), lambda k: (0, k)),
                   pl.BlockSpec((FF_CHUNK, D_MODEL), down_block)],
        out_shape=[jax.ShapeDtypeStruct((tm, D_MODEL), F32),
                   jax.ShapeDtypeStruct(w_up.shape, BF16), jax.ShapeDtypeStruct(w_down.shape, BF16)],
        scratch_shapes=[pltpu.VMEM((tm, D_MODEL), BF16), pltpu.VMEM((n_chunks, tm, FF_CHUNK), F32),
                        pltpu.VMEM((tm, D_MODEL), F32)],
        compiler_params=_params(),
        name="ffn_round",
    )(x, mod, mod, mod, w_up, w_down, ln_g.reshape(1, D_MODEL), ln_b.reshape(1, D_MODEL))


def _proj_parts(x_ref, sh_ref, sc_ref, wa_ref, wq_ref, wr_ref, wif_ref, bif_ref, cos_ref, sin_ref,
                q_ref, k_ref, v_ref, o_ref, aq_ref, ak_ref, av_ref, gm_ref, ga_ref, gt_ref):
    x = x_ref[...]
    tm = x.shape[0]
    h = (x * (1.0 + _rows(sc_ref)) + _rows(sh_ref)).astype(BF16)
    lane = lax.broadcasted_iota(jnp.int32, (tm, LANES), 1)
    pc = PROJ_CHUNK

    def seg(w_ref, lo):
        return jnp.dot(h, w_ref[:, lo:lo + pc], preferred_element_type=F32)

    def plain(dst_ref, w_ref, lo, c, scale=None):
        def run():
            z = seg(w_ref, lo + c)
            dst_ref[:, c:c + pc] = (z if scale is None else z * scale).astype(dst_ref.dtype)
        return run

    def gate(dst_ref, w_ref, lo, c):
        def run():
            dst_ref[:, c:c + pc] = _sigmoid(seg(w_ref, lo + c)).astype(BF16)
        return run

    def forget_input_gates():
        zg = lax.dot_general(h, wif_ref[...], _NT, preferred_element_type=F32) + bif_ref[...]
        logsig = jnp.minimum(zg, 0.0) - jnp.log(1.0 + jnp.exp(-jnp.abs(zg)))
        gt_ref[...] = jnp.where(lane < M_HEADS, zg, logsig)

    def rotary(dst_ref, w_ref, lo, c):
        def run():
            cos = cos_ref[...]
            sin = sin_ref[...]
            low_half = (lane & (A_DH // 2)) == 0
            z = seg(w_ref, lo + c)
            for half in range(2):
                zh = z[:, half * LANES:(half + 1) * LANES]
                partner = jnp.where(low_half, pltpu.roll(zh, LANES - A_DH // 2, 1),
                                    pltpu.roll(zh, A_DH // 2, 1))
                dst_ref[:, c + half * LANES:c + (half + 1) * LANES] = (
                    zh * cos + partner * sin).astype(dst_ref.dtype)
        return run

    parts = [plain(q_ref, wa_ref, A_Q, c) for c in range(0, W_MQ, pc)]
    parts += [plain(k_ref, wa_ref, A_K, c, K_SCALE) for c in range(0, W_MQ, pc)]
    parts += [plain(v_ref, wa_ref, A_V, c) for c in range(0, W_MV, pc)]
    parts += [forget_input_gates]
    parts += [rotary(aq_ref, wq_ref, 0, c) for c in range(0, W_AQ, pc)]
    parts += [rotary(ak_ref, wr_ref, R_AK, 0), plain(av_ref, wr_ref, R_AV, 0)]
    parts += [gate(o_ref, wa_ref, A_O, c) for c in range(0, W_MV, pc)]
    parts += [gate(gm_ref, wr_ref, R_GM, c) for c in range(0, D_MODEL, pc)]
    parts += [gate(ga_ref, wr_ref, R_GA, c) for c in range(0, D_MODEL, pc)]
    return parts


def _proj_kernel(*refs):
    for part in _proj_parts(*refs):
        part()


def _proj(x, mod, per_row, rows_per_batch, chunks, w, cos_t, sin_t):
    m = x.shape[0]
    tm = ROW_TILE
    n_pos_tiles = cos_t.shape[0] // tm

    def tok(width):
        return pl.BlockSpec((tm, width), lambda i: (i, 0))

    widths = (W_MQ, W_MQ, W_MV, W_MV, W_AQ, W_AKV, W_AKV, D_MODEL, D_MODEL, LANES)
    dtypes = (BF16, BF16, BF16, BF16, F32, F32, F32, BF16, BF16, F32)
    weights = (w["w_a"], w["w_aq"], w["w_r"], w["w_if"], w["bif"])
    return pl.pallas_call(
        _proj_kernel,
        grid=(m // tm,),
        in_specs=[tok(D_MODEL)] + _mod_specs(per_row, tm, rows_per_batch // tm, chunks)
        + [_resident(a.shape) for a in weights]
        + [pl.BlockSpec((tm, LANES), lambda i: (i % n_pos_tiles, 0)),
           pl.BlockSpec((tm, LANES), lambda i: (i % n_pos_tiles, 0))],
        out_specs=[tok(wd) for wd in widths],
        out_shape=[jax.ShapeDtypeStruct((m, wd), d) for wd, d in zip(widths, dtypes)],
        compiler_params=_params(),
        name="proj",
    )(x, mod, mod, *weights, cos_t, sin_t)


def _mlstm_kernel(q_ref, k_ref, v_ref, g_ref, mo_ref, ng_ref, c0_ref, n0_ref, m0_ref,
                  out_ref, c_ref, n_ref, m_ref, gt_s, ct_s, *, nb, tpb, hps):
    first_chunk = pl.program_id(2) == 0

    @pl.when(first_chunk)
    def _():
        c_ref[...] = c0_ref[...]
        n_ref[...] = n0_ref[...]

    @pl.when(first_chunk & (pl.program_id(1) == 0))
    def _():
        m_ref[...] = m0_ref[...]

    head0 = 0 if hps == M_HEADS else pl.program_id(1) * hps
    _mlstm_chunk(q_ref, k_ref, v_ref, g_ref, mo_ref, ng_ref, out_ref, c_ref, n_ref, m_ref, gt_s, ct_s,
                 nb=nb, tpb=tpb, hps=hps, head0=head0)


def _mlstm_chunk(q_ref, k_ref, v_ref, g_ref, mo_ref, ng_ref, out_ref, c_ref, n_ref, m_ref, gt_s, ct_s,
                 *, nb, tpb, hps, head0, between=None):
    L = nb * tpb
    shift = tpb.bit_length() - 1
    row = lax.broadcasted_iota(jnp.int32, (L, L), 0)
    col = lax.broadcasted_iota(jnp.int32, (L, L), 1)
    same = (row >> shift) == (col >> shift)
    causal = same & (col <= row)
    lane = lax.broadcasted_iota(jnp.int32, (L, LANES), 1)
    row_seq = lax.broadcasted_iota(jnp.int32, (L, 1), 0) >> shift

    def lane_col(x, idx):
        return jnp.sum(jnp.where(lane == idx, x, 0.0), axis=1, keepdims=True)

    gates = g_ref[...]
    tri = jnp.where(causal, 1.0, 0.0).astype(BF16)
    g_hi = gates.astype(BF16)
    rem = gates - g_hi.astype(F32)
    g_mid = rem.astype(BF16)
    g_lo = (rem - g_mid.astype(F32)).astype(BF16)
    cum = (jnp.dot(tri, g_hi, preferred_element_type=F32)
           + jnp.dot(tri, g_mid, preferred_element_type=F32)
           + jnp.dot(tri, g_lo, preferred_element_type=F32))
    gt_s[...] = gates.T
    ct_s[...] = cum.T
    m_rows = m_ref[...]
    m_next = m_rows

    def head_stages(hl):
        head = head0 + hl
        qs = slice(hl * M_DQK, (hl + 1) * M_DQK)
        vs = slice(hl * M_DV, (hl + 1) * M_DV)
        st = {}

        def gate_stage():
            b_c = lane_col(cum, M_HEADS + head)
            m_p = lane_col(m_rows, head)
            i_r = gt_s[pl.ds(head, 1), :]
            b_r = ct_s[pl.ds(M_HEADS + head, 1), :]
            log_d = jnp.where(causal, b_c - b_r + i_r, -jnp.inf)
            m_t = jnp.maximum(b_c + m_p, jnp.max(log_d, axis=1, keepdims=True))
            dmat = jnp.exp(log_d - m_t)
            e_int = jnp.exp(b_c + m_p - m_t)
            if nb == 1:
                last = slice(L - 1, L)
                m_new = jnp.broadcast_to(m_t[last], (L, 1))
                e_c = jnp.broadcast_to(e_int[last], (L, 1))
                w_mat = jnp.broadcast_to(dmat[last], (L, L))
            else:
                b_last = jnp.min(jnp.where(same, b_r, jnp.inf), axis=1, keepdims=True)
                log_w = jnp.where(same, b_last - b_r + i_r, -jnp.inf)
                m_new = jnp.maximum(b_last + m_p, jnp.max(log_w, axis=1, keepdims=True))
                w_mat = jnp.exp(log_w - m_new)
                e_c = jnp.exp(b_last + m_p - m_new)
            st.update(m_t=m_t, dmat=dmat, e_int=e_int, m_new=m_new, e_c=e_c, w_mat=w_mat)

        def score_stage():
            q, k = q_ref[:, qs], k_ref[:, qs]
            st["s"] = lax.dot_general(q, k, _NT, preferred_element_type=F32) * st.pop("dmat")

        def state_stage():
            q, k = q_ref[:, qs], k_ref[:, qs]
            v_t = v_ref[:, vs].astype(F32).T
            w_mat = st["w_mat"]
            e_cb = jnp.broadcast_to(st["e_c"], (L, LANES))
            inter = jnp.zeros((L, M_DV), F32)
            for j in range(nb):
                first = slice(j * tpb, j * tpb + 1)
                c_j = c_ref[j, hl]
                q_j = q if nb == 1 else jnp.where(row_seq == j, q, jnp.zeros_like(q))
                inter = inter + lax.dot_general(q_j, c_j.astype(BF16), _NT, preferred_element_type=F32)
                lhs = (v_t * w_mat[first]).astype(BF16)
                c_ref[j, hl] = e_cb[first] * c_j + jnp.dot(lhs, k, preferred_element_type=F32)
            st["inter"] = inter

        def output_stage():
            q, k, v = q_ref[:, qs], k_ref[:, qs], v_ref[:, vs]
            s, e_int, m_t = st.pop("s"), st.pop("e_int"), st.pop("m_t")
            n_rows = n_ref[:, qs]
            qn = jnp.sum(q.astype(F32) * n_rows, axis=1, keepdims=True)
            num = jnp.dot(s.astype(BF16), v, preferred_element_type=F32) + e_int * st.pop("inter")
            den = jnp.sum(s, axis=1, keepdims=True) + e_int * qn
            hh = num / jnp.maximum(jnp.abs(den), jnp.exp(-m_t))
            mu = jnp.mean(hh, axis=1, keepdims=True)
            dlt = hh - mu
            var = jnp.mean(dlt * dlt, axis=1, keepdims=True)
            y = dlt * lax.rsqrt(var + HEAD_NORM_EPS)
            out_ref[:, vs] = (y * ng_ref[:, vs] * mo_ref[:, vs].astype(F32)).astype(BF16)
            n_ref[:, qs] = (st.pop("e_c") * n_rows
                            + jnp.dot(st.pop("w_mat").astype(BF16), k, preferred_element_type=F32))

        return (gate_stage, score_stage, state_stage, output_stage), st

    heads = [head_stages(hl) for hl in range(hps)]
    for stage in range(4):
        if between is not None:
            between()
        for stages, _ in heads:
            stages[stage]()
    for hl, (_, st) in enumerate(heads):
        m_next = jnp.where(lane == head0 + hl, st["m_new"], m_next)

    m_ref[...] = m_next


def _mlstm(q, k, v, gates, mo, norm_g, state, nb, tpb, n_chunks, hps):
    m = q.shape[0]
    L = nb * tpb
    n_blocks = m // (L * n_chunks)
    hd = M_HEADS * M_DQK

    def tok(width):
        return pl.BlockSpec((L, width), lambda b, g, c: (b * n_chunks + c, g))

    c_spec = pl.BlockSpec((nb, hps, M_DV, M_DQK), lambda b, g, c: (b, g, 0, 0))
    n_spec = pl.BlockSpec((L, hps * M_DQK), lambda b, g, c: (b, g))
    m_spec = pl.BlockSpec((L, LANES), lambda b, g, c: (b, 0))
    in_specs = [tok(hps * M_DQK), tok(hps * M_DQK), tok(hps * M_DV),
                pl.BlockSpec((L, LANES), lambda b, g, c: (b * n_chunks + c, 0)),
                tok(hps * M_DV), pl.BlockSpec((1, hps * M_DV), lambda b, g, c: (0, g)),
                c_spec, n_spec, m_spec]
    args = [q, k, v, gates, mo, norm_g.reshape(1, M_HEADS * M_DV), *state]
    return pl.pallas_call(
        functools.partial(_mlstm_kernel, nb=nb, tpb=tpb, hps=hps),
        grid=(n_blocks, M_HEADS // hps, n_chunks),
        in_specs=in_specs,
        out_specs=[tok(hps * M_DV), c_spec, n_spec, m_spec],
        out_shape=[jax.ShapeDtypeStruct((m, M_HEADS * M_DV), BF16),
                   jax.ShapeDtypeStruct((n_blocks * nb, M_HEADS, M_DV, M_DQK), F32),
                   jax.ShapeDtypeStruct((n_blocks * L, hd), F32),
                   jax.ShapeDtypeStruct((n_blocks * L, LANES), F32)],
        scratch_shapes=[pltpu.VMEM((LANES, L), F32), pltpu.VMEM((LANES, L), F32)],
        compiler_params=_params(),
        name="mlstm",
    )(*args)


def _swa_kernel(sink_ref, q_ref, kp_ref, kc_ref, vp_ref, vc_ref, o_ref, kn_ref, vn_ref, *, nbb, n_new):
    tq = SAMPLE_TQ
    qi = lax.broadcasted_iota(jnp.int32, (tq, WINDOW), 0)
    kj = lax.broadcasted_iota(jnp.int32, (tq, WINDOW), 1)
    keep_old = lax.broadcasted_iota(jnp.int32, (WINDOW, W_AKV), 0) < WINDOW - n_new

    def new_rows(ref, jb, rows):
        x = ref[jb * n_new:(jb + 1) * n_new, :]
        return jnp.concatenate([x, jnp.zeros((rows - n_new, x.shape[1]), x.dtype)], axis=0)

    blocks = []
    for jb in range(nbb):
        kc, vc = new_rows(kc_ref, jb, WINDOW), new_rows(vc_ref, jb, WINDOW)
        kp, vp = kp_ref[jb], vp_ref[jb]
        kn_ref[jb] = jnp.where(keep_old, pltpu.roll(kp, WINDOW - n_new, 0), pltpu.roll(kc, WINDOW - n_new, 0))
        vn_ref[jb] = jnp.where(keep_old, pltpu.roll(vp, WINDOW - n_new, 0), pltpu.roll(vc, WINDOW - n_new, 0))
        q = new_rows(q_ref, jb, tq).astype(BF16)
        blocks.append(_swa_stages(sink_ref, q, kp, vp, kc, vc, kj > qi, o_ref, jb * n_new, n_new))

    for stage in range(3):
        for pairs in blocks:
            for pair in pairs:
                pair[stage]()


def _swa_stages(sink_ref, q, kp, vp, kc, vc, valid_prev, o_ref, o_row0=0, o_rows=None):
    tq = q.shape[0]
    o_rows = tq if o_rows is None else o_rows
    reps = A_HEADS // A_KV_HEADS
    low_q =lax.broadcasted_iota(jnp.int32, (tq, LANES), 1) < A_DH
    low_k = lax.broadcasted_iota(jnp.int32, (WINDOW, LANES), 1) < A_DH
    key_row = lax.broadcasted_iota(jnp.int32, (WINDOW, LANES), 0)
    qi = lax.broadcasted_iota(jnp.int32, (tq, WINDOW), 0)
    kj = lax.broadcasted_iota(jnp.int32, (tq, WINDOW), 1)
    sink_lane = kj == 0
    valid_cur = kj <= qi
    scale = jnp.asarray(ATTN_SCALE, BF16)
    neg_inf = -jnp.inf

    def block_diag(x):
        return jnp.concatenate([jnp.where(low_k, x, 0.0), jnp.where(low_k, 0.0, x)], axis=0).astype(BF16)

    def pair_stages(p):
        ks = slice(p * LANES, (p + 1) * LANES)
        state = {}

        def scores():
            q4 = jnp.concatenate([q[:, (reps * p + r) * LANES:(reps * p + r + 1) * LANES] * scale
                                  for r in range(reps)], axis=0)
            state["sp"] = lax.dot_general(q4, block_diag(kp[:, ks]), _NT, preferred_element_type=F32)
            state["sc"] = lax.dot_general(q4, block_diag(kc[:, ks]), _NT, preferred_element_type=F32)

        def softmax():
            sp, sc = state.pop("sp"), state.pop("sc")
            pps, pcs, invs = [], [], []
            for r in range(reps):
                rows = slice(r * tq, (r + 1) * tq)
                pp_r, pc_r, inv_r = [], [], []
                for half in range(2):
                    cols = slice(half * WINDOW, (half + 1) * WINDOW)
                    head = 2 * reps * p + reps * half + r
                    sp_i = jnp.where(sink_lane, sink_ref[head], jnp.where(valid_prev, sp[rows, cols], neg_inf))
                    sc_i = jnp.where(valid_cur, sc[rows, cols], neg_inf)
                    mx = jnp.max(jnp.maximum(sp_i, sc_i), axis=1, keepdims=True)
                    pp = jnp.exp(sp_i - mx)
                    pc = jnp.exp(sc_i - mx)
                    inv_r.append(1.0 / jnp.sum(pp + pc, axis=1, keepdims=True))
                    pp_r.append(pp.astype(BF16))
                    pc_r.append(pc.astype(BF16))
                pps.append(jnp.concatenate(pp_r, axis=1))
                pcs.append(jnp.concatenate(pc_r, axis=1))
                invs.append(jnp.where(low_q, inv_r[0], inv_r[1]))
            state.update(pp=jnp.concatenate(pps, axis=0), pc=jnp.concatenate(pcs, axis=0), inv=invs)

        def values():
            v_prev = block_diag(jnp.where(key_row == 0, 0.0, vp[:, ks]))
            o4 = (jnp.dot(state.pop("pp"), v_prev, preferred_element_type=F32)
                  + jnp.dot(state.pop("pc"), block_diag(vc[:, ks]), preferred_element_type=F32))
            invs = state.pop("inv")
            for r in range(reps):
                blk = reps * p + r
                o_ref[o_row0:o_row0 + o_rows, blk * LANES:(blk + 1) * LANES] = (
                    o4[r * tq:r * tq + o_rows] * invs[r][:o_rows]).astype(o_ref.dtype)

        return scores, softmax, values

    return [pair_stages(p) for p in range(A_KV_HEADS // 2)]


def _swa_block(sink_ref, q, kp, vp, kc, vc, valid_prev, o_ref, between=None):
    for scores, softmax, values in _swa_stages(sink_ref, q, kp, vp, kc, vc, valid_prev, o_ref):
        if between is not None:
            between()
        scores()
        softmax()
        if between is not None:
            between()
        values()


def _swa(sinks, q, k_cache, k_new, v_cache, v_new, *, nbb, n_new):
    n = k_cache.shape[0]
    assert n_new <= SAMPLE_TQ and q.shape[0] == n * n_new
    cache_spec = pl.BlockSpec((nbb, WINDOW, W_AKV), lambda i: (i, 0, 0))
    new_spec = pl.BlockSpec((nbb * n_new, W_AKV), lambda i: (i, 0))
    q_spec = pl.BlockSpec((nbb * n_new, W_AQ), lambda i: (i, 0))
    return pl.pallas_call(
        functools.partial(_swa_kernel, nbb=nbb, n_new=n_new),
        grid=(n // nbb,),
        in_specs=[pl.BlockSpec(memory_space=pltpu.SMEM), q_spec, cache_spec, new_spec, cache_spec, new_spec],
        out_specs=[q_spec, cache_spec, cache_spec],
        out_shape=[jax.ShapeDtypeStruct((n * n_new, W_AQ), F32)]
        + [jax.ShapeDtypeStruct((n, WINDOW, W_AKV), F32)] * 2,
        compiler_params=_params(),
        name="swa",
    )(sinks, q, k_cache, k_new, v_cache, v_new)


def _merge_kernel(x_ref, g_ref, hm_ref, oa_ref, gm_ref, ga_ref, wm_ref, wa_ref, wo_ref,
                  lg_ref, lb_ref, o_ref):
    ym = jnp.dot(hm_ref[...], wm_ref[...], preferred_element_type=F32)
    ya = jnp.dot(oa_ref[...].astype(BF16), wa_ref[...], preferred_element_type=F32)
    mix = gm_ref[...].astype(F32) * ym + ga_ref[...].astype(F32) * ya
    t = jnp.dot(mix.astype(BF16), wo_ref[...], preferred_element_type=F32)
    y = DEEPNORM_ALPHA * x_ref[...] + (1.0 + _rows(g_ref)) * t
    o_ref[...] = _layer_norm(y, lg_ref[...], lb_ref[...], LN_EPS)


def _merge(x, mod, per_row, rows_per_batch, chunk, hm, oa, gm, ga, wm, wa, wo, ln_g, ln_b):
    m = x.shape[0]
    tm = ROW_TILE

    def tok():
        return pl.BlockSpec((tm, D_MODEL), lambda i: (i, 0))

    return pl.pallas_call(
        _merge_kernel,
        grid=(m // tm,),
        in_specs=[tok()] + _mod_specs(per_row, tm, rows_per_batch // tm, (chunk,))
        + [tok(), tok(), tok(), tok()]
        + [_resident((D_MODEL, D_MODEL))] * 3 + [_resident((1, D_MODEL))] * 2,
        out_specs=tok(),
        out_shape=jax.ShapeDtypeStruct((m, D_MODEL), F32),
        compiler_params=_params(),
        name="merge",
    )(x, mod, hm, oa, gm, ga, wm, wa, wo, ln_g.reshape(1, D_MODEL), ln_b.reshape(1, D_MODEL))


def _mixer_kernel(sink_ref, xa_ref, xb_ref, sh_ref, sc_ref, g_ref, wa_ref, wq_ref, wr_ref, wif_ref, bif_ref,
                  cos_ref, sin_ref, ng_ref, wm_ref, wba_ref, wo_ref, lg_ref, lb_ref,
                  y_ref, c_ref, n_ref, m_ref, kb_ref, vb_ref, *scratch, tiles_per_seq):
    n_staged = (len(scratch) - 6) // 2
    even, odd = scratch[:n_staged], scratch[n_staged:2 * n_staged]
    kprev_s, vprev_s, hm_s, oa_s, gtt_s, ctt_s = scratch[2 * n_staged:]
    tm = xa_ref.shape[0]
    s = pl.program_id(0)
    tile_b = jnp.maximum(s - 1, 0)
    seq_start = tile_b % tiles_per_seq == 0

    @pl.when(s == 0)
    def _():
        for ref in odd:
            ref[...] = jnp.zeros_like(ref)

    @pl.when(seq_start)
    def _():
        c_ref[...] = jnp.zeros_like(c_ref)
        n_ref[...] = jnp.zeros_like(n_ref)
        m_ref[...] = jnp.zeros_like(m_ref)
        kprev_s[...] = jnp.zeros_like(kprev_s)
        vprev_s[...] = jnp.zeros_like(vprev_s)

    def step(produce, consume):
        proj_parts = _proj_parts(xa_ref, sh_ref, sc_ref, wa_ref, wq_ref, wr_ref, wif_ref, bif_ref,
                                 cos_ref, sin_ref, *produce)
        pending = iter(proj_parts)

        counts = iter(MIXER_PROJ_SCHEDULE)

        def between():
            for _ in range(next(counts)):
                part = next(pending, None)
                if part is not None:
                    part()

        q_s, k_s, v_s, mo_s, aq_s, ak_s, av_s, gm_s, ga_s, gt_s = consume
        qi = lax.broadcasted_iota(jnp.int32, (WINDOW, WINDOW), 0)
        kj = lax.broadcasted_iota(jnp.int32, (WINDOW, WINDOW), 1)
        for j in range(tm // MLSTM_CHUNK):
            rows = pl.ds(j * MLSTM_CHUNK, MLSTM_CHUNK)
            _mlstm_chunk(q_s.at[rows], k_s.at[rows], v_s.at[rows], gt_s.at[rows], mo_s.at[rows], ng_ref,
                         hm_s.at[rows], c_ref, n_ref, m_ref, gtt_s, ctt_s,
                         nb=1, tpb=MLSTM_CHUNK, hps=M_HEADS, head0=0, between=between)
            valid_prev = kj > qi
            if j == 0:
                valid_prev = valid_prev & jnp.logical_not(seq_start)
                k_prev, v_prev = kprev_s[...], vprev_s[...]
            else:
                before = pl.ds((j - 1) * WINDOW, WINDOW)
                k_prev, v_prev = ak_s[before], av_s[before]
            _swa_block(sink_ref, aq_s[rows], k_prev, v_prev, ak_s[rows], av_s[rows], valid_prev,
                       oa_s.at[rows], between=between)

        for part in pending:
            part()
        _merge_kernel(xb_ref, g_ref, hm_s, oa_s, gm_s, ga_s, wm_ref, wba_ref, wo_ref, lg_ref, lb_ref, y_ref)

        last = pl.ds(tm - WINDOW, WINDOW)
        k_last, v_last = ak_s[last], av_s[last]
        kb_ref[0] = k_last.T
        vb_ref[0] = v_last.T
        kprev_s[...] = k_last
        vprev_s[...] = v_last

    @pl.when(s % 2 == 0)
    def _():
        step(even, odd)

    @pl.when(s % 2 == 1)
    def _():
        step(odd, even)


def _mixer(x1, mod, w, sinks, norm_g, cos_t, sin_t, bp, sp):
    tm = MIXER_TILE
    tps = sp // tm
    n_tiles = bp * tps
    hd = M_HEADS * M_DQK
    weights = (w["w_a"], w["w_aq"], w["w_r"], w["w_if"], w["bif"])

    def tile_a(s):
        return jnp.minimum(s, n_tiles - 1)

    def tile_b(s):
        return jnp.maximum(s - 1, 0)

    def mod_spec(tile, chunk):
        return pl.BlockSpec((1, 1, D_MODEL), lambda s: (tile(s) // tps, 0, chunk))

    def per_seq(*shape):
        return pl.BlockSpec((1,) + shape, lambda s: (tile_b(s) // tps,) + (0,) * len(shape))

    pos = pl.BlockSpec((tm, LANES), lambda s: (tile_a(s) % tps, 0))
    staged = [(W_MQ, BF16), (W_MQ, BF16), (W_MV, BF16), (W_MV, BF16), (W_AQ, BF16),
              (W_AKV, F32), (W_AKV, F32), (D_MODEL, BF16), (D_MODEL, BF16), (LANES, F32)]
    return pl.pallas_call(
        functools.partial(_mixer_kernel, tiles_per_seq=tps),
        grid=(n_tiles + 1,),
        in_specs=[pl.BlockSpec(memory_space=pltpu.SMEM),
                  pl.BlockSpec((tm, D_MODEL), lambda s: (tile_a(s), 0)),
                  pl.BlockSpec((tm, D_MODEL), lambda s: (tile_b(s), 0)),
                  mod_spec(tile_a, 3), mod_spec(tile_a, 4), mod_spec(tile_b, 5)]
        + [_resident(a.shape) for a in weights] + [pos, pos, _resident((1, M_HEADS * M_DV))]
        + [_resident((D_MODEL, D_MODEL))] * 3 + [_resident((1, D_MODEL))] * 2,
        out_specs=[pl.BlockSpec((tm, D_MODEL), lambda s: (tile_b(s), 0)),
                   per_seq(M_HEADS, M_DV, M_DQK),
                   pl.BlockSpec((MLSTM_CHUNK, hd), lambda s: (tile_b(s) // tps, 0)),
                   pl.BlockSpec((MLSTM_CHUNK, LANES), lambda s: (tile_b(s) // tps, 0)),
                   per_seq(W_AKV, WINDOW), per_seq(W_AKV, WINDOW)],
        out_shape=[jax.ShapeDtypeStruct((bp * sp, D_MODEL), F32),
                   jax.ShapeDtypeStruct((bp, M_HEADS, M_DV, M_DQK), F32),
                   jax.ShapeDtypeStruct((bp * MLSTM_CHUNK, hd), F32),
                   jax.ShapeDtypeStruct((bp * MLSTM_CHUNK, LANES), F32),
                   jax.ShapeDtypeStruct((bp, W_AKV, WINDOW), F32),
                   jax.ShapeDtypeStruct((bp, W_AKV, WINDOW), F32)],
        scratch_shapes=[pltpu.VMEM((tm, width), dtype) for width, dtype in staged + staged]
        + [pltpu.VMEM((WINDOW, W_AKV), F32), pltpu.VMEM((WINDOW, W_AKV), F32),
           pltpu.VMEM((tm, W_MV), BF16), pltpu.VMEM((tm, W_AQ), BF16),
           pltpu.VMEM((LANES, MLSTM_CHUNK), F32), pltpu.VMEM((LANES, MLSTM_CHUNK), F32)],
        compiler_params=_params(),
        name="mixer",
    )(sinks, x1, x1, mod, mod, mod, *weights, cos_t, sin_t, norm_g.reshape(1, M_HEADS * M_DV),
      w["wm"], w["wa"], w["wo"], w["ln2_g"].reshape(1, D_MODEL), w["ln2_b"].reshape(1, D_MODEL))


def _rope_tables(pos):
    half = A_DH // 2
    inv = np.float32(ROPE_THETA) ** (-np.arange(half, dtype=np.float32) / np.float32(half))
    ang = pos.astype(np.float32)[:, None] * inv[None, :]
    cos, sin = np.cos(ang), np.sin(ang)
    return jnp.asarray(np.tile(cos, (1, 4))), jnp.asarray(np.concatenate([-sin, sin, -sin, sin], axis=1))


def _from_stored(kv_t):
    return jnp.transpose(kv_t.reshape(kv_t.shape[0], A_KV_HEADS, A_DH, WINDOW), (0, 3, 1, 2))


def _token_stage_2(x1, mod, per_row, rows_per_batch, w, hm, oa, gm, ga):
    x2 = _merge(x1, mod, per_row, rows_per_batch, 5, hm, oa, gm, ga,
                w["wm"], w["wa"], w["wo"], w["ln2_g"], w["ln2_b"])
    return _ffn(x2, mod, per_row, rows_per_batch, (6, 7, 8), w["up2"], w["down2"], w["ln3_g"], w["ln3_b"])


def kernel(x_prompt, x_sample, state_mlstm_C, state_mlstm_n, state_mlstm_m, cache_swa_k, cache_swa_v, c_prompt, c_sample, w_ada, b_ada, w_ffn1_up, w_ffn1_down, ln1_g, ln1_b, w_in, b_igate, b_fgate, m_norm_g, sinks, w_branch_m, w_branch_a, w_out, ln2_g, ln2_b, w_ffn2_up, w_ffn2_down, ln3_g, ln3_b):
    assert w_ada.shape[0] == DEPTH == 1
    bp, sp, _ = x_prompt.shape
    bs, ts, _ = x_sample.shape

    w_in_t = jnp.transpose(w_in[0])
    w = dict(
        bif=jnp.concatenate([b_igate[0], b_fgate[0], jnp.zeros((LANES - 2 * M_HEADS,), F32)]).reshape(1, LANES),
        ln1_g=ln1_g[0], ln1_b=ln1_b[0], ln2_g=ln2_g[0], ln2_b=ln2_b[0], ln3_g=ln3_g[0], ln3_b=ln3_b[0],
    )

    ms = bs * ts
    c_all = jnp.concatenate([jnp.repeat(c_sample, ts, axis=0), c_prompt], axis=0)
    mod = _ada(c_all, w_ada[0], b_ada[0])
    mod_p = mod[ms:].reshape(bp, 1, ADA_CHUNKS * D_MODEL)

    x1s, w["up1"], w["down1"] = _ffn_round(x_sample.reshape(ms, D_MODEL), mod, (0, 1, 2),
                                           w_ffn1_up[0], w_ffn1_down[0], w["ln1_g"], w["ln1_b"])

    mp = bp * sp
    reps = A_HEADS // A_KV_HEADS
    pair_low = lambda j: (j // reps) * (2 * reps) + j % reps
    pair_rows = lambda j: (j // (2 * reps)) * (2 * reps) + (j % 2) * reps + (j % (2 * reps)) // 2
    jobs = (
        _cast_job(w_ffn2_up[0], (D_MODEL, FF_CHUNK), 1),
        _cast_job(w_ffn2_down[0], (FF_CHUNK, D_MODEL), 0, first_step=2 * D_FF // FF_CHUNK),
        _cast_job(w_in_t, (D_MODEL, LANES), 1, transposed=True, n_blocks=IN_IF // LANES),
        _cast_job(w_in_t, (D_MODEL, LANES), 1, transposed=True, n_blocks=W_AQ // LANES, shift=IN_AQ - IN_IF,
                  sources=(lambda j: IN_IF // A_DH + pair_low(j), lambda j: IN_IF // A_DH + pair_low(j) + reps)),
        _cast_job(w_in_t, (D_MODEL, LANES), 1, transposed=True, n_blocks=(IN_END - IN_AK) // LANES,
                  shift=IN_AQ - IN_IF, sources=(lambda j: (IN_AK - (IN_AQ - IN_IF)) // LANES + j,)),
        _cast_job(w_branch_m[0], (D_MODEL, LANES), 1),
        _cast_job(w_out[0], (D_MODEL, LANES), 1),
        _cast_job(w_branch_a[0], (A_DH, D_MODEL), 0, sources=(pair_rows,)),
        _cast_job(w_in_t, (LANES, D_MODEL), 0, n_blocks=1, sources=(lambda j: IN_IF // LANES + j,)),
    )
    x1p, w["up2"], w["down2"], w["w_a"], w["w_aq"], w["w_r"], w["wm"], w["wo"], w["wa"], w["w_if"] = _ffn(
        x_prompt.reshape(mp, D_MODEL), mod_p, False, sp, (0, 1, 2),
        w["up1"], w["down1"], w["ln1_g"], w["ln1_b"], cast_jobs=jobs)
    cos_t, sin_t = _rope_tables(np.arange(sp))
    x2p, c_p, n_rows, m_rows, kb_p, vb_p = _mixer(x1p, mod_p, w, sinks[0], m_norm_g[0], cos_t, sin_t, bp, sp)
    y_p = _ffn(x2p, mod_p, False, sp, (6, 7, 8), w["up2"], w["down2"], w["ln3_g"], w["ln3_b"])
    n_p = n_rows.reshape(bp, MLSTM_CHUNK, M_HEADS * M_DQK)[:, 0].reshape(bp, M_HEADS, M_DQK)
    m_p = m_rows.reshape(bp, MLSTM_CHUNK, LANES)[:, 0, :M_HEADS]
    kb_p, vb_p = _from_stored(kb_p), _from_stored(vb_p)

    cos_t, sin_t = _rope_tables(PAST_LEN + np.arange(ms) % ts)
    qm, km, vm, mo, aq, ak, av, gm, ga, gt = _proj(x1s, mod, True, ms, (3, 4), w, cos_t, sin_t)
    seqs = MLSTM_CHUNK // ts
    n0_rows = jnp.repeat(state_mlstm_n[0].reshape(bs, M_HEADS * M_DQK), ts, axis=0)
    m0_rows = jnp.repeat(jnp.pad(state_mlstm_m[0], ((0, 0), (0, LANES - M_HEADS))), ts, axis=0)
    hm, c_s, n_rows, m_rows = _mlstm(qm, km, vm, gt, mo, m_norm_g[0],
                                     (state_mlstm_C[0], n0_rows, m0_rows), seqs, ts, 1, 2)
    n_s = n_rows.reshape(bs, ts, M_HEADS * M_DQK)[:, 0].reshape(bs, M_HEADS, M_DQK)
    m_s = m_rows.reshape(bs, ts, LANES)[:, 0, :M_HEADS]
    oa, kb_s, vb_s = _swa(
        sinks[0], aq, cache_swa_k[0].reshape(bs, WINDOW, W_AKV), ak,
        cache_swa_v[0].reshape(bs, WINDOW, W_AKV), av, nbb=SWA_SAMPLE_SEQS, n_new=ts)
    y_s = _token_stage_2(x1s, mod, True, ms, w, hm, oa, gm, ga)
    kb_s = kb_s.reshape(bs, WINDOW, A_KV_HEADS, A_DH)
    vb_s = vb_s.reshape(bs, WINDOW, A_KV_HEADS, A_DH)

    return (y_p.reshape(bp, sp, D_MODEL), y_s.reshape(bs, ts, D_MODEL),
            c_p[None], n_p[None], m_p[None], kb_p[None], vb_p[None],
            c_s[None], n_s[None], m_s[None], kb_s[None], vb_s[None])
```

```python
import functools

import jax
import jax.numpy as jnp
import numpy as np
from jax import lax
from jax.experimental import pallas as pl
from jax.experimental.pallas import tpu as pltpu

F32 = jnp.float32
BF16 = jnp.bfloat16

D_MODEL = 1024
D_FF = 2816
DEPTH = 1
M_HEADS = 4
M_DQK = 128
M_DV = 256
A_HEADS = 16
A_KV_HEADS = 4
A_DH = 64
WINDOW = 128
PAST_LEN = 8192
ROPE_THETA = 10000.0
ATTN_SCALE = A_DH ** -0.5
LN_EPS = 1e-5
HEAD_NORM_EPS = 1e-6
ADA_CHUNKS = 9
DEEPNORM_ALPHA = (2.0 * DEPTH) ** 0.25
K_SCALE = M_DQK ** -0.5

LANES = 128
BF16_SUBLANES = 16
VMEM_LIMIT_BYTES = 56 * 1024 * 1024

W_MQ = M_HEADS * M_DQK
W_MV = M_HEADS * M_DV
W_AQ = A_HEADS * A_DH
W_AKV = A_KV_HEADS * A_DH
IN_IF = 2 * W_MQ + 2 * W_MV
IN_AQ = IN_IF + 2 * M_HEADS
IN_AK = IN_AQ + W_AQ
IN_END = IN_AK + 2 * W_AKV + 2 * D_MODEL
A_Q, A_K, A_V, A_O = 0, W_MQ, 2 * W_MQ, 2 * W_MQ + W_MV
R_AK, R_AV, R_GM, R_GA = 0, W_AKV, 2 * W_AKV, 2 * W_AKV + D_MODEL

ROW_TILE = 512
MIXER_TILE = 256
MIXER_PROJ_SCHEDULE = (1, 1, 1, 1, 2, 2, 2, 2) * 2
ADA_TILE = 1536
FF_CHUNK = 256
PROJ_CHUNK = 2 * LANES
FFN_EPILOGUE_PIECES = 4
MLSTM_CHUNK = 128
SAMPLE_TQ = BF16_SUBLANES
SWA_SAMPLE_SEQS = 16

_NT = (((1,), (1,)), ((), ()))


def _params():
    return pltpu.CompilerParams(vmem_limit_bytes=VMEM_LIMIT_BYTES)


def _resident(shape):
    return pl.BlockSpec(shape, lambda *_: (0,) * len(shape), pipeline_mode=pl.Buffered(1))


def _rows(ref):
    v = ref[...]
    return v.reshape(v.shape[-2], v.shape[-1])


def _layer_norm(y, g, b, eps):
    mu = jnp.mean(y, axis=-1, keepdims=True)
    d = y - mu
    var = jnp.mean(d * d, axis=-1, keepdims=True)
    return d * lax.rsqrt(var + eps) * g + b


def _sigmoid(x):
    return 1.0 / (1.0 + jnp.exp(-x))


def _mod_specs(per_row, tm, tiles_per_batch, chunks):
    if per_row:
        return [pl.BlockSpec((tm, D_MODEL), lambda i, c=c: (i, c)) for c in chunks]
    return [pl.BlockSpec((1, 1, D_MODEL), lambda i, c=c: (i // tiles_per_batch, 0, c)) for c in chunks]


def _ada_kernel(c_ref, w_ref, b_ref, o_ref):
    c = c_ref[...]
    s = (c * _sigmoid(c)).astype(BF16)
    o_ref[...] = jnp.dot(s, w_ref[...].astype(BF16), preferred_element_type=F32) + b_ref[...]


def _ada(c_all, w_ada, b_ada):
    rows = c_all.shape[0]
    n_out = w_ada.shape[1]
    tn = ADA_TILE
    return pl.pallas_call(
        _ada_kernel,
        grid=(n_out // tn,),
        in_specs=[pl.BlockSpec((rows, D_MODEL), lambda j: (0, 0)),
                  pl.BlockSpec((D_MODEL, tn), lambda j: (0, j)),
                  pl.BlockSpec((1, tn), lambda j: (0, j))],
        out_specs=pl.BlockSpec((rows, tn), lambda j: (0, j)),
        out_shape=jax.ShapeDtypeStruct((rows, n_out), F32),
        compiler_params=_params(),
        name="ada",
    )(c_all, w_ada, b_ada.reshape(1, n_out))


def _ffn_kernel(x_ref, xp_ref, sh_ref, sc_ref, g_ref, wup_ref, wdn_ref, lg_ref, lb_ref, *rest,
                n_tiles, cast_layout):
    n_cast = len(cast_layout)
    n_src = sum(layout[0] for layout in cast_layout)
    cast_in, o_ref, cast_out = rest[:n_src], rest[n_src], rest[n_src + 1:n_src + 1 + n_cast]
    act_ref, f_ref = rest[n_src + 1 + n_cast:]

    def cast_chunk(job):
        if job >= n_cast:
            return
        first = sum(layout[0] for layout in cast_layout[:job])
        n, transposed, shift = cast_layout[job]
        refs = cast_in[first:first + n]
        if shift:
            rows = refs[0].shape[0]
            parts = [jnp.concatenate([lo[...], hi[...]], axis=0)[shift:shift + rows]
                     for lo, hi in zip(refs[0::2], refs[1::2])]
        else:
            parts = [ref[...] for ref in refs]
        block = jnp.concatenate(parts, axis=0)
        cast_out[job][...] = (block.T if transposed else block).astype(BF16)

    tm = x_ref.shape[0]
    s = pl.program_id(0)
    piece_rows = tm // FFN_EPILOGUE_PIECES

    def epilogue(piece):
        rows = pl.ds(piece * piece_rows, piece_rows)
        g = _rows(g_ref)
        if g.shape[0] != 1:
            g = g[piece * piece_rows:(piece + 1) * piece_rows]
        y = DEEPNORM_ALPHA * xp_ref[rows, :] + (0.5 * (1.0 + g)) * f_ref[rows, :]
        o_ref[rows, :] = _layer_norm(y, lg_ref[...], lb_ref[...], LN_EPS)

    def matmuls(previous_epilogue):
        h = (x_ref[...] * (1.0 + _rows(sc_ref)) + _rows(sh_ref)).astype(BF16)
        chunks = list(range(0, D_FF, FF_CHUNK))
        every = len(chunks) // FFN_EPILOGUE_PIECES
        piece = 0
        for i, c in enumerate(chunks):
            a = jnp.dot(h, wup_ref[:, c:c + FF_CHUNK], preferred_element_type=F32)
            u = jnp.dot(h, wup_ref[:, D_FF + c:D_FF + c + FF_CHUNK], preferred_element_type=F32)
            act_ref[:, c:c + FF_CHUNK] = (a * _sigmoid(a) * u).astype(BF16)
            cast_chunk(i)
            if previous_epilogue and i % every == every - 1 and piece < FFN_EPILOGUE_PIECES:
                epilogue(piece)
                piece += 1
        f_ref[...] = jnp.dot(act_ref[...], wdn_ref[...], preferred_element_type=F32)

    if n_tiles == 1:
        matmuls(False)
        for piece in range(FFN_EPILOGUE_PIECES):
            epilogue(piece)
        return

    @pl.when(s == 0)
    def _():
        f_ref[...] = jnp.zeros_like(f_ref)

    @pl.when(s < n_tiles)
    def _():
        matmuls(True)

    @pl.when(s == n_tiles)
    def _():
        for job in range(n_cast):
            cast_chunk(job)
        for piece in range(FFN_EPILOGUE_PIECES):
            epilogue(piece)


def _cast_job(array, block, axis, first_step=0, sources=None, transposed=False, n_blocks=None, shift=0):
    sources = sources or (lambda j: j,)
    src_axis = 1 - axis if transposed else axis
    src_block = block[::-1] if transposed else block
    src_block = (src_block[0] // len(sources), src_block[1])
    if n_blocks is None:
        n_blocks = array.shape[src_axis] // block[axis]

    def step_block(s):
        return jnp.clip(s - first_step, 0, n_blocks - 1)

    def at(axis_, j):
        return (j, 0) if axis_ == 0 else (0, j)

    in_specs = [pl.BlockSpec(src_block, lambda s, f=f, d=d: at(src_axis, f(step_block(s)) + d))
                for f in sources for d in range(2 if shift else 1)]
    out_dims = list(array.shape[::-1] if transposed else array.shape)
    out_dims[axis] = n_blocks * block[axis]
    return dict(array=array, in_specs=in_specs, layout=(len(in_specs), transposed, shift),
                out_spec=pl.BlockSpec(block, lambda s: at(axis, step_block(s))),
                out_shape=jax.ShapeDtypeStruct(tuple(out_dims), BF16), steps=first_step + n_blocks)


def _ffn(x, mod, per_row, rows_per_batch, chunks, w_up, w_down, ln_g, ln_b, cast_jobs=()):
    m = x.shape[0]
    tm = ROW_TILE
    n_tiles = m // tm
    tpb = rows_per_batch // tm

    def cur(s):
        return jnp.minimum(s, n_tiles - 1)

    def prev(s):
        return jnp.maximum(s - 1, 0)

    def mod_spec(tile, chunk):
        if per_row:
            return pl.BlockSpec((tm, D_MODEL), lambda s: (tile(s), chunk))
        return pl.BlockSpec((1, 1, D_MODEL), lambda s: (tile(s) // tpb, 0, chunk))

    in_specs = [pl.BlockSpec((tm, D_MODEL), lambda s: (cur(s), 0)),
                pl.BlockSpec((tm, D_MODEL), lambda s: (prev(s), 0)),
                mod_spec(cur, chunks[0]), mod_spec(cur, chunks[1]), mod_spec(prev, chunks[2]),
                _resident((D_MODEL, 2 * D_FF)), _resident((D_FF, D_MODEL)),
                _resident((1, D_MODEL)), _resident((1, D_MODEL))]
    args = [x, x, mod, mod, mod, w_up, w_down, ln_g.reshape(1, D_MODEL), ln_b.reshape(1, D_MODEL)]
    out_specs = [pl.BlockSpec((tm, D_MODEL), lambda s: (prev(s), 0))]
    out_shape = [jax.ShapeDtypeStruct((m, D_MODEL), F32)]
    n_steps = 1 if n_tiles == 1 else n_tiles + 1
    for job in cast_jobs:
        assert job["steps"] <= n_steps
        in_specs += job["in_specs"]
        args += [job["array"]] * len(job["in_specs"])
        out_specs.append(job["out_spec"])
        out_shape.append(job["out_shape"])
    outs = pl.pallas_call(
        functools.partial(_ffn_kernel, n_tiles=n_tiles, cast_layout=tuple(j["layout"] for j in cast_jobs)),
        grid=(n_steps,),
        in_specs=in_specs,
        out_specs=out_specs,
        out_shape=out_shape,
        scratch_shapes=[pltpu.VMEM((tm, D_FF), BF16), pltpu.VMEM((tm, D_MODEL), F32)],
        compiler_params=_params(),
        name="ffn",
    )(*args)
    return outs if cast_jobs else outs[0]


def _ffn_round_kernel(x_ref, sh_ref, sc_ref, g_ref, wup_ref, wdn_ref, lg_ref, lb_ref,
                      o_ref, up_ref, dn_ref, h_ref, acc_ref, f_ref, *, n_up, n_chunks):
    k = pl.program_id(0)

    @pl.when(k == 0)
    def _():
        h = (x_ref[...] * (1.0 + _rows(sc_ref)) + _rows(sh_ref)).astype(BF16)
        for i in range(n_up):
            h_ref[i] = h[:, i * LANES:(i + 1) * LANES]
        acc_ref[...] = jnp.zeros_like(acc_ref)
        f_ref[...] = jnp.zeros_like(f_ref)

    @pl.when(k < n_up)
    def _():
        h = h_ref[k]
        for c in range(2 * n_chunks):
            up = wup_ref[:, c * FF_CHUNK:(c + 1) * FF_CHUNK].astype(BF16)
            up_ref[:, c * FF_CHUNK:(c + 1) * FF_CHUNK] = up
            acc_ref[c] += jnp.dot(h, up, preferred_element_type=F32)

    @pl.when(k >= n_up)
    def _():
        j = k - n_up
        dn = wdn_ref[...].astype(BF16)
        dn_ref[...] = dn
        a = acc_ref[j]
        act = (a * _sigmoid(a) * acc_ref[n_chunks + j]).astype(BF16)
        f_ref[...] += jnp.dot(act, dn, preferred_element_type=F32)

    @pl.when(k == n_up + n_chunks - 1)
    def _():
        y = DEEPNORM_ALPHA * x_ref[...] + (0.5 * (1.0 + _rows(g_ref))) * f_ref[...]
        o_ref[...] = _layer_norm(y, lg_ref[...], lb_ref[...], LN_EPS)


def _ffn_round(x, mod, chunks, w_up, w_down, ln_g, ln_b):
    tm = x.shape[0]
    assert tm == ROW_TILE
    n_chunks = D_FF // FF_CHUNK
    n_up = D_MODEL // LANES

    def tok(chunk):
        return pl.BlockSpec((tm, D_MODEL), lambda k: (0, chunk), pipeline_mode=pl.Buffered(1))

    def up_block(k):
        return (jnp.minimum(k, n_up - 1), 0)

    def down_block(k):
        return (jnp.maximum(k - n_up, 0), 0)

    return pl.pallas_call(
        functools.partial(_ffn_round_kernel, n_up=n_up, n_chunks=n_chunks),
        grid=(n_up + n_chunks,),
        in_specs=[tok(0), tok(chunks[0]), tok(chunks[1]), tok(chunks[2]),
                  pl.BlockSpec((LANES, 2 * D_FF), up_block),
                  pl.BlockSpec((FF_CHUNK, D_MODEL), down_block),
                  _resident((1, D_MODEL)), _resident((1, D_MODEL))],
        out_specs=[pl.BlockSpec((tm, D_MODEL), lambda k: (0, 0)),
                   pl.BlockSpec((LANES, 2 * D_FF), up_block),
                   pl.BlockSpec((FF_CHUNK, D_MODEL), down_block)],
        out_shape=[jax.ShapeDtypeStruct((tm, D_MODEL), F32),
                   jax.ShapeDtypeStruct(w_up.shape, BF16), jax.ShapeDtypeStruct(w_down.shape, BF16)],
        scratch_shapes=[pltpu.VMEM((n_up, tm, LANES), BF16), pltpu.VMEM((2 * n_chunks, tm, FF_CHUNK), F32),
                        pltpu.VMEM((tm, D_MODEL), F32)],
        compiler_params=_params(),
        name="ffn_round",
    )(x, mod, mod, mod, w_up, w_down, ln_g.reshape(1, D_MODEL), ln_b.reshape(1, D_MODEL))


def _proj_parts(x_ref, sh_ref, sc_ref, wa_ref, wq_ref, wr_ref, wif_ref, bif_ref, cos_ref, sin_ref,
                q_ref, k_ref, v_ref, o_ref, aq_ref, ak_ref, av_ref, gm_ref, ga_ref, gt_ref):
    x = x_ref[...]
    tm = x.shape[0]
    h = (x * (1.0 + _rows(sc_ref)) + _rows(sh_ref)).astype(BF16)
    lane = lax.broadcasted_iota(jnp.int32, (tm, LANES), 1)
    pc = PROJ_CHUNK

    def seg(w_ref, lo):
        return jnp.dot(h, w_ref[:, lo:lo + pc], preferred_element_type=F32)

    def plain(dst_ref, w_ref, lo, c, scale=None):
        def run():
            z = seg(w_ref, lo + c)
            dst_ref[:, c:c + pc] = (z if scale is None else z * scale).astype(dst_ref.dtype)
        return run

    def gate(dst_ref, w_ref, lo, c):
        def run():
            dst_ref[:, c:c + pc] = _sigmoid(seg(w_ref, lo + c)).astype(BF16)
        return run

    def forget_input_gates():
        zg = lax.dot_general(h, wif_ref[...], _NT, preferred_element_type=F32) + bif_ref[...]
        logsig = jnp.minimum(zg, 0.0) - jnp.log(1.0 + jnp.exp(-jnp.abs(zg)))
        gt_ref[...] = jnp.where(lane < M_HEADS, zg, logsig)

    def rotary(dst_ref, w_ref, lo, c):
        def run():
            cos = cos_ref[...]
            sin = sin_ref[...]
            low_half = (lane & (A_DH // 2)) == 0
            z = seg(w_ref, lo + c)
            for half in range(2):
                zh = z[:, half * LANES:(half + 1) * LANES]
                partner = jnp.where(low_half, pltpu.roll(zh, LANES - A_DH // 2, 1),
                                    pltpu.roll(zh, A_DH // 2, 1))
                dst_ref[:, c + half * LANES:c + (half + 1) * LANES] = (
                    zh * cos + partner * sin).astype(dst_ref.dtype)
        return run

    parts = [plain(q_ref, wa_ref, A_Q, c) for c in range(0, W_MQ, pc)]
    parts += [plain(k_ref, wa_ref, A_K, c, K_SCALE) for c in range(0, W_MQ, pc)]
    parts += [plain(v_ref, wa_ref, A_V, c) for c in range(0, W_MV, pc)]
    parts += [forget_input_gates]
    parts += [rotary(aq_ref, wq_ref, 0, c) for c in range(0, W_AQ, pc)]
    parts += [rotary(ak_ref, wr_ref, R_AK, 0), plain(av_ref, wr_ref, R_AV, 0)]
    parts += [gate(o_ref, wa_ref, A_O, c) for c in range(0, W_MV, pc)]
    parts += [gate(gm_ref, wr_ref, R_GM, c) for c in range(0, D_MODEL, pc)]
    parts += [gate(ga_ref, wr_ref, R_GA, c) for c in range(0, D_MODEL, pc)]
    return parts


def _proj_kernel(*refs):
    for part in _proj_parts(*refs):
        part()


def _proj(x, mod, per_row, rows_per_batch, chunks, w, cos_t, sin_t):
    m = x.shape[0]
    tm = ROW_TILE
    n_pos_tiles = cos_t.shape[0] // tm

    def tok(width):
        return pl.BlockSpec((tm, width), lambda i: (i, 0))

    widths = (W_MQ, W_MQ, W_MV, W_MV, W_AQ, W_AKV, W_AKV, D_MODEL, D_MODEL, LANES)
    dtypes = (BF16, BF16, BF16, BF16, F32, F32, F32, BF16, BF16, F32)
    weights = (w["w_a"], w["w_aq"], w["w_r"], w["w_if"], w["bif"])
    return pl.pallas_call(
        _proj_kernel,
        grid=(m // tm,),
        in_specs=[tok(D_MODEL)] + _mod_specs(per_row, tm, rows_per_batch // tm, chunks)
        + [_resident(a.shape) for a in weights]
        + [pl.BlockSpec((tm, LANES), lambda i: (i % n_pos_tiles, 0)),
           pl.BlockSpec((tm, LANES), lambda i: (i % n_pos_tiles, 0))],
        out_specs=[tok(wd) for wd in widths],
        out_shape=[jax.ShapeDtypeStruct((m, wd), d) for wd, d in zip(widths, dtypes)],
        compiler_params=_params(),
        name="proj",
    )(x, mod, mod, *weights, cos_t, sin_t)


def _mlstm_kernel(q_ref, k_ref, v_ref, g_ref, mo_ref, ng_ref, c0_ref, n0_ref, m0_ref,
                  out_ref, c_ref, n_ref, m_ref, gt_s, ct_s, *, nb, tpb, hps):
    first_chunk = pl.program_id(2) == 0

    @pl.when(first_chunk)
    def _():
        c_ref[...] = c0_ref[...]
        n_ref[...] = n0_ref[...]

    @pl.when(first_chunk & (pl.program_id(1) == 0))
    def _():
        m_ref[...] = m0_ref[...]

    head0 = 0 if hps == M_HEADS else pl.program_id(1) * hps
    _mlstm_chunk(q_ref, k_ref, v_ref, g_ref, mo_ref, ng_ref, out_ref, c_ref, n_ref, m_ref, gt_s, ct_s,
                 nb=nb, tpb=tpb, hps=hps, head0=head0)


def _mlstm_chunk(q_ref, k_ref, v_ref, g_ref, mo_ref, ng_ref, out_ref, c_ref, n_ref, m_ref, gt_s, ct_s,
                 *, nb, tpb, hps, head0, between=None):
    L = nb * tpb
    shift = tpb.bit_length() - 1
    row = lax.broadcasted_iota(jnp.int32, (L, L), 0)
    col = lax.broadcasted_iota(jnp.int32, (L, L), 1)
    same = (row >> shift) == (col >> shift)
    causal = same & (col <= row)
    lane = lax.broadcasted_iota(jnp.int32, (L, LANES), 1)
    row_seq = lax.broadcasted_iota(jnp.int32, (L, 1), 0) >> shift

    def lane_col(x, idx):
        return jnp.sum(jnp.where(lane == idx, x, 0.0), axis=1, keepdims=True)

    gates = g_ref[...]
    tri = jnp.where(causal, 1.0, 0.0).astype(BF16)
    g_hi = gates.astype(BF16)
    rem = gates - g_hi.astype(F32)
    g_mid = rem.astype(BF16)
    g_lo = (rem - g_mid.astype(F32)).astype(BF16)
    cum = (jnp.dot(tri, g_hi, preferred_element_type=F32)
           + jnp.dot(tri, g_mid, preferred_element_type=F32)
           + jnp.dot(tri, g_lo, preferred_element_type=F32))
    gt_s[...] = gates.T
    ct_s[...] = cum.T
    m_rows = m_ref[...]
    m_next = m_rows

    def head_stages(hl):
        head = head0 + hl
        qs = slice(hl * M_DQK, (hl + 1) * M_DQK)
        vs = slice(hl * M_DV, (hl + 1) * M_DV)
        st = {}

        def gate_stage():
            b_c = lane_col(cum, M_HEADS + head)
            m_p = lane_col(m_rows, head)
            i_r = gt_s[pl.ds(head, 1), :]
            b_r = ct_s[pl.ds(M_HEADS + head, 1), :]
            log_d = jnp.where(causal, b_c - b_r + i_r, -jnp.inf)
            m_t = jnp.maximum(b_c + m_p, jnp.max(log_d, axis=1, keepdims=True))
            dmat = jnp.exp(log_d - m_t)
            e_int = jnp.exp(b_c + m_p - m_t)
            if nb == 1:
                last = slice(L - 1, L)
                m_new = jnp.broadcast_to(m_t[last], (L, 1))
                e_c = jnp.broadcast_to(e_int[last], (L, 1))
                w_mat = jnp.broadcast_to(dmat[last], (L, L))
            else:
                b_last = jnp.min(jnp.where(same, b_r, jnp.inf), axis=1, keepdims=True)
                log_w = jnp.where(same, b_last - b_r + i_r, -jnp.inf)
                m_new = jnp.maximum(b_last + m_p, jnp.max(log_w, axis=1, keepdims=True))
                w_mat = jnp.exp(log_w - m_new)
                e_c = jnp.exp(b_last + m_p - m_new)
            st.update(m_t=m_t, dmat=dmat, e_int=e_int, m_new=m_new, e_c=e_c, w_mat=w_mat)

        def score_stage():
            q, k = q_ref[:, qs], k_ref[:, qs]
            st["s"] = lax.dot_general(q, k, _NT, preferred_element_type=F32) * st.pop("dmat")

        def state_stage():
            q, k = q_ref[:, qs], k_ref[:, qs]
            v_t = v_ref[:, vs].astype(F32).T
            w_mat = st["w_mat"]
            e_cb = jnp.broadcast_to(st["e_c"], (L, LANES))
            inter = jnp.zeros((L, M_DV), F32)
            for j in range(nb):
                first = slice(j * tpb, j * tpb + 1)
                c_j = c_ref[j, hl]
                q_j = q if nb == 1 else jnp.where(row_seq == j, q, jnp.zeros_like(q))
                inter = inter + lax.dot_general(q_j, c_j.astype(BF16), _NT, preferred_element_type=F32)
                lhs = (v_t * w_mat[first]).astype(BF16)
                c_ref[j, hl] = e_cb[first] * c_j + jnp.dot(lhs, k, preferred_element_type=F32)
            st["inter"] = inter

        def output_stage():
            q, k, v = q_ref[:, qs], k_ref[:, qs], v_ref[:, vs]
            s, e_int, m_t = st.pop("s"), st.pop("e_int"), st.pop("m_t")
            n_rows = n_ref[:, qs]
            qn = jnp.sum(q.astype(F32) * n_rows, axis=1, keepdims=True)
            num = jnp.dot(s.astype(BF16), v, preferred_element_type=F32) + e_int * st.pop("inter")
            den = jnp.sum(s, axis=1, keepdims=True) + e_int * qn
            hh = num / jnp.maximum(jnp.abs(den), jnp.exp(-m_t))
            mu = jnp.mean(hh, axis=1, keepdims=True)
            dlt = hh - mu
            var = jnp.mean(dlt * dlt, axis=1, keepdims=True)
            y = dlt * lax.rsqrt(var + HEAD_NORM_EPS)
            out_ref[:, vs] = (y * ng_ref[:, vs] * mo_ref[:, vs].astype(F32)).astype(BF16)
            n_ref[:, qs] = (st.pop("e_c") * n_rows
                            + jnp.dot(st.pop("w_mat").astype(BF16), k, preferred_element_type=F32))

        return (gate_stage, score_stage, state_stage, output_stage), st

    heads = [head_stages(hl) for hl in range(hps)]
    for stage in range(4):
        if between is not None:
            between()
        for stages, _ in heads:
            stages[stage]()
    for hl, (_, st) in enumerate(heads):
        m_next = jnp.where(lane == head0 + hl, st["m_new"], m_next)

    m_ref[...] = m_next


def _mlstm(q, k, v, gates, mo, norm_g, state, nb, tpb, n_chunks, hps):
    m = q.shape[0]
    L = nb * tpb
    n_blocks = m // (L * n_chunks)
    hd = M_HEADS * M_DQK

    def tok(width):
        return pl.BlockSpec((L, width), lambda b, g, c: (b * n_chunks + c, g))

    c_spec = pl.BlockSpec((nb, hps, M_DV, M_DQK), lambda b, g, c: (b, g, 0, 0))
    n_spec = pl.BlockSpec((L, hps * M_DQK), lambda b, g, c: (b, g))
    m_spec = pl.BlockSpec((L, LANES), lambda b, g, c: (b, 0))
    in_specs = [tok(hps * M_DQK), tok(hps * M_DQK), tok(hps * M_DV),
                pl.BlockSpec((L, LANES), lambda b, g, c: (b * n_chunks + c, 0)),
                tok(hps * M_DV), pl.BlockSpec((1, hps * M_DV), lambda b, g, c: (0, g)),
                c_spec, n_spec, m_spec]
    args = [q, k, v, gates, mo, norm_g.reshape(1, M_HEADS * M_DV), *state]
    return pl.pallas_call(
        functools.partial(_mlstm_kernel, nb=nb, tpb=tpb, hps=hps),
        grid=(n_blocks, M_HEADS // hps, n_chunks),
        in_specs=in_specs,
        out_specs=[tok(hps * M_DV), c_spec, n_spec, m_spec],
        out_shape=[jax.ShapeDtypeStruct((m, M_HEADS * M_DV), BF16),
                   jax.ShapeDtypeStruct((n_blocks * nb, M_HEADS, M_DV, M_DQK), F32),
                   jax.ShapeDtypeStruct((n_blocks * L, hd), F32),
                   jax.ShapeDtypeStruct((n_blocks * L, LANES), F32)],
        scratch_shapes=[pltpu.VMEM((LANES, L), F32), pltpu.VMEM((LANES, L), F32)],
        compiler_params=_params(),
        name="mlstm",
    )(*args)


def _swa_kernel(sink_ref, q_ref, kp_ref, kc_ref, vp_ref, vc_ref, o_ref, kn_ref, vn_ref, *, nbb, n_new):
    tq = SAMPLE_TQ
    qi = lax.broadcasted_iota(jnp.int32, (tq, WINDOW), 0)
    kj = lax.broadcasted_iota(jnp.int32, (tq, WINDOW), 1)
    keep_old = lax.broadcasted_iota(jnp.int32, (WINDOW, W_AKV), 0) < WINDOW - n_new

    def new_rows(ref, jb, rows):
        x = ref[jb * n_new:(jb + 1) * n_new, :]
        return jnp.concatenate([x, jnp.zeros((rows - n_new, x.shape[1]), x.dtype)], axis=0)

    blocks = []
    for jb in range(nbb):
        kc, vc = new_rows(kc_ref, jb, WINDOW), new_rows(vc_ref, jb, WINDOW)
        kp, vp = kp_ref[jb], vp_ref[jb]
        kn_ref[jb] = jnp.where(keep_old, pltpu.roll(kp, WINDOW - n_new, 0), pltpu.roll(kc, WINDOW - n_new, 0))
        vn_ref[jb] = jnp.where(keep_old, pltpu.roll(vp, WINDOW - n_new, 0), pltpu.roll(vc, WINDOW - n_new, 0))
        q = new_rows(q_ref, jb, tq).astype(BF16)
        blocks.append(_swa_stages(sink_ref, q, kp, vp, kc, vc, kj > qi, o_ref, jb * n_new, n_new))

    for stage in range(3):
        for pairs in blocks:
            for pair in pairs:
                pair[stage]()


def _swa_stages(sink_ref, q, kp, vp, kc, vc, valid_prev, o_ref, o_row0=0, o_rows=None):
    tq = q.shape[0]
    o_rows = tq if o_rows is None else o_rows
    reps = A_HEADS // A_KV_HEADS
    low_q =lax.broadcasted_iota(jnp.int32, (tq, LANES), 1) < A_DH
    low_k = lax.broadcasted_iota(jnp.int32, (WINDOW, LANES), 1) < A_DH
    key_row = lax.broadcasted_iota(jnp.int32, (WINDOW, LANES), 0)
    qi = lax.broadcasted_iota(jnp.int32, (tq, WINDOW), 0)
    kj = lax.broadcasted_iota(jnp.int32, (tq, WINDOW), 1)
    sink_lane = kj == 0
    valid_cur = kj <= qi
    scale = jnp.asarray(ATTN_SCALE, BF16)
    neg_inf = -jnp.inf

    def block_diag(x):
        return jnp.concatenate([jnp.where(low_k, x, 0.0), jnp.where(low_k, 0.0, x)], axis=0).astype(BF16)

    def pair_stages(p):
        ks = slice(p * LANES, (p + 1) * LANES)
        state = {}

        def scores():
            q4 = jnp.concatenate([q[:, (reps * p + r) * LANES:(reps * p + r + 1) * LANES] * scale
                                  for r in range(reps)], axis=0)
            state["sp"] = lax.dot_general(q4, block_diag(kp[:, ks]), _NT, preferred_element_type=F32)
            state["sc"] = lax.dot_general(q4, block_diag(kc[:, ks]), _NT, preferred_element_type=F32)

        def softmax():
            sp, sc = state.pop("sp"), state.pop("sc")
            pps, pcs, invs = [], [], []
            for r in range(reps):
                rows = slice(r * tq, (r + 1) * tq)
                pp_r, pc_r, inv_r = [], [], []
                for half in range(2):
                    cols = slice(half * WINDOW, (half + 1) * WINDOW)
                    head = 2 * reps * p + reps * half + r
                    sp_i = jnp.where(sink_lane, sink_ref[head], jnp.where(valid_prev, sp[rows, cols], neg_inf))
                    sc_i = jnp.where(valid_cur, sc[rows, cols], neg_inf)
                    mx = jnp.max(jnp.maximum(sp_i, sc_i), axis=1, keepdims=True)
                    pp = jnp.exp(sp_i - mx)
                    pc = jnp.exp(sc_i - mx)
                    inv_r.append(1.0 / jnp.sum(pp + pc, axis=1, keepdims=True))
                    pp_r.append(pp.astype(BF16))
                    pc_r.append(pc.astype(BF16))
                pps.append(jnp.concatenate(pp_r, axis=1))
                pcs.append(jnp.concatenate(pc_r, axis=1))
                invs.append(jnp.where(low_q, inv_r[0], inv_r[1]))
            state.update(pp=jnp.concatenate(pps, axis=0), pc=jnp.concatenate(pcs, axis=0), inv=invs)

        def values():
            v_prev = block_diag(jnp.where(key_row == 0, 0.0, vp[:, ks]))
            o4 = (jnp.dot(state.pop("pp"), v_prev, preferred_element_type=F32)
                  + jnp.dot(state.pop("pc"), block_diag(vc[:, ks]), preferred_element_type=F32))
            invs = state.pop("inv")
            for r in range(reps):
                blk = reps * p + r
                o_ref[o_row0:o_row0 + o_rows, blk * LANES:(blk + 1) * LANES] = (
                    o4[r * tq:r * tq + o_rows] * invs[r][:o_rows]).astype(o_ref.dtype)

        return scores, softmax, values

    return [pair_stages(p) for p in range(A_KV_HEADS // 2)]


def _swa_block(sink_ref, q, kp, vp, kc, vc, valid_prev, o_ref, between=None):
    for scores, softmax, values in _swa_stages(sink_ref, q, kp, vp, kc, vc, valid_prev, o_ref):
        if between is not None:
            between()
        scores()
        softmax()
        if between is not None:
            between()
        values()


def _swa(sinks, q, k_cache, k_new, v_cache, v_new, *, nbb, n_new):
    n = k_cache.shape[0]
    assert n_new <= SAMPLE_TQ and q.shape[0] == n * n_new
    cache_spec = pl.BlockSpec((nbb, WINDOW, W_AKV), lambda i: (i, 0, 0))
    new_spec = pl.BlockSpec((nbb * n_new, W_AKV), lambda i: (i, 0))
    q_spec = pl.BlockSpec((nbb * n_new, W_AQ), lambda i: (i, 0))
    return pl.pallas_call(
        functools.partial(_swa_kernel, nbb=nbb, n_new=n_new),
        grid=(n // nbb,),
        in_specs=[pl.BlockSpec(memory_space=pltpu.SMEM), q_spec, cache_spec, new_spec, cache_spec, new_spec],
        out_specs=[q_spec, cache_spec, cache_spec],
        out_shape=[jax.ShapeDtypeStruct((n * n_new, W_AQ), F32)]
        + [jax.ShapeDtypeStruct((n, WINDOW, W_AKV), F32)] * 2,
        compiler_params=_params(),
        name="swa",
    )(sinks, q, k_cache, k_new, v_cache, v_new)


def _merge_kernel(x_ref, g_ref, hm_ref, oa_ref, gm_ref, ga_ref, wm_ref, wa_ref, wo_ref,
                  lg_ref, lb_ref, o_ref):
    ym = jnp.dot(hm_ref[...], wm_ref[...], preferred_element_type=F32)
    ya = jnp.dot(oa_ref[...].astype(BF16), wa_ref[...], preferred_element_type=F32)
    mix = gm_ref[...].astype(F32) * ym + ga_ref[...].astype(F32) * ya
    t = jnp.dot(mix.astype(BF16), wo_ref[...], preferred_element_type=F32)
    y = DEEPNORM_ALPHA * x_ref[...] + (1.0 + _rows(g_ref)) * t
    o_ref[...] = _layer_norm(y, lg_ref[...], lb_ref[...], LN_EPS)


def _merge(x, mod, per_row, rows_per_batch, chunk, hm, oa, gm, ga, wm, wa, wo, ln_g, ln_b):
    m = x.shape[0]
    tm = ROW_TILE

    def tok():
        return pl.BlockSpec((tm, D_MODEL), lambda i: (i, 0))

    return pl.pallas_call(
        _merge_kernel,
        grid=(m // tm,),
        in_specs=[tok()] + _mod_specs(per_row, tm, rows_per_batch // tm, (chunk,))
        + [tok(), tok(), tok(), tok()]
        + [_resident((D_MODEL, D_MODEL))] * 3 + [_resident((1, D_MODEL))] * 2,
        out_specs=tok(),
        out_shape=jax.ShapeDtypeStruct((m, D_MODEL), F32),
        compiler_params=_params(),
        name="merge",
    )(x, mod, hm, oa, gm, ga, wm, wa, wo, ln_g.reshape(1, D_MODEL), ln_b.reshape(1, D_MODEL))


def _mixer_kernel(sink_ref, xa_ref, xb_ref, sh_ref, sc_ref, g_ref, wa_ref, wq_ref, wr_ref, wif_ref, bif_ref,
                  cos_ref, sin_ref, ng_ref, wm_ref, wba_ref, wo_ref, lg_ref, lb_ref,
                  y_ref, c_ref, n_ref, m_ref, kb_ref, vb_ref, *scratch, tiles_per_seq):
    n_staged = (len(scratch) - 6) // 2
    even, odd = scratch[:n_staged], scratch[n_staged:2 * n_staged]
    kprev_s, vprev_s, hm_s, oa_s, gtt_s, ctt_s = scratch[2 * n_staged:]
    tm = xa_ref.shape[0]
    s = pl.program_id(0)
    tile_b = jnp.maximum(s - 1, 0)
    seq_start = tile_b % tiles_per_seq == 0

    @pl.when(s == 0)
    def _():
        for ref in odd:
            ref[...] = jnp.zeros_like(ref)

    @pl.when(seq_start)
    def _():
        c_ref[...] = jnp.zeros_like(c_ref)
        n_ref[...] = jnp.zeros_like(n_ref)
        m_ref[...] = jnp.zeros_like(m_ref)
        kprev_s[...] = jnp.zeros_like(kprev_s)
        vprev_s[...] = jnp.zeros_like(vprev_s)

    def step(produce, consume):
        proj_parts = _proj_parts(xa_ref, sh_ref, sc_ref, wa_ref, wq_ref, wr_ref, wif_ref, bif_ref,
                                 cos_ref, sin_ref, *produce)
        pending = iter(proj_parts)

        counts = iter(MIXER_PROJ_SCHEDULE)

        def between():
            for _ in range(next(counts)):
                part = next(pending, None)
                if part is not None:
                    part()

        q_s, k_s, v_s, mo_s, aq_s, ak_s, av_s, gm_s, ga_s, gt_s = consume
        qi = lax.broadcasted_iota(jnp.int32, (WINDOW, WINDOW), 0)
        kj = lax.broadcasted_iota(jnp.int32, (WINDOW, WINDOW), 1)
        for j in range(tm // MLSTM_CHUNK):
            rows = pl.ds(j * MLSTM_CHUNK, MLSTM_CHUNK)
            _mlstm_chunk(q_s.at[rows], k_s.at[rows], v_s.at[rows], gt_s.at[rows], mo_s.at[rows], ng_ref,
                         hm_s.at[rows], c_ref, n_ref, m_ref, gtt_s, ctt_s,
                         nb=1, tpb=MLSTM_CHUNK, hps=M_HEADS, head0=0, between=between)
            valid_prev = kj > qi
            if j == 0:
                valid_prev = valid_prev & jnp.logical_not(seq_start)
                k_prev, v_prev = kprev_s[...], vprev_s[...]
            else:
                before = pl.ds((j - 1) * WINDOW, WINDOW)
                k_prev, v_prev = ak_s[before], av_s[before]
            _swa_block(sink_ref, aq_s[rows], k_prev, v_prev, ak_s[rows], av_s[rows], valid_prev,
                       oa_s.at[rows], between=between)

        for part in pending:
            part()
        _merge_kernel(xb_ref, g_ref, hm_s, oa_s, gm_s, ga_s, wm_ref, wba_ref, wo_ref, lg_ref, lb_ref, y_ref)

        last = pl.ds(tm - WINDOW, WINDOW)
        k_last, v_last = ak_s[last], av_s[last]
        kb_ref[0] = k_last.T
        vb_ref[0] = v_last.T
        kprev_s[...] = k_last
        vprev_s[...] = v_last

    @pl.when(s % 2 == 0)
    def _():
        step(even, odd)

    @pl.when(s % 2 == 1)
    def _():
        step(odd, even)


def _mixer(x1, mod, w, sinks, norm_g, cos_t, sin_t, bp, sp):
    tm = MIXER_TILE
    tps = sp // tm
    n_tiles = bp * tps
    hd = M_HEADS * M_DQK
    weights = (w["w_a"], w["w_aq"], w["w_r"], w["w_if"], w["bif"])

    def tile_a(s):
        return jnp.minimum(s, n_tiles - 1)

    def tile_b(s):
        return jnp.maximum(s - 1, 0)

    def mod_spec(tile, chunk):
        return pl.BlockSpec((1, 1, D_MODEL), lambda s: (tile(s) // tps, 0, chunk))

    def per_seq(*shape):
        return pl.BlockSpec((1,) + shape, lambda s: (tile_b(s) // tps,) + (0,) * len(shape))

    pos = pl.BlockSpec((tm, LANES), lambda s: (tile_a(s) % tps, 0))
    staged = [(W_MQ, BF16), (W_MQ, BF16), (W_MV, BF16), (W_MV, BF16), (W_AQ, BF16),
              (W_AKV, F32), (W_AKV, F32), (D_MODEL, BF16), (D_MODEL, BF16), (LANES, F32)]
    return pl.pallas_call(
        functools.partial(_mixer_kernel, tiles_per_seq=tps),
        grid=(n_tiles + 1,),
        in_specs=[pl.BlockSpec(memory_space=pltpu.SMEM),
                  pl.BlockSpec((tm, D_MODEL), lambda s: (tile_a(s), 0)),
                  pl.BlockSpec((tm, D_MODEL), lambda s: (tile_b(s), 0)),
                  mod_spec(tile_a, 3), mod_spec(tile_a, 4), mod_spec(tile_b, 5)]
        + [_resident(a.shape) for a in weights] + [pos, pos, _resident((1, M_HEADS * M_DV))]
        + [_resident((D_MODEL, D_MODEL))] * 3 + [_resident((1, D_MODEL))] * 2,
        out_specs=[pl.BlockSpec((tm, D_MODEL), lambda s: (tile_b(s), 0)),
                   per_seq(M_HEADS, M_DV, M_DQK),
                   pl.BlockSpec((MLSTM_CHUNK, hd), lambda s: (tile_b(s) // tps, 0)),
                   pl.BlockSpec((MLSTM_CHUNK, LANES), lambda s: (tile_b(s) // tps, 0)),
                   per_seq(W_AKV, WINDOW), per_seq(W_AKV, WINDOW)],
        out_shape=[jax.ShapeDtypeStruct((bp * sp, D_MODEL), F32),
                   jax.ShapeDtypeStruct((bp, M_HEADS, M_DV, M_DQK), F32),
                   jax.ShapeDtypeStruct((bp * MLSTM_CHUNK, hd), F32),
                   jax.ShapeDtypeStruct((bp * MLSTM_CHUNK, LANES), F32),
                   jax.ShapeDtypeStruct((bp, W_AKV, WINDOW), F32),
                   jax.ShapeDtypeStruct((bp, W_AKV, WINDOW), F32)],
        scratch_shapes=[pltpu.VMEM((tm, width), dtype) for width, dtype in staged + staged]
        + [pltpu.VMEM((WINDOW, W_AKV), F32), pltpu.VMEM((WINDOW, W_AKV), F32),
           pltpu.VMEM((tm, W_MV), BF16), pltpu.VMEM((tm, W_AQ), BF16),
           pltpu.VMEM((LANES, MLSTM_CHUNK), F32), pltpu.VMEM((LANES, MLSTM_CHUNK), F32)],
        compiler_params=_params(),
        name="mixer",
    )(sinks, x1, x1, mod, mod, mod, *weights, cos_t, sin_t, norm_g.reshape(1, M_HEADS * M_DV),
      w["wm"], w["wa"], w["wo"], w["ln2_g"].reshape(1, D_MODEL), w["ln2_b"].reshape(1, D_MODEL))


def _rope_tables(pos):
    half = A_DH // 2
    inv = np.float32(ROPE_THETA) ** (-np.arange(half, dtype=np.float32) / np.float32(half))
    ang = pos.astype(np.float32)[:, None] * inv[None, :]
    cos, sin = np.cos(ang), np.sin(ang)
    return jnp.asarray(np.tile(cos, (1, 4))), jnp.asarray(np.concatenate([-sin, sin, -sin, sin], axis=1))


def _from_stored(kv_t):
    return jnp.transpose(kv_t.reshape(kv_t.shape[0], A_KV_HEADS, A_DH, WINDOW), (0, 3, 1, 2))


def _token_stage_2(x1, mod, per_row, rows_per_batch, w, hm, oa, gm, ga):
    x2 = _merge(x1, mod, per_row, rows_per_batch, 5, hm, oa, gm, ga,
                w["wm"], w["wa"], w["wo"], w["ln2_g"], w["ln2_b"])
    return _ffn(x2, mod, per_row, rows_per_batch, (6, 7, 8), w["up2"], w["down2"], w["ln3_g"], w["ln3_b"])


def kernel(x_prompt, x_sample, state_mlstm_C, state_mlstm_n, state_mlstm_m, cache_swa_k, cache_swa_v, c_prompt, c_sample, w_ada, b_ada, w_ffn1_up, w_ffn1_down, ln1_g, ln1_b, w_in, b_igate, b_fgate, m_norm_g, sinks, w_branch_m, w_branch_a, w_out, ln2_g, ln2_b, w_ffn2_up, w_ffn2_down, ln3_g, ln3_b):
    assert w_ada.shape[0] == DEPTH == 1
    bp, sp, _ = x_prompt.shape
    bs, ts, _ = x_sample.shape

    w_in_t = jnp.transpose(w_in[0])
    w = dict(
        bif=jnp.concatenate([b_igate[0], b_fgate[0], jnp.zeros((LANES - 2 * M_HEADS,), F32)]).reshape(1, LANES),
        ln1_g=ln1_g[0], ln1_b=ln1_b[0], ln2_g=ln2_g[0], ln2_b=ln2_b[0], ln3_g=ln3_g[0], ln3_b=ln3_b[0],
    )

    ms = bs * ts
    c_all = jnp.concatenate([jnp.repeat(c_sample, ts, axis=0), c_prompt], axis=0)
    mod = _ada(c_all, w_ada[0], b_ada[0])
    mod_p = mod[ms:].reshape(bp, 1, ADA_CHUNKS * D_MODEL)

    x1s, w["up1"], w["down1"] = _ffn_round(x_sample.reshape(ms, D_MODEL), mod, (0, 1, 2),
                                           w_ffn1_up[0], w_ffn1_down[0], w["ln1_g"], w["ln1_b"])

    mp = bp * sp
    reps = A_HEADS // A_KV_HEADS
    pair_low = lambda j: (j // reps) * (2 * reps) + j % reps
    pair_rows = lambda j: (j // (2 * reps)) * (2 * reps) + (j % 2) * reps + (j % (2 * reps)) // 2
    jobs = (
        _cast_job(w_ffn2_up[0], (D_MODEL, FF_CHUNK), 1),
        _cast_job(w_ffn2_down[0], (FF_CHUNK, D_MODEL), 0, first_step=2 * D_FF // FF_CHUNK),
        _cast_job(w_in_t, (D_MODEL, LANES), 1, transposed=True, n_blocks=IN_IF // LANES),
        _cast_job(w_in_t, (D_MODEL, LANES), 1, transposed=True, n_blocks=W_AQ // LANES, shift=IN_AQ - IN_IF,
                  sources=(lambda j: IN_IF // A_DH + pair_low(j), lambda j: IN_IF // A_DH + pair_low(j) + reps)),
        _cast_job(w_in_t, (D_MODEL, LANES), 1, transposed=True, n_blocks=(IN_END - IN_AK) // LANES,
                  shift=IN_AQ - IN_IF, sources=(lambda j: (IN_AK - (IN_AQ - IN_IF)) // LANES + j,)),
        _cast_job(w_branch_m[0], (D_MODEL, LANES), 1),
        _cast_job(w_out[0], (D_MODEL, LANES), 1),
        _cast_job(w_branch_a[0], (A_DH, D_MODEL), 0, sources=(pair_rows,)),
        _cast_job(w_in_t, (LANES, D_MODEL), 0, n_blocks=1, sources=(lambda j: IN_IF // LANES + j,)),
    )
    x1p, w["up2"], w["down2"], w["w_a"], w["w_aq"], w["w_r"], w["wm"], w["wo"], w["wa"], w["w_if"] = _ffn(
        x_prompt.reshape(mp, D_MODEL), mod_p, False, sp, (0, 1, 2),
        w["up1"], w["down1"], w["ln1_g"], w["ln1_b"], cast_jobs=jobs)
    cos_t, sin_t = _rope_tables(np.arange(sp))
    x2p, c_p, n_rows, m_rows, kb_p, vb_p = _mixer(x1p, mod_p, w, sinks[0], m_norm_g[0], cos_t, sin_t, bp, sp)
    y_p = _ffn(x2p, mod_p, False, sp, (6, 7, 8), w["up2"], w["down2"], w["ln3_g"], w["ln3_b"])
    n_p = n_rows.reshape(bp, MLSTM_CHUNK, M_HEADS * M_DQK)[:, 0].reshape(bp, M_HEADS, M_DQK)
    m_p = m_rows.reshape(bp, MLSTM_CHUNK, LANES)[:, 0, :M_HEADS]
    kb_p, vb_p = _from_stored(kb_p), _from_stored(vb_p)

    cos_t, sin_t = _rope_tables(PAST_LEN + np.arange(ms) % ts)
    qm, km, vm, mo, aq, ak, av, gm, ga, gt = _proj(x1s, mod, True, ms, (3, 4), w, cos_t, sin_t)
    seqs = MLSTM_CHUNK // ts
    n0_rows = jnp.repeat(state_mlstm_n[0].reshape(bs, M_HEADS * M_DQK), ts, axis=0)
    m0_rows = jnp.repeat(jnp.pad(state_mlstm_m[0], ((0, 0), (0, LANES - M_HEADS))), ts, axis=0)
    hm, c_s, n_rows, m_rows = _mlstm(qm, km, vm, gt, mo, m_norm_g[0],
                                     (state_mlstm_C[0], n0_rows, m0_rows), seqs, ts, 1, 2)
    n_s = n_rows.reshape(bs, ts, M_HEADS * M_DQK)[:, 0].reshape(bs, M_HEADS, M_DQK)
    m_s = m_rows.reshape(bs, ts, LANES)[:, 0, :M_HEADS]
    oa, kb_s, vb_s = _swa(
        sinks[0], aq, cache_swa_k[0].reshape(bs, WINDOW, W_AKV), ak,
        cache_swa_v[0].reshape(bs, WINDOW, W_AKV), av, nbb=SWA_SAMPLE_SEQS, n_new=ts)
    y_s = _token_stage_2(x1s, mod, True, ms, w, hm, oa, gm, ga)
    kb_s = kb_s.reshape(bs, WINDOW, A_KV_HEADS, A_DH)
    vb_s = vb_s.reshape(bs, WINDOW, A_KV_HEADS, A_DH)

    return (y_p.reshape(bp, sp, D_MODEL), y_s.reshape(bs, ts, D_MODEL),
            c_p[None], n_p[None], m_p[None], kb_p[None], vb_p[None],
            c_s[None], n_s[None], m_s[None], kb_s[None], vb_s[None])
```
